```python
import jax, jax.numpy as jnp
from jax import lax
import numpy as np

D_MODEL = 1024
BATCH = 2
SEQ = 16384
DEPTH = 2
DEC_BATCH = 32
DEC_SEQ = 32
PAST_LEN = 4096

CHUNK = 64
N_A = DEPTH // 2
N_B = DEPTH - N_A
A_HEADS = 8
A_DK = D_MODEL // 8
A_DV = D_MODEL // 4
A_QK = A_HEADS * A_DK
A_V = A_HEADS * A_DV
A_SPLITS = (A_QK, 2 * A_QK, 2 * A_QK + A_V, 2 * A_QK + 2 * A_V, 2 * A_QK + 3 * A_V, 2 * A_QK + 3 * A_V + A_HEADS)
A_IN = 2 * A_QK + 3 * A_V + 2 * A_HEADS
B_HEADS = 16
B_DH = D_MODEL // B_HEADS
B_W = B_HEADS * B_DH
B_IN = 2 * B_W
KV_COLS = 2 * B_W + B_HEADS
Q_BLOCK = 128
PLE_DIM = 256
EPS = 1e-6
NEG = -1e30

kernel_name = 'yoco_mlstm_fox_stream_step'


def rmsnorm(x, g):
    xf = x.astype(jnp.float32)
    y = xf * lax.rsqrt(jnp.mean(xf * xf, axis=-1, keepdims=True) + EPS)
    return (y * g.astype(jnp.float32)).astype(x.dtype)


def mlstm_chunk(carry, xs):
    C, n, m = carry
    q, k, v, ig, lf = xs
    L = q.shape[2]
    b = jnp.cumsum(lf, axis=-1)
    causal = jnp.tril(jnp.ones((L, L), dtype=bool))
    D = jnp.where(causal, b[..., :, None] - b[..., None, :] + ig[..., None, :], NEG)
    inter = b + m[..., None]
    m_t = jnp.maximum(inter, jnp.max(D, axis=-1))
    w_inter = jnp.exp(inter - m_t)
    s = jnp.einsum('bhtd,bhsd->bhts', q, k) * jnp.exp(D - m_t[..., None])
    num = w_inter[..., None] * jnp.einsum('bhtd,bhde->bhte', q, C) + jnp.einsum('bhts,bhse->bhte', s, v)
    den = w_inter * jnp.einsum('bhtd,bhd->bht', q, n) + jnp.sum(s, axis=-1)
    h = num / jnp.maximum(jnp.abs(den), jnp.exp(-m_t))[..., None]
    bL = b[..., -1]
    m_new = m_t[..., -1]
    w_old = jnp.exp(bL + m - m_new)
    w_s = jnp.exp(bL[..., None] - b + ig - m_new[..., None])
    ks = w_s[..., None] * k
    C_new = w_old[..., None, None] * C + jnp.einsum('bhsd,bhse->bhde', ks, v)
    n_new = w_old[..., None] * n + jnp.sum(ks, axis=2)
    return (C_new, n_new, m_new), h


def mlstm_layer(h, C0, n0, m0, norm_g, w_in, b_i, b_f, hnorm_g, w_out):
    bsz, seq, _ = h.shape
    u = rmsnorm(h, norm_g) @ w_in
    q, k, v, o, z, ig, fg = jnp.split(u, list(A_SPLITS), axis=-1)
    to_bhsd = lambda t, d: t.reshape(bsz, seq, A_HEADS, d).transpose(0, 2, 1, 3).astype(jnp.float32)
    q = to_bhsd(q, A_DK)
    k = to_bhsd(k, A_DK) * (A_DK ** -0.5)
    v = to_bhsd(v, A_DV)
    ig = (ig + b_i).astype(jnp.float32).transpose(0, 2, 1)
    lf = jax.nn.log_sigmoid((fg + b_f).astype(jnp.float32)).transpose(0, 2, 1)
    L = CHUNK if seq % CHUNK == 0 else seq
    nc = seq // L

    def chunks(t):
        return jnp.moveaxis(t.reshape(t.shape[:2] + (nc, L) + t.shape[3:]), 2, 0)

    carry0 = (C0.astype(jnp.float32), n0.astype(jnp.float32), m0.astype(jnp.float32))
    (C, n, m), hs = lax.scan(mlstm_chunk, carry0, (chunks(q), chunks(k), chunks(v), chunks(ig), chunks(lf)))
    hs = jnp.moveaxis(hs, 0, 2).reshape(bsz, A_HEADS, seq, A_DV).transpose(0, 2, 1, 3)
    hs = rmsnorm(hs, hnorm_g) * jax.nn.sigmoid(o.astype(jnp.float32)).reshape(bsz, seq, A_HEADS, A_DV)
    y = hs.reshape(bsz, seq, A_V).astype(h.dtype) * jax.nn.silu(z)
    return y @ w_out, C, n, m


def shared_kv(h, norm_g, w_kv, b_f, k_g):
    bsz, seq, _ = h.shape
    k, v, fg = jnp.split(rmsnorm(h, norm_g) @ w_kv, [B_W, 2 * B_W], axis=-1)
    k = rmsnorm(k.reshape(bsz, seq, B_HEADS, B_DH), k_g)
    v = v.reshape(bsz, seq, B_HEADS, B_DH)
    logf = jax.nn.log_sigmoid((fg + b_f).astype(jnp.float32))
    return k, v, logf


def fox_attend(q, k, v, Fq, Fk, q_pos, k_pos):
    s = jnp.einsum('bqhd,bkhd->bhqk', q, k).astype(jnp.float32) * (B_DH ** -0.5)
    s = s + jnp.swapaxes(Fq, 1, 2)[:, :, :, None] - jnp.swapaxes(Fk, 1, 2)[:, :, None, :]
    s = jnp.where(k_pos[None, :] <= q_pos[:, None], s, NEG)
    p = jax.nn.softmax(s, axis=-1)
    return jnp.einsum('bhqk,bkhd->bqhd', p.astype(v.dtype), v)


def fox_blocks(q, Fq, q_pos, k, v, Fk, k_pos):
    bsz, tq = q.shape[:2]
    blk = Q_BLOCK if tq % Q_BLOCK == 0 else tq
    nb = tq // blk
    qb = jnp.moveaxis(q.reshape(bsz, nb, blk, B_HEADS, B_DH), 1, 0)
    Fb = jnp.moveaxis(Fq.reshape(bsz, nb, blk, B_HEADS), 1, 0)
    pb = q_pos.reshape(nb, blk)
    out = lax.map(lambda a: fox_attend(a[0], k, v, a[1], Fk, a[2], k_pos), (qb, Fb, pb))
    return jnp.moveaxis(out, 0, 1).reshape(bsz, tq, B_HEADS, B_DH)


def fox_layer(h, k, v, Fk, k_pos, Fq, q_pos, norm_g, w_in, q_g, w_out):
    bsz, seq, _ = h.shape
    q, z = jnp.split(rmsnorm(h, norm_g) @ w_in, [B_W], axis=-1)
    q = rmsnorm(q.reshape(bsz, seq, B_HEADS, B_DH), q_g)
    o = fox_blocks(q, Fq, q_pos, k, v, Fk, k_pos).reshape(bsz, seq, B_W)
    return (o * jax.nn.silu(z)) @ w_out


def trunk(x, p, C0, n0, m0, past, W):
    (a_norm, a_w_in, a_b_i, a_b_f, a_hnorm, a_w_out, kv_norm, kv_w, kv_b_f, k_norm,
     b_norm, b_w_in, q_norm, b_w_out, ple_w, ple_norm, ple_w_g) = W
    seq = x.shape[1]
    h = x
    Cs, ns, ms = [], [], []
    for i in range(DEPTH):
        if i < N_A:
            y, C, n, m = mlstm_layer(h, C0[i], n0[i], m0[i], a_norm[i], a_w_in[i], a_b_i[i], a_b_f[i], a_hnorm[i], a_w_out[i])
            Cs.append(C)
            ns.append(n)
            ms.append(m)
            h = h + y
        else:
            if i == N_A:
                k_new, v_new, lf_new = shared_kv(h, kv_norm, kv_w, kv_b_f, k_norm)
                if past is None:
                    k_all, v_all, lf_all = k_new, v_new, lf_new
                    offset = 0
                else:
                    k_c, v_c, lf_c = past
                    k_all = jnp.concatenate([k_c.astype(k_new.dtype), k_new], axis=1)
                    v_all = jnp.concatenate([v_c.astype(v_new.dtype), v_new], axis=1)
                    lf_all = jnp.concatenate([lf_c.astype(jnp.float32), lf_new], axis=1)
                    offset = k_c.shape[1]
                F = jnp.cumsum(lf_all, axis=1)
                Fq = F[:, offset:]
                k_pos = jnp.arange(offset + seq)
                q_pos = k_pos[offset:]
            j = i - N_A
            h = h + fox_layer(h, k_all, v_all, F, k_pos, Fq, q_pos, b_norm[j], b_w_in[j], q_norm[j], b_w_out[j])
        h = h + (p[i] @ ple_w[i]) * jax.nn.sigmoid(rmsnorm(h, ple_norm[i]) @ ple_w_g[i])
    return h, jnp.stack(Cs), jnp.stack(ns), jnp.stack(ms), k_new, v_new, lf_new


def setup_inputs(seed: int = 0) -> dict:
    key = jax.random.key(seed)
    ks = jax.random.split(key, 32)
    f32 = jnp.float32
    nrm = lambda k, shape, scale: jax.random.normal(k, shape, f32) * scale
    gain = lambda k, shape: 1.0 + 0.02 * jax.random.normal(k, shape, f32)
    return {
        'x_prompt': nrm(ks[0], (BATCH, SEQ, D_MODEL), 1.0),
        'x_sample': nrm(ks[1], (DEC_BATCH, DEC_SEQ, D_MODEL), 1.0),
        'cache_k': nrm(ks[2], (DEC_BATCH, PAST_LEN, B_HEADS, B_DH), 1.0),
        'cache_v': nrm(ks[3], (DEC_BATCH, PAST_LEN, B_HEADS, B_DH), 1.0),
        'cache_logf': jax.nn.log_sigmoid(nrm(ks[4], (DEC_BATCH, PAST_LEN, B_HEADS), 1.0) + jnp.linspace(1.0, 6.0, B_HEADS, dtype=f32)),
        'state_C': nrm(ks[5], (N_A, DEC_BATCH, A_HEADS, A_DK, A_DV), 0.1),
        'state_n': nrm(ks[6], (N_A, DEC_BATCH, A_HEADS, A_DK), 0.1),
        'state_m': nrm(ks[7], (N_A, DEC_BATCH, A_HEADS), 1.0),
        'p_prompt': nrm(ks[8], (DEPTH, BATCH, SEQ, PLE_DIM), 1.0),
        'p_sample': nrm(ks[9], (DEPTH, DEC_BATCH, DEC_SEQ, PLE_DIM), 1.0),
        'a_norm': gain(ks[10], (N_A, D_MODEL)),
        'a_w_in': nrm(ks[11], (N_A, D_MODEL, A_IN), D_MODEL ** -0.5),
        'a_b_i': nrm(ks[12], (N_A, A_HEADS), 0.1),
        'a_b_f': jnp.linspace(3.0, 6.0, A_HEADS, dtype=f32)[None, :] + nrm(ks[13], (N_A, A_HEADS), 0.1),
        'a_hnorm': gain(ks[14], (N_A, A_HEADS, A_DV)),
        'a_w_out': nrm(ks[15], (N_A, A_V, D_MODEL), A_V ** -0.5),
        'kv_norm': gain(ks[16], (D_MODEL,)),
        'kv_w': nrm(ks[17], (D_MODEL, KV_COLS), D_MODEL ** -0.5),
        'kv_b_f': jnp.linspace(1.0, 6.0, B_HEADS, dtype=f32) + nrm(ks[18], (B_HEADS,), 0.1),
        'k_norm': gain(ks[19], (B_DH,)),
        'b_norm': gain(ks[20], (N_B, D_MODEL)),
        'b_w_in': nrm(ks[21], (N_B, D_MODEL, B_IN), D_MODEL ** -0.5),
        'q_norm': gain(ks[22], (N_B, B_DH)),
        'b_w_out': nrm(ks[23], (N_B, B_W, D_MODEL), B_W ** -0.5),
        'ple_w': nrm(ks[24], (DEPTH, PLE_DIM, D_MODEL), PLE_DIM ** -0.5),
        'ple_norm': gain(ks[25], (DEPTH, D_MODEL)),
        'ple_w_g': nrm(ks[26], (DEPTH, D_MODEL, D_MODEL), D_MODEL ** -0.5),
    }


def reference(x_prompt, x_sample, cache_k, cache_v, cache_logf, state_C, state_n, state_m, p_prompt, p_sample,
              a_norm, a_w_in, a_b_i, a_b_f, a_hnorm, a_w_out, kv_norm, kv_w, kv_b_f, k_norm,
              b_norm, b_w_in, q_norm, b_w_out, ple_w, ple_norm, ple_w_g):
    W = (a_norm, a_w_in, a_b_i, a_b_f, a_hnorm, a_w_out, kv_norm, kv_w, kv_b_f, k_norm,
         b_norm, b_w_in, q_norm, b_w_out, ple_w, ple_norm, ple_w_g)
    bsz = x_prompt.shape[0]
    C0 = jnp.zeros((N_A, bsz, A_HEADS, A_DK, A_DV), jnp.float32)
    n0 = jnp.zeros((N_A, bsz, A_HEADS, A_DK), jnp.float32)
    m0 = jnp.zeros((N_A, bsz, A_HEADS), jnp.float32)
    y_prompt, prompt_C, prompt_n, prompt_m, prompt_k, prompt_v, prompt_logf = trunk(
        x_prompt, p_prompt, C0, n0, m0, None, W)
    y_sample, sample_C, sample_n, sample_m, sample_k, sample_v, sample_logf = trunk(
        x_sample, p_sample, state_C, state_n, state_m, (cache_k, cache_v, cache_logf), W)
    return (y_prompt, y_sample, prompt_C, prompt_n, prompt_m, prompt_k, prompt_v, prompt_logf,
            sample_C, sample_n, sample_m, sample_k, sample_v, sample_logf)
```

```python
import functools

import jax
import jax.numpy as jnp
from jax import lax
from jax.experimental import pallas as pl
from jax.experimental.pallas import tpu as pltpu

D_MODEL = 1024
A_HEADS = 8
A_DK = 128
A_DV = 256
A_QK = A_HEADS * A_DK
A_V = A_HEADS * A_DV
A_MAIN = 2 * A_QK + 3 * A_V
B_HEADS = 16
B_DH = 64
B_W = B_HEADS * B_DH
B_PAIRS = B_HEADS // 2
PLE_DIM = 256
EPS = 1e-6
NEG = -1e30

LANES = 128
VMEM_LIMIT = 56 * 1024 * 1024

F32 = jnp.float32
BF16 = jnp.bfloat16
HI = lax.Precision.HIGHEST
NT = (((1,), (1,)), ((), ()))
TN = (((0,), (0,)), ((), ()))


def _params(*sem):
    return pltpu.CompilerParams(dimension_semantics=sem, vmem_limit_bytes=VMEM_LIMIT)


def _tile(n, pref):
    t = min(n, pref)
    while n % t:
        t //= 2
    return t


def _rms(x, g):
    ms = jnp.mean(x * x, axis=-1, keepdims=True)
    return x * lax.rsqrt(ms + EPS) * g


def _log_sigmoid(x):
    return jnp.minimum(x, 0.0) - jnp.log1p(jnp.exp(-jnp.abs(x)))


def _sigmoid(x):
    return 1.0 / (1.0 + jnp.exp(-x))


def _diag_to_col(row, eye):
    n = eye.shape[0]
    return jnp.sum(jnp.where(eye, jnp.broadcast_to(row, (n, n)), 0.0), axis=1, keepdims=True)


def _seg_cumsum_lanes(x, seg):
    w = x.shape[-1]
    pos = lax.broadcasted_iota(jnp.int32, x.shape, x.ndim - 1) % seg
    k = 1
    while k < min(seg, w):
        x = x + jnp.where(pos >= k, pltpu.roll(x, k, x.ndim - 1), 0.0)
        k *= 2
    return x


def _a_inproj_kernel(x_ref, g_ref, w_ref, wgt_ref, u_ref, gt_ref, xn_ref):
    @pl.when(pl.program_id(1) == 0)
    def _():
        xn = _rms(x_ref[...], g_ref[...])
        xn_ref[...] = xn.astype(BF16)
        gt_ref[...] = lax.dot_general(wgt_ref[...], xn, NT, precision=HI, preferred_element_type=F32)

    u_ref[...] = jnp.dot(xn_ref[...], w_ref[...], preferred_element_type=F32)


def _a_inproj(x, g, w_main, w_gt):
    t = x.shape[0]
    tm = _tile(t, 512)
    tn = 1024
    return pl.pallas_call(
        _a_inproj_kernel,
        grid=(t // tm, A_MAIN // tn),
        in_specs=[
            pl.BlockSpec((tm, D_MODEL), lambda i, j: (i, 0)),
            pl.BlockSpec((1, D_MODEL), lambda i, j: (0, 0)),
            pl.BlockSpec((D_MODEL, tn), lambda i, j: (0, j)),
            pl.BlockSpec((2 * A_HEADS, D_MODEL), lambda i, j: (0, 0)),
        ],
        out_specs=[
            pl.BlockSpec((tm, tn), lambda i, j: (i, j)),
            pl.BlockSpec((2 * A_HEADS, tm), lambda i, j: (0, i)),
        ],
        out_shape=[
            jax.ShapeDtypeStruct((t, A_MAIN), F32),
            jax.ShapeDtypeStruct((2 * A_HEADS, t), F32),
        ],
        scratch_shapes=[pltpu.VMEM((tm, D_MODEL), BF16)],
        compiler_params=_params("parallel", "arbitrary"),
        name="a_inproj",
    )(x, g, w_main, w_gt)


def _mlstm_kernel(q_ref, k_ref, v_ref, g_ref, bias_ref, c0_ref, n0_ref, m0_ref,
                  h_ref, c_ref, n_ref, m_ref, *, n_seq, seq_len, n_chunks):
    R = LANES

    @pl.when(pl.program_id(1) == 0)
    def _():
        c_ref[...] = c0_ref[...]
        n_ref[...] = n0_ref[...]
        m_ref[...] = m0_ref[...]

    row = lax.broadcasted_iota(jnp.int32, (R, R), 0)
    col = lax.broadcasted_iota(jnp.int32, (R, R), 1)
    eye = row == col
    valid = (col <= row) & ((row // seq_len) == (col // seq_len))
    row1 = lax.broadcasted_iota(jnp.int32, (R, 1), 0)
    lane1 = lax.broadcasted_iota(jnp.int32, (1, R), 1)
    bias = bias_ref[...]
    scale = A_DK ** -0.5

    def chunk(r0):
        g = g_ref[:, pl.ds(r0, R)]
        ig_all = g[0:A_HEADS] + bias[0:A_HEADS]
        lf_all = _log_sigmoid(g[A_HEADS:] + bias[A_HEADS:])
        b_all = _seg_cumsum_lanes(lf_all, seq_len)
        for h in range(A_HEADS):
            q = q_ref[pl.ds(r0, R), h * A_DK:(h + 1) * A_DK]
            k = k_ref[pl.ds(r0, R), h * A_DK:(h + 1) * A_DK]
            v = v_ref[pl.ds(r0, R), h * A_DV:(h + 1) * A_DV]
            qb, kb, vb = q.astype(BF16), k.astype(BF16), v.astype(BF16)
            b_row = b_all[h:h + 1]
            ig_row = ig_all[h:h + 1]
            b_col = _diag_to_col(b_row, eye)
            m_prev = [m_ref[s, h][:, 0:1] for s in range(n_seq)]
            n_prev = [n_ref[s, h] for s in range(n_seq)]
            m_prev_col = m_prev[0]
            n_rows = n_prev[0]
            for s in range(1, n_seq):
                m_prev_col = jnp.where(row1 >= s * seq_len, m_prev[s], m_prev_col)
                n_rows = jnp.where(row1 >= s * seq_len, n_prev[s], n_rows)
            d = jnp.where(valid, b_col - b_row + ig_row, NEG)
            inter = b_col + m_prev_col
            m_t = jnp.maximum(inter, jnp.max(d, axis=1, keepdims=True))
            p = jnp.exp(d - m_t)
            w_inter = jnp.exp(inter - m_t)
            qk = lax.dot_general(qb, kb, NT, preferred_element_type=F32) * scale
            s_mat = qk * p
            qc = [jnp.dot(qb[s * seq_len:(s + 1) * seq_len], c_ref[s, h].astype(BF16),
                          preferred_element_type=F32) for s in range(n_seq)]
            qc = qc[0] if n_seq == 1 else jnp.concatenate(qc, axis=0)
            num = w_inter * qc + jnp.dot(s_mat.astype(BF16), vb, preferred_element_type=F32)
            den = (w_inter * jnp.sum(q * n_rows, axis=1, keepdims=True)
                   + jnp.sum(s_mat, axis=1, keepdims=True))
            hval = num * (1.0 / jnp.maximum(jnp.abs(den), jnp.exp(-m_t)))
            h_ref[pl.ds(r0, R), h * A_DV:(h + 1) * A_DV] = hval
            b_last = [b_row[:, (s + 1) * seq_len - 1:(s + 1) * seq_len] for s in range(n_seq)]
            m_new = [m_t[(s + 1) * seq_len - 1:(s + 1) * seq_len] for s in range(n_seq)]
            b_last_row, m_new_row = b_last[0], m_new[0]
            for s in range(1, n_seq):
                b_last_row = jnp.where(lane1 >= s * seq_len, b_last[s], b_last_row)
                m_new_row = jnp.where(lane1 >= s * seq_len, m_new[s], m_new_row)
            w_s_row = jnp.exp(b_last_row - b_row + ig_row - m_new_row)
            ks = k * (_diag_to_col(w_s_row, eye) * scale)
            ksb = ks.astype(BF16)
            for s in range(n_seq):
                sl = slice(s * seq_len, (s + 1) * seq_len)
                w_old = jnp.exp(b_last[s] + m_prev[s] - m_new[s])
                c_ref[s, h] = w_old * c_ref[s, h] + lax.dot_general(
                    ksb[sl], vb[sl], TN, preferred_element_type=F32)
                n_ref[s, h] = w_old * n_prev[s] + jnp.sum(ks[sl], axis=0, keepdims=True)
                m_ref[s, h] = jnp.broadcast_to(m_new[s], (1, LANES))

    if n_chunks == 1:
        chunk(0)
    else:
        def body(ci, carry):
            chunk(pl.multiple_of(ci * R, R))
            return carry
        lax.fori_loop(0, n_chunks, body, 0)


def _mlstm(u, gt, bias, c0, n0, m0, bsz, seq):
    if seq % LANES == 0:
        seq_len, n_seq = LANES, 1
        n_chunks = _tile(seq // LANES, 4)
    else:
        assert LANES % seq == 0 and bsz % (LANES // seq) == 0, (bsz, seq)
        seq_len, n_seq, n_chunks = seq, LANES // seq, 1
    tok = n_chunks * LANES
    steps = seq // (n_chunks * seq_len)
    t = bsz * seq
    tok_map = lambda b, c: (b * steps + c, 0)
    state4 = lambda shape: pl.BlockSpec((n_seq,) + shape, lambda b, c: (b, 0, 0, 0))
    kern = functools.partial(_mlstm_kernel, n_seq=n_seq, seq_len=seq_len, n_chunks=n_chunks)
    return pl.pallas_call(
        kern,
        grid=(bsz // n_seq, steps),
        in_specs=[
            pl.BlockSpec((tok, A_QK), tok_map),
            pl.BlockSpec((tok, A_QK), lambda b, c: (b * steps + c, 1)),
            pl.BlockSpec((tok, A_V), lambda b, c: (b * steps + c, 1)),
            pl.BlockSpec((2 * A_HEADS, tok), lambda b, c: (0, b * steps + c)),
            pl.BlockSpec((2 * A_HEADS, 1), lambda b, c: (0, 0)),
            state4((A_HEADS, A_DK, A_DV)),
            state4((A_HEADS, 1, A_DK)),
            state4((A_HEADS, 1, LANES)),
        ],
        out_specs=[
            pl.BlockSpec((tok, A_V), tok_map),
            state4((A_HEADS, A_DK, A_DV)),
            state4((A_HEADS, 1, A_DK)),
            state4((A_HEADS, 1, LANES)),
        ],
        out_shape=[
            jax.ShapeDtypeStruct((t, A_V), F32),
            jax.ShapeDtypeStruct((bsz, A_HEADS, A_DK, A_DV), F32),
            jax.ShapeDtypeStruct((bsz, A_HEADS, 1, A_DK), F32),
            jax.ShapeDtypeStruct((bsz, A_HEADS, 1, LANES), F32),
        ],
        compiler_params=_params("parallel", "arbitrary"),
        name="mlstm",
    )(u, u, u, gt, bias, c0, n0, m0)


def _ple(h, p_ref, pw_ref, pn_ref, pg_ref):
    pp = jnp.dot(p_ref[...].astype(BF16), pw_ref[...], preferred_element_type=F32)
    hn = _rms(h, pn_ref[...])
    gate = _sigmoid(jnp.dot(hn.astype(BF16), pg_ref[...], preferred_element_type=F32))
    return h + pp * gate


def _a_post_kernel(hs_ref, o_ref, z_ref, x_ref, p_ref, hg_ref, wo_ref, pw_ref, pn_ref, pg_ref,
                   out_ref, y_ref):
    for h in range(A_HEADS):
        sl = slice(h * A_DV, (h + 1) * A_DV)
        y = _rms(hs_ref[:, sl], hg_ref[:, sl]) * _sigmoid(o_ref[:, sl])
        z = z_ref[:, sl]
        y_ref[:, sl] = (y * (z * _sigmoid(z))).astype(BF16)
    h1 = x_ref[...] + jnp.dot(y_ref[...], wo_ref[...], preferred_element_type=F32)
    out_ref[...] = _ple(h1, p_ref, pw_ref, pn_ref, pg_ref)


def _const_spec(shape):
    return pl.BlockSpec(shape, lambda i: (0,) * len(shape))


def _a_post(hs, u, x, p, hnorm, w_out, ple_w, ple_norm, ple_wg):
    t = x.shape[0]
    tm = _tile(t, 256)
    return pl.pallas_call(
        _a_post_kernel,
        grid=(t // tm,),
        in_specs=[
            pl.BlockSpec((tm, A_V), lambda i: (i, 0)),
            pl.BlockSpec((tm, A_V), lambda i: (i, 2)),
            pl.BlockSpec((tm, A_V), lambda i: (i, 3)),
            pl.BlockSpec((tm, D_MODEL), lambda i: (i, 0)),
            pl.BlockSpec((tm, PLE_DIM), lambda i: (i, 0)),
            _const_spec((1, A_V)),
            _const_spec((A_V, D_MODEL)),
            _const_spec((PLE_DIM, D_MODEL)),
            _const_spec((1, D_MODEL)),
            _const_spec((D_MODEL, D_MODEL)),
        ],
        out_specs=pl.BlockSpec((tm, D_MODEL), lambda i: (i, 0)),
        out_shape=jax.ShapeDtypeStruct((t, D_MODEL), F32),
        scratch_shapes=[pltpu.VMEM((tm, A_V), BF16)],
        compiler_params=_params("parallel"),
        name="a_post",
    )(hs, u, u, x, p, hnorm, w_out, ple_w, ple_norm, ple_wg)


def _head_ms(x, gm_ref):
    return jnp.dot((x * x).astype(BF16), gm_ref[...], preferred_element_type=F32) * (1.0 / B_DH)


def _kv_kernel(h_ref, g_ref, w_ref, wf_ref, wft_ref, bf_ref, bft_ref, kg_ref, gm_ref,
               k_ref, v_ref, kb_ref, vb_ref, lf_ref, lft_ref):
    xn = _rms(h_ref[...], g_ref[...])
    kv = jnp.dot(xn.astype(BF16), w_ref[...], preferred_element_type=F32)
    k = kv[:, :B_W]
    v = kv[:, B_W:]
    kn = k * lax.rsqrt(_head_ms(k, gm_ref) + EPS) * kg_ref[...]
    k_ref[...] = kn
    v_ref[...] = v
    kb_ref[...] = kn.astype(BF16)
    vb_ref[...] = v.astype(BF16)
    fg = jnp.dot(xn, wf_ref[...], precision=HI, preferred_element_type=F32)
    lf_ref[...] = _log_sigmoid(fg + bf_ref[...])
    fgt = lax.dot_general(wft_ref[...], xn, NT, precision=HI, preferred_element_type=F32)
    lft_ref[...] = _log_sigmoid(fgt + bft_ref[...])


def _kv_proj(h, g, w_kv, w_f, w_ft, b_f, b_ft, k_g, gmat):
    t = h.shape[0]
    tm = _tile(t, 512)
    row = lambda w: pl.BlockSpec((tm, w), lambda i: (i, 0))
    return pl.pallas_call(
        _kv_kernel,
        grid=(t // tm,),
        in_specs=[
            row(D_MODEL),
            _const_spec((1, D_MODEL)),
            _const_spec((D_MODEL, 2 * B_W)),
            _const_spec((D_MODEL, B_HEADS)),
            _const_spec((B_HEADS, D_MODEL)),
            _const_spec((1, B_HEADS)),
            _const_spec((B_HEADS, 1)),
            _const_spec((1, B_W)),
            _const_spec((B_W, B_W)),
        ],
        out_specs=[row(B_W), row(B_W), row(B_W), row(B_W), row(B_HEADS),
                   pl.BlockSpec((B_HEADS, tm), lambda i: (0, i))],
        out_shape=[
            jax.ShapeDtypeStruct((t, B_W), F32),
            jax.ShapeDtypeStruct((t, B_W), F32),
            jax.ShapeDtypeStruct((t, B_W), BF16),
            jax.ShapeDtypeStruct((t, B_W), BF16),
            jax.ShapeDtypeStruct((t, B_HEADS), F32),
            jax.ShapeDtypeStruct((B_HEADS, t), F32),
        ],
        compiler_params=_params("parallel"),
        name="kv_proj",
    )(h, g, w_kv, w_f, w_ft, b_f, b_ft, k_g, gmat)


def _q_kernel(h_ref, g_ref, w_ref, qg_ref, gm_ref, q_ref, z_ref):
    xn = _rms(h_ref[...], g_ref[...])
    qz = jnp.dot(xn.astype(BF16), w_ref[...], preferred_element_type=F32)
    q = qz[:, :B_W]
    z_ref[...] = qz[:, B_W:]
    qn = q * lax.rsqrt(_head_ms(q, gm_ref) + EPS) * qg_ref[...]
    q_ref[...] = (qn * (B_DH ** -0.5)).astype(BF16)


def _q_proj(h, g, w_in, q_g, gmat):
    t = h.shape[0]
    tm = _tile(t, 512)
    row = lambda w: pl.BlockSpec((tm, w), lambda i: (i, 0))
    return pl.pallas_call(
        _q_kernel,
        grid=(t // tm,),
        in_specs=[row(D_MODEL), _const_spec((1, D_MODEL)), _const_spec((D_MODEL, 2 * B_W)),
                  _const_spec((1, B_W)), _const_spec((B_W, B_W))],
        out_specs=[row(B_W), row(B_W)],
        out_shape=[jax.ShapeDtypeStruct((t, B_W), BF16), jax.ShapeDtypeStruct((t, B_W), F32)],
        compiler_params=_params("parallel"),
        name="q_proj",
    )(h, g, w_in, q_g, gmat)


def _cumsum_kernel(x_ref, f_ref, carry_ref):
    @pl.when(pl.program_id(1) == 0)
    def _():
        carry_ref[...] = jnp.zeros_like(carry_ref)

    w = x_ref.shape[-1]
    f = _seg_cumsum_lanes(x_ref[...], w) + carry_ref[:, 0:1]
    f_ref[...] = f
    carry_ref[...] = jnp.broadcast_to(f[:, w - 1:w], carry_ref.shape)


def _cumsum_rows(x, bsz, width):
    tb = _tile(width, 2048) if width % LANES == 0 and (width // LANES) & (width // LANES - 1) == 0 else width
    nb = width // tb
    if x.ndim == 2:
        in_spec = pl.BlockSpec((B_HEADS, tb), lambda b, j: (0, b * nb + j))
    else:
        in_spec = pl.BlockSpec((None, B_HEADS, tb), lambda b, j: (b, 0, j))
    return pl.pallas_call(
        _cumsum_kernel,
        grid=(bsz, nb),
        in_specs=[in_spec],
        out_specs=pl.BlockSpec((None, B_HEADS, tb), lambda b, j: (b, 0, j)),
        out_shape=jax.ShapeDtypeStruct((bsz, B_HEADS, width), F32),
        scratch_shapes=[pltpu.VMEM((B_HEADS, LANES), F32)],
        compiler_params=_params("parallel", "arbitrary"),
        name="logf_cumsum",
    )(x)


def _softmax_step(t, hh, m_ref, l_ref, acc_ref, vb):
    m_prev = m_ref[hh]
    m_new = jnp.maximum(m_prev, jnp.max(t, axis=1, keepdims=True))
    alpha = jnp.exp(m_prev - m_new)
    p = jnp.exp(t - m_new)
    l_ref[hh] = alpha * l_ref[hh] + jnp.sum(p, axis=1, keepdims=True)
    acc_ref[hh] = alpha * acc_ref[hh] + jnp.dot(p.astype(BF16), vb, preferred_element_type=F32)
    m_ref[hh] = m_new


def _fox_kernel(q_ref, k_ref, v_ref, fq_ref, fk_ref, o_ref,
                qm_ref, fqc_ref, m_ref, l_ref, acc_ref, *, tq, tk, nk):
    qi = pl.program_id(2)
    kj = pl.program_id(3)
    lane = lax.broadcasted_iota(jnp.int32, (tq, LANES), 1)

    @pl.when(kj == 0)
    def _():
        q = q_ref[...]
        zero = jnp.zeros_like(q)
        qm_ref[0] = jnp.where(lane < B_DH, q, zero)
        qm_ref[1] = jnp.where(lane >= B_DH, q, zero)
        r = lax.broadcasted_iota(jnp.int32, (LANES, LANES), 0)
        c = lax.broadcasted_iota(jnp.int32, (LANES, LANES), 1)
        eye = r == c
        for hh in range(2):
            for s in range(tq // LANES):
                sl = slice(s * LANES, (s + 1) * LANES)
                fqc_ref[hh, sl, :] = _diag_to_col(fq_ref[hh:hh + 1, sl], eye)
        m_ref[...] = jnp.full_like(m_ref, NEG)
        l_ref[...] = jnp.zeros_like(l_ref)
        acc_ref[...] = jnp.zeros_like(acc_ref)

    @pl.when(kj * tk <= qi * tq + (tq - 1))
    def _():
        q_pos = qi * tq + lax.broadcasted_iota(jnp.int32, (tq, tk), 0)
        k_pos = kj * tk + lax.broadcasted_iota(jnp.int32, (tq, tk), 1)
        causal = k_pos <= q_pos
        kb = k_ref[...]
        vb = v_ref[...]
        for hh in range(2):
            s = lax.dot_general(qm_ref[hh], kb, NT, preferred_element_type=F32)
            t = s + (fqc_ref[hh] - fk_ref[hh:hh + 1, :])
            t = jnp.where(causal, t, NEG)
            _softmax_step(t, hh, m_ref, l_ref, acc_ref, vb)

    @pl.when(kj == nk - 1)
    def _():
        o0 = acc_ref[0] * (1.0 / l_ref[0])
        o1 = acc_ref[1] * (1.0 / l_ref[1])
        o_ref[...] = jnp.where(lane < B_DH, o0, o1)


def _fox_prompt(q, kb, vb, f_row, bsz, seq):
    tq = _tile(seq, 512)
    tk = _tile(seq, 512)
    nq, nk = seq // tq, seq // tk
    last = lambda qi: (qi * tq + tq - 1) // tk
    kern = functools.partial(_fox_kernel, tq=tq, tk=tk, nk=nk)
    return pl.pallas_call(
        kern,
        grid=(bsz, B_PAIRS, nq, nk),
        in_specs=[
            pl.BlockSpec((tq, LANES), lambda b, p, qi, kj: (b * nq + qi, p)),
            pl.BlockSpec((tk, LANES), lambda b, p, qi, kj: (b * nk + jnp.minimum(kj, last(qi)), p)),
            pl.BlockSpec((tk, LANES), lambda b, p, qi, kj: (b * nk + jnp.minimum(kj, last(qi)), p)),
            pl.BlockSpec((None, None, 2, tq), lambda b, p, qi, kj: (b, p, 0, qi)),
            pl.BlockSpec((None, None, 2, tk), lambda b, p, qi, kj: (b, p, 0, jnp.minimum(kj, last(qi)))),
        ],
        out_specs=pl.BlockSpec((tq, LANES), lambda b, p, qi, kj: (b * nq + qi, p)),
        out_shape=jax.ShapeDtypeStruct((bsz * seq, B_W), F32),
        scratch_shapes=[
            pltpu.VMEM((2, tq, LANES), BF16),
            pltpu.VMEM((2, tq, 1), F32),
            pltpu.VMEM((2, tq, 1), F32),
            pltpu.VMEM((2, tq, 1), F32),
            pltpu.VMEM((2, tq, LANES), F32),
        ],
        compiler_params=_params("parallel", "parallel", "parallel", "arbitrary"),
        name="fox_prompt",
    )(q, kb, vb, f_row, f_row)


def _fox_dec_kernel(q_ref, kc_ref, vc_ref, kn_ref, vn_ref, fn_ref, fc_ref, o_ref,
                    qs_ref, fqc_ref, m_ref, l_ref, acc_ref, *, sq, nkb):
    kj = pl.program_id(1)
    rows = 2 * sq
    lane = lax.broadcasted_iota(jnp.int32, (sq, LANES), 1)
    row_s = lax.broadcasted_iota(jnp.int32, (rows, 1), 0)

    @pl.when(kj == 0)
    def _():
        r = lax.broadcasted_iota(jnp.int32, (rows, LANES), 0)
        c = lax.broadcasted_iota(jnp.int32, (rows, LANES), 1)
        pick = c == (r % sq)
        for p in range(B_PAIRS):
            q = q_ref[:, p * LANES:(p + 1) * LANES]
            zero = jnp.zeros_like(q)
            qs_ref[p, 0:sq, :] = jnp.where(lane < B_DH, q, zero)
            qs_ref[p, sq:rows, :] = jnp.where(lane >= B_DH, q, zero)
            f2 = fn_ref[p]
            fm = jnp.where(r < sq, f2[0:1], f2[1:2])
            fqc_ref[p] = jnp.sum(jnp.where(pick, fm, 0.0), axis=1, keepdims=True)
        m_ref[...] = jnp.full_like(m_ref, NEG)
        l_ref[...] = jnp.zeros_like(l_ref)
        acc_ref[...] = jnp.zeros_like(acc_ref)

    def step(p, kb, vb, fk2, mask):
        s = lax.dot_general(qs_ref[p], kb, NT, preferred_element_type=F32)
        fk = jnp.where(row_s < sq, fk2[0:1], fk2[1:2])
        t = s + (fqc_ref[p] - fk)
        if mask is not None:
            t = jnp.where(mask, t, NEG)
        _softmax_step(t, p, m_ref, l_ref, acc_ref, vb)

    @pl.when(kj < nkb)
    def _():
        for p in range(B_PAIRS):
            sl = slice(p * LANES, (p + 1) * LANES)
            step(p, kc_ref[:, sl].astype(BF16), vc_ref[:, sl].astype(BF16), fc_ref[p], None)

    @pl.when(kj == nkb)
    def _():
        i = lax.broadcasted_iota(jnp.int32, (rows, LANES), 0) % sq
        j = lax.broadcasted_iota(jnp.int32, (rows, LANES), 1)
        mask = (j <= i) & (j < sq)
        for p in range(B_PAIRS):
            sl = slice(p * LANES, (p + 1) * LANES)
            step(p, kn_ref[:, sl], vn_ref[:, sl], fn_ref[p], mask)
            o = acc_ref[p] * (1.0 / l_ref[p])
            o_ref[:, sl] = jnp.where(lane < B_DH, o[0:sq], o[sq:rows])


def _fox_decode(q, cache_k, cache_v, kn, vn, f_new, f_cache, bsz, sq):
    past = cache_k.shape[1]
    tk = _tile(past, 1024)
    nkb = past // tk
    rows = 2 * sq
    cmap = lambda b, kj: (b, jnp.minimum(kj, nkb - 1), 0)
    kern = functools.partial(_fox_dec_kernel, sq=sq, nkb=nkb)
    return pl.pallas_call(
        kern,
        grid=(bsz, nkb + 1),
        in_specs=[
            pl.BlockSpec((None, sq, B_W), lambda b, kj: (b, 0, 0)),
            pl.BlockSpec((None, tk, B_W), cmap),
            pl.BlockSpec((None, tk, B_W), cmap),
            pl.BlockSpec((None, LANES, B_W), lambda b, kj: (b, 0, 0)),
            pl.BlockSpec((None, LANES, B_W), lambda b, kj: (b, 0, 0)),
            pl.BlockSpec((None, B_PAIRS, 2, LANES), lambda b, kj: (b, 0, 0, 0)),
            pl.BlockSpec((None, B_PAIRS, 2, tk), lambda b, kj: (b, 0, 0, jnp.minimum(kj, nkb - 1))),
        ],
        out_specs=pl.BlockSpec((None, sq, B_W), lambda b, kj: (b, 0, 0)),
        out_shape=jax.ShapeDtypeStruct((bsz, sq, B_W), F32),
        scratch_shapes=[
            pltpu.VMEM((B_PAIRS, rows, LANES), BF16),
            pltpu.VMEM((B_PAIRS, rows, 1), F32),
            pltpu.VMEM((B_PAIRS, rows, 1), F32),
            pltpu.VMEM((B_PAIRS, rows, 1), F32),
            pltpu.VMEM((B_PAIRS, rows, LANES), F32),
        ],
        compiler_params=_params("parallel", "arbitrary"),
        name="fox_decode",
    )(q, cache_k, cache_v, kn, vn, f_new, f_cache)


def _b_post_kernel(o_ref, z_ref, h_ref, p_ref, wo_ref, pw_ref, pn_ref, pg_ref, out_ref):
    z = z_ref[...]
    y = o_ref[...] * (z * _sigmoid(z))
    h2 = h_ref[...] + jnp.dot(y.astype(BF16), wo_ref[...], preferred_element_type=F32)
    out_ref[...] = _ple(h2, p_ref, pw_ref, pn_ref, pg_ref)


def _b_post(o, z, h, p, w_out, ple_w, ple_norm, ple_wg):
    t = h.shape[0]
    tm = _tile(t, 512)
    row = lambda w: pl.BlockSpec((tm, w), lambda i: (i, 0))
    return pl.pallas_call(
        _b_post_kernel,
        grid=(t // tm,),
        in_specs=[row(B_W), row(B_W), row(D_MODEL), row(PLE_DIM),
                  _const_spec((B_W, D_MODEL)), _const_spec((PLE_DIM, D_MODEL)),
                  _const_spec((1, D_MODEL)), _const_spec((D_MODEL, D_MODEL))],
        out_specs=row(D_MODEL),
        out_shape=jax.ShapeDtypeStruct((t, D_MODEL), F32),
        compiler_params=_params("parallel"),
        name="b_post",
    )(o, z, h, p, w_out, ple_w, ple_norm, ple_wg)


def _prep_weights(a_norm, a_w_in, a_b_i, a_b_f, a_hnorm, a_w_out, kv_norm, kv_w, kv_b_f, k_norm,
                  b_norm, b_w_in, q_norm, b_w_out, ple_w, ple_norm, ple_w_g):
    head_of_lane = jnp.arange(B_W, dtype=jnp.int32) // B_DH
    return dict(
        a_norm=a_norm[0].reshape(1, D_MODEL),
        a_w_main=a_w_in[0][:, :A_MAIN].astype(BF16),
        a_w_gt=a_w_in[0][:, A_MAIN:].T,
        a_bias=jnp.concatenate([a_b_i[0], a_b_f[0]]).reshape(2 * A_HEADS, 1),
        a_hnorm=a_hnorm[0].reshape(1, A_V),
        a_w_out=a_w_out[0].astype(BF16),
        kv_norm=kv_norm.reshape(1, D_MODEL),
        kv_w=kv_w[:, :2 * B_W].astype(BF16),
        kv_wf=kv_w[:, 2 * B_W:],
        kv_wft=kv_w[:, 2 * B_W:].T,
        kv_bf=kv_b_f.reshape(1, B_HEADS),
        kv_bft=kv_b_f.reshape(B_HEADS, 1),
        k_norm=jnp.tile(k_norm, B_HEADS).reshape(1, B_W),
        gmat=(head_of_lane[:, None] == head_of_lane[None, :]).astype(BF16),
        b_norm=b_norm[0].reshape(1, D_MODEL),
        b_w_in=b_w_in[0].astype(BF16),
        q_norm=jnp.tile(q_norm[0], B_HEADS).reshape(1, B_W),
        b_w_out=b_w_out[0].astype(BF16),
        ple_w=ple_w.astype(BF16),
        ple_norm=ple_norm.reshape(-1, 1, D_MODEL),
        ple_wg=ple_w_g.astype(BF16),
    )


def _trunk(x, p, c0, n0, m0, past, w):
    bsz, seq, _ = x.shape
    t = bsz * seq
    xf = x.reshape(t, D_MODEL)
    p0 = p[0].reshape(t, PLE_DIM)
    p1 = p[1].reshape(t, PLE_DIM)

    u, gt = _a_inproj(xf, w["a_norm"], w["a_w_main"], w["a_w_gt"])
    n0r = n0.reshape(bsz, A_HEADS, 1, A_DK)
    m0r = jnp.broadcast_to(m0.reshape(bsz, A_HEADS, 1, 1), (bsz, A_HEADS, 1, LANES))
    hs, c_new, n_new, m_new = _mlstm(u, gt, w["a_bias"], c0, n0r, m0r, bsz, seq)
    h1 = _a_post(hs, u, xf, p0, w["a_hnorm"], w["a_w_out"], w["ple_w"][0], w["ple_norm"][0], w["ple_wg"][0])

    k, v, kb, vb, lf, lft = _kv_proj(h1, w["kv_norm"], w["kv_w"], w["kv_wf"], w["kv_wft"],
                                     w["kv_bf"], w["kv_bft"], w["k_norm"], w["gmat"])
    q, z = _q_proj(h1, w["b_norm"], w["b_w_in"], w["q_norm"], w["gmat"])
    if past is None:
        f_row = _cumsum_rows(lft, bsz, seq).reshape(bsz, B_PAIRS, 2, seq)
        o = _fox_prompt(q, kb, vb, f_row, bsz, seq)
    else:
        cache_k, cache_v, cache_lf = past
        plen = cache_k.shape[1]
        assert seq <= LANES
        width = -(-(plen + seq) // LANES) * LANES
        lf_rows = jnp.concatenate(
            [jnp.swapaxes(cache_lf.astype(F32), 1, 2),
             jnp.swapaxes(lft.reshape(B_HEADS, bsz, seq), 0, 1),
             jnp.zeros((bsz, B_HEADS, width - plen - seq), F32)], axis=2)
        f_all = _cumsum_rows(lf_rows, bsz, width).reshape(bsz, B_PAIRS, 2, width)
        f_cache = f_all[..., :plen]
        f_new = jnp.pad(f_all[..., plen:plen + seq], ((0, 0), (0, 0), (0, 0), (0, LANES - seq)))
        pad_new = lambda a: jnp.pad(a.reshape(bsz, seq, B_W), ((0, 0), (0, LANES - seq), (0, 0)))
        o = _fox_decode(q.reshape(bsz, seq, B_W), cache_k.reshape(bsz, plen, B_W).astype(F32),
                        cache_v.reshape(bsz, plen, B_W).astype(F32), pad_new(kb), pad_new(vb),
                        f_new, f_cache, bsz, seq).reshape(t, B_W)
    y = _b_post(o, z, h1, p1, w["b_w_out"], w["ple_w"][1], w["ple_norm"][1], w["ple_wg"][1])

    return (y.reshape(bsz, seq, D_MODEL),
            c_new[None], n_new.reshape(1, bsz, A_HEADS, A_DK), m_new[None, :, :, 0, 0],
            k.reshape(bsz, seq, B_HEADS, B_DH), v.reshape(bsz, seq, B_HEADS, B_DH),
            lf.reshape(bsz, seq, B_HEADS))


def kernel(x_prompt, x_sample, cache_k, cache_v, cache_logf, state_C, state_n, state_m, p_prompt, p_sample,
           a_norm, a_w_in, a_b_i, a_b_f, a_hnorm, a_w_out, kv_norm, kv_w, kv_b_f, k_norm,
           b_norm, b_w_in, q_norm, b_w_out, ple_w, ple_norm, ple_w_g):
    w = _prep_weights(a_norm, a_w_in, a_b_i, a_b_f, a_hnorm, a_w_out, kv_norm, kv_w, kv_b_f, k_norm,
                      b_norm, b_w_in, q_norm, b_w_out, ple_w, ple_norm, ple_w_g)
    bsz = x_prompt.shape[0]
    c0 = jnp.zeros((bsz, A_HEADS, A_DK, A_DV), F32)
    n0 = jnp.zeros((bsz, A_HEADS, A_DK), F32)
    m0 = jnp.zeros((bsz, A_HEADS), F32)
    prompt = _trunk(x_prompt, p_prompt, c0, n0, m0, None, w)
    sample = _trunk(x_sample, p_sample, state_C[0].astype(F32), state_n[0].astype(F32),
                    state_m[0].astype(F32), (cache_k, cache_v, cache_logf), w)
    return (prompt[0], sample[0]) + prompt[1:] + sample[1:]
```

```python
import functools

import jax
import jax.numpy as jnp
from jax import lax
from jax.experimental import pallas as pl
from jax.experimental.pallas import tpu as pltpu

D_MODEL = 1024
A_HEADS = 8
A_DK = 128
A_DV = 256
A_QK = A_HEADS * A_DK
A_V = A_HEADS * A_DV
A_MAIN = 2 * A_QK + 3 * A_V
B_HEADS = 16
B_DH = 64
B_W = B_HEADS * B_DH
B_PAIRS = B_HEADS // 2
PLE_DIM = 256
EPS = 1e-6
NEG = -1e30
LOG2E = 1.4426950408889634
ACC_ROWS = B_DH + 16

LANES = 128
VMEM_LIMIT = 56 * 1024 * 1024

F32 = jnp.float32
BF16 = jnp.bfloat16
HI = lax.Precision.HIGHEST
NT = (((1,), (1,)), ((), ()))
TN = (((0,), (0,)), ((), ()))


def _params(*sem):
    return pltpu.CompilerParams(dimension_semantics=sem, vmem_limit_bytes=VMEM_LIMIT)


def _tile(n, pref):
    t = min(n, pref)
    while n % t:
        t //= 2
    return t


def _rms(x, g):
    ms = jnp.mean(x * x, axis=-1, keepdims=True)
    return x * lax.rsqrt(ms + EPS) * g


def _log_sigmoid(x):
    return jnp.minimum(x, 0.0) - jnp.log1p(jnp.exp(-jnp.abs(x)))


def _sigmoid(x):
    return 1.0 / (1.0 + jnp.exp(-x))


def _diag_to_col(row, eye):
    n = eye.shape[0]
    return jnp.sum(jnp.where(eye, jnp.broadcast_to(row, (n, n)), 0.0), axis=1, keepdims=True)


def _seg_cumsum_lanes(x, seg):
    w = x.shape[-1]
    pos = lax.broadcasted_iota(jnp.int32, x.shape, x.ndim - 1) % seg
    k = 1
    while k < min(seg, w):
        x = x + jnp.where(pos >= k, pltpu.roll(x, k, x.ndim - 1), 0.0)
        k *= 2
    return x


def _a_inproj_kernel(x_ref, g_ref, w_ref, wgt_ref, u_ref, gt_ref, xn_ref):
    @pl.when(pl.program_id(1) == 0)
    def _():
        xn = _rms(x_ref[...], g_ref[...])
        xn_ref[...] = xn.astype(BF16)
        gt_ref[...] = lax.dot_general(wgt_ref[...], xn, NT, precision=HI, preferred_element_type=F32)

    u_ref[...] = jnp.dot(xn_ref[...], w_ref[...], preferred_element_type=F32)


def _a_inproj(x, g, w_main, w_gt):
    t = x.shape[0]
    tm = _tile(t, 512)
    tn = 1024
    return pl.pallas_call(
        _a_inproj_kernel,
        grid=(t // tm, A_MAIN // tn),
        in_specs=[
            pl.BlockSpec((tm, D_MODEL), lambda i, j: (i, 0)),
            pl.BlockSpec((1, D_MODEL), lambda i, j: (0, 0)),
            pl.BlockSpec((D_MODEL, tn), lambda i, j: (0, j)),
            pl.BlockSpec((2 * A_HEADS, D_MODEL), lambda i, j: (0, 0)),
        ],
        out_specs=[
            pl.BlockSpec((tm, tn), lambda i, j: (i, j)),
            pl.BlockSpec((2 * A_HEADS, tm), lambda i, j: (0, i)),
        ],
        out_shape=[
            jax.ShapeDtypeStruct((t, A_MAIN), F32),
            jax.ShapeDtypeStruct((2 * A_HEADS, t), F32),
        ],
        scratch_shapes=[pltpu.VMEM((tm, D_MODEL), BF16)],
        compiler_params=_params("parallel", "arbitrary"),
        name="a_inproj",
    )(x, g, w_main, w_gt)


def _mlstm_kernel(q_ref, k_ref, v_ref, g_ref, bias_ref, c0_ref, n0_ref, m0_ref,
                  h_ref, c_ref, n_ref, m_ref, *, n_seq, seq_len, n_chunks):
    R = LANES

    @pl.when(pl.program_id(1) == 0)
    def _():
        c_ref[...] = c0_ref[...]
        n_ref[...] = n0_ref[...]
        m_ref[...] = m0_ref[...]

    row = lax.broadcasted_iota(jnp.int32, (R, R), 0)
    col = lax.broadcasted_iota(jnp.int32, (R, R), 1)
    eye = row == col
    valid = (col <= row) & ((row // seq_len) == (col // seq_len))
    row1 = lax.broadcasted_iota(jnp.int32, (R, 1), 0)
    lane1 = lax.broadcasted_iota(jnp.int32, (1, R), 1)
    bias = bias_ref[...]
    scale = A_DK ** -0.5

    def chunk(r0):
        g = g_ref[:, pl.ds(r0, R)]
        ig_all = g[0:A_HEADS] + bias[0:A_HEADS]
        lf_all = _log_sigmoid(g[A_HEADS:] + bias[A_HEADS:])
        b_all = _seg_cumsum_lanes(lf_all, seq_len)
        for h in range(A_HEADS):
            q = q_ref[pl.ds(r0, R), h * A_DK:(h + 1) * A_DK]
            k = k_ref[pl.ds(r0, R), h * A_DK:(h + 1) * A_DK]
            v = v_ref[pl.ds(r0, R), h * A_DV:(h + 1) * A_DV]
            qb, kb, vb = q.astype(BF16), k.astype(BF16), v.astype(BF16)
            b_row = b_all[h:h + 1]
            ig_row = ig_all[h:h + 1]
            b_col = _diag_to_col(b_row, eye)
            m_prev = [m_ref[s, h][:, 0:1] for s in range(n_seq)]
            n_prev = [n_ref[s, h] for s in range(n_seq)]
            m_prev_col = m_prev[0]
            n_rows = n_prev[0]
            for s in range(1, n_seq):
                m_prev_col = jnp.where(row1 >= s * seq_len, m_prev[s], m_prev_col)
                n_rows = jnp.where(row1 >= s * seq_len, n_prev[s], n_rows)
            d = jnp.where(valid, b_col - b_row + ig_row, NEG)
            inter = b_col + m_prev_col
            m_t = jnp.maximum(inter, jnp.max(d, axis=1, keepdims=True))
            p = jnp.exp(d - m_t)
            w_inter = jnp.exp(inter - m_t)
            qk = lax.dot_general(qb, kb, NT, preferred_element_type=F32) * scale
            s_mat = qk * p
            qc = [jnp.dot(qb[s * seq_len:(s + 1) * seq_len], c_ref[s, h].astype(BF16),
                          preferred_element_type=F32) for s in range(n_seq)]
            qc = qc[0] if n_seq == 1 else jnp.concatenate(qc, axis=0)
            num = w_inter * qc + jnp.dot(s_mat.astype(BF16), vb, preferred_element_type=F32)
            den = (w_inter * jnp.sum(q * n_rows, axis=1, keepdims=True)
                   + jnp.sum(s_mat, axis=1, keepdims=True))
            hval = num * (1.0 / jnp.maximum(jnp.abs(den), jnp.exp(-m_t)))
            h_ref[pl.ds(r0, R), h * A_DV:(h + 1) * A_DV] = hval
            b_last = [b_row[:, (s + 1) * seq_len - 1:(s + 1) * seq_len] for s in range(n_seq)]
            m_new = [m_t[(s + 1) * seq_len - 1:(s + 1) * seq_len] for s in range(n_seq)]
            b_last_row, m_new_row = b_last[0], m_new[0]
            for s in range(1, n_seq):
                b_last_row = jnp.where(lane1 >= s * seq_len, b_last[s], b_last_row)
                m_new_row = jnp.where(lane1 >= s * seq_len, m_new[s], m_new_row)
            w_s_row = jnp.exp(b_last_row - b_row + ig_row - m_new_row)
            ks = k * (_diag_to_col(w_s_row, eye) * scale)
            ksb = ks.astype(BF16)
            for s in range(n_seq):
                sl = slice(s * seq_len, (s + 1) * seq_len)
                w_old = jnp.exp(b_last[s] + m_prev[s] - m_new[s])
                c_ref[s, h] = w_old * c_ref[s, h] + lax.dot_general(
                    ksb[sl], vb[sl], TN, preferred_element_type=F32)
                n_ref[s, h] = w_old * n_prev[s] + jnp.sum(ks[sl], axis=0, keepdims=True)
                m_ref[s, h] = jnp.broadcast_to(m_new[s], (1, LANES))

    if n_chunks == 1:
        chunk(0)
    else:
        def body(ci, carry):
            chunk(pl.multiple_of(ci * R, R))
            return carry
        lax.fori_loop(0, n_chunks, body, 0)


def _mlstm(u, gt, bias, c0, n0, m0, bsz, seq):
    if seq % LANES == 0:
        seq_len, n_seq = LANES, 1
        n_chunks = _tile(seq // LANES, 4)
    else:
        assert LANES % seq == 0 and bsz % (LANES // seq) == 0, (bsz, seq)
        seq_len, n_seq, n_chunks = seq, LANES // seq, 1
    tok = n_chunks * LANES
    steps = seq // (n_chunks * seq_len)
    t = bsz * seq
    tok_map = lambda b, c: (b * steps + c, 0)
    state4 = lambda shape: pl.BlockSpec((n_seq,) + shape, lambda b, c: (b, 0, 0, 0))
    kern = functools.partial(_mlstm_kernel, n_seq=n_seq, seq_len=seq_len, n_chunks=n_chunks)
    return pl.pallas_call(
        kern,
        grid=(bsz // n_seq, steps),
        in_specs=[
            pl.BlockSpec((tok, A_QK), tok_map),
            pl.BlockSpec((tok, A_QK), lambda b, c: (b * steps + c, 1)),
            pl.BlockSpec((tok, A_V), lambda b, c: (b * steps + c, 1)),
            pl.BlockSpec((2 * A_HEADS, tok), lambda b, c: (0, b * steps + c)),
            pl.BlockSpec((2 * A_HEADS, 1), lambda b, c: (0, 0)),
            state4((A_HEADS, A_DK, A_DV)),
            state4((A_HEADS, 1, A_DK)),
            state4((A_HEADS, 1, LANES)),
        ],
        out_specs=[
            pl.BlockSpec((tok, A_V), tok_map),
            state4((A_HEADS, A_DK, A_DV)),
            state4((A_HEADS, 1, A_DK)),
            state4((A_HEADS, 1, LANES)),
        ],
        out_shape=[
            jax.ShapeDtypeStruct((t, A_V), F32),
            jax.ShapeDtypeStruct((bsz, A_HEADS, A_DK, A_DV), F32),
            jax.ShapeDtypeStruct((bsz, A_HEADS, 1, A_DK), F32),
            jax.ShapeDtypeStruct((bsz, A_HEADS, 1, LANES), F32),
        ],
        compiler_params=_params("parallel", "arbitrary"),
        name="mlstm",
    )(u, u, u, gt, bias, c0, n0, m0)


def _ple(h, p_ref, pw_ref, pn_ref, pg_ref):
    pp = jnp.dot(p_ref[...].astype(BF16), pw_ref[...], preferred_element_type=F32)
    hn = _rms(h, pn_ref[...])
    gate = _sigmoid(jnp.dot(hn.astype(BF16), pg_ref[...], preferred_element_type=F32))
    return h + pp * gate


def _a_post_kernel(hs_ref, o_ref, z_ref, x_ref, p_ref, hg_ref, wo_ref, pw_ref, pn_ref, pg_ref,
                   out_ref, y_ref):
    for h in range(A_HEADS):
        sl = slice(h * A_DV, (h + 1) * A_DV)
        y = _rms(hs_ref[:, sl], hg_ref[:, sl]) * _sigmoid(o_ref[:, sl])
        z = z_ref[:, sl]
        y_ref[:, sl] = (y * (z * _sigmoid(z))).astype(BF16)
    h1 = x_ref[...] + jnp.dot(y_ref[...], wo_ref[...], preferred_element_type=F32)
    out_ref[...] = _ple(h1, p_ref, pw_ref, pn_ref, pg_ref)


def _const_spec(shape):
    return pl.BlockSpec(shape, lambda i: (0,) * len(shape))


def _a_post(hs, u, x, p, hnorm, w_out, ple_w, ple_norm, ple_wg):
    t = x.shape[0]
    tm = _tile(t, 256)
    return pl.pallas_call(
        _a_post_kernel,
        grid=(t // tm,),
        in_specs=[
            pl.BlockSpec((tm, A_V), lambda i: (i, 0)),
            pl.BlockSpec((tm, A_V), lambda i: (i, 2)),
            pl.BlockSpec((tm, A_V), lambda i: (i, 3)),
            pl.BlockSpec((tm, D_MODEL), lambda i: (i, 0)),
            pl.BlockSpec((tm, PLE_DIM), lambda i: (i, 0)),
            _const_spec((1, A_V)),
            _const_spec((A_V, D_MODEL)),
            _const_spec((PLE_DIM, D_MODEL)),
            _const_spec((1, D_MODEL)),
            _const_spec((D_MODEL, D_MODEL)),
        ],
        out_specs=pl.BlockSpec((tm, D_MODEL), lambda i: (i, 0)),
        out_shape=jax.ShapeDtypeStruct((t, D_MODEL), F32),
        scratch_shapes=[pltpu.VMEM((tm, A_V), BF16)],
        compiler_params=_params("parallel"),
        name="a_post",
    )(hs, u, u, x, p, hnorm, w_out, ple_w, ple_norm, ple_wg)


def _head_ms(x, gm_ref):
    return jnp.dot((x * x).astype(BF16), gm_ref[...], preferred_element_type=F32) * (1.0 / B_DH)


def _kv_kernel(h_ref, g_ref, w_ref, wf_ref, wft_ref, bf_ref, bft_ref, kg_ref, gm_ref,
               k_ref, v_ref, kb_ref, vb_ref, vt_ref, lf_ref, lft_ref):
    xn = _rms(h_ref[...], g_ref[...])
    kv = jnp.dot(xn.astype(BF16), w_ref[...], preferred_element_type=F32)
    k = kv[:, :B_W]
    v = kv[:, B_W:]
    kn = k * lax.rsqrt(_head_ms(k, gm_ref) + EPS) * kg_ref[...]
    k_ref[...] = kn
    v_ref[...] = v
    kb_ref[...] = kn.astype(BF16)
    vb_ref[...] = v.astype(BF16)
    vt_ref[...] = v.T.astype(BF16)
    fg = jnp.dot(xn, wf_ref[...], precision=HI, preferred_element_type=F32)
    lf_ref[...] = _log_sigmoid(fg + bf_ref[...])
    fgt = lax.dot_general(wft_ref[...], xn, NT, precision=HI, preferred_element_type=F32)
    lft_ref[...] = _log_sigmoid(fgt + bft_ref[...])


def _kv_proj(h, g, w_kv, w_f, w_ft, b_f, b_ft, k_g, gmat):
    t = h.shape[0]
    tm = _tile(t, 512)
    row = lambda w: pl.BlockSpec((tm, w), lambda i: (i, 0))
    return pl.pallas_call(
        _kv_kernel,
        grid=(t // tm,),
        in_specs=[
            row(D_MODEL),
            _const_spec((1, D_MODEL)),
            _const_spec((D_MODEL, 2 * B_W)),
            _const_spec((D_MODEL, B_HEADS)),
            _const_spec((B_HEADS, D_MODEL)),
            _const_spec((1, B_HEADS)),
            _const_spec((B_HEADS, 1)),
            _const_spec((1, B_W)),
            _const_spec((B_W, B_W)),
        ],
        out_specs=[row(B_W), row(B_W), row(B_W), row(B_W),
                   pl.BlockSpec((B_W, tm), lambda i: (0, i)), row(B_HEADS),
                   pl.BlockSpec((B_HEADS, tm), lambda i: (0, i))],
        out_shape=[
            jax.ShapeDtypeStruct((t, B_W), F32),
            jax.ShapeDtypeStruct((t, B_W), F32),
            jax.ShapeDtypeStruct((t, B_W), BF16),
            jax.ShapeDtypeStruct((t, B_W), BF16),
            jax.ShapeDtypeStruct((B_W, t), BF16),
            jax.ShapeDtypeStruct((t, B_HEADS), F32),
            jax.ShapeDtypeStruct((B_HEADS, t), F32),
        ],
        compiler_params=_params("parallel"),
        name="kv_proj",
    )(h, g, w_kv, w_f, w_ft, b_f, b_ft, k_g, gmat)


def _q_kernel(h_ref, g_ref, w_ref, qg_ref, gm_ref, q_ref, z_ref):
    xn = _rms(h_ref[...], g_ref[...])
    qz = jnp.dot(xn.astype(BF16), w_ref[...], preferred_element_type=F32)
    q = qz[:, :B_W]
    z_ref[...] = qz[:, B_W:]
    qn = q * lax.rsqrt(_head_ms(q, gm_ref) + EPS) * qg_ref[...]
    q_ref[...] = (qn * (B_DH ** -0.5 * LOG2E)).astype(BF16)


def _q_proj(h, g, w_in, q_g, gmat):
    t = h.shape[0]
    tm = _tile(t, 512)
    row = lambda w: pl.BlockSpec((tm, w), lambda i: (i, 0))
    return pl.pallas_call(
        _q_kernel,
        grid=(t // tm,),
        in_specs=[row(D_MODEL), _const_spec((1, D_MODEL)), _const_spec((D_MODEL, 2 * B_W)),
                  _const_spec((1, B_W)), _const_spec((B_W, B_W))],
        out_specs=[row(B_W), row(B_W)],
        out_shape=[jax.ShapeDtypeStruct((t, B_W), BF16), jax.ShapeDtypeStruct((t, B_W), F32)],
        compiler_params=_params("parallel"),
        name="q_proj",
    )(h, g, w_in, q_g, gmat)


def _cumsum_kernel(x_ref, f_ref, carry_ref):
    @pl.when(pl.program_id(1) == 0)
    def _():
        carry_ref[...] = jnp.zeros_like(carry_ref)

    w = x_ref.shape[-1]
    f = _seg_cumsum_lanes(x_ref[...], w) + carry_ref[:, 0:1]
    f_ref[...] = f
    carry_ref[...] = jnp.broadcast_to(f[:, w - 1:w], carry_ref.shape)


def _cumsum_rows(x, bsz, width):
    tb = _tile(width, 2048) if width % LANES == 0 and (width // LANES) & (width // LANES - 1) == 0 else width
    nb = width // tb
    if x.ndim == 2:
        in_spec = pl.BlockSpec((B_HEADS, tb), lambda b, j: (0, b * nb + j))
    else:
        in_spec = pl.BlockSpec((None, B_HEADS, tb), lambda b, j: (b, 0, j))
    return pl.pallas_call(
        _cumsum_kernel,
        grid=(bsz, nb),
        in_specs=[in_spec],
        out_specs=pl.BlockSpec((None, B_HEADS, tb), lambda b, j: (b, 0, j)),
        out_shape=jax.ShapeDtypeStruct((bsz, B_HEADS, width), F32),
        scratch_shapes=[pltpu.VMEM((B_HEADS, LANES), F32)],
        compiler_params=_params("parallel", "arbitrary"),
        name="logf_cumsum",
    )(x)


def _softmax_step(t, hh, m_ref, l_ref, acc_ref, vb):
    m_prev = m_ref[hh]
    m_new = jnp.maximum(m_prev, jnp.max(t, axis=1, keepdims=True))
    alpha = jnp.exp2(m_prev - m_new)
    p = jnp.exp2(t - m_new)
    l_ref[hh] = alpha * l_ref[hh] + jnp.sum(p, axis=1, keepdims=True)
    acc_ref[hh] = alpha * acc_ref[hh] + jnp.dot(p.astype(BF16), vb, preferred_element_type=F32)
    m_ref[hh] = m_new


def _fox_kernel(q_ref, k_ref, vt_ref, f_ref, o_ref, qt_ref, m_ref, acc_ref, s00, s01, s10, s11, *, tq):
    tk = tq // 2
    sbuf = ((s00, s01), (s10, s11))
    qi = pl.program_id(2)
    q0 = pl.multiple_of(qi * tq, tq)
    feat = lax.broadcasted_iota(jnp.int32, (LANES, tq), 0)
    qt = q_ref[...].astype(F32).T
    qt_ref[0] = jnp.where(feat < B_DH, qt, 0.0).astype(BF16)
    qt_ref[1] = jnp.where(feat >= B_DH, qt, 0.0).astype(BF16)
    m_ref[...] = jnp.full_like(m_ref, NEG)
    acc_ref[...] = jnp.zeros_like(acc_ref)
    f_base = f_ref[:, pl.ds(q0, tq)][:, tq - 1:tq]
    ones = jnp.ones((ACC_ROWS - B_DH, tk), BF16)

    def scores(k0, slot):
        kb = k_ref[pl.ds(k0, tk), :]
        for hh in range(2):
            sbuf[slot][hh][...] = jnp.dot(kb, qt_ref[hh], preferred_element_type=F32)

    def update(k0, slot, causal):
        fk = (f_ref[:, pl.ds(k0, tk)] - f_base) * LOG2E
        for hh in range(2):
            fk_col = jnp.concatenate(
                [jnp.broadcast_to(fk[hh:hh + 1, c * LANES:(c + 1) * LANES], (LANES, LANES)).T
                 for c in range(tk // LANES)], axis=0)
            t = sbuf[slot][hh][...] - pltpu.repeat(fk_col, tq // LANES, axis=1)
            if causal is not None:
                t = jnp.where(causal, t, NEG)
            m_prev = m_ref[hh]
            m_new = jnp.maximum(m_prev, jnp.max(t, axis=0, keepdims=True))
            alpha = jnp.exp2(m_prev - m_new)
            p = jnp.exp2(t - m_new).astype(BF16)
            vt = jnp.concatenate([vt_ref[hh * B_DH:(hh + 1) * B_DH, pl.ds(k0, tk)], ones], axis=0)
            acc_ref[hh] = alpha * acc_ref[hh] + jnp.dot(vt, p, preferred_element_type=F32)
            m_ref[hh] = m_new

    scores(0, 0)

    def pair(jj, carry):
        k0 = pl.multiple_of(jj * tq, tq)
        scores(k0 + tk, 1)
        update(k0, 0, None)
        scores(k0 + tq, 0)
        update(k0 + tk, 1, None)
        return carry

    lax.fori_loop(0, qi, pair, 0)
    kpos = lax.broadcasted_iota(jnp.int32, (tk, tq), 0)
    qpos = lax.broadcasted_iota(jnp.int32, (tk, tq), 1)
    scores(q0 + tk, 1)
    update(q0, 0, kpos <= qpos)
    update(q0 + tk, 1, (kpos + tk) <= qpos)
    o_t = jnp.concatenate(
        [acc_ref[hh, 0:B_DH] * (1.0 / acc_ref[hh, B_DH:B_DH + 1]) for hh in range(2)], axis=0)
    o_ref[...] = o_t.T


def _fox_prompt(q, kb, vt, f_row, bsz, seq):
    tq = _tile(seq, 512)
    tk = tq // 2
    nq = seq // tq
    kern = functools.partial(_fox_kernel, tq=tq)
    return pl.pallas_call(
        kern,
        grid=(bsz, B_PAIRS, nq),
        in_specs=[
            pl.BlockSpec((tq, LANES), lambda b, p, qi: (b * nq + qi, p)),
            pl.BlockSpec((seq, LANES), lambda b, p, qi: (b, p)),
            pl.BlockSpec((LANES, seq), lambda b, p, qi: (p, b)),
            pl.BlockSpec((None, None, 2, seq), lambda b, p, qi: (b, p, 0, 0)),
        ],
        out_specs=pl.BlockSpec((tq, LANES), lambda b, p, qi: (b * nq + qi, p)),
        out_shape=jax.ShapeDtypeStruct((bsz * seq, B_W), F32),
        scratch_shapes=[
            pltpu.VMEM((2, LANES, tq), BF16),
            pltpu.VMEM((2, 1, tq), F32),
            pltpu.VMEM((2, ACC_ROWS, tq), F32),
        ] + [pltpu.VMEM((tk, tq), F32)] * 4,
        compiler_params=_params("parallel", "parallel", "arbitrary"),
        name="fox_prompt",
    )(q, kb, vt, f_row)


def _fox_dec_kernel(q_ref, kc_ref, vc_ref, kn_ref, vn_ref, fn_ref, fc_ref, o_ref,
                    qs_ref, fqc_ref, m_ref, l_ref, acc_ref, *, sq, nkb):
    kj = pl.program_id(1)
    rows = 2 * sq
    lane = lax.broadcasted_iota(jnp.int32, (sq, LANES), 1)
    row_s = lax.broadcasted_iota(jnp.int32, (rows, 1), 0)

    @pl.when(kj == 0)
    def _():
        r = lax.broadcasted_iota(jnp.int32, (rows, LANES), 0)
        c = lax.broadcasted_iota(jnp.int32, (rows, LANES), 1)
        pick = c == (r % sq)
        for p in range(B_PAIRS):
            q = q_ref[:, p * LANES:(p + 1) * LANES]
            zero = jnp.zeros_like(q)
            qs_ref[p, 0:sq, :] = jnp.where(lane < B_DH, q, zero)
            qs_ref[p, sq:rows, :] = jnp.where(lane >= B_DH, q, zero)
            f2 = fn_ref[p]
            fm = jnp.where(r < sq, f2[0:1], f2[1:2])
            fqc_ref[p] = jnp.sum(jnp.where(pick, fm, 0.0), axis=1, keepdims=True)
        m_ref[...] = jnp.full_like(m_ref, NEG)
        l_ref[...] = jnp.zeros_like(l_ref)
        acc_ref[...] = jnp.zeros_like(acc_ref)

    def step(p, kb, vb, fk2, mask):
        s = lax.dot_general(qs_ref[p], kb, NT, preferred_element_type=F32)
        fk = jnp.where(row_s < sq, fk2[0:1], fk2[1:2])
        t = s + (fqc_ref[p] - fk) * LOG2E
        if mask is not None:
            t = jnp.where(mask, t, NEG)
        _softmax_step(t, p, m_ref, l_ref, acc_ref, vb)

    @pl.when(kj < nkb)
    def _():
        for p in range(B_PAIRS):
            sl = slice(p * LANES, (p + 1) * LANES)
            step(p, kc_ref[:, sl].astype(BF16), vc_ref[:, sl].astype(BF16), fc_ref[p], None)

    @pl.when(kj == nkb)
    def _():
        i = lax.broadcasted_iota(jnp.int32, (rows, LANES), 0) % sq
        j = lax.broadcasted_iota(jnp.int32, (rows, LANES), 1)
        mask = (j <= i) & (j < sq)
        for p in range(B_PAIRS):
            sl = slice(p * LANES, (p + 1) * LANES)
            step(p, kn_ref[:, sl], vn_ref[:, sl], fn_ref[p], mask)
            o = acc_ref[p] * (1.0 / l_ref[p])
            o_ref[:, sl] = jnp.where(lane < B_DH, o[0:sq], o[sq:rows])


def _fox_decode(q, cache_k, cache_v, kn, vn, f_new, f_cache, bsz, sq):
    past = cache_k.shape[1]
    tk = _tile(past, 1024)
    nkb = past // tk
    rows = 2 * sq
    cmap = lambda b, kj: (b, jnp.minimum(kj, nkb - 1), 0)
    kern = functools.partial(_fox_dec_kernel, sq=sq, nkb=nkb)
    return pl.pallas_call(
        kern,
        grid=(bsz, nkb + 1),
        in_specs=[
            pl.BlockSpec((None, sq, B_W), lambda b, kj: (b, 0, 0)),
            pl.BlockSpec((None, tk, B_W), cmap),
            pl.BlockSpec((None, tk, B_W), cmap),
            pl.BlockSpec((None, LANES, B_W), lambda b, kj: (b, 0, 0)),
            pl.BlockSpec((None, LANES, B_W), lambda b, kj: (b, 0, 0)),
            pl.BlockSpec((None, B_PAIRS, 2, LANES), lambda b, kj: (b, 0, 0, 0)),
            pl.BlockSpec((None, B_PAIRS, 2, tk), lambda b, kj: (b, 0, 0, jnp.minimum(kj, nkb - 1))),
        ],
        out_specs=pl.BlockSpec((None, sq, B_W), lambda b, kj: (b, 0, 0)),
        out_shape=jax.ShapeDtypeStruct((bsz, sq, B_W), F32),
        scratch_shapes=[
            pltpu.VMEM((B_PAIRS, rows, LANES), BF16),
            pltpu.VMEM((B_PAIRS, rows, 1), F32),
            pltpu.VMEM((B_PAIRS, rows, 1), F32),
            pltpu.VMEM((B_PAIRS, rows, 1), F32),
            pltpu.VMEM((B_PAIRS, rows, LANES), F32),
        ],
        compiler_params=_params("parallel", "arbitrary"),
        name="fox_decode",
    )(q, cache_k, cache_v, kn, vn, f_new, f_cache)


def _b_post_kernel(o_ref, z_ref, h_ref, p_ref, wo_ref, pw_ref, pn_ref, pg_ref, out_ref):
    z = z_ref[...]
    y = o_ref[...] * (z * _sigmoid(z))
    h2 = h_ref[...] + jnp.dot(y.astype(BF16), wo_ref[...], preferred_element_type=F32)
    out_ref[...] = _ple(h2, p_ref, pw_ref, pn_ref, pg_ref)


def _b_post(o, z, h, p, w_out, ple_w, ple_norm, ple_wg):
    t = h.shape[0]
    tm = _tile(t, 512)
    row = lambda w: pl.BlockSpec((tm, w), lambda i: (i, 0))
    return pl.pallas_call(
        _b_post_kernel,
        grid=(t // tm,),
        in_specs=[row(B_W), row(B_W), row(D_MODEL), row(PLE_DIM),
                  _const_spec((B_W, D_MODEL)), _const_spec((PLE_DIM, D_MODEL)),
                  _const_spec((1, D_MODEL)), _const_spec((D_MODEL, D_MODEL))],
        out_specs=row(D_MODEL),
        out_shape=jax.ShapeDtypeStruct((t, D_MODEL), F32),
        compiler_params=_params("parallel"),
        name="b_post",
    )(o, z, h, p, w_out, ple_w, ple_norm, ple_wg)


def _prep_weights(a_norm, a_w_in, a_b_i, a_b_f, a_hnorm, a_w_out, kv_norm, kv_w, kv_b_f, k_norm,
                  b_norm, b_w_in, q_norm, b_w_out, ple_w, ple_norm, ple_w_g):
    head_of_lane = jnp.arange(B_W, dtype=jnp.int32) // B_DH
    return dict(
        a_norm=a_norm[0].reshape(1, D_MODEL),
        a_w_main=a_w_in[0][:, :A_MAIN].astype(BF16),
        a_w_gt=a_w_in[0][:, A_MAIN:].T,
        a_bias=jnp.concatenate([a_b_i[0], a_b_f[0]]).reshape(2 * A_HEADS, 1),
        a_hnorm=a_hnorm[0].reshape(1, A_V),
        a_w_out=a_w_out[0].astype(BF16),
        kv_norm=kv_norm.reshape(1, D_MODEL),
        kv_w=kv_w[:, :2 * B_W].astype(BF16),
        kv_wf=kv_w[:, 2 * B_W:],
        kv_wft=kv_w[:, 2 * B_W:].T,
        kv_bf=kv_b_f.reshape(1, B_HEADS),
        kv_bft=kv_b_f.reshape(B_HEADS, 1),
        k_norm=jnp.tile(k_norm, B_HEADS).reshape(1, B_W),
        gmat=(head_of_lane[:, None] == head_of_lane[None, :]).astype(BF16),
        b_norm=b_norm[0].reshape(1, D_MODEL),
        b_w_in=b_w_in[0].astype(BF16),
        q_norm=jnp.tile(q_norm[0], B_HEADS).reshape(1, B_W),
        b_w_out=b_w_out[0].astype(BF16),
        ple_w=ple_w.astype(BF16),
        ple_norm=ple_norm.reshape(-1, 1, D_MODEL),
        ple_wg=ple_w_g.astype(BF16),
    )


def _trunk(x, p, c0, n0, m0, past, w):
    bsz, seq, _ = x.shape
    t = bsz * seq
    xf = x.reshape(t, D_MODEL)
    p0 = p[0].reshape(t, PLE_DIM)
    p1 = p[1].reshape(t, PLE_DIM)

    u, gt = _a_inproj(xf, w["a_norm"], w["a_w_main"], w["a_w_gt"])
    n0r = n0.reshape(bsz, A_HEADS, 1, A_DK)
    m0r = jnp.broadcast_to(m0.reshape(bsz, A_HEADS, 1, 1), (bsz, A_HEADS, 1, LANES))
    hs, c_new, n_new, m_new = _mlstm(u, gt, w["a_bias"], c0, n0r, m0r, bsz, seq)
    h1 = _a_post(hs, u, xf, p0, w["a_hnorm"], w["a_w_out"], w["ple_w"][0], w["ple_norm"][0], w["ple_wg"][0])

    k, v, kb, vb, vt, lf, lft = _kv_proj(h1, w["kv_norm"], w["kv_w"], w["kv_wf"], w["kv_wft"],
                                     w["kv_bf"], w["kv_bft"], w["k_norm"], w["gmat"])
    q, z = _q_proj(h1, w["b_norm"], w["b_w_in"], w["q_norm"], w["gmat"])
    if past is None:
        f_row = _cumsum_rows(lft, bsz, seq).reshape(bsz, B_PAIRS, 2, seq)
        o = _fox_prompt(q, kb, vt, f_row, bsz, seq)
    else:
        cache_k, cache_v, cache_lf = past
        plen = cache_k.shape[1]
        assert seq <= LANES
        width = -(-(plen + seq) // LANES) * LANES
        lf_rows = jnp.concatenate(
            [jnp.swapaxes(cache_lf.astype(F32), 1, 2),
             jnp.swapaxes(lft.reshape(B_HEADS, bsz, seq), 0, 1),
             jnp.zeros((bsz, B_HEADS, width - plen - seq), F32)], axis=2)
        f_all = _cumsum_rows(lf_rows, bsz, width).reshape(bsz, B_PAIRS, 2, width)
        f_cache = f_all[..., :plen]
        f_new = jnp.pad(f_all[..., plen:plen + seq], ((0, 0), (0, 0), (0, 0), (0, LANES - seq)))
        pad_new = lambda a: jnp.pad(a.reshape(bsz, seq, B_W), ((0, 0), (0, LANES - seq), (0, 0)))
        o = _fox_decode(q.reshape(bsz, seq, B_W), cache_k.reshape(bsz, plen, B_W).astype(F32),
                        cache_v.reshape(bsz, plen, B_W).astype(F32), pad_new(kb), pad_new(vb),
                        f_new, f_cache, bsz, seq).reshape(t, B_W)
    y = _b_post(o, z, h1, p1, w["b_w_out"], w["ple_w"][1], w["ple_norm"][1], w["ple_wg"][1])

    return (y.reshape(bsz, seq, D_MODEL),
            c_new[None], n_new.reshape(1, bsz, A_HEADS, A_DK), m_new[None, :, :, 0, 0],
            k.reshape(bsz, seq, B_HEADS, B_DH), v.reshape(bsz, seq, B_HEADS, B_DH),
            lf.reshape(bsz, seq, B_HEADS))


def kernel(x_prompt, x_sample, cache_k, cache_v, cache_logf, state_C, state_n, state_m, p_prompt, p_sample,
           a_norm, a_w_in, a_b_i, a_b_f, a_hnorm, a_w_out, kv_norm, kv_w, kv_b_f, k_norm,
           b_norm, b_w_in, q_norm, b_w_out, ple_w, ple_norm, ple_w_g):
    w = _prep_weights(a_norm, a_w_in, a_b_i, a_b_f, a_hnorm, a_w_out, kv_norm, kv_w, kv_b_f, k_norm,
                      b_norm, b_w_in, q_norm, b_w_out, ple_w, ple_norm, ple_w_g)
    bsz = x_prompt.shape[0]
    c0 = jnp.zeros((bsz, A_HEADS, A_DK, A_DV), F32)
    n0 = jnp.zeros((bsz, A_HEADS, A_DK), F32)
    m0 = jnp.zeros((bsz, A_HEADS), F32)
    prompt = _trunk(x_prompt, p_prompt, c0, n0, m0, None, w)
    sample = _trunk(x_sample, p_sample, state_C[0].astype(F32), state_n[0].astype(F32),
                    state_m[0].astype(F32), (cache_k, cache_v, cache_logf), w)
    return (prompt[0], sample[0]) + prompt[1:] + sample[1:]
```

```python
import functools

import jax
import jax.numpy as jnp
from jax import lax
from jax.experimental import pallas as pl
from jax.experimental.pallas import tpu as pltpu

D_MODEL = 1024
A_HEADS = 8
A_DK = 128
A_DV = 256
A_QK = A_HEADS * A_DK
A_V = A_HEADS * A_DV
A_MAIN = 2 * A_QK + 3 * A_V
B_HEADS = 16
B_DH = 64
B_W = B_HEADS * B_DH
B_PAIRS = B_HEADS // 2
PLE_DIM = 256
EPS = 1e-6
NEG = -1e30
LOG2E = 1.4426950408889634
ACC_ROWS = B_DH + 16

LANES = 128
VMEM_LIMIT = 56 * 1024 * 1024

F32 = jnp.float32
BF16 = jnp.bfloat16
NN = (((1,), (0,)), ((), ()))
NT = (((1,), (1,)), ((), ()))
TN = (((0,), (0,)), ((), ()))


def _params(*sem):
    return pltpu.CompilerParams(dimension_semantics=sem, vmem_limit_bytes=VMEM_LIMIT)


def _tile(n, pref):
    t = min(n, pref)
    while n % t:
        t //= 2
    return t


def _rms(x, g):
    ms = jnp.mean(x * x, axis=-1, keepdims=True)
    return x * lax.rsqrt(ms + EPS) * g


def _log_sigmoid(x):
    return jnp.minimum(x, 0.0) - jnp.log1p(jnp.exp(-jnp.abs(x)))


def _sigmoid(x):
    return 1.0 / (1.0 + jnp.exp(-x))


def _diag_to_col(row, eye):
    n = eye.shape[0]
    return jnp.sum(jnp.where(eye, jnp.broadcast_to(row, (n, n)), 0.0), axis=1, keepdims=True)


def _seg_cumsum_lanes(x, seg):
    w = x.shape[-1]
    pos = lax.broadcasted_iota(jnp.int32, x.shape, x.ndim - 1) % seg
    k = 1
    while k < min(seg, w):
        x = x + jnp.where(pos >= k, pltpu.roll(x, k, x.ndim - 1), 0.0)
        k *= 2
    return x


def _split(x):
    hi = x.astype(BF16)
    return hi, (x - hi.astype(F32)).astype(BF16)


def _dot_split(a, b, dims):
    dot = functools.partial(lax.dot_general, dimension_numbers=dims, preferred_element_type=F32)
    return dot(a[0], b[0]) + (dot(a[0], b[1]) + dot(a[1], b[0]))


def _a_inproj_kernel(x_ref, g_ref, w_ref, wgh_ref, wgl_ref, u_ref, gt_ref, xn_ref):
    @pl.when(pl.program_id(1) == 0)
    def _():
        xn = _rms(x_ref[...], g_ref[...])
        xs = _split(xn)
        xn_ref[...] = xs[0]
        gt_ref[...] = _dot_split((wgh_ref[...], wgl_ref[...]), xs, NT)

    u_ref[...] = jnp.dot(xn_ref[...], w_ref[...], preferred_element_type=F32).astype(BF16)


def _a_inproj(x, g, w_main, w_gt):
    t = x.shape[0]
    tm = _tile(t, 1024)
    tn = 1024
    gate_spec = pl.BlockSpec((2 * A_HEADS, D_MODEL), lambda i, j: (0, 0))
    return pl.pallas_call(
        _a_inproj_kernel,
        grid=(t // tm, A_MAIN // tn),
        in_specs=[
            pl.BlockSpec((tm, D_MODEL), lambda i, j: (i, 0)),
            pl.BlockSpec((1, D_MODEL), lambda i, j: (0, 0)),
            pl.BlockSpec((D_MODEL, tn), lambda i, j: (0, j)),
            gate_spec,
            gate_spec,
        ],
        out_specs=[
            pl.BlockSpec((tm, tn), lambda i, j: (i, j)),
            pl.BlockSpec((2 * A_HEADS, tm), lambda i, j: (0, i)),
        ],
        out_shape=[
            jax.ShapeDtypeStruct((t, A_MAIN), BF16),
            jax.ShapeDtypeStruct((2 * A_HEADS, t), F32),
        ],
        scratch_shapes=[pltpu.VMEM((tm, D_MODEL), BF16)],
        compiler_params=_params("parallel", "arbitrary"),
        name="a_inproj",
    )(x, g, w_main, *w_gt)


def _mlstm_kernel(q_ref, k_ref, v_ref, g_ref, bias_ref, c0_ref, n0_ref, m0_ref,
                  h_ref, c_ref, n_ref, m_ref, *, n_seq, seq_len, n_chunks):
    R = LANES

    @pl.when(pl.program_id(1) == 0)
    def _():
        c_ref[...] = c0_ref[...]
        n_ref[...] = n0_ref[...]
        m_ref[...] = m0_ref[...]

    row = lax.broadcasted_iota(jnp.int32, (R, R), 0)
    col = lax.broadcasted_iota(jnp.int32, (R, R), 1)
    eye = row == col
    valid = (col <= row) & ((row // seq_len) == (col // seq_len))
    row1 = lax.broadcasted_iota(jnp.int32, (R, 1), 0)
    lane1 = lax.broadcasted_iota(jnp.int32, (1, R), 1)
    bias = bias_ref[...]
    scale = A_DK ** -0.5

    def chunk(r0):
        g = g_ref[:, pl.ds(r0, R)]
        ig_all = g[0:A_HEADS] + bias[0:A_HEADS]
        lf_all = _log_sigmoid(g[A_HEADS:] + bias[A_HEADS:])
        b_all = _seg_cumsum_lanes(lf_all, seq_len)
        for h in range(A_HEADS):
            qb = q_ref[pl.ds(r0, R), h * A_DK:(h + 1) * A_DK]
            kb = k_ref[pl.ds(r0, R), h * A_DK:(h + 1) * A_DK]
            vb = v_ref[pl.ds(r0, R), h * A_DV:(h + 1) * A_DV]
            q, k = qb.astype(F32), kb.astype(F32)
            b_row = b_all[h:h + 1]
            ig_row = ig_all[h:h + 1]
            b_col = _diag_to_col(b_row, eye)
            m_prev = [m_ref[s, h][:, 0:1] for s in range(n_seq)]
            n_prev = [n_ref[s, h] for s in range(n_seq)]
            m_prev_col = m_prev[0]
            n_rows = n_prev[0]
            for s in range(1, n_seq):
                m_prev_col = jnp.where(row1 >= s * seq_len, m_prev[s], m_prev_col)
                n_rows = jnp.where(row1 >= s * seq_len, n_prev[s], n_rows)
            d = jnp.where(valid, b_col - b_row + ig_row, NEG)
            inter = b_col + m_prev_col
            m_t = jnp.maximum(inter, jnp.max(d, axis=1, keepdims=True))
            p = jnp.exp(d - m_t)
            w_inter = jnp.exp(inter - m_t)
            qk = lax.dot_general(qb, kb, NT, preferred_element_type=F32) * scale
            s_mat = qk * p
            qc = [jnp.dot(qb[s * seq_len:(s + 1) * seq_len], c_ref[s, h].astype(BF16),
                          preferred_element_type=F32) for s in range(n_seq)]
            qc = qc[0] if n_seq == 1 else jnp.concatenate(qc, axis=0)
            num = w_inter * qc + jnp.dot(s_mat.astype(BF16), vb, preferred_element_type=F32)
            den = (w_inter * jnp.sum(q * n_rows, axis=1, keepdims=True)
                   + jnp.sum(s_mat, axis=1, keepdims=True))
            hval = num * (1.0 / jnp.maximum(jnp.abs(den), jnp.exp(-m_t)))
            h_ref[pl.ds(r0, R), h * A_DV:(h + 1) * A_DV] = hval
            b_last = [b_row[:, (s + 1) * seq_len - 1:(s + 1) * seq_len] for s in range(n_seq)]
            m_new = [m_t[(s + 1) * seq_len - 1:(s + 1) * seq_len] for s in range(n_seq)]
            b_last_row, m_new_row = b_last[0], m_new[0]
            for s in range(1, n_seq):
                b_last_row = jnp.where(lane1 >= s * seq_len, b_last[s], b_last_row)
                m_new_row = jnp.where(lane1 >= s * seq_len, m_new[s], m_new_row)
            w_s_row = jnp.exp(b_last_row - b_row + ig_row - m_new_row)
            ks = k * (_diag_to_col(w_s_row, eye) * scale)
            ksb = ks.astype(BF16)
            for s in range(n_seq):
                sl = slice(s * seq_len, (s + 1) * seq_len)
                w_old = jnp.exp(b_last[s] + m_prev[s] - m_new[s])
                c_ref[s, h] = w_old * c_ref[s, h] + lax.dot_general(
                    ksb[sl], vb[sl], TN, preferred_element_type=F32)
                n_ref[s, h] = w_old * n_prev[s] + jnp.sum(ks[sl], axis=0, keepdims=True)
                m_ref[s, h] = jnp.broadcast_to(m_new[s], (1, LANES))

    if n_chunks == 1:
        chunk(0)
    else:
        def body(ci, carry):
            chunk(pl.multiple_of(ci * R, R))
            return carry
        lax.fori_loop(0, n_chunks, body, 0)


def _mlstm(u, gt, bias, c0, n0, m0, bsz, seq):
    if seq % LANES == 0:
        seq_len, n_seq = LANES, 1
        n_chunks = _tile(seq // LANES, 4)
    else:
        assert LANES % seq == 0 and bsz % (LANES // seq) == 0, (bsz, seq)
        seq_len, n_seq, n_chunks = seq, LANES // seq, 1
    tok = n_chunks * LANES
    steps = seq // (n_chunks * seq_len)
    t = bsz * seq
    tok_map = lambda b, c: (b * steps + c, 0)
    state4 = lambda shape: pl.BlockSpec((n_seq,) + shape, lambda b, c: (b, 0, 0, 0))
    kern = functools.partial(_mlstm_kernel, n_seq=n_seq, seq_len=seq_len, n_chunks=n_chunks)
    return pl.pallas_call(
        kern,
        grid=(bsz // n_seq, steps),
        in_specs=[
            pl.BlockSpec((tok, A_QK), tok_map),
            pl.BlockSpec((tok, A_QK), lambda b, c: (b * steps + c, 1)),
            pl.BlockSpec((tok, A_V), lambda b, c: (b * steps + c, 1)),
            pl.BlockSpec((2 * A_HEADS, tok), lambda b, c: (0, b * steps + c)),
            pl.BlockSpec((2 * A_HEADS, 1), lambda b, c: (0, 0)),
            state4((A_HEADS, A_DK, A_DV)),
            state4((A_HEADS, 1, A_DK)),
            state4((A_HEADS, 1, LANES)),
        ],
        out_specs=[
            pl.BlockSpec((tok, A_V), tok_map),
            state4((A_HEADS, A_DK, A_DV)),
            state4((A_HEADS, 1, A_DK)),
            state4((A_HEADS, 1, LANES)),
        ],
        out_shape=[
            jax.ShapeDtypeStruct((t, A_V), F32),
            jax.ShapeDtypeStruct((bsz, A_HEADS, A_DK, A_DV), F32),
            jax.ShapeDtypeStruct((bsz, A_HEADS, 1, A_DK), F32),
            jax.ShapeDtypeStruct((bsz, A_HEADS, 1, LANES), F32),
        ],
        compiler_params=_params("parallel", "arbitrary"),
        name="mlstm",
    )(u, u, u, gt, bias, c0, n0, m0)


def _ple(h, p_ref, pw_ref, pn_ref, pg_ref):
    pp = jnp.dot(p_ref[...].astype(BF16), pw_ref[...], preferred_element_type=F32)
    hn = _rms(h, pn_ref[...])
    gate = _sigmoid(jnp.dot(hn.astype(BF16), pg_ref[...], preferred_element_type=F32))
    return h + pp * gate


def _a_post_kernel(hs_ref, o_ref, z_ref, x_ref, p_ref, hg_ref, wo_ref, pw_ref, pn_ref, pg_ref,
                   out_ref, y_ref):
    for h in range(A_HEADS):
        sl = slice(h * A_DV, (h + 1) * A_DV)
        y = _rms(hs_ref[:, sl], hg_ref[:, sl]) * _sigmoid(o_ref[:, sl].astype(F32))
        z = z_ref[:, sl].astype(F32)
        y_ref[:, sl] = (y * (z * _sigmoid(z))).astype(BF16)
    h1 = x_ref[...] + jnp.dot(y_ref[...], wo_ref[...], preferred_element_type=F32)
    out_ref[...] = _ple(h1, p_ref, pw_ref, pn_ref, pg_ref)


def _const_spec(shape):
    return pl.BlockSpec(shape, lambda i: (0,) * len(shape))


def _a_post(hs, u, x, p, hnorm, w_out, ple_w, ple_norm, ple_wg):
    t = x.shape[0]
    tm = _tile(t, 256)
    return pl.pallas_call(
        _a_post_kernel,
        grid=(t // tm,),
        in_specs=[
            pl.BlockSpec((tm, A_V), lambda i: (i, 0)),
            pl.BlockSpec((tm, A_V), lambda i: (i, 2)),
            pl.BlockSpec((tm, A_V), lambda i: (i, 3)),
            pl.BlockSpec((tm, D_MODEL), lambda i: (i, 0)),
            pl.BlockSpec((tm, PLE_DIM), lambda i: (i, 0)),
            _const_spec((1, A_V)),
            _const_spec((A_V, D_MODEL)),
            _const_spec((PLE_DIM, D_MODEL)),
            _const_spec((1, D_MODEL)),
            _const_spec((D_MODEL, D_MODEL)),
        ],
        out_specs=pl.BlockSpec((tm, D_MODEL), lambda i: (i, 0)),
        out_shape=jax.ShapeDtypeStruct((t, D_MODEL), F32),
        scratch_shapes=[pltpu.VMEM((tm, A_V), BF16)],
        compiler_params=_params("parallel"),
        name="a_post",
    )(hs, u, u, x, p, hnorm, w_out, ple_w, ple_norm, ple_wg)


def _head_ms(x, gm_ref):
    return jnp.dot((x * x).astype(BF16), gm_ref[...], preferred_element_type=F32) * (1.0 / B_DH)


def _kv_kernel(h_ref, g_ref, w_ref, wfh_ref, wfl_ref, wfth_ref, wftl_ref, bf_ref, bft_ref, kg_ref, gm_ref,
               k_ref, v_ref, kb_ref, vb_ref, vt_ref, lf_ref, lft_ref):
    xs = _split(_rms(h_ref[...], g_ref[...]))
    kv = jnp.dot(xs[0], w_ref[...], preferred_element_type=F32)
    k = kv[:, :B_W]
    v = kv[:, B_W:]
    kn = k * lax.rsqrt(_head_ms(k, gm_ref) + EPS) * kg_ref[...]
    k_ref[...] = kn
    v_ref[...] = v
    kb_ref[...] = kn.astype(BF16)
    vb_ref[...] = v.astype(BF16)
    vt_ref[...] = v.T.astype(BF16)
    fg = _dot_split(xs, (wfh_ref[...], wfl_ref[...]), NN)
    lf_ref[...] = _log_sigmoid(fg + bf_ref[...])
    fgt = _dot_split((wfth_ref[...], wftl_ref[...]), xs, NT)
    lft_ref[...] = _log_sigmoid(fgt + bft_ref[...])


def _kv_proj(h, g, w_kv, w_f, w_ft, b_f, b_ft, k_g, gmat):
    t = h.shape[0]
    tm = _tile(t, 512)
    row = lambda w: pl.BlockSpec((tm, w), lambda i: (i, 0))
    return pl.pallas_call(
        _kv_kernel,
        grid=(t // tm,),
        in_specs=[
            row(D_MODEL),
            _const_spec((1, D_MODEL)),
            _const_spec((D_MODEL, 2 * B_W)),
            _const_spec((D_MODEL, B_HEADS)),
            _const_spec((D_MODEL, B_HEADS)),
            _const_spec((B_HEADS, D_MODEL)),
            _const_spec((B_HEADS, D_MODEL)),
            _const_spec((1, B_HEADS)),
            _const_spec((B_HEADS, 1)),
            _const_spec((1, B_W)),
            _const_spec((B_W, B_W)),
        ],
        out_specs=[row(B_W), row(B_W), row(B_W), row(B_W),
                   pl.BlockSpec((B_W, tm), lambda i: (0, i)), row(B_HEADS),
                   pl.BlockSpec((B_HEADS, tm), lambda i: (0, i))],
        out_shape=[
            jax.ShapeDtypeStruct((t, B_W), F32),
            jax.ShapeDtypeStruct((t, B_W), F32),
            jax.ShapeDtypeStruct((t, B_W), BF16),
            jax.ShapeDtypeStruct((t, B_W), BF16),
            jax.ShapeDtypeStruct((B_W, t), BF16),
            jax.ShapeDtypeStruct((t, B_HEADS), F32),
            jax.ShapeDtypeStruct((B_HEADS, t), F32),
        ],
        compiler_params=_params("parallel"),
        name="kv_proj",
    )(h, g, w_kv, *w_f, *w_ft, b_f, b_ft, k_g, gmat)


def _q_kernel(h_ref, g_ref, w_ref, qg_ref, gm_ref, q_ref, z_ref):
    xn = _rms(h_ref[...], g_ref[...])
    qz = jnp.dot(xn.astype(BF16), w_ref[...], preferred_element_type=F32)
    q = qz[:, :B_W]
    z_ref[...] = qz[:, B_W:]
    qn = q * lax.rsqrt(_head_ms(q, gm_ref) + EPS) * qg_ref[...]
    q_ref[...] = (qn * (B_DH ** -0.5 * LOG2E)).astype(BF16)


def _q_proj(h, g, w_in, q_g, gmat):
    t = h.shape[0]
    tm = _tile(t, 512)
    row = lambda w: pl.BlockSpec((tm, w), lambda i: (i, 0))
    return pl.pallas_call(
        _q_kernel,
        grid=(t // tm,),
        in_specs=[row(D_MODEL), _const_spec((1, D_MODEL)), _const_spec((D_MODEL, 2 * B_W)),
                  _const_spec((1, B_W)), _const_spec((B_W, B_W))],
        out_specs=[row(B_W), row(B_W)],
        out_shape=[jax.ShapeDtypeStruct((t, B_W), BF16), jax.ShapeDtypeStruct((t, B_W), F32)],
        compiler_params=_params("parallel"),
        name="q_proj",
    )(h, g, w_in, q_g, gmat)


def _cumsum_kernel(x_ref, f_ref, carry_ref):
    @pl.when(pl.program_id(1) == 0)
    def _():
        carry_ref[...] = jnp.zeros_like(carry_ref)

    w = x_ref.shape[-1]
    f = _seg_cumsum_lanes(x_ref[...], w) + carry_ref[:, 0:1]
    f_ref[...] = f
    carry_ref[...] = jnp.broadcast_to(f[:, w - 1:w], carry_ref.shape)


def _cumsum_rows(x, bsz, width):
    tb = _tile(width, 2048) if width % LANES == 0 and (width // LANES) & (width // LANES - 1) == 0 else width
    nb = width // tb
    if x.ndim == 2:
        in_spec = pl.BlockSpec((B_HEADS, tb), lambda b, j: (0, b * nb + j))
    else:
        in_spec = pl.BlockSpec((None, B_HEADS, tb), lambda b, j: (b, 0, j))
    return pl.pallas_call(
        _cumsum_kernel,
        grid=(bsz, nb),
        in_specs=[in_spec],
        out_specs=pl.BlockSpec((None, B_HEADS, tb), lambda b, j: (b, 0, j)),
        out_shape=jax.ShapeDtypeStruct((bsz, B_HEADS, width), F32),
        scratch_shapes=[pltpu.VMEM((B_HEADS, LANES), F32)],
        compiler_params=_params("parallel", "arbitrary"),
        name="logf_cumsum",
    )(x)


def _fox_kernel(q_ref, k_ref, vt_ref, f_ref, o_ref, qt_ref, m_ref, cm_ref, acc_ref,
                s00, s01, s10, s11, *, tq):
    tk = tq // 2
    sbuf = ((s00, s01), (s10, s11))
    qi = pl.program_id(2)
    q0 = pl.multiple_of(qi * tq, tq)
    feat = lax.broadcasted_iota(jnp.int32, (LANES, tq), 0)
    qt = q_ref[...].astype(F32).T
    qt_ref[0] = jnp.where(feat < B_DH, qt, 0.0).astype(BF16)
    qt_ref[1] = jnp.where(feat >= B_DH, qt, 0.0).astype(BF16)
    m_ref[...] = jnp.full_like(m_ref, NEG)
    acc_ref[...] = jnp.zeros_like(acc_ref)
    f_base = f_ref[:, pl.ds(q0, tq)][:, tq - 1:tq]
    ones = jnp.ones((ACC_ROWS - B_DH, tk), BF16)
    kpos = lax.broadcasted_iota(jnp.int32, (tk, tq), 0)
    qpos = lax.broadcasted_iota(jnp.int32, (tk, tq), 1)
    diag_masks = (kpos <= qpos, (kpos + tk) <= qpos)

    def scores(k0, slot, causal=None):
        kb = k_ref[pl.ds(k0, tk), :]
        fk = (f_ref[:, pl.ds(k0, tk)] - f_base) * LOG2E
        for hh in range(2):
            fk_col = jnp.concatenate(
                [jnp.broadcast_to(fk[hh:hh + 1, c * LANES:(c + 1) * LANES], (LANES, LANES)).T
                 for c in range(tk // LANES)], axis=0)
            t = (jnp.dot(kb, qt_ref[hh], preferred_element_type=F32)
                 - pltpu.repeat(fk_col, tq // LANES, axis=1))
            if causal is not None:
                t = jnp.where(causal, t, NEG)
            sbuf[slot][hh][...] = t
            cm_ref[slot, hh] = jnp.max(t, axis=0, keepdims=True)

    def update(k0, slot):
        for hh in range(2):
            m_prev = m_ref[hh]
            m_new = jnp.maximum(m_prev, cm_ref[slot, hh])
            alpha = jnp.exp2(m_prev - m_new)
            p = jnp.exp2(sbuf[slot][hh][...] - m_new).astype(BF16)
            vt = jnp.concatenate([vt_ref[hh * B_DH:(hh + 1) * B_DH, pl.ds(k0, tk)], ones], axis=0)
            acc_ref[hh] = alpha * acc_ref[hh] + jnp.dot(vt, p, preferred_element_type=F32)
            m_ref[hh] = m_new

    def pair(k0, next_causal):
        scores(k0 + tk, 1)
        update(k0, 0)
        scores(k0 + tq, 0, next_causal)
        update(k0 + tk, 1)

    @pl.when(qi == 0)
    def _():
        scores(q0, 0, diag_masks[0])

    @pl.when(qi > 0)
    def _():
        scores(0, 0)

        def body(jj, carry):
            pair(pl.multiple_of(jj * tq, tq), None)
            return carry

        lax.fori_loop(0, qi - 1, body, 0)
        pair(q0 - tq, diag_masks[0])

    scores(q0 + tk, 1, diag_masks[1])
    update(q0, 0)
    update(q0 + tk, 1)
    o_t = jnp.concatenate(
        [acc_ref[hh, 0:B_DH] * (1.0 / acc_ref[hh, B_DH:B_DH + 1]) for hh in range(2)], axis=0)
    o_ref[...] = o_t.T


def _fox_prompt(q, kb, vt, f_row, bsz, seq):
    tq = _tile(seq, 512)
    tk = tq // 2
    nq = seq // tq
    kern = functools.partial(_fox_kernel, tq=tq)
    return pl.pallas_call(
        kern,
        grid=(bsz, B_PAIRS, nq),
        in_specs=[
            pl.BlockSpec((tq, LANES), lambda b, p, qi: (b * nq + qi, p)),
            pl.BlockSpec((seq, LANES), lambda b, p, qi: (b, p)),
            pl.BlockSpec((LANES, seq), lambda b, p, qi: (p, b)),
            pl.BlockSpec((None, None, 2, seq), lambda b, p, qi: (b, p, 0, 0)),
        ],
        out_specs=pl.BlockSpec((tq, LANES), lambda b, p, qi: (b * nq + qi, p)),
        out_shape=jax.ShapeDtypeStruct((bsz * seq, B_W), F32),
        scratch_shapes=[
            pltpu.VMEM((2, LANES, tq), BF16),
            pltpu.VMEM((2, 1, tq), F32),
            pltpu.VMEM((2, 2, 1, tq), F32),
            pltpu.VMEM((2, ACC_ROWS, tq), F32),
        ] + [pltpu.VMEM((tk, tq), F32)] * 4,
        compiler_params=_params("parallel", "parallel", "arbitrary"),
        name="fox_prompt",
    )(q, kb, vt, f_row)


def _fox_dec_kernel(q_ref, kc_ref, vc_ref, kn_ref, vn_ref, fn_ref, fc_ref, o_ref,
                    qs_ref, fqc_ref, m_ref, l_ref, acc_ref, *, sq, nkb):
    kj = pl.program_id(1)
    rows = 2 * sq
    lane = lax.broadcasted_iota(jnp.int32, (sq, LANES), 1)
    row_s = lax.broadcasted_iota(jnp.int32, (rows, 1), 0)

    @pl.when(kj == 0)
    def _():
        r = lax.broadcasted_iota(jnp.int32, (rows, LANES), 0)
        c = lax.broadcasted_iota(jnp.int32, (rows, LANES), 1)
        pick = c == (r % sq)
        for p in range(B_PAIRS):
            q = q_ref[:, p * LANES:(p + 1) * LANES]
            zero = jnp.zeros_like(q)
            qs_ref[p, 0:sq, :] = jnp.where(lane < B_DH, q, zero)
            qs_ref[p, sq:rows, :] = jnp.where(lane >= B_DH, q, zero)
            f2 = fn_ref[p]
            fm = jnp.where(r < sq, f2[0:1], f2[1:2])
            fqc_ref[p] = jnp.sum(jnp.where(pick, fm, 0.0), axis=1, keepdims=True)
        m_ref[...] = jnp.full_like(m_ref, NEG)
        l_ref[...] = jnp.zeros_like(l_ref)
        acc_ref[...] = jnp.zeros_like(acc_ref)

    def step(p, kb, vb, fk2, mask):
        s = lax.dot_general(qs_ref[p], kb, NT, preferred_element_type=F32)
        fk = jnp.where(row_s < sq, fk2[0:1], fk2[1:2])
        t = s + (fqc_ref[p] - fk) * LOG2E
        if mask is not None:
            t = jnp.where(mask, t, NEG)
        m_prev = m_ref[p]
        m_new = jnp.maximum(m_prev, jnp.max(t, axis=1, keepdims=True))
        alpha = jnp.exp2(m_prev - m_new)
        pr = jnp.exp2(t - m_new)
        l_ref[p] = alpha * l_ref[p] + jnp.sum(pr, axis=1, keepdims=True)
        acc_ref[p] = alpha * acc_ref[p] + jnp.dot(pr.astype(BF16), vb, preferred_element_type=F32)
        m_ref[p] = m_new

    @pl.when(kj < nkb)
    def _():
        for p in range(B_PAIRS):
            sl = slice(p * LANES, (p + 1) * LANES)
            step(p, kc_ref[:, sl], vc_ref[:, sl], fc_ref[p], None)

    @pl.when(kj == nkb)
    def _():
        i = lax.broadcasted_iota(jnp.int32, (rows, LANES), 0) % sq
        j = lax.broadcasted_iota(jnp.int32, (rows, LANES), 1)
        mask = (j <= i) & (j < sq)
        for p in range(B_PAIRS):
            sl = slice(p * LANES, (p + 1) * LANES)
            step(p, kn_ref[:, sl], vn_ref[:, sl], fn_ref[p], mask)
            o = acc_ref[p] * (1.0 / l_ref[p])
            o_ref[:, sl] = jnp.where(lane < B_DH, o[0:sq], o[sq:rows])


def _fox_decode(q, cache_k, cache_v, kn, vn, f_new, f_cache, bsz, sq):
    past = cache_k.shape[1]
    tk = _tile(past, 1024)
    nkb = past // tk
    rows = 2 * sq
    cmap = lambda b, kj: (b, jnp.minimum(kj, nkb - 1), 0)
    kern = functools.partial(_fox_dec_kernel, sq=sq, nkb=nkb)
    return pl.pallas_call(
        kern,
        grid=(bsz, nkb + 1),
        in_specs=[
            pl.BlockSpec((None, sq, B_W), lambda b, kj: (b, 0, 0)),
            pl.BlockSpec((None, tk, B_W), cmap),
            pl.BlockSpec((None, tk, B_W), cmap),
            pl.BlockSpec((None, LANES, B_W), lambda b, kj: (b, 0, 0)),
            pl.BlockSpec((None, LANES, B_W), lambda b, kj: (b, 0, 0)),
            pl.BlockSpec((None, B_PAIRS, 2, LANES), lambda b, kj: (b, 0, 0, 0)),
            pl.BlockSpec((None, B_PAIRS, 2, tk), lambda b, kj: (b, 0, 0, jnp.minimum(kj, nkb - 1))),
        ],
        out_specs=pl.BlockSpec((None, sq, B_W), lambda b, kj: (b, 0, 0)),
        out_shape=jax.ShapeDtypeStruct((bsz, sq, B_W), F32),
        scratch_shapes=[
            pltpu.VMEM((B_PAIRS, rows, LANES), BF16),
            pltpu.VMEM((B_PAIRS, rows, 1), F32),
            pltpu.VMEM((B_PAIRS, rows, 1), F32),
            pltpu.VMEM((B_PAIRS, rows, 1), F32),
            pltpu.VMEM((B_PAIRS, rows, LANES), F32),
        ],
        compiler_params=_params("parallel", "arbitrary"),
        name="fox_decode",
    )(q, cache_k, cache_v, kn, vn, f_new, f_cache)


def _b_post_kernel(o_ref, z_ref, h_ref, p_ref, wo_ref, pw_ref, pn_ref, pg_ref, out_ref):
    z = z_ref[...]
    y = o_ref[...] * (z * _sigmoid(z))
    h2 = h_ref[...] + jnp.dot(y.astype(BF16), wo_ref[...], preferred_element_type=F32)
    out_ref[...] = _ple(h2, p_ref, pw_ref, pn_ref, pg_ref)


def _b_post(o, z, h, p, w_out, ple_w, ple_norm, ple_wg):
    t = h.shape[0]
    tm = _tile(t, 512)
    row = lambda w: pl.BlockSpec((tm, w), lambda i: (i, 0))
    return pl.pallas_call(
        _b_post_kernel,
        grid=(t // tm,),
        in_specs=[row(B_W), row(B_W), row(D_MODEL), row(PLE_DIM),
                  _const_spec((B_W, D_MODEL)), _const_spec((PLE_DIM, D_MODEL)),
                  _const_spec((1, D_MODEL)), _const_spec((D_MODEL, D_MODEL))],
        out_specs=row(D_MODEL),
        out_shape=jax.ShapeDtypeStruct((t, D_MODEL), F32),
        compiler_params=_params("parallel"),
        name="b_post",
    )(o, z, h, p, w_out, ple_w, ple_norm, ple_wg)


def _prep_weights(a_norm, a_w_in, a_b_i, a_b_f, a_hnorm, a_w_out, kv_norm, kv_w, kv_b_f, k_norm,
                  b_norm, b_w_in, q_norm, b_w_out, ple_w, ple_norm, ple_w_g):
    head_of_lane = jnp.arange(B_W, dtype=jnp.int32) // B_DH
    return dict(
        a_norm=a_norm[0].reshape(1, D_MODEL),
        a_w_main=a_w_in[0][:, :A_MAIN].astype(BF16),
        a_w_gt=_split(a_w_in[0][:, A_MAIN:].T),
        a_bias=jnp.concatenate([a_b_i[0], a_b_f[0]]).reshape(2 * A_HEADS, 1),
        a_hnorm=a_hnorm[0].reshape(1, A_V),
        a_w_out=a_w_out[0].astype(BF16),
        kv_norm=kv_norm.reshape(1, D_MODEL),
        kv_w=kv_w[:, :2 * B_W].astype(BF16),
        kv_wf=_split(kv_w[:, 2 * B_W:]),
        kv_wft=_split(kv_w[:, 2 * B_W:].T),
        kv_bf=kv_b_f.reshape(1, B_HEADS),
        kv_bft=kv_b_f.reshape(B_HEADS, 1),
        k_norm=jnp.tile(k_norm, B_HEADS).reshape(1, B_W),
        gmat=(head_of_lane[:, None] == head_of_lane[None, :]).astype(BF16),
        b_norm=b_norm[0].reshape(1, D_MODEL),
        b_w_in=b_w_in[0].astype(BF16),
        q_norm=jnp.tile(q_norm[0], B_HEADS).reshape(1, B_W),
        b_w_out=b_w_out[0].astype(BF16),
        ple_w=ple_w.astype(BF16),
        ple_norm=ple_norm.reshape(-1, 1, D_MODEL),
        ple_wg=ple_w_g.astype(BF16),
    )


def _trunk(x, p, c0, n0, m0, past, w):
    bsz, seq, _ = x.shape
    t = bsz * seq
    xf = x.reshape(t, D_MODEL)
    p0 = p[0].reshape(t, PLE_DIM)
    p1 = p[1].reshape(t, PLE_DIM)

    u, gt = _a_inproj(xf, w["a_norm"], w["a_w_main"], w["a_w_gt"])
    n0r = n0.reshape(bsz, A_HEADS, 1, A_DK)
    m0r = jnp.broadcast_to(m0.reshape(bsz, A_HEADS, 1, 1), (bsz, A_HEADS, 1, LANES))
    hs, c_new, n_new, m_new = _mlstm(u, gt, w["a_bias"], c0, n0r, m0r, bsz, seq)
    h1 = _a_post(hs, u, xf, p0, w["a_hnorm"], w["a_w_out"], w["ple_w"][0], w["ple_norm"][0], w["ple_wg"][0])

    k, v, kb, vb, vt, lf, lft = _kv_proj(h1, w["kv_norm"], w["kv_w"], w["kv_wf"], w["kv_wft"],
                                     w["kv_bf"], w["kv_bft"], w["k_norm"], w["gmat"])
    q, z = _q_proj(h1, w["b_norm"], w["b_w_in"], w["q_norm"], w["gmat"])
    if past is None:
        f_row = _cumsum_rows(lft, bsz, seq).reshape(bsz, B_PAIRS, 2, seq)
        o = _fox_prompt(q, kb, vt, f_row, bsz, seq)
    else:
        cache_k, cache_v, cache_lf = past
        plen = cache_k.shape[1]
        assert seq <= LANES
        width = -(-(plen + seq) // LANES) * LANES
        lf_rows = jnp.concatenate(
            [jnp.swapaxes(cache_lf.astype(F32), 1, 2),
             jnp.swapaxes(lft.reshape(B_HEADS, bsz, seq), 0, 1),
             jnp.zeros((bsz, B_HEADS, width - plen - seq), F32)], axis=2)
        f_all = _cumsum_rows(lf_rows, bsz, width).reshape(bsz, B_PAIRS, 2, width)
        f_cache = f_all[..., :plen]
        f_new = jnp.pad(f_all[..., plen:plen + seq], ((0, 0), (0, 0), (0, 0), (0, LANES - seq)))
        pad_new = lambda a: jnp.pad(a.reshape(bsz, seq, B_W), ((0, 0), (0, LANES - seq), (0, 0)))
        o = _fox_decode(q.reshape(bsz, seq, B_W), cache_k.astype(BF16).reshape(bsz, plen, B_W),
                        cache_v.astype(BF16).reshape(bsz, plen, B_W), pad_new(kb), pad_new(vb),
                        f_new, f_cache, bsz, seq).reshape(t, B_W)
    y = _b_post(o, z, h1, p1, w["b_w_out"], w["ple_w"][1], w["ple_norm"][1], w["ple_wg"][1])

    return (y.reshape(bsz, seq, D_MODEL),
            c_new[None], n_new.reshape(1, bsz, A_HEADS, A_DK), m_new[None, :, :, 0, 0],
            k.reshape(bsz, seq, B_HEADS, B_DH), v.reshape(bsz, seq, B_HEADS, B_DH),
            lf.reshape(bsz, seq, B_HEADS))


def kernel(x_prompt, x_sample, cache_k, cache_v, cache_logf, state_C, state_n, state_m, p_prompt, p_sample,
           a_norm, a_w_in, a_b_i, a_b_f, a_hnorm, a_w_out, kv_norm, kv_w, kv_b_f, k_norm,
           b_norm, b_w_in, q_norm, b_w_out, ple_w, ple_norm, ple_w_g):
    w = _prep_weights(a_norm, a_w_in, a_b_i, a_b_f, a_hnorm, a_w_out, kv_norm, kv_w, kv_b_f, k_norm,
                      b_norm, b_w_in, q_norm, b_w_out, ple_w, ple_norm, ple_w_g)
    bsz = x_prompt.shape[0]
    c0 = jnp.zeros((bsz, A_HEADS, A_DK, A_DV), F32)
    n0 = jnp.zeros((bsz, A_HEADS, A_DK), F32)
    m0 = jnp.zeros((bsz, A_HEADS), F32)
    prompt = _trunk(x_prompt, p_prompt, c0, n0, m0, None, w)
    sample = _trunk(x_sample, p_sample, state_C[0].astype(F32), state_n[0].astype(F32),
                    state_m[0].astype(F32), (cache_k, cache_v, cache_logf), w)
    return (prompt[0], sample[0]) + prompt[1:] + sample[1:]
```

```python
import functools

import jax
import jax.numpy as jnp
from jax import lax
from jax.experimental import pallas as pl
from jax.experimental.pallas import tpu as pltpu

D_MODEL = 1024
A_HEADS = 8
A_DK = 128
A_DV = 256
A_QK = A_HEADS * A_DK
A_V = A_HEADS * A_DV
A_MAIN = 2 * A_QK + 3 * A_V
B_HEADS = 16
B_DH = 64
B_W = B_HEADS * B_DH
B_PAIRS = B_HEADS // 2
PLE_DIM = 256
EPS = 1e-6
NEG = -1e30
LOG2E = 1.4426950408889634
ACC_ROWS = B_DH + 16

LANES = 128
VMEM_LIMIT = 56 * 1024 * 1024

F32 = jnp.float32
BF16 = jnp.bfloat16
NN = (((1,), (0,)), ((), ()))
NT = (((1,), (1,)), ((), ()))
TN = (((0,), (0,)), ((), ()))


def _params(*sem):
    return pltpu.CompilerParams(dimension_semantics=sem, vmem_limit_bytes=VMEM_LIMIT)


def _tile(n, pref):
    t = min(n, pref)
    while n % t:
        t //= 2
    return t


def _rms(x, g):
    ms = jnp.mean(x * x, axis=-1, keepdims=True)
    return x * lax.rsqrt(ms + EPS) * g


def _log_sigmoid(x):
    return jnp.minimum(x, 0.0) - jnp.log1p(jnp.exp(-jnp.abs(x)))


def _sigmoid(x):
    return 1.0 / (1.0 + jnp.exp(-x))


def _diag_to_col(row, eye):
    n = eye.shape[0]
    return jnp.sum(jnp.where(eye, jnp.broadcast_to(row, (n, n)), 0.0), axis=1, keepdims=True)


def _seg_cumsum_lanes(x, seg):
    w = x.shape[-1]
    pos = lax.broadcasted_iota(jnp.int32, x.shape, x.ndim - 1) % seg
    k = 1
    while k < min(seg, w):
        x = x + jnp.where(pos >= k, pltpu.roll(x, k, x.ndim - 1), 0.0)
        k *= 2
    return x


def _split(x):
    hi = x.astype(BF16)
    return hi, (x - hi.astype(F32)).astype(BF16)


def _dot_split(a, b, dims):
    dot = functools.partial(lax.dot_general, dimension_numbers=dims, preferred_element_type=F32)
    return dot(a[0], b[0]) + (dot(a[0], b[1]) + dot(a[1], b[0]))


def _a_inproj_kernel(x_ref, g_ref, w_ref, wgh_ref, wgl_ref, u_ref, gt_ref, xn_ref):
    @pl.when(pl.program_id(1) == 0)
    def _():
        xn = _rms(x_ref[...], g_ref[...])
        xs = _split(xn)
        xn_ref[...] = xs[0]
        gt_ref[...] = _dot_split((wgh_ref[...], wgl_ref[...]), xs, NT)

    u_ref[...] = jnp.dot(xn_ref[...], w_ref[...], preferred_element_type=F32).astype(BF16)


def _a_inproj(x, g, w_main, w_gt):
    t = x.shape[0]
    tm = _tile(t, 1024)
    tn = 1024
    gate_spec = pl.BlockSpec((2 * A_HEADS, D_MODEL), lambda i, j: (0, 0))
    return pl.pallas_call(
        _a_inproj_kernel,
        grid=(t // tm, A_MAIN // tn),
        in_specs=[
            pl.BlockSpec((tm, D_MODEL), lambda i, j: (i, 0)),
            pl.BlockSpec((1, D_MODEL), lambda i, j: (0, 0)),
            pl.BlockSpec((D_MODEL, tn), lambda i, j: (0, j)),
            gate_spec,
            gate_spec,
        ],
        out_specs=[
            pl.BlockSpec((tm, tn), lambda i, j: (i, j)),
            pl.BlockSpec((2 * A_HEADS, tm), lambda i, j: (0, i)),
        ],
        out_shape=[
            jax.ShapeDtypeStruct((t, A_MAIN), BF16),
            jax.ShapeDtypeStruct((2 * A_HEADS, t), F32),
        ],
        scratch_shapes=[pltpu.VMEM((tm, D_MODEL), BF16)],
        compiler_params=_params("parallel", "arbitrary"),
        name="a_inproj",
    )(x, g, w_main, *w_gt)


def _mlstm_kernel(q_ref, k_ref, v_ref, g_ref, bias_ref, c0_ref, n0_ref, m0_ref,
                  h_ref, c_ref, n_ref, m_ref, *, n_seq, seq_len, n_chunks):
    R = LANES

    @pl.when(pl.program_id(1) == 0)
    def _():
        c_ref[...] = c0_ref[...]
        n_ref[...] = n0_ref[...]
        m_ref[...] = m0_ref[...]

    row = lax.broadcasted_iota(jnp.int32, (R, R), 0)
    col = lax.broadcasted_iota(jnp.int32, (R, R), 1)
    eye = row == col
    valid = (col <= row) & ((row // seq_len) == (col // seq_len))
    row1 = lax.broadcasted_iota(jnp.int32, (R, 1), 0)
    lane1 = lax.broadcasted_iota(jnp.int32, (1, R), 1)
    bias = bias_ref[...]
    scale = A_DK ** -0.5

    def chunk(r0):
        g = g_ref[:, pl.ds(r0, R)]
        ig_all = g[0:A_HEADS] + bias[0:A_HEADS]
        lf_all = _log_sigmoid(g[A_HEADS:] + bias[A_HEADS:])
        b_all = _seg_cumsum_lanes(lf_all, seq_len)
        for h in range(A_HEADS):
            qb = q_ref[pl.ds(r0, R), h * A_DK:(h + 1) * A_DK]
            kb = k_ref[pl.ds(r0, R), h * A_DK:(h + 1) * A_DK]
            vb = v_ref[pl.ds(r0, R), h * A_DV:(h + 1) * A_DV]
            q, k = qb.astype(F32), kb.astype(F32)
            b_row = b_all[h:h + 1]
            ig_row = ig_all[h:h + 1]
            b_col = _diag_to_col(b_row, eye)
            m_prev = [m_ref[s, h][:, 0:1] for s in range(n_seq)]
            n_prev = [n_ref[s, h] for s in range(n_seq)]
            m_prev_col = m_prev[0]
            n_rows = n_prev[0]
            for s in range(1, n_seq):
                m_prev_col = jnp.where(row1 >= s * seq_len, m_prev[s], m_prev_col)
                n_rows = jnp.where(row1 >= s * seq_len, n_prev[s], n_rows)
            d = jnp.where(valid, b_col - b_row + ig_row, NEG)
            inter = b_col + m_prev_col
            m_t = jnp.maximum(inter, jnp.max(d, axis=1, keepdims=True))
            p = jnp.exp(d - m_t)
            w_inter = jnp.exp(inter - m_t)
            qk = lax.dot_general(qb, kb, NT, preferred_element_type=F32) * scale
            s_mat = qk * p
            qc = [jnp.dot(qb[s * seq_len:(s + 1) * seq_len], c_ref[s, h].astype(BF16),
                          preferred_element_type=F32) for s in range(n_seq)]
            qc = qc[0] if n_seq == 1 else jnp.concatenate(qc, axis=0)
            num = w_inter * qc + jnp.dot(s_mat.astype(BF16), vb, preferred_element_type=F32)
            den = (w_inter * jnp.sum(q * n_rows, axis=1, keepdims=True)
                   + jnp.sum(s_mat, axis=1, keepdims=True))
            hval = num * (1.0 / jnp.maximum(jnp.abs(den), jnp.exp(-m_t)))
            h_ref[pl.ds(r0, R), h * A_DV:(h + 1) * A_DV] = hval
            b_last = [b_row[:, (s + 1) * seq_len - 1:(s + 1) * seq_len] for s in range(n_seq)]
            m_new = [m_t[(s + 1) * seq_len - 1:(s + 1) * seq_len] for s in range(n_seq)]
            b_last_row, m_new_row = b_last[0], m_new[0]
            for s in range(1, n_seq):
                b_last_row = jnp.where(lane1 >= s * seq_len, b_last[s], b_last_row)
                m_new_row = jnp.where(lane1 >= s * seq_len, m_new[s], m_new_row)
            w_s_row = jnp.exp(b_last_row - b_row + ig_row - m_new_row)
            ks = k * (_diag_to_col(w_s_row, eye) * scale)
            ksb = ks.astype(BF16)
            for s in range(n_seq):
                sl = slice(s * seq_len, (s + 1) * seq_len)
                w_old = jnp.exp(b_last[s] + m_prev[s] - m_new[s])
                c_ref[s, h] = w_old * c_ref[s, h] + lax.dot_general(
                    ksb[sl], vb[sl], TN, preferred_element_type=F32)
                n_ref[s, h] = w_old * n_prev[s] + jnp.sum(ks[sl], axis=0, keepdims=True)
                m_ref[s, h] = jnp.broadcast_to(m_new[s], (1, LANES))

    if n_chunks == 1:
        chunk(0)
    else:
        def body(ci, carry):
            chunk(pl.multiple_of(ci * R, R))
            return carry
        lax.fori_loop(0, n_chunks, body, 0)


def _mlstm(u, gt, bias, c0, n0, m0, bsz, seq):
    if seq % LANES == 0:
        seq_len, n_seq = LANES, 1
        n_chunks = _tile(seq // LANES, 4)
    else:
        assert LANES % seq == 0 and bsz % (LANES // seq) == 0, (bsz, seq)
        seq_len, n_seq, n_chunks = seq, LANES // seq, 1
    tok = n_chunks * LANES
    steps = seq // (n_chunks * seq_len)
    t = bsz * seq
    tok_map = lambda b, c: (b * steps + c, 0)
    state4 = lambda shape: pl.BlockSpec((n_seq,) + shape, lambda b, c: (b, 0, 0, 0))
    kern = functools.partial(_mlstm_kernel, n_seq=n_seq, seq_len=seq_len, n_chunks=n_chunks)
    return pl.pallas_call(
        kern,
        grid=(bsz // n_seq, steps),
        in_specs=[
            pl.BlockSpec((tok, A_QK), tok_map),
            pl.BlockSpec((tok, A_QK), lambda b, c: (b * steps + c, 1)),
            pl.BlockSpec((tok, A_V), lambda b, c: (b * steps + c, 1)),
            pl.BlockSpec((2 * A_HEADS, tok), lambda b, c: (0, b * steps + c)),
            pl.BlockSpec((2 * A_HEADS, 1), lambda b, c: (0, 0)),
            state4((A_HEADS, A_DK, A_DV)),
            state4((A_HEADS, 1, A_DK)),
            state4((A_HEADS, 1, LANES)),
        ],
        out_specs=[
            pl.BlockSpec((tok, A_V), tok_map),
            state4((A_HEADS, A_DK, A_DV)),
            state4((A_HEADS, 1, A_DK)),
            state4((A_HEADS, 1, LANES)),
        ],
        out_shape=[
            jax.ShapeDtypeStruct((t, A_V), F32),
            jax.ShapeDtypeStruct((bsz, A_HEADS, A_DK, A_DV), F32),
            jax.ShapeDtypeStruct((bsz, A_HEADS, 1, A_DK), F32),
            jax.ShapeDtypeStruct((bsz, A_HEADS, 1, LANES), F32),
        ],
        compiler_params=_params("parallel", "arbitrary"),
        name="mlstm",
    )(u, u, u, gt, bias, c0, n0, m0)


def _ple(h, p_ref, pw_ref, pn_ref, pg_ref):
    pp = jnp.dot(p_ref[...].astype(BF16), pw_ref[...], preferred_element_type=F32)
    hn = _rms(h, pn_ref[...])
    gate = _sigmoid(jnp.dot(hn.astype(BF16), pg_ref[...], preferred_element_type=F32))
    return h + pp * gate


def _a_post_kernel(hs_ref, o_ref, z_ref, x_ref, p_ref, hg_ref, wo_ref, pw_ref, pn_ref, pg_ref,
                   out_ref, y_ref):
    for h in range(A_HEADS):
        sl = slice(h * A_DV, (h + 1) * A_DV)
        y = _rms(hs_ref[:, sl], hg_ref[:, sl]) * _sigmoid(o_ref[:, sl].astype(F32))
        z = z_ref[:, sl].astype(F32)
        y_ref[:, sl] = (y * (z * _sigmoid(z))).astype(BF16)
    h1 = x_ref[...] + jnp.dot(y_ref[...], wo_ref[...], preferred_element_type=F32)
    out_ref[...] = _ple(h1, p_ref, pw_ref, pn_ref, pg_ref)


def _const_spec(shape):
    return pl.BlockSpec(shape, lambda i: (0,) * len(shape))


def _a_post(hs, u, x, p, hnorm, w_out, ple_w, ple_norm, ple_wg):
    t = x.shape[0]
    tm = _tile(t, 256)
    return pl.pallas_call(
        _a_post_kernel,
        grid=(t // tm,),
        in_specs=[
            pl.BlockSpec((tm, A_V), lambda i: (i, 0)),
            pl.BlockSpec((tm, A_V), lambda i: (i, 2)),
            pl.BlockSpec((tm, A_V), lambda i: (i, 3)),
            pl.BlockSpec((tm, D_MODEL), lambda i: (i, 0)),
            pl.BlockSpec((tm, PLE_DIM), lambda i: (i, 0)),
            _const_spec((1, A_V)),
            _const_spec((A_V, D_MODEL)),
            _const_spec((PLE_DIM, D_MODEL)),
            _const_spec((1, D_MODEL)),
            _const_spec((D_MODEL, D_MODEL)),
        ],
        out_specs=pl.BlockSpec((tm, D_MODEL), lambda i: (i, 0)),
        out_shape=jax.ShapeDtypeStruct((t, D_MODEL), F32),
        scratch_shapes=[pltpu.VMEM((tm, A_V), BF16)],
        compiler_params=_params("parallel"),
        name="a_post",
    )(hs, u, u, x, p, hnorm, w_out, ple_w, ple_norm, ple_wg)


def _head_ms(x, gm_ref):
    return jnp.dot((x * x).astype(BF16), gm_ref[...], preferred_element_type=F32) * (1.0 / B_DH)


def _kv_kernel(h_ref, g_ref, w_ref, wfh_ref, wfl_ref, wfth_ref, wftl_ref, bf_ref, bft_ref, kg_ref, gm_ref,
               k_ref, v_ref, kb_ref, vb_ref, vt_ref, lf_ref, lft_ref):
    xs = _split(_rms(h_ref[...], g_ref[...]))
    kv = jnp.dot(xs[0], w_ref[...], preferred_element_type=F32)
    k = kv[:, :B_W]
    v = kv[:, B_W:]
    kn = k * lax.rsqrt(_head_ms(k, gm_ref) + EPS) * kg_ref[...]
    k_ref[...] = kn
    v_ref[...] = v
    kb_ref[...] = kn.astype(BF16)
    vb_ref[...] = v.astype(BF16)
    vt_ref[...] = v.T.astype(BF16)
    fg = _dot_split(xs, (wfh_ref[...], wfl_ref[...]), NN)
    lf_ref[...] = _log_sigmoid(fg + bf_ref[...])
    fgt = _dot_split((wfth_ref[...], wftl_ref[...]), xs, NT)
    lft_ref[...] = _log_sigmoid(fgt + bft_ref[...])


def _kv_proj(h, g, w_kv, w_f, w_ft, b_f, b_ft, k_g, gmat):
    t = h.shape[0]
    tm = _tile(t, 512)
    row = lambda w: pl.BlockSpec((tm, w), lambda i: (i, 0))
    return pl.pallas_call(
        _kv_kernel,
        grid=(t // tm,),
        in_specs=[
            row(D_MODEL),
            _const_spec((1, D_MODEL)),
            _const_spec((D_MODEL, 2 * B_W)),
            _const_spec((D_MODEL, B_HEADS)),
            _const_spec((D_MODEL, B_HEADS)),
            _const_spec((B_HEADS, D_MODEL)),
            _const_spec((B_HEADS, D_MODEL)),
            _const_spec((1, B_HEADS)),
            _const_spec((B_HEADS, 1)),
            _const_spec((1, B_W)),
            _const_spec((B_W, B_W)),
        ],
        out_specs=[row(B_W), row(B_W), row(B_W), row(B_W),
                   pl.BlockSpec((B_W, tm), lambda i: (0, i)), row(B_HEADS),
                   pl.BlockSpec((B_HEADS, tm), lambda i: (0, i))],
        out_shape=[
            jax.ShapeDtypeStruct((t, B_W), F32),
            jax.ShapeDtypeStruct((t, B_W), F32),
            jax.ShapeDtypeStruct((t, B_W), BF16),
            jax.ShapeDtypeStruct((t, B_W), BF16),
            jax.ShapeDtypeStruct((B_W, t), BF16),
            jax.ShapeDtypeStruct((t, B_HEADS), F32),
            jax.ShapeDtypeStruct((B_HEADS, t), F32),
        ],
        compiler_params=_params("parallel"),
        name="kv_proj",
    )(h, g, w_kv, *w_f, *w_ft, b_f, b_ft, k_g, gmat)


def _q_kernel(h_ref, g_ref, w_ref, qg_ref, gm_ref, q_ref, z_ref):
    xn = _rms(h_ref[...], g_ref[...])
    qz = jnp.dot(xn.astype(BF16), w_ref[...], preferred_element_type=F32)
    q = qz[:, :B_W]
    z_ref[...] = qz[:, B_W:]
    qn = q * lax.rsqrt(_head_ms(q, gm_ref) + EPS) * qg_ref[...]
    q_ref[...] = (qn * (B_DH ** -0.5 * LOG2E)).astype(BF16)


def _q_proj(h, g, w_in, q_g, gmat):
    t = h.shape[0]
    tm = _tile(t, 512)
    row = lambda w: pl.BlockSpec((tm, w), lambda i: (i, 0))
    return pl.pallas_call(
        _q_kernel,
        grid=(t // tm,),
        in_specs=[row(D_MODEL), _const_spec((1, D_MODEL)), _const_spec((D_MODEL, 2 * B_W)),
                  _const_spec((1, B_W)), _const_spec((B_W, B_W))],
        out_specs=[row(B_W), row(B_W)],
        out_shape=[jax.ShapeDtypeStruct((t, B_W), BF16), jax.ShapeDtypeStruct((t, B_W), F32)],
        compiler_params=_params("parallel"),
        name="q_proj",
    )(h, g, w_in, q_g, gmat)


def _cumsum_kernel(x_ref, f_ref, carry_ref):
    @pl.when(pl.program_id(1) == 0)
    def _():
        carry_ref[...] = jnp.zeros_like(carry_ref)

    w = x_ref.shape[-1]
    f = _seg_cumsum_lanes(x_ref[...], w) + carry_ref[:, 0:1]
    f_ref[...] = f
    carry_ref[...] = jnp.broadcast_to(f[:, w - 1:w], carry_ref.shape)


def _cumsum_rows(x, bsz, width):
    tb = _tile(width, 2048) if width % LANES == 0 and (width // LANES) & (width // LANES - 1) == 0 else width
    nb = width // tb
    if x.ndim == 2:
        in_spec = pl.BlockSpec((B_HEADS, tb), lambda b, j: (0, b * nb + j))
    else:
        in_spec = pl.BlockSpec((None, B_HEADS, tb), lambda b, j: (b, 0, j))
    return pl.pallas_call(
        _cumsum_kernel,
        grid=(bsz, nb),
        in_specs=[in_spec],
        out_specs=pl.BlockSpec((None, B_HEADS, tb), lambda b, j: (b, 0, j)),
        out_shape=jax.ShapeDtypeStruct((bsz, B_HEADS, width), F32),
        scratch_shapes=[pltpu.VMEM((B_HEADS, LANES), F32)],
        compiler_params=_params("parallel", "arbitrary"),
        name="logf_cumsum",
    )(x)


def _fox_kernel(q_ref, k_ref, vt_ref, f_ref, o_ref, qt_ref, m_ref, cm_ref, acc_ref,
                s00, s01, s10, s11, *, tq):
    tk = tq // 2
    sbuf = ((s00, s01), (s10, s11))
    qi = pl.program_id(2)
    q0 = pl.multiple_of(qi * tq, tq)
    feat = lax.broadcasted_iota(jnp.int32, (LANES, tq), 0)
    qt = q_ref[...].astype(F32).T
    qt_ref[0] = jnp.where(feat < B_DH, qt, 0.0).astype(BF16)
    qt_ref[1] = jnp.where(feat >= B_DH, qt, 0.0).astype(BF16)
    m_ref[...] = jnp.full_like(m_ref, NEG)
    acc_ref[...] = jnp.zeros_like(acc_ref)
    f_base = f_ref[:, pl.ds(q0, tq)][:, tq - 1:tq]
    ones = jnp.ones((ACC_ROWS - B_DH, tk), BF16)
    kpos = lax.broadcasted_iota(jnp.int32, (tk, tq), 0)
    qpos = lax.broadcasted_iota(jnp.int32, (tk, tq), 1)
    diag_masks = (kpos <= qpos, (kpos + tk) <= qpos)

    def scores(k0, slot, causal=None):
        kb = k_ref[pl.ds(k0, tk), :]
        fk = (f_ref[:, pl.ds(k0, tk)] - f_base) * LOG2E
        for hh in range(2):
            fk_col = jnp.concatenate(
                [jnp.broadcast_to(fk[hh:hh + 1, c * LANES:(c + 1) * LANES], (LANES, LANES)).T
                 for c in range(tk // LANES)], axis=0)
            t = (jnp.dot(kb, qt_ref[hh], preferred_element_type=F32)
                 - pltpu.repeat(fk_col, tq // LANES, axis=1))
            if causal is not None:
                t = jnp.where(causal, t, NEG)
            sbuf[slot][hh][...] = t
            cm_ref[slot, hh] = jnp.max(t, axis=0, keepdims=True)

    def update(k0, slot):
        for hh in range(2):
            m_prev = m_ref[hh]
            m_new = jnp.maximum(m_prev, cm_ref[slot, hh])
            alpha = jnp.exp2(m_prev - m_new)
            p = jnp.exp2(sbuf[slot][hh][...] - m_new).astype(BF16)
            vt = jnp.concatenate([vt_ref[hh * B_DH:(hh + 1) * B_DH, pl.ds(k0, tk)], ones], axis=0)
            acc_ref[hh] = alpha * acc_ref[hh] + jnp.dot(vt, p, preferred_element_type=F32)
            m_ref[hh] = m_new

    def pair(k0, next_causal):
        scores(k0 + tk, 1)
        update(k0, 0)
        scores(k0 + tq, 0, next_causal)
        update(k0 + tk, 1)

    @pl.when(qi == 0)
    def _():
        scores(q0, 0, diag_masks[0])

    @pl.when(qi > 0)
    def _():
        scores(0, 0)

        def body(jj, carry):
            pair(pl.multiple_of(jj * tq, tq), None)
            return carry

        lax.fori_loop(0, qi - 1, body, 0)
        pair(q0 - tq, diag_masks[0])

    scores(q0 + tk, 1, diag_masks[1])
    update(q0, 0)
    update(q0 + tk, 1)
    o_t = jnp.concatenate(
        [acc_ref[hh, 0:B_DH] * (1.0 / acc_ref[hh, B_DH:B_DH + 1]) for hh in range(2)], axis=0)
    o_ref[...] = o_t.T


def _fox_prompt(q, kb, vt, f_row, bsz, seq):
    tq = _tile(seq, 512)
    tk = tq // 2
    nq = seq // tq
    kern = functools.partial(_fox_kernel, tq=tq)
    return pl.pallas_call(
        kern,
        grid=(bsz, B_PAIRS, nq),
        in_specs=[
            pl.BlockSpec((tq, LANES), lambda b, p, qi: (b * nq + qi, p)),
            pl.BlockSpec((seq, LANES), lambda b, p, qi: (b, p)),
            pl.BlockSpec((LANES, seq), lambda b, p, qi: (p, b)),
            pl.BlockSpec((None, None, 2, seq), lambda b, p, qi: (b, p, 0, 0)),
        ],
        out_specs=pl.BlockSpec((tq, LANES), lambda b, p, qi: (b * nq + qi, p)),
        out_shape=jax.ShapeDtypeStruct((bsz * seq, B_W), F32),
        scratch_shapes=[
            pltpu.VMEM((2, LANES, tq), BF16),
            pltpu.VMEM((2, 1, tq), F32),
            pltpu.VMEM((2, 2, 1, tq), F32),
            pltpu.VMEM((2, ACC_ROWS, tq), F32),
        ] + [pltpu.VMEM((tk, tq), F32)] * 4,
        compiler_params=_params("parallel", "parallel", "arbitrary"),
        name="fox_prompt",
    )(q, kb, vt, f_row)


def _fox_dec_kernel(q_ref, kc_ref, vc_ref, kn_ref, vn_ref, fn_ref, fm_ref, o_ref,
                    qs_ref, fqc_ref, m_ref, l_ref, acc_ref, *, sq, nkb, tk):
    kj = pl.program_id(1)
    rows = 2 * sq
    half = B_HEADS // 2

    @pl.when(kj == 0)
    def _():
        r = lax.broadcasted_iota(jnp.int32, (rows, LANES), 0)
        c = lax.broadcasted_iota(jnp.int32, (rows, LANES), 1)
        pick = c == (r % sq)
        for i in range(half):
            qs_ref[i, 0:sq, :] = q_ref[:, i * B_DH:(i + 1) * B_DH]
            qs_ref[i, sq:rows, :] = q_ref[:, (i + half) * B_DH:(i + half + 1) * B_DH]
            fm = jnp.where(r < sq, fn_ref[i:i + 1, :], fn_ref[i + half:i + half + 1, :])
            fqc_ref[i] = jnp.sum(jnp.where(pick, fm, 0.0), axis=1, keepdims=True)
        m_ref[...] = jnp.full_like(m_ref, NEG)
        l_ref[...] = jnp.zeros_like(l_ref)
        acc_ref[...] = jnp.zeros_like(acc_ref)

    def step(i, kb, vb, fk_row, valid):
        s = lax.dot_general(qs_ref[i], kb, NT, preferred_element_type=F32)
        t = jnp.where(valid, s + (fqc_ref[i] - fk_row) * LOG2E, NEG)
        m_prev = m_ref[i]
        m_new = jnp.maximum(m_prev, jnp.max(t, axis=1, keepdims=True))
        alpha = jnp.exp2(m_prev - m_new)
        p = jnp.exp2(t - m_new)
        l_ref[i] = alpha * l_ref[i] + jnp.sum(p, axis=1, keepdims=True)
        acc_ref[i] = alpha * acc_ref[i] + jnp.dot(p.astype(BF16), vb, preferred_element_type=F32)
        m_ref[i] = m_new

    @pl.when(kj < nkb)
    def _():
        r = lax.broadcasted_iota(jnp.int32, (rows, 2 * tk), 0)
        c = lax.broadcasted_iota(jnp.int32, (rows, 2 * tk), 1)
        valid = ((c % 2) == 1) == (r >= sq)
        for i in range(half):
            mix = pl.ds(i, 2 * tk, stride=half)
            step(i, kc_ref[mix, :].astype(BF16), vc_ref[mix, :].astype(BF16), fm_ref[i:i + 1, :], valid)

    @pl.when(kj == nkb)
    def _():
        r = lax.broadcasted_iota(jnp.int32, (rows, 2 * LANES), 0)
        c = lax.broadcasted_iota(jnp.int32, (rows, 2 * LANES), 1)
        valid = ((c >= LANES) == (r >= sq)) & ((c % LANES) <= (r % sq))
        for i in range(half):
            lo = slice(i * B_DH, (i + 1) * B_DH)
            hi = slice((i + half) * B_DH, (i + half + 1) * B_DH)
            kb = jnp.concatenate([kn_ref[:, lo], kn_ref[:, hi]], axis=0)
            vb = jnp.concatenate([vn_ref[:, lo], vn_ref[:, hi]], axis=0)
            fk = jnp.concatenate([fn_ref[i:i + 1, :], fn_ref[i + half:i + half + 1, :]], axis=1)
            step(i, kb, vb, fk, valid)
            o = acc_ref[i] * (1.0 / l_ref[i])
            o_ref[:, lo] = o[0:sq]
            o_ref[:, hi] = o[sq:rows]


def _fox_decode(q, cache_k, cache_v, kn, vn, f_new, f_mix, bsz, sq):
    past = cache_k.shape[1] // B_HEADS
    tk = _tile(past, 512)
    nkb = past // tk
    rows = 2 * sq
    half = B_HEADS // 2
    cmap = lambda b, kj: (b, jnp.minimum(kj, nkb - 1), 0)
    kern = functools.partial(_fox_dec_kernel, sq=sq, nkb=nkb, tk=tk)
    return pl.pallas_call(
        kern,
        grid=(bsz, nkb + 1),
        in_specs=[
            pl.BlockSpec((None, sq, B_W), lambda b, kj: (b, 0, 0)),
            pl.BlockSpec((None, tk * B_HEADS, B_DH), cmap),
            pl.BlockSpec((None, tk * B_HEADS, B_DH), cmap),
            pl.BlockSpec((None, LANES, B_W), lambda b, kj: (b, 0, 0)),
            pl.BlockSpec((None, LANES, B_W), lambda b, kj: (b, 0, 0)),
            pl.BlockSpec((None, B_HEADS, LANES), lambda b, kj: (b, 0, 0)),
            pl.BlockSpec((None, half, 2 * tk), lambda b, kj: (b, 0, jnp.minimum(kj, nkb - 1))),
        ],
        out_specs=pl.BlockSpec((None, sq, B_W), lambda b, kj: (b, 0, 0)),
        out_shape=jax.ShapeDtypeStruct((bsz, sq, B_W), F32),
        scratch_shapes=[
            pltpu.VMEM((half, rows, B_DH), BF16),
            pltpu.VMEM((half, rows, 1), F32),
            pltpu.VMEM((half, rows, 1), F32),
            pltpu.VMEM((half, rows, 1), F32),
            pltpu.VMEM((half, rows, B_DH), F32),
        ],
        compiler_params=_params("parallel", "arbitrary"),
        name="fox_decode",
    )(q, cache_k, cache_v, kn, vn, f_new, f_mix)


def _b_post_kernel(o_ref, z_ref, h_ref, p_ref, wo_ref, pw_ref, pn_ref, pg_ref, out_ref):
    z = z_ref[...]
    y = o_ref[...] * (z * _sigmoid(z))
    h2 = h_ref[...] + jnp.dot(y.astype(BF16), wo_ref[...], preferred_element_type=F32)
    out_ref[...] = _ple(h2, p_ref, pw_ref, pn_ref, pg_ref)


def _b_post(o, z, h, p, w_out, ple_w, ple_norm, ple_wg):
    t = h.shape[0]
    tm = _tile(t, 512)
    row = lambda w: pl.BlockSpec((tm, w), lambda i: (i, 0))
    return pl.pallas_call(
        _b_post_kernel,
        grid=(t // tm,),
        in_specs=[row(B_W), row(B_W), row(D_MODEL), row(PLE_DIM),
                  _const_spec((B_W, D_MODEL)), _const_spec((PLE_DIM, D_MODEL)),
                  _const_spec((1, D_MODEL)), _const_spec((D_MODEL, D_MODEL))],
        out_specs=row(D_MODEL),
        out_shape=jax.ShapeDtypeStruct((t, D_MODEL), F32),
        compiler_params=_params("parallel"),
        name="b_post",
    )(o, z, h, p, w_out, ple_w, ple_norm, ple_wg)


def _prep_weights(a_norm, a_w_in, a_b_i, a_b_f, a_hnorm, a_w_out, kv_norm, kv_w, kv_b_f, k_norm,
                  b_norm, b_w_in, q_norm, b_w_out, ple_w, ple_norm, ple_w_g):
    head_of_lane = jnp.arange(B_W, dtype=jnp.int32) // B_DH
    return dict(
        a_norm=a_norm[0].reshape(1, D_MODEL),
        a_w_main=a_w_in[0][:, :A_MAIN].astype(BF16),
        a_w_gt=_split(a_w_in[0][:, A_MAIN:].T),
        a_bias=jnp.concatenate([a_b_i[0], a_b_f[0]]).reshape(2 * A_HEADS, 1),
        a_hnorm=a_hnorm[0].reshape(1, A_V),
        a_w_out=a_w_out[0].astype(BF16),
        kv_norm=kv_norm.reshape(1, D_MODEL),
        kv_w=kv_w[:, :2 * B_W].astype(BF16),
        kv_wf=_split(kv_w[:, 2 * B_W:]),
        kv_wft=_split(kv_w[:, 2 * B_W:].T),
        kv_bf=kv_b_f.reshape(1, B_HEADS),
        kv_bft=kv_b_f.reshape(B_HEADS, 1),
        k_norm=jnp.tile(k_norm, B_HEADS).reshape(1, B_W),
        gmat=(head_of_lane[:, None] == head_of_lane[None, :]).astype(BF16),
        b_norm=b_norm[0].reshape(1, D_MODEL),
        b_w_in=b_w_in[0].astype(BF16),
        q_norm=jnp.tile(q_norm[0], B_HEADS).reshape(1, B_W),
        b_w_out=b_w_out[0].astype(BF16),
        ple_w=ple_w.astype(BF16),
        ple_norm=ple_norm.reshape(-1, 1, D_MODEL),
        ple_wg=ple_w_g.astype(BF16),
    )


def _trunk(x, p, c0, n0, m0, past, w):
    bsz, seq, _ = x.shape
    t = bsz * seq
    xf = x.reshape(t, D_MODEL)
    p0 = p[0].reshape(t, PLE_DIM)
    p1 = p[1].reshape(t, PLE_DIM)

    u, gt = _a_inproj(xf, w["a_norm"], w["a_w_main"], w["a_w_gt"])
    n0r = n0.reshape(bsz, A_HEADS, 1, A_DK)
    m0r = jnp.broadcast_to(m0.reshape(bsz, A_HEADS, 1, 1), (bsz, A_HEADS, 1, LANES))
    hs, c_new, n_new, m_new = _mlstm(u, gt, w["a_bias"], c0, n0r, m0r, bsz, seq)
    h1 = _a_post(hs, u, xf, p0, w["a_hnorm"], w["a_w_out"], w["ple_w"][0], w["ple_norm"][0], w["ple_wg"][0])

    k, v, kb, vb, vt, lf, lft = _kv_proj(h1, w["kv_norm"], w["kv_w"], w["kv_wf"], w["kv_wft"],
                                     w["kv_bf"], w["kv_bft"], w["k_norm"], w["gmat"])
    q, z = _q_proj(h1, w["b_norm"], w["b_w_in"], w["q_norm"], w["gmat"])
    if past is None:
        f_row = _cumsum_rows(lft, bsz, seq).reshape(bsz, B_PAIRS, 2, seq)
        o = _fox_prompt(q, kb, vt, f_row, bsz, seq)
    else:
        cache_k, cache_v, cache_lf = past
        plen = cache_k.shape[1]
        assert seq <= LANES
        width = -(-(plen + seq) // LANES) * LANES
        lf_rows = jnp.concatenate(
            [jnp.swapaxes(cache_lf.astype(F32), 1, 2),
             jnp.swapaxes(lft.reshape(B_HEADS, bsz, seq), 0, 1),
             jnp.zeros((bsz, B_HEADS, width - plen - seq), F32)], axis=2)
        f_all = _cumsum_rows(lf_rows, bsz, width)
        half = B_HEADS // 2
        f_mix = jnp.transpose(f_all[..., :plen].reshape(bsz, 2, half, plen), (0, 2, 3, 1)).reshape(bsz, half, 2 * plen)
        f_new = jnp.pad(f_all[..., plen:plen + seq], ((0, 0), (0, 0), (0, LANES - seq)))
        pad_new = lambda a: jnp.pad(a.reshape(bsz, seq, B_W), ((0, 0), (0, LANES - seq), (0, 0)))
        o = _fox_decode(q.reshape(bsz, seq, B_W), cache_k.reshape(bsz, plen * B_HEADS, B_DH).astype(F32),
                        cache_v.reshape(bsz, plen * B_HEADS, B_DH).astype(F32), pad_new(kb), pad_new(vb),
                        f_new, f_mix, bsz, seq).reshape(t, B_W)
    y = _b_post(o, z, h1, p1, w["b_w_out"], w["ple_w"][1], w["ple_norm"][1], w["ple_wg"][1])

    return (y.reshape(bsz, seq, D_MODEL),
            c_new[None], n_new.reshape(1, bsz, A_HEADS, A_DK), m_new[None, :, :, 0, 0],
            k.reshape(bsz, seq, B_HEADS, B_DH), v.reshape(bsz, seq, B_HEADS, B_DH),
            lf.reshape(bsz, seq, B_HEADS))


def kernel(x_prompt, x_sample, cache_k, cache_v, cache_logf, state_C, state_n, state_m, p_prompt, p_sample,
           a_norm, a_w_in, a_b_i, a_b_f, a_hnorm, a_w_out, kv_norm, kv_w, kv_b_f, k_norm,
           b_norm, b_w_in, q_norm, b_w_out, ple_w, ple_norm, ple_w_g):
    w = _prep_weights(a_norm, a_w_in, a_b_i, a_b_f, a_hnorm, a_w_out, kv_norm, kv_w, kv_b_f, k_norm,
                      b_norm, b_w_in, q_norm, b_w_out, ple_w, ple_norm, ple_w_g)
    bsz = x_prompt.shape[0]
    c0 = jnp.zeros((bsz, A_HEADS, A_DK, A_DV), F32)
    n0 = jnp.zeros((bsz, A_HEADS, A_DK), F32)
    m0 = jnp.zeros((bsz, A_HEADS), F32)
    prompt = _trunk(x_prompt, p_prompt, c0, n0, m0, None, w)
    sample = _trunk(x_sample, p_sample, state_C[0].astype(F32), state_n[0].astype(F32),
                    state_m[0].astype(F32), (cache_k, cache_v, cache_logf), w)
    return (prompt[0], sample[0]) + prompt[1:] + sample[1:]
```

```python
import functools

import jax
import jax.numpy as jnp
from jax import lax
from jax.experimental import pallas as pl
from jax.experimental.pallas import tpu as pltpu

D_MODEL = 1024
A_HEADS = 8
A_DK = 128
A_DV = 256
A_QK = A_HEADS * A_DK
A_V = A_HEADS * A_DV
A_MAIN = 2 * A_QK + 3 * A_V
B_HEADS = 16
B_DH = 64
B_W = B_HEADS * B_DH
B_PAIRS = B_HEADS // 2
PLE_DIM = 256
EPS = 1e-6
NEG = -1e30
LOG2E = 1.4426950408889634
ACC_ROWS = B_DH + 16

LANES = 128
VMEM_LIMIT = 56 * 1024 * 1024

F32 = jnp.float32
BF16 = jnp.bfloat16
NN = (((1,), (0,)), ((), ()))
NT = (((1,), (1,)), ((), ()))
TN = (((0,), (0,)), ((), ()))


def _params(*sem):
    return pltpu.CompilerParams(dimension_semantics=sem, vmem_limit_bytes=VMEM_LIMIT)


def _tile(n, pref):
    t = min(n, pref)
    while n % t:
        t //= 2
    return t


def _rms(x, g):
    ms = jnp.mean(x * x, axis=-1, keepdims=True)
    return x * lax.rsqrt(ms + EPS) * g


def _log_sigmoid(x):
    return jnp.minimum(x, 0.0) - jnp.log1p(jnp.exp(-jnp.abs(x)))


def _sigmoid(x):
    return 1.0 / (1.0 + jnp.exp(-x))


def _diag_to_col(row, eye):
    n = eye.shape[0]
    return jnp.sum(jnp.where(eye, jnp.broadcast_to(row, (n, n)), 0.0), axis=1, keepdims=True)


def _seg_cumsum_lanes(x, seg):
    w = x.shape[-1]
    pos = lax.broadcasted_iota(jnp.int32, x.shape, x.ndim - 1) % seg
    k = 1
    while k < min(seg, w):
        x = x + jnp.where(pos >= k, pltpu.roll(x, k, x.ndim - 1), 0.0)
        k *= 2
    return x


def _split(x):
    hi = x.astype(BF16)
    return hi, (x - hi.astype(F32)).astype(BF16)


def _dot_split(a, b, dims):
    dot = functools.partial(lax.dot_general, dimension_numbers=dims, preferred_element_type=F32)
    return dot(a[0], b[0]) + (dot(a[0], b[1]) + dot(a[1], b[0]))


def _a_inproj_kernel(x_ref, g_ref, w_ref, wgh_ref, wgl_ref, u_ref, gt_ref, xn_ref):
    @pl.when(pl.program_id(1) == 0)
    def _():
        xn = _rms(x_ref[...], g_ref[...])
        xs = _split(xn)
        xn_ref[...] = xs[0]
        gt_ref[...] = _dot_split((wgh_ref[...], wgl_ref[...]), xs, NT)

    u_ref[...] = jnp.dot(xn_ref[...], w_ref[...], preferred_element_type=F32).astype(BF16)


def _a_inproj(x, g, w_main, w_gt):
    t = x.shape[0]
    tm = _tile(t, 1024)
    tn = 1024
    gate_spec = pl.BlockSpec((2 * A_HEADS, D_MODEL), lambda i, j: (0, 0))
    return pl.pallas_call(
        _a_inproj_kernel,
        grid=(t // tm, A_MAIN // tn),
        in_specs=[
            pl.BlockSpec((tm, D_MODEL), lambda i, j: (i, 0)),
            pl.BlockSpec((1, D_MODEL), lambda i, j: (0, 0)),
            pl.BlockSpec((D_MODEL, tn), lambda i, j: (0, j)),
            gate_spec,
            gate_spec,
        ],
        out_specs=[
            pl.BlockSpec((tm, tn), lambda i, j: (i, j)),
            pl.BlockSpec((2 * A_HEADS, tm), lambda i, j: (0, i)),
        ],
        out_shape=[
            jax.ShapeDtypeStruct((t, A_MAIN), BF16),
            jax.ShapeDtypeStruct((2 * A_HEADS, t), F32),
        ],
        scratch_shapes=[pltpu.VMEM((tm, D_MODEL), BF16)],
        compiler_params=_params("parallel", "arbitrary"),
        name="a_inproj",
    )(x, g, w_main, *w_gt)


def _mlstm_kernel(q_ref, k_ref, v_ref, g_ref, bias_ref, c0_ref, n0_ref, m0_ref,
                  h_ref, c_ref, n_ref, m_ref, *, n_seq, seq_len, n_chunks):
    R = LANES

    @pl.when(pl.program_id(1) == 0)
    def _():
        c_ref[...] = c0_ref[...]
        n_ref[...] = n0_ref[...]
        m_ref[...] = m0_ref[...]

    row = lax.broadcasted_iota(jnp.int32, (R, R), 0)
    col = lax.broadcasted_iota(jnp.int32, (R, R), 1)
    eye = row == col
    valid = (col <= row) & ((row // seq_len) == (col // seq_len))
    row1 = lax.broadcasted_iota(jnp.int32, (R, 1), 0)
    lane1 = lax.broadcasted_iota(jnp.int32, (1, R), 1)
    bias = bias_ref[...]
    scale = A_DK ** -0.5

    def chunk(r0):
        g = g_ref[:, pl.ds(r0, R)]
        ig_all = g[0:A_HEADS] + bias[0:A_HEADS]
        lf_all = _log_sigmoid(g[A_HEADS:] + bias[A_HEADS:])
        b_all = _seg_cumsum_lanes(lf_all, seq_len)
        for h in range(A_HEADS):
            qb = q_ref[pl.ds(r0, R), h * A_DK:(h + 1) * A_DK]
            kb = k_ref[pl.ds(r0, R), h * A_DK:(h + 1) * A_DK]
            vb = v_ref[pl.ds(r0, R), h * A_DV:(h + 1) * A_DV]
            q, k = qb.astype(F32), kb.astype(F32)
            b_row = b_all[h:h + 1]
            ig_row = ig_all[h:h + 1]
            b_col = _diag_to_col(b_row, eye)
            m_prev = [m_ref[s, h][:, 0:1] for s in range(n_seq)]
            n_prev = [n_ref[s, h] for s in range(n_seq)]
            m_prev_col = m_prev[0]
            n_rows = n_prev[0]
            for s in range(1, n_seq):
                m_prev_col = jnp.where(row1 >= s * seq_len, m_prev[s], m_prev_col)
                n_rows = jnp.where(row1 >= s * seq_len, n_prev[s], n_rows)
            d = jnp.where(valid, b_col - b_row + ig_row, NEG)
            inter = b_col + m_prev_col
            m_t = jnp.maximum(inter, jnp.max(d, axis=1, keepdims=True))
            p = jnp.exp(d - m_t)
            w_inter = jnp.exp(inter - m_t)
            qk = lax.dot_general(qb, kb, NT, preferred_element_type=F32) * scale
            s_mat = qk * p
            qc = [jnp.dot(qb[s * seq_len:(s + 1) * seq_len], c_ref[s, h].astype(BF16),
                          preferred_element_type=F32) for s in range(n_seq)]
            qc = qc[0] if n_seq == 1 else jnp.concatenate(qc, axis=0)
            num = w_inter * qc + jnp.dot(s_mat.astype(BF16), vb, preferred_element_type=F32)
            den = (w_inter * jnp.sum(q * n_rows, axis=1, keepdims=True)
                   + jnp.sum(s_mat, axis=1, keepdims=True))
            hval = num * (1.0 / jnp.maximum(jnp.abs(den), jnp.exp(-m_t)))
            h_ref[pl.ds(r0, R), h * A_DV:(h + 1) * A_DV] = hval
            b_last = [b_row[:, (s + 1) * seq_len - 1:(s + 1) * seq_len] for s in range(n_seq)]
            m_new = [m_t[(s + 1) * seq_len - 1:(s + 1) * seq_len] for s in range(n_seq)]
            b_last_row, m_new_row = b_last[0], m_new[0]
            for s in range(1, n_seq):
                b_last_row = jnp.where(lane1 >= s * seq_len, b_last[s], b_last_row)
                m_new_row = jnp.where(lane1 >= s * seq_len, m_new[s], m_new_row)
            w_s_row = jnp.exp(b_last_row - b_row + ig_row - m_new_row)
            ks = k * (_diag_to_col(w_s_row, eye) * scale)
            ksb = ks.astype(BF16)
            for s in range(n_seq):
                sl = slice(s * seq_len, (s + 1) * seq_len)
                w_old = jnp.exp(b_last[s] + m_prev[s] - m_new[s])
                c_ref[s, h] = w_old * c_ref[s, h] + lax.dot_general(
                    ksb[sl], vb[sl], TN, preferred_element_type=F32)
                n_ref[s, h] = w_old * n_prev[s] + jnp.sum(ks[sl], axis=0, keepdims=True)
                m_ref[s, h] = jnp.broadcast_to(m_new[s], (1, LANES))

    if n_chunks == 1:
        chunk(0)
    else:
        def body(ci, carry):
            chunk(pl.multiple_of(ci * R, R))
            return carry
        lax.fori_loop(0, n_chunks, body, 0)


def _mlstm(u, gt, bias, c0, n0, m0, bsz, seq):
    if seq % LANES == 0:
        seq_len, n_seq = LANES, 1
        n_chunks = _tile(seq // LANES, 4)
    else:
        assert LANES % seq == 0 and bsz % (LANES // seq) == 0, (bsz, seq)
        seq_len, n_seq, n_chunks = seq, LANES // seq, 1
    tok = n_chunks * LANES
    steps = seq // (n_chunks * seq_len)
    t = bsz * seq
    tok_map = lambda b, c: (b * steps + c, 0)
    state4 = lambda shape: pl.BlockSpec((n_seq,) + shape, lambda b, c: (b, 0, 0, 0))
    kern = functools.partial(_mlstm_kernel, n_seq=n_seq, seq_len=seq_len, n_chunks=n_chunks)
    return pl.pallas_call(
        kern,
        grid=(bsz // n_seq, steps),
        in_specs=[
            pl.BlockSpec((tok, A_QK), tok_map),
            pl.BlockSpec((tok, A_QK), lambda b, c: (b * steps + c, 1)),
            pl.BlockSpec((tok, A_V), lambda b, c: (b * steps + c, 1)),
            pl.BlockSpec((2 * A_HEADS, tok), lambda b, c: (0, b * steps + c)),
            pl.BlockSpec((2 * A_HEADS, 1), lambda b, c: (0, 0)),
            state4((A_HEADS, A_DK, A_DV)),
            state4((A_HEADS, 1, A_DK)),
            state4((A_HEADS, 1, LANES)),
        ],
        out_specs=[
            pl.BlockSpec((tok, A_V), tok_map),
            state4((A_HEADS, A_DK, A_DV)),
            state4((A_HEADS, 1, A_DK)),
            state4((A_HEADS, 1, LANES)),
        ],
        out_shape=[
            jax.ShapeDtypeStruct((t, A_V), F32),
            jax.ShapeDtypeStruct((bsz, A_HEADS, A_DK, A_DV), F32),
            jax.ShapeDtypeStruct((bsz, A_HEADS, 1, A_DK), F32),
            jax.ShapeDtypeStruct((bsz, A_HEADS, 1, LANES), F32),
        ],
        compiler_params=_params("parallel", "arbitrary"),
        name="mlstm",
    )(u, u, u, gt, bias, c0, n0, m0)


def _ple(h, p_ref, pw_ref, pn_ref, pg_ref):
    pp = jnp.dot(p_ref[...].astype(BF16), pw_ref[...], preferred_element_type=F32)
    hn = _rms(h, pn_ref[...])
    gate = _sigmoid(jnp.dot(hn.astype(BF16), pg_ref[...], preferred_element_type=F32))
    return h + pp * gate


def _a_post_kernel(hs_ref, o_ref, z_ref, x_ref, p_ref, hg_ref, wo_ref, pw_ref, pn_ref, pg_ref,
                   out_ref, y_ref):
    for h in range(A_HEADS):
        sl = slice(h * A_DV, (h + 1) * A_DV)
        y = _rms(hs_ref[:, sl], hg_ref[:, sl]) * _sigmoid(o_ref[:, sl].astype(F32))
        z = z_ref[:, sl].astype(F32)
        y_ref[:, sl] = (y * (z * _sigmoid(z))).astype(BF16)
    h1 = x_ref[...] + jnp.dot(y_ref[...], wo_ref[...], preferred_element_type=F32)
    out_ref[...] = _ple(h1, p_ref, pw_ref, pn_ref, pg_ref)


def _const_spec(shape):
    return pl.BlockSpec(shape, lambda i: (0,) * len(shape))


def _a_post(hs, u, x, p, hnorm, w_out, ple_w, ple_norm, ple_wg):
    t = x.shape[0]
    tm = _tile(t, 256)
    return pl.pallas_call(
        _a_post_kernel,
        grid=(t // tm,),
        in_specs=[
            pl.BlockSpec((tm, A_V), lambda i: (i, 0)),
            pl.BlockSpec((tm, A_V), lambda i: (i, 2)),
            pl.BlockSpec((tm, A_V), lambda i: (i, 3)),
            pl.BlockSpec((tm, D_MODEL), lambda i: (i, 0)),
            pl.BlockSpec((tm, PLE_DIM), lambda i: (i, 0)),
            _const_spec((1, A_V)),
            _const_spec((A_V, D_MODEL)),
            _const_spec((PLE_DIM, D_MODEL)),
            _const_spec((1, D_MODEL)),
            _const_spec((D_MODEL, D_MODEL)),
        ],
        out_specs=pl.BlockSpec((tm, D_MODEL), lambda i: (i, 0)),
        out_shape=jax.ShapeDtypeStruct((t, D_MODEL), F32),
        scratch_shapes=[pltpu.VMEM((tm, A_V), BF16)],
        compiler_params=_params("parallel"),
        name="a_post",
    )(hs, u, u, x, p, hnorm, w_out, ple_w, ple_norm, ple_wg)


def _head_ms(x, gm_ref):
    return jnp.dot((x * x).astype(BF16), gm_ref[...], preferred_element_type=F32) * (1.0 / B_DH)


def _kv_kernel(h_ref, g_ref, w_ref, wfh_ref, wfl_ref, wfth_ref, wftl_ref, bf_ref, bft_ref, kg_ref, gm_ref,
               k_ref, v_ref, kb_ref, vb_ref, vt_ref, lf_ref, lft_ref):
    xs = _split(_rms(h_ref[...], g_ref[...]))
    kv = jnp.dot(xs[0], w_ref[...], preferred_element_type=F32)
    k = kv[:, :B_W]
    v = kv[:, B_W:]
    kn = k * lax.rsqrt(_head_ms(k, gm_ref) + EPS) * kg_ref[...]
    k_ref[...] = kn
    v_ref[...] = v
    kb_ref[...] = kn.astype(BF16)
    vb_ref[...] = v.astype(BF16)
    vt_ref[...] = v.T.astype(BF16)
    fg = _dot_split(xs, (wfh_ref[...], wfl_ref[...]), NN)
    lf_ref[...] = _log_sigmoid(fg + bf_ref[...])
    fgt = _dot_split((wfth_ref[...], wftl_ref[...]), xs, NT)
    lft_ref[...] = _log_sigmoid(fgt + bft_ref[...])


def _kv_proj(h, g, w_kv, w_f, w_ft, b_f, b_ft, k_g, gmat):
    t = h.shape[0]
    tm = _tile(t, 512)
    row = lambda w: pl.BlockSpec((tm, w), lambda i: (i, 0))
    return pl.pallas_call(
        _kv_kernel,
        grid=(t // tm,),
        in_specs=[
            row(D_MODEL),
            _const_spec((1, D_MODEL)),
            _const_spec((D_MODEL, 2 * B_W)),
            _const_spec((D_MODEL, B_HEADS)),
            _const_spec((D_MODEL, B_HEADS)),
            _const_spec((B_HEADS, D_MODEL)),
            _const_spec((B_HEADS, D_MODEL)),
            _const_spec((1, B_HEADS)),
            _const_spec((B_HEADS, 1)),
            _const_spec((1, B_W)),
            _const_spec((B_W, B_W)),
        ],
        out_specs=[row(B_W), row(B_W), row(B_W), row(B_W),
                   pl.BlockSpec((B_W, tm), lambda i: (0, i)), row(B_HEADS),
                   pl.BlockSpec((B_HEADS, tm), lambda i: (0, i))],
        out_shape=[
            jax.ShapeDtypeStruct((t, B_W), F32),
            jax.ShapeDtypeStruct((t, B_W), F32),
            jax.ShapeDtypeStruct((t, B_W), BF16),
            jax.ShapeDtypeStruct((t, B_W), BF16),
            jax.ShapeDtypeStruct((B_W, t), BF16),
            jax.ShapeDtypeStruct((t, B_HEADS), F32),
            jax.ShapeDtypeStruct((B_HEADS, t), F32),
        ],
        compiler_params=_params("parallel"),
        name="kv_proj",
    )(h, g, w_kv, *w_f, *w_ft, b_f, b_ft, k_g, gmat)


def _q_kernel(h_ref, g_ref, w_ref, qg_ref, gm_ref, q_ref, z_ref):
    xn = _rms(h_ref[...], g_ref[...])
    qz = jnp.dot(xn.astype(BF16), w_ref[...], preferred_element_type=F32)
    q = qz[:, :B_W]
    z_ref[...] = qz[:, B_W:]
    qn = q * lax.rsqrt(_head_ms(q, gm_ref) + EPS) * qg_ref[...]
    q_ref[...] = (qn * (B_DH ** -0.5 * LOG2E)).astype(BF16)


def _q_proj(h, g, w_in, q_g, gmat):
    t = h.shape[0]
    tm = _tile(t, 512)
    row = lambda w: pl.BlockSpec((tm, w), lambda i: (i, 0))
    return pl.pallas_call(
        _q_kernel,
        grid=(t // tm,),
        in_specs=[row(D_MODEL), _const_spec((1, D_MODEL)), _const_spec((D_MODEL, 2 * B_W)),
                  _const_spec((1, B_W)), _const_spec((B_W, B_W))],
        out_specs=[row(B_W), row(B_W)],
        out_shape=[jax.ShapeDtypeStruct((t, B_W), BF16), jax.ShapeDtypeStruct((t, B_W), F32)],
        compiler_params=_params("parallel"),
        name="q_proj",
    )(h, g, w_in, q_g, gmat)


def _cumsum_kernel(x_ref, f_ref, carry_ref):
    @pl.when(pl.program_id(1) == 0)
    def _():
        carry_ref[...] = jnp.zeros_like(carry_ref)

    w = x_ref.shape[-1]
    f = _seg_cumsum_lanes(x_ref[...], w) + carry_ref[:, 0:1]
    f_ref[...] = f
    carry_ref[...] = jnp.broadcast_to(f[:, w - 1:w], carry_ref.shape)


def _cumsum_rows(x, bsz, width):
    tb = _tile(width, 2048) if width % LANES == 0 and (width // LANES) & (width // LANES - 1) == 0 else width
    nb = width // tb
    if x.ndim == 2:
        in_spec = pl.BlockSpec((B_HEADS, tb), lambda b, j: (0, b * nb + j))
    else:
        in_spec = pl.BlockSpec((None, B_HEADS, tb), lambda b, j: (b, 0, j))
    return pl.pallas_call(
        _cumsum_kernel,
        grid=(bsz, nb),
        in_specs=[in_spec],
        out_specs=pl.BlockSpec((None, B_HEADS, tb), lambda b, j: (b, 0, j)),
        out_shape=jax.ShapeDtypeStruct((bsz, B_HEADS, width), F32),
        scratch_shapes=[pltpu.VMEM((B_HEADS, LANES), F32)],
        compiler_params=_params("parallel", "arbitrary"),
        name="logf_cumsum",
    )(x)


def _fox_kernel(q_ref, k_ref, vt_ref, f_ref, o_ref, qt_ref, m_ref, cm_ref, acc_ref,
                s00, s01, s10, s11, *, tq):
    tk = tq // 2
    sbuf = ((s00, s01), (s10, s11))
    qi = pl.program_id(2)
    q0 = pl.multiple_of(qi * tq, tq)
    feat = lax.broadcasted_iota(jnp.int32, (LANES, tq), 0)
    qt = q_ref[...].astype(F32).T
    qt_ref[0] = jnp.where(feat < B_DH, qt, 0.0).astype(BF16)
    qt_ref[1] = jnp.where(feat >= B_DH, qt, 0.0).astype(BF16)
    m_ref[...] = jnp.full_like(m_ref, NEG)
    acc_ref[...] = jnp.zeros_like(acc_ref)
    f_base = f_ref[:, pl.ds(q0, tq)][:, tq - 1:tq]
    ones = jnp.ones((ACC_ROWS - B_DH, tk), BF16)
    kpos = lax.broadcasted_iota(jnp.int32, (tk, tq), 0)
    qpos = lax.broadcasted_iota(jnp.int32, (tk, tq), 1)
    diag_masks = (kpos <= qpos, (kpos + tk) <= qpos)

    def scores(k0, slot, causal=None):
        kb = k_ref[pl.ds(k0, tk), :]
        fk = (f_ref[:, pl.ds(k0, tk)] - f_base) * LOG2E
        for hh in range(2):
            fk_col = jnp.concatenate(
                [jnp.broadcast_to(fk[hh:hh + 1, c * LANES:(c + 1) * LANES], (LANES, LANES)).T
                 for c in range(tk // LANES)], axis=0)
            t = (jnp.dot(kb, qt_ref[hh], preferred_element_type=F32)
                 - pltpu.repeat(fk_col, tq // LANES, axis=1))
            if causal is not None:
                t = jnp.where(causal, t, NEG)
            sbuf[slot][hh][...] = t
            cm_ref[slot, hh] = jnp.max(t, axis=0, keepdims=True)

    def update(k0, slot):
        for hh in range(2):
            m_prev = m_ref[hh]
            m_new = jnp.maximum(m_prev, cm_ref[slot, hh])
            alpha = jnp.exp2(m_prev - m_new)
            p = jnp.exp2(sbuf[slot][hh][...] - m_new).astype(BF16)
            vt = jnp.concatenate([vt_ref[hh * B_DH:(hh + 1) * B_DH, pl.ds(k0, tk)], ones], axis=0)
            acc_ref[hh] = alpha * acc_ref[hh] + jnp.dot(vt, p, preferred_element_type=F32)
            m_ref[hh] = m_new

    def pair(k0, next_causal):
        scores(k0 + tk, 1)
        update(k0, 0)
        scores(k0 + tq, 0, next_causal)
        update(k0 + tk, 1)

    @pl.when(qi == 0)
    def _():
        scores(q0, 0, diag_masks[0])

    @pl.when(qi > 0)
    def _():
        scores(0, 0)

        def body(jj, carry):
            pair(pl.multiple_of(jj * tq, tq), None)
            return carry

        lax.fori_loop(0, qi - 1, body, 0)
        pair(q0 - tq, diag_masks[0])

    scores(q0 + tk, 1, diag_masks[1])
    update(q0, 0)
    update(q0 + tk, 1)
    o_t = jnp.concatenate(
        [acc_ref[hh, 0:B_DH] * (1.0 / acc_ref[hh, B_DH:B_DH + 1]) for hh in range(2)], axis=0)
    o_ref[...] = o_t.T


def _fox_prompt(q, kb, vt, f_row, bsz, seq):
    tq = _tile(seq, 512)
    tk = tq // 2
    nq = seq // tq
    kern = functools.partial(_fox_kernel, tq=tq)
    return pl.pallas_call(
        kern,
        grid=(bsz, B_PAIRS, nq),
        in_specs=[
            pl.BlockSpec((tq, LANES), lambda b, p, qi: (b * nq + qi, p)),
            pl.BlockSpec((seq, LANES), lambda b, p, qi: (b, p)),
            pl.BlockSpec((LANES, seq), lambda b, p, qi: (p, b)),
            pl.BlockSpec((None, None, 2, seq), lambda b, p, qi: (b, p, 0, 0)),
        ],
        out_specs=pl.BlockSpec((tq, LANES), lambda b, p, qi: (b * nq + qi, p)),
        out_shape=jax.ShapeDtypeStruct((bsz * seq, B_W), F32),
        scratch_shapes=[
            pltpu.VMEM((2, LANES, tq), BF16),
            pltpu.VMEM((2, 1, tq), F32),
            pltpu.VMEM((2, 2, 1, tq), F32),
            pltpu.VMEM((2, ACC_ROWS, tq), F32),
        ] + [pltpu.VMEM((tk, tq), F32)] * 4,
        compiler_params=_params("parallel", "parallel", "arbitrary"),
        name="fox_prompt",
    )(q, kb, vt, f_row)


def _fox_dec_kernel(q_ref, kt_ref, vt_ref, kn_ref, vn_ref, fn_ref, fc_ref, o_ref,
                    fqc_ref, m_ref, l_ref, a_ref, acc_ref, s_ref, p_ref, *, sq, nkb):
    kj = pl.program_id(1)

    @pl.when(kj == 0)
    def _():
        r = lax.broadcasted_iota(jnp.int32, (sq, LANES), 0)
        c = lax.broadcasted_iota(jnp.int32, (sq, LANES), 1)
        for h in range(B_HEADS):
            fqc_ref[h] = jnp.sum(jnp.where(r == c, fn_ref[h:h + 1, :], 0.0), axis=1, keepdims=True)
        m_ref[...] = jnp.full_like(m_ref, NEG)
        l_ref[...] = jnp.zeros_like(l_ref)
        acc_ref[...] = jnp.zeros_like(acc_ref)

    def attend(width, score, f_keys, values, valid):
        for h in range(B_HEADS):
            s_ref[h, :, 0:width] = score(h)
        for h in range(B_HEADS):
            t = s_ref[h, :, 0:width] + (fqc_ref[h] - f_keys(h)) * LOG2E
            if valid is not None:
                t = jnp.where(valid, t, NEG)
            m_prev = m_ref[h]
            m_new = jnp.maximum(m_prev, jnp.max(t, axis=1, keepdims=True))
            alpha = jnp.exp2(m_prev - m_new)
            p = jnp.exp2(t - m_new)
            l_ref[h] = alpha * l_ref[h] + jnp.sum(p, axis=1, keepdims=True)
            m_ref[h] = m_new
            a_ref[h] = alpha
            p_ref[h, :, 0:width] = p.astype(BF16)
        for h in range(B_HEADS):
            acc_ref[h] = a_ref[h] * acc_ref[h] + values(h, p_ref[h, :, 0:width])

    @pl.when(kj < nkb)
    def _():
        attend(
            kt_ref.shape[-1],
            lambda h: jnp.dot(q_ref[:, h * B_DH:(h + 1) * B_DH], kt_ref[h].astype(BF16),
                              preferred_element_type=F32),
            lambda h: fc_ref[h:h + 1, :],
            lambda h, p: lax.dot_general(p, vt_ref[h].astype(BF16), NT, preferred_element_type=F32),
            None)

    @pl.when(kj == nkb)
    def _():
        i = lax.broadcasted_iota(jnp.int32, (sq, LANES), 0)
        j = lax.broadcasted_iota(jnp.int32, (sq, LANES), 1)
        head = lambda h: slice(h * B_DH, (h + 1) * B_DH)
        attend(
            LANES,
            lambda h: lax.dot_general(q_ref[:, head(h)], kn_ref[:, head(h)], NT, preferred_element_type=F32),
            lambda h: fn_ref[h:h + 1, :],
            lambda h, p: jnp.dot(p, vn_ref[:, head(h)], preferred_element_type=F32),
            j <= i)
        for h in range(B_HEADS):
            o_ref[:, head(h)] = acc_ref[h] * (1.0 / l_ref[h])


def _fox_decode(q, cache_kt, cache_vt, kn, vn, f_new, f_cache, bsz, sq):
    past = cache_kt.shape[3]
    tk = _tile(past, 1024)
    nkb = past // tk
    cmap = lambda b, kj: (b, 0, 0, jnp.minimum(kj, nkb - 1))
    kern = functools.partial(_fox_dec_kernel, sq=sq, nkb=nkb)
    return pl.pallas_call(
        kern,
        grid=(bsz, nkb + 1),
        in_specs=[
            pl.BlockSpec((None, sq, B_W), lambda b, kj: (b, 0, 0)),
            pl.BlockSpec((None, B_HEADS, B_DH, tk), cmap),
            pl.BlockSpec((None, B_HEADS, B_DH, tk), cmap),
            pl.BlockSpec((None, LANES, B_W), lambda b, kj: (b, 0, 0)),
            pl.BlockSpec((None, LANES, B_W), lambda b, kj: (b, 0, 0)),
            pl.BlockSpec((None, B_HEADS, LANES), lambda b, kj: (b, 0, 0)),
            pl.BlockSpec((None, B_HEADS, tk), lambda b, kj: (b, 0, jnp.minimum(kj, nkb - 1))),
        ],
        out_specs=pl.BlockSpec((None, sq, B_W), lambda b, kj: (b, 0, 0)),
        out_shape=jax.ShapeDtypeStruct((bsz, sq, B_W), F32),
        scratch_shapes=[
            pltpu.VMEM((B_HEADS, sq, 1), F32),
            pltpu.VMEM((B_HEADS, sq, 1), F32),
            pltpu.VMEM((B_HEADS, sq, 1), F32),
            pltpu.VMEM((B_HEADS, sq, 1), F32),
            pltpu.VMEM((B_HEADS, sq, B_DH), F32),
            pltpu.VMEM((B_HEADS, sq, tk), F32),
            pltpu.VMEM((B_HEADS, sq, tk), BF16),
        ],
        compiler_params=_params("parallel", "arbitrary"),
        name="fox_decode",
    )(q, cache_kt, cache_vt, kn, vn, f_new, f_cache)


def _b_post_kernel(o_ref, z_ref, h_ref, p_ref, wo_ref, pw_ref, pn_ref, pg_ref, out_ref):
    z = z_ref[...]
    y = o_ref[...] * (z * _sigmoid(z))
    h2 = h_ref[...] + jnp.dot(y.astype(BF16), wo_ref[...], preferred_element_type=F32)
    out_ref[...] = _ple(h2, p_ref, pw_ref, pn_ref, pg_ref)


def _b_post(o, z, h, p, w_out, ple_w, ple_norm, ple_wg):
    t = h.shape[0]
    tm = _tile(t, 512)
    row = lambda w: pl.BlockSpec((tm, w), lambda i: (i, 0))
    return pl.pallas_call(
        _b_post_kernel,
        grid=(t // tm,),
        in_specs=[row(B_W), row(B_W), row(D_MODEL), row(PLE_DIM),
                  _const_spec((B_W, D_MODEL)), _const_spec((PLE_DIM, D_MODEL)),
                  _const_spec((1, D_MODEL)), _const_spec((D_MODEL, D_MODEL))],
        out_specs=row(D_MODEL),
        out_shape=jax.ShapeDtypeStruct((t, D_MODEL), F32),
        compiler_params=_params("parallel"),
        name="b_post",
    )(o, z, h, p, w_out, ple_w, ple_norm, ple_wg)


def _prep_weights(a_norm, a_w_in, a_b_i, a_b_f, a_hnorm, a_w_out, kv_norm, kv_w, kv_b_f, k_norm,
                  b_norm, b_w_in, q_norm, b_w_out, ple_w, ple_norm, ple_w_g):
    head_of_lane = jnp.arange(B_W, dtype=jnp.int32) // B_DH
    return dict(
        a_norm=a_norm[0].reshape(1, D_MODEL),
        a_w_main=a_w_in[0][:, :A_MAIN].astype(BF16),
        a_w_gt=_split(a_w_in[0][:, A_MAIN:].T),
        a_bias=jnp.concatenate([a_b_i[0], a_b_f[0]]).reshape(2 * A_HEADS, 1),
        a_hnorm=a_hnorm[0].reshape(1, A_V),
        a_w_out=a_w_out[0].astype(BF16),
        kv_norm=kv_norm.reshape(1, D_MODEL),
        kv_w=kv_w[:, :2 * B_W].astype(BF16),
        kv_wf=_split(kv_w[:, 2 * B_W:]),
        kv_wft=_split(kv_w[:, 2 * B_W:].T),
        kv_bf=kv_b_f.reshape(1, B_HEADS),
        kv_bft=kv_b_f.reshape(B_HEADS, 1),
        k_norm=jnp.tile(k_norm, B_HEADS).reshape(1, B_W),
        gmat=(head_of_lane[:, None] == head_of_lane[None, :]).astype(BF16),
        b_norm=b_norm[0].reshape(1, D_MODEL),
        b_w_in=b_w_in[0].astype(BF16),
        q_norm=jnp.tile(q_norm[0], B_HEADS).reshape(1, B_W),
        b_w_out=b_w_out[0].astype(BF16),
        ple_w=ple_w.astype(BF16),
        ple_norm=ple_norm.reshape(-1, 1, D_MODEL),
        ple_wg=ple_w_g.astype(BF16),
    )


def _trunk(x, p, c0, n0, m0, past, w):
    bsz, seq, _ = x.shape
    t = bsz * seq
    xf = x.reshape(t, D_MODEL)
    p0 = p[0].reshape(t, PLE_DIM)
    p1 = p[1].reshape(t, PLE_DIM)

    u, gt = _a_inproj(xf, w["a_norm"], w["a_w_main"], w["a_w_gt"])
    n0r = n0.reshape(bsz, A_HEADS, 1, A_DK)
    m0r = jnp.broadcast_to(m0.reshape(bsz, A_HEADS, 1, 1), (bsz, A_HEADS, 1, LANES))
    hs, c_new, n_new, m_new = _mlstm(u, gt, w["a_bias"], c0, n0r, m0r, bsz, seq)
    h1 = _a_post(hs, u, xf, p0, w["a_hnorm"], w["a_w_out"], w["ple_w"][0], w["ple_norm"][0], w["ple_wg"][0])

    k, v, kb, vb, vt, lf, lft = _kv_proj(h1, w["kv_norm"], w["kv_w"], w["kv_wf"], w["kv_wft"],
                                     w["kv_bf"], w["kv_bft"], w["k_norm"], w["gmat"])
    q, z = _q_proj(h1, w["b_norm"], w["b_w_in"], w["q_norm"], w["gmat"])
    if past is None:
        f_row = _cumsum_rows(lft, bsz, seq).reshape(bsz, B_PAIRS, 2, seq)
        o = _fox_prompt(q, kb, vt, f_row, bsz, seq)
    else:
        cache_k, cache_v, cache_lf = past
        plen = cache_k.shape[1]
        assert seq <= LANES
        width = -(-(plen + seq) // LANES) * LANES
        lf_rows = jnp.concatenate(
            [jnp.swapaxes(cache_lf.astype(F32), 1, 2),
             jnp.swapaxes(lft.reshape(B_HEADS, bsz, seq), 0, 1),
             jnp.zeros((bsz, B_HEADS, width - plen - seq), F32)], axis=2)
        f_all = _cumsum_rows(lf_rows, bsz, width)
        f_new = jnp.pad(f_all[..., plen:plen + seq], ((0, 0), (0, 0), (0, LANES - seq)))
        pad_new = lambda a: jnp.pad(a.reshape(bsz, seq, B_W), ((0, 0), (0, LANES - seq), (0, 0)))
        to_t = lambda a: jnp.transpose(a.astype(F32), (0, 2, 3, 1))
        o = _fox_decode(q.reshape(bsz, seq, B_W), to_t(cache_k), to_t(cache_v), pad_new(kb), pad_new(vb),
                        f_new, f_all[..., :plen], bsz, seq).reshape(t, B_W)
    y = _b_post(o, z, h1, p1, w["b_w_out"], w["ple_w"][1], w["ple_norm"][1], w["ple_wg"][1])

    return (y.reshape(bsz, seq, D_MODEL),
            c_new[None], n_new.reshape(1, bsz, A_HEADS, A_DK), m_new[None, :, :, 0, 0],
            k.reshape(bsz, seq, B_HEADS, B_DH), v.reshape(bsz, seq, B_HEADS, B_DH),
            lf.reshape(bsz, seq, B_HEADS))


def kernel(x_prompt, x_sample, cache_k, cache_v, cache_logf, state_C, state_n, state_m, p_prompt, p_sample,
           a_norm, a_w_in, a_b_i, a_b_f, a_hnorm, a_w_out, kv_norm, kv_w, kv_b_f, k_norm,
           b_norm, b_w_in, q_norm, b_w_out, ple_w, ple_norm, ple_w_g):
    w = _prep_weights(a_norm, a_w_in, a_b_i, a_b_f, a_hnorm, a_w_out, kv_norm, kv_w, kv_b_f, k_norm,
                      b_norm, b_w_in, q_norm, b_w_out, ple_w, ple_norm, ple_w_g)
    bsz = x_prompt.shape[0]
    c0 = jnp.zeros((bsz, A_HEADS, A_DK, A_DV), F32)
    n0 = jnp.zeros((bsz, A_HEADS, A_DK), F32)
    m0 = jnp.zeros((bsz, A_HEADS), F32)
    prompt = _trunk(x_prompt, p_prompt, c0, n0, m0, None, w)
    sample = _trunk(x_sample, p_sample, state_C[0].astype(F32), state_n[0].astype(F32),
                    state_m[0].astype(F32), (cache_k, cache_v, cache_logf), w)
    return (prompt[0], sample[0]) + prompt[1:] + sample[1:]
```

```python
import functools

import jax
import jax.numpy as jnp
from jax import lax
from jax.experimental import pallas as pl
from jax.experimental.pallas import tpu as pltpu

D_MODEL = 1024
A_HEADS = 8
A_DK = 128
A_DV = 256
A_QK = A_HEADS * A_DK
A_V = A_HEADS * A_DV
A_MAIN = 2 * A_QK + 3 * A_V
B_HEADS = 16
B_DH = 64
B_W = B_HEADS * B_DH
B_PAIRS = B_HEADS // 2
PLE_DIM = 256
EPS = 1e-6
NEG = -1e30
LOG2E = 1.4426950408889634
ACC_ROWS = B_DH + 16
BIAS_LANES = 6
MAX_BOUND = 60.0
PAIR_UNROLL = 4

LANES = 128
VMEM_LIMIT = 56 * 1024 * 1024

F32 = jnp.float32
BF16 = jnp.bfloat16
NN = (((1,), (0,)), ((), ()))
NT = (((1,), (1,)), ((), ()))
TN = (((0,), (0,)), ((), ()))


def _params(*sem):
    return pltpu.CompilerParams(dimension_semantics=sem, vmem_limit_bytes=VMEM_LIMIT)


def _tile(n, pref):
    t = min(n, pref)
    while n % t:
        t //= 2
    return t


def _rms(x, g):
    ms = jnp.mean(x * x, axis=-1, keepdims=True)
    return x * lax.rsqrt(ms + EPS) * g


def _log_sigmoid(x):
    return jnp.minimum(x, 0.0) - jnp.log1p(jnp.exp(-jnp.abs(x)))


def _sigmoid(x):
    return 1.0 / (1.0 + jnp.exp(-x))


def _diag_to_col(row, eye):
    n = eye.shape[0]
    return jnp.sum(jnp.where(eye, jnp.broadcast_to(row, (n, n)), 0.0), axis=1, keepdims=True)


def _seg_cumsum_lanes(x, seg):
    w = x.shape[-1]
    pos = lax.broadcasted_iota(jnp.int32, x.shape, x.ndim - 1) % seg
    k = 1
    while k < min(seg, w):
        x = x + jnp.where(pos >= k, pltpu.roll(x, k, x.ndim - 1), 0.0)
        k *= 2
    return x


def _split(x):
    hi = x.astype(BF16)
    return hi, (x - hi.astype(F32)).astype(BF16)


def _dot_split(a, b, dims):
    dot = functools.partial(lax.dot_general, dimension_numbers=dims, preferred_element_type=F32)
    return dot(a[0], b[0]) + (dot(a[0], b[1]) + dot(a[1], b[0]))


def _split3(x):
    hi = x.astype(BF16)
    r = x - hi.astype(F32)
    mid = r.astype(BF16)
    return hi, mid, (r - mid.astype(F32)).astype(BF16)


def _a_inproj_kernel(x_ref, g_ref, w_ref, wgh_ref, wgl_ref, u_ref, gt_ref, xn_ref):
    @pl.when(pl.program_id(1) == 0)
    def _():
        xn = _rms(x_ref[...], g_ref[...])
        xs = _split(xn)
        xn_ref[...] = xs[0]
        gt_ref[...] = _dot_split((wgh_ref[...], wgl_ref[...]), xs, NT)

    u_ref[...] = jnp.dot(xn_ref[...], w_ref[...], preferred_element_type=F32).astype(BF16)


def _a_inproj(x, g, w_main, w_gt):
    t = x.shape[0]
    tm = _tile(t, 1024)
    tn = 1024
    gate_spec = pl.BlockSpec((2 * A_HEADS, D_MODEL), lambda i, j: (0, 0))
    return pl.pallas_call(
        _a_inproj_kernel,
        grid=(t // tm, A_MAIN // tn),
        in_specs=[
            pl.BlockSpec((tm, D_MODEL), lambda i, j: (i, 0)),
            pl.BlockSpec((1, D_MODEL), lambda i, j: (0, 0)),
            pl.BlockSpec((D_MODEL, tn), lambda i, j: (0, j)),
            gate_spec,
            gate_spec,
        ],
        out_specs=[
            pl.BlockSpec((tm, tn), lambda i, j: (i, j)),
            pl.BlockSpec((2 * A_HEADS, tm), lambda i, j: (0, i)),
        ],
        out_shape=[
            jax.ShapeDtypeStruct((t, A_MAIN), BF16),
            jax.ShapeDtypeStruct((2 * A_HEADS, t), F32),
        ],
        scratch_shapes=[pltpu.VMEM((tm, D_MODEL), BF16)],
        compiler_params=_params("parallel", "arbitrary"),
        name="a_inproj",
    )(x, g, w_main, *w_gt)


def _mlstm_kernel(q_ref, k_ref, v_ref, g_ref, bias_ref, c0_ref, n0_ref, m0_ref,
                  h_ref, c_ref, n_ref, m_ref, *, n_seq, seq_len, n_chunks):
    R = LANES

    @pl.when(pl.program_id(1) == 0)
    def _():
        c_ref[...] = c0_ref[...]
        n_ref[...] = n0_ref[...]
        m_ref[...] = m0_ref[...]

    row = lax.broadcasted_iota(jnp.int32, (R, R), 0)
    col = lax.broadcasted_iota(jnp.int32, (R, R), 1)
    eye = row == col
    valid = (col <= row) & ((row // seq_len) == (col // seq_len))
    row1 = lax.broadcasted_iota(jnp.int32, (R, 1), 0)
    lane1 = lax.broadcasted_iota(jnp.int32, (1, R), 1)
    bias = bias_ref[...]
    scale = A_DK ** -0.5

    def chunk(r0):
        g = g_ref[:, pl.ds(r0, R)]
        ig_all = g[0:A_HEADS] + bias[0:A_HEADS]
        lf_all = _log_sigmoid(g[A_HEADS:] + bias[A_HEADS:])
        b_all = _seg_cumsum_lanes(lf_all, seq_len)
        for h in range(A_HEADS):
            qb = q_ref[pl.ds(r0, R), h * A_DK:(h + 1) * A_DK]
            kb = k_ref[pl.ds(r0, R), h * A_DK:(h + 1) * A_DK]
            vb = v_ref[pl.ds(r0, R), h * A_DV:(h + 1) * A_DV]
            q, k = qb.astype(F32), kb.astype(F32)
            b_row = b_all[h:h + 1]
            ig_row = ig_all[h:h + 1]
            b_col = _diag_to_col(b_row, eye)
            m_prev = [m_ref[s, h][:, 0:1] for s in range(n_seq)]
            n_prev = [n_ref[s, h] for s in range(n_seq)]
            m_prev_col = m_prev[0]
            n_rows = n_prev[0]
            for s in range(1, n_seq):
                m_prev_col = jnp.where(row1 >= s * seq_len, m_prev[s], m_prev_col)
                n_rows = jnp.where(row1 >= s * seq_len, n_prev[s], n_rows)
            d = jnp.where(valid, b_col - b_row + ig_row, NEG)
            inter = b_col + m_prev_col
            m_t = jnp.maximum(inter, jnp.max(d, axis=1, keepdims=True))
            p = jnp.exp(d - m_t)
            w_inter = jnp.exp(inter - m_t)
            qk = lax.dot_general(qb, kb, NT, preferred_element_type=F32) * scale
            s_mat = qk * p
            qc = [jnp.dot(qb[s * seq_len:(s + 1) * seq_len], c_ref[s, h].astype(BF16),
                          preferred_element_type=F32) for s in range(n_seq)]
            qc = qc[0] if n_seq == 1 else jnp.concatenate(qc, axis=0)
            num = w_inter * qc + jnp.dot(s_mat.astype(BF16), vb, preferred_element_type=F32)
            den = (w_inter * jnp.sum(q * n_rows, axis=1, keepdims=True)
                   + jnp.sum(s_mat, axis=1, keepdims=True))
            hval = num * (1.0 / jnp.maximum(jnp.abs(den), jnp.exp(-m_t)))
            h_ref[pl.ds(r0, R), h * A_DV:(h + 1) * A_DV] = hval
            b_last = [b_row[:, (s + 1) * seq_len - 1:(s + 1) * seq_len] for s in range(n_seq)]
            m_new = [m_t[(s + 1) * seq_len - 1:(s + 1) * seq_len] for s in range(n_seq)]
            b_last_row, m_new_row = b_last[0], m_new[0]
            for s in range(1, n_seq):
                b_last_row = jnp.where(lane1 >= s * seq_len, b_last[s], b_last_row)
                m_new_row = jnp.where(lane1 >= s * seq_len, m_new[s], m_new_row)
            w_s_row = jnp.exp(b_last_row - b_row + ig_row - m_new_row)
            ks = k * (_diag_to_col(w_s_row, eye) * scale)
            ksb = ks.astype(BF16)
            for s in range(n_seq):
                sl = slice(s * seq_len, (s + 1) * seq_len)
                w_old = jnp.exp(b_last[s] + m_prev[s] - m_new[s])
                c_ref[s, h] = w_old * c_ref[s, h] + lax.dot_general(
                    ksb[sl], vb[sl], TN, preferred_element_type=F32)
                n_ref[s, h] = w_old * n_prev[s] + jnp.sum(ks[sl], axis=0, keepdims=True)
                m_ref[s, h] = jnp.broadcast_to(m_new[s], (1, LANES))

    if n_chunks == 1:
        chunk(0)
    else:
        def body(ci, carry):
            chunk(pl.multiple_of(ci * R, R))
            return carry
        lax.fori_loop(0, n_chunks, body, 0)


def _mlstm(u, gt, bias, c0, n0, m0, bsz, seq):
    if seq % LANES == 0:
        seq_len, n_seq = LANES, 1
        n_chunks = _tile(seq // LANES, 4)
    else:
        assert LANES % seq == 0 and bsz % (LANES // seq) == 0, (bsz, seq)
        seq_len, n_seq, n_chunks = seq, LANES // seq, 1
    tok = n_chunks * LANES
    steps = seq // (n_chunks * seq_len)
    t = bsz * seq
    tok_map = lambda b, c: (b * steps + c, 0)
    state4 = lambda shape: pl.BlockSpec((n_seq,) + shape, lambda b, c: (b, 0, 0, 0))
    kern = functools.partial(_mlstm_kernel, n_seq=n_seq, seq_len=seq_len, n_chunks=n_chunks)
    return pl.pallas_call(
        kern,
        grid=(bsz // n_seq, steps),
        in_specs=[
            pl.BlockSpec((tok, A_QK), tok_map),
            pl.BlockSpec((tok, A_QK), lambda b, c: (b * steps + c, 1)),
            pl.BlockSpec((tok, A_V), lambda b, c: (b * steps + c, 1)),
            pl.BlockSpec((2 * A_HEADS, tok), lambda b, c: (0, b * steps + c)),
            pl.BlockSpec((2 * A_HEADS, 1), lambda b, c: (0, 0)),
            state4((A_HEADS, A_DK, A_DV)),
            state4((A_HEADS, 1, A_DK)),
            state4((A_HEADS, 1, LANES)),
        ],
        out_specs=[
            pl.BlockSpec((tok, A_V), tok_map),
            state4((A_HEADS, A_DK, A_DV)),
            state4((A_HEADS, 1, A_DK)),
            state4((A_HEADS, 1, LANES)),
        ],
        out_shape=[
            jax.ShapeDtypeStruct((t, A_V), F32),
            jax.ShapeDtypeStruct((bsz, A_HEADS, A_DK, A_DV), F32),
            jax.ShapeDtypeStruct((bsz, A_HEADS, 1, A_DK), F32),
            jax.ShapeDtypeStruct((bsz, A_HEADS, 1, LANES), F32),
        ],
        compiler_params=_params("parallel", "arbitrary"),
        name="mlstm",
    )(u, u, u, gt, bias, c0, n0, m0)


def _ple(h, p_ref, pw_ref, pn_ref, pg_ref):
    pp = jnp.dot(p_ref[...].astype(BF16), pw_ref[...], preferred_element_type=F32)
    hn = _rms(h, pn_ref[...])
    gate = _sigmoid(jnp.dot(hn.astype(BF16), pg_ref[...], preferred_element_type=F32))
    return h + pp * gate


def _a_post_kernel(hs_ref, o_ref, z_ref, x_ref, p_ref, hg_ref, wo_ref, pw_ref, pn_ref, pg_ref,
                   out_ref, y_ref):
    for h in range(A_HEADS):
        sl = slice(h * A_DV, (h + 1) * A_DV)
        y = _rms(hs_ref[:, sl], hg_ref[:, sl]) * _sigmoid(o_ref[:, sl].astype(F32))
        z = z_ref[:, sl].astype(F32)
        y_ref[:, sl] = (y * (z * _sigmoid(z))).astype(BF16)
    h1 = x_ref[...] + jnp.dot(y_ref[...], wo_ref[...], preferred_element_type=F32)
    out_ref[...] = _ple(h1, p_ref, pw_ref, pn_ref, pg_ref)


def _const_spec(shape):
    return pl.BlockSpec(shape, lambda i: (0,) * len(shape))


def _a_post(hs, u, x, p, hnorm, w_out, ple_w, ple_norm, ple_wg):
    t = x.shape[0]
    tm = _tile(t, 256)
    return pl.pallas_call(
        _a_post_kernel,
        grid=(t // tm,),
        in_specs=[
            pl.BlockSpec((tm, A_V), lambda i: (i, 0)),
            pl.BlockSpec((tm, A_V), lambda i: (i, 2)),
            pl.BlockSpec((tm, A_V), lambda i: (i, 3)),
            pl.BlockSpec((tm, D_MODEL), lambda i: (i, 0)),
            pl.BlockSpec((tm, PLE_DIM), lambda i: (i, 0)),
            _const_spec((1, A_V)),
            _const_spec((A_V, D_MODEL)),
            _const_spec((PLE_DIM, D_MODEL)),
            _const_spec((1, D_MODEL)),
            _const_spec((D_MODEL, D_MODEL)),
        ],
        out_specs=pl.BlockSpec((tm, D_MODEL), lambda i: (i, 0)),
        out_shape=jax.ShapeDtypeStruct((t, D_MODEL), F32),
        scratch_shapes=[pltpu.VMEM((tm, A_V), BF16)],
        compiler_params=_params("parallel"),
        name="a_post",
    )(hs, u, u, x, p, hnorm, w_out, ple_w, ple_norm, ple_wg)


def _head_ms(x, gm_ref):
    return jnp.dot((x * x).astype(BF16), gm_ref[...], preferred_element_type=F32) * (1.0 / B_DH)


def _kv_kernel(h_ref, g_ref, w_ref, wfh_ref, wfl_ref, wfth_ref, wftl_ref, bf_ref, bft_ref, kg_ref, gm_ref,
               k_ref, v_ref, kb_ref, vb_ref, vt_ref, lf_ref, lft_ref):
    xs = _split(_rms(h_ref[...], g_ref[...]))
    kv = jnp.dot(xs[0], w_ref[...], preferred_element_type=F32)
    k = kv[:, :B_W]
    v = kv[:, B_W:]
    kn = k * lax.rsqrt(_head_ms(k, gm_ref) + EPS) * kg_ref[...]
    k_ref[...] = kn
    v_ref[...] = v
    kb_ref[...] = kn.astype(BF16)
    vb_ref[...] = v.astype(BF16)
    vt_ref[...] = v.T.astype(BF16)
    fg = _dot_split(xs, (wfh_ref[...], wfl_ref[...]), NN)
    lf_ref[...] = _log_sigmoid(fg + bf_ref[...])
    fgt = _dot_split((wfth_ref[...], wftl_ref[...]), xs, NT)
    lft_ref[...] = _log_sigmoid(fgt + bft_ref[...])


def _kv_proj(h, g, w_kv, w_f, w_ft, b_f, b_ft, k_g, gmat):
    t = h.shape[0]
    tm = _tile(t, 512)
    row = lambda w: pl.BlockSpec((tm, w), lambda i: (i, 0))
    return pl.pallas_call(
        _kv_kernel,
        grid=(t // tm,),
        in_specs=[
            row(D_MODEL),
            _const_spec((1, D_MODEL)),
            _const_spec((D_MODEL, 2 * B_W)),
            _const_spec((D_MODEL, B_HEADS)),
            _const_spec((D_MODEL, B_HEADS)),
            _const_spec((B_HEADS, D_MODEL)),
            _const_spec((B_HEADS, D_MODEL)),
            _const_spec((1, B_HEADS)),
            _const_spec((B_HEADS, 1)),
            _const_spec((1, B_W)),
            _const_spec((B_W, B_W)),
        ],
        out_specs=[row(B_W), row(B_W), row(B_W), row(B_W),
                   pl.BlockSpec((B_W, tm), lambda i: (0, i)), row(B_HEADS),
                   pl.BlockSpec((B_HEADS, tm), lambda i: (0, i))],
        out_shape=[
            jax.ShapeDtypeStruct((t, B_W), F32),
            jax.ShapeDtypeStruct((t, B_W), F32),
            jax.ShapeDtypeStruct((t, B_W), BF16),
            jax.ShapeDtypeStruct((t, B_W), BF16),
            jax.ShapeDtypeStruct((B_W, t), BF16),
            jax.ShapeDtypeStruct((t, B_HEADS), F32),
            jax.ShapeDtypeStruct((B_HEADS, t), F32),
        ],
        compiler_params=_params("parallel"),
        name="kv_proj",
    )(h, g, w_kv, *w_f, *w_ft, b_f, b_ft, k_g, gmat)


def _q_kernel(h_ref, g_ref, w_ref, qg_ref, gm_ref, q_ref, z_ref):
    xn = _rms(h_ref[...], g_ref[...])
    qz = jnp.dot(xn.astype(BF16), w_ref[...], preferred_element_type=F32)
    q = qz[:, :B_W]
    z_ref[...] = qz[:, B_W:]
    qn = q * lax.rsqrt(_head_ms(q, gm_ref) + EPS) * qg_ref[...]
    q_ref[...] = (qn * (B_DH ** -0.5 * LOG2E)).astype(BF16)


def _q_proj(h, g, w_in, q_g, gmat):
    t = h.shape[0]
    tm = _tile(t, 512)
    row = lambda w: pl.BlockSpec((tm, w), lambda i: (i, 0))
    return pl.pallas_call(
        _q_kernel,
        grid=(t // tm,),
        in_specs=[row(D_MODEL), _const_spec((1, D_MODEL)), _const_spec((D_MODEL, 2 * B_W)),
                  _const_spec((1, B_W)), _const_spec((B_W, B_W))],
        out_specs=[row(B_W), row(B_W)],
        out_shape=[jax.ShapeDtypeStruct((t, B_W), BF16), jax.ShapeDtypeStruct((t, B_W), F32)],
        compiler_params=_params("parallel"),
        name="q_proj",
    )(h, g, w_in, q_g, gmat)


def _cumsum_kernel(x_ref, f_ref, carry_ref):
    @pl.when(pl.program_id(1) == 0)
    def _():
        carry_ref[...] = jnp.zeros_like(carry_ref)

    w = x_ref.shape[-1]
    f = _seg_cumsum_lanes(x_ref[...], w) + carry_ref[:, 0:1]
    f_ref[...] = f
    carry_ref[...] = jnp.broadcast_to(f[:, w - 1:w], carry_ref.shape)


def _cumsum_rows(x, bsz, width):
    tb = _tile(width, 2048) if width % LANES == 0 and (width // LANES) & (width // LANES - 1) == 0 else width
    nb = width // tb
    if x.ndim == 2:
        in_spec = pl.BlockSpec((B_HEADS, tb), lambda b, j: (0, b * nb + j))
    else:
        in_spec = pl.BlockSpec((None, B_HEADS, tb), lambda b, j: (b, 0, j))
    return pl.pallas_call(
        _cumsum_kernel,
        grid=(bsz, nb),
        in_specs=[in_spec],
        out_specs=pl.BlockSpec((None, B_HEADS, tb), lambda b, j: (b, 0, j)),
        out_shape=jax.ShapeDtypeStruct((bsz, B_HEADS, width), F32),
        scratch_shapes=[pltpu.VMEM((B_HEADS, LANES), F32)],
        compiler_params=_params("parallel", "arbitrary"),
        name="logf_cumsum",
    )(x)


def _cumsum_aug_kernel(x_ref, place_ref, f_ref, ka_ref, carry_ref):
    @pl.when(pl.program_id(1) == 0)
    def _():
        carry_ref[...] = jnp.zeros_like(carry_ref)

    w = x_ref.shape[-1]
    f = _seg_cumsum_lanes(x_ref[...], w) + carry_ref[:, 0:1]
    f_ref[...] = f
    carry_ref[...] = jnp.broadcast_to(f[:, w - 1:w], carry_ref.shape)
    rows = jnp.concatenate(_split3(f * LOG2E) + (jnp.ones((B_HEADS, w), BF16),), axis=0)
    ka_ref[...] = lax.dot_general(rows, place_ref[...], TN, preferred_element_type=F32).astype(BF16)


def _bias_placement():
    r = jnp.arange(4 * B_HEADS)[:, None]
    c = jnp.arange(B_W)[None, :]
    part, head = r // B_HEADS, r % B_HEADS
    lane0 = (head // 2) * LANES + BIAS_LANES * (head % 2)
    is_part = (part < 3) & (c == lane0 + part)
    is_one = (part == 3) & (c >= lane0 + 3) & (c < lane0 + 6)
    return (is_part | is_one).astype(BF16)


def _cumsum_aug(x, bsz, width):
    tb = _tile(width, 2048)
    nb = width // tb
    return pl.pallas_call(
        _cumsum_aug_kernel,
        grid=(bsz, nb),
        in_specs=[pl.BlockSpec((B_HEADS, tb), lambda b, j: (0, b * nb + j)),
                  pl.BlockSpec((4 * B_HEADS, B_W), lambda b, j: (0, 0))],
        out_specs=[pl.BlockSpec((None, B_HEADS, tb), lambda b, j: (b, 0, j)),
                   pl.BlockSpec((tb, B_W), lambda b, j: (b * nb + j, 0))],
        out_shape=[jax.ShapeDtypeStruct((bsz, B_HEADS, width), F32),
                   jax.ShapeDtypeStruct((bsz * width, B_W), BF16)],
        scratch_shapes=[pltpu.VMEM((B_HEADS, LANES), F32)],
        compiler_params=_params("parallel", "arbitrary"),
        name="logf_cumsum_aug",
    )(x, _bias_placement())


def _fox_bounded_kernel(bound_ref, q_ref, k_ref, ka_ref, vt_ref, f_ref, o_ref, qt_ref, acc_ref,
                        p00, p01, p10, p11, *, tq):
    tk = tq // 2
    pbuf = ((p00, p01), (p10, p11))
    qi = pl.program_id(2)
    q0 = pl.multiple_of(qi * tq, tq)
    feat = lax.broadcasted_iota(jnp.int32, (LANES, tq), 0)
    qt = q_ref[...].astype(F32).T
    beta = _split3(f_ref[:, pl.ds(q0, tq)] * LOG2E - bound_ref[...])
    for hh in range(2):
        lo = BIAS_LANES * hh
        qt_ref[hh, 0:LANES, :] = jnp.where((feat >= hh * B_DH) & (feat < (hh + 1) * B_DH), qt, 0.0).astype(BF16)
        aug = jnp.where((feat >= lo) & (feat < lo + 3), -1.0, 0.0)
        for part in range(3):
            aug = jnp.where(feat == lo + 3 + part, beta[part][hh:hh + 1].astype(F32), aug)
        qt_ref[hh, LANES:2 * LANES, :] = aug.astype(BF16)
    acc_ref[...] = jnp.zeros_like(acc_ref)
    ones = jnp.ones((ACC_ROWS - B_DH, tk), BF16)
    kpos = lax.broadcasted_iota(jnp.int32, (tk, tq), 0)
    qpos = lax.broadcasted_iota(jnp.int32, (tk, tq), 1)
    diag_masks = (kpos <= qpos, (kpos + tk) <= qpos)

    def weights(k0, slot, causal=None):
        lhs = jnp.concatenate([k_ref[pl.ds(k0, tk), :], ka_ref[pl.ds(k0, tk), :]], axis=1)
        for hh in range(2):
            e = jnp.dot(lhs, qt_ref[hh], preferred_element_type=F32)
            if causal is not None:
                e = jnp.where(causal, e, NEG)
            pbuf[slot][hh][...] = jnp.exp2(e).astype(BF16)

    def accumulate(k0, slot):
        for hh in range(2):
            vt = jnp.concatenate([vt_ref[hh * B_DH:(hh + 1) * B_DH, pl.ds(k0, tk)], ones], axis=0)
            acc_ref[hh] += jnp.dot(vt, pbuf[slot][hh][...], preferred_element_type=F32)

    def pair(k0, next_causal):
        weights(k0 + tk, 1)
        accumulate(k0, 0)
        weights(k0 + tq, 0, next_causal)
        accumulate(k0 + tk, 1)

    @pl.when(qi == 0)
    def _():
        weights(q0, 0, diag_masks[0])

    @pl.when(qi > 0)
    def _():
        weights(0, 0)
        n_pairs = qi - 1

        def unrolled(jj, carry):
            for u in range(PAIR_UNROLL):
                pair(pl.multiple_of((jj * PAIR_UNROLL + u) * tq, tq), None)
            return carry

        def single(jj, carry):
            pair(pl.multiple_of(jj * tq, tq), None)
            return carry

        lax.fori_loop(0, n_pairs // PAIR_UNROLL, unrolled, 0)
        lax.fori_loop((n_pairs // PAIR_UNROLL) * PAIR_UNROLL, n_pairs, single, 0)
        pair(q0 - tq, diag_masks[0])

    weights(q0 + tk, 1, diag_masks[1])
    accumulate(q0, 0)
    accumulate(q0 + tk, 1)
    o_t = jnp.concatenate(
        [acc_ref[hh, 0:B_DH] * (1.0 / acc_ref[hh, B_DH:B_DH + 1]) for hh in range(2)], axis=0)
    o_ref[...] = o_t.T


def _fox_prompt_bounded(q, kb, ka, vt, f_row, bound, bsz, seq):
    tq = _tile(seq, 512)
    nq = seq // tq
    kern = functools.partial(_fox_bounded_kernel, tq=tq)
    return pl.pallas_call(
        kern,
        grid=(bsz, B_PAIRS, nq),
        in_specs=[
            pl.BlockSpec((1, 1), lambda b, p, qi: (0, 0)),
            pl.BlockSpec((tq, LANES), lambda b, p, qi: (b * nq + qi, p)),
            pl.BlockSpec((seq, LANES), lambda b, p, qi: (b, p)),
            pl.BlockSpec((seq, LANES), lambda b, p, qi: (b, p)),
            pl.BlockSpec((LANES, seq), lambda b, p, qi: (p, b)),
            pl.BlockSpec((None, None, 2, seq), lambda b, p, qi: (b, p, 0, 0)),
        ],
        out_specs=pl.BlockSpec((tq, LANES), lambda b, p, qi: (b * nq + qi, p)),
        out_shape=jax.ShapeDtypeStruct((bsz * seq, B_W), F32),
        scratch_shapes=[
            pltpu.VMEM((2, 2 * LANES, tq), BF16),
            pltpu.VMEM((2, ACC_ROWS, tq), F32),
        ] + [pltpu.VMEM((tq // 2, tq), BF16)] * 4,
        compiler_params=_params("parallel", "parallel", "arbitrary"),
        name="fox_prompt_bounded",
    )(bound, q, kb, ka, vt, f_row)


def _fox_kernel(q_ref, k_ref, vt_ref, f_ref, o_ref, qt_ref, m_ref, cm_ref, acc_ref,
                s00, s01, s10, s11, *, tq):
    tk = tq // 2
    sbuf = ((s00, s01), (s10, s11))
    qi = pl.program_id(2)
    q0 = pl.multiple_of(qi * tq, tq)
    feat = lax.broadcasted_iota(jnp.int32, (LANES, tq), 0)
    qt = q_ref[...].astype(F32).T
    qt_ref[0] = jnp.where(feat < B_DH, qt, 0.0).astype(BF16)
    qt_ref[1] = jnp.where(feat >= B_DH, qt, 0.0).astype(BF16)
    m_ref[...] = jnp.full_like(m_ref, NEG)
    acc_ref[...] = jnp.zeros_like(acc_ref)
    f_base = f_ref[:, pl.ds(q0, tq)][:, tq - 1:tq]
    ones = jnp.ones((ACC_ROWS - B_DH, tk), BF16)
    kpos = lax.broadcasted_iota(jnp.int32, (tk, tq), 0)
    qpos = lax.broadcasted_iota(jnp.int32, (tk, tq), 1)
    diag_masks = (kpos <= qpos, (kpos + tk) <= qpos)

    def scores(k0, slot, causal=None):
        kb = k_ref[pl.ds(k0, tk), :]
        fk = (f_ref[:, pl.ds(k0, tk)] - f_base) * LOG2E
        for hh in range(2):
            fk_col = jnp.concatenate(
                [jnp.broadcast_to(fk[hh:hh + 1, c * LANES:(c + 1) * LANES], (LANES, LANES)).T
                 for c in range(tk // LANES)], axis=0)
            t = (jnp.dot(kb, qt_ref[hh], preferred_element_type=F32)
                 - pltpu.repeat(fk_col, tq // LANES, axis=1))
            if causal is not None:
                t = jnp.where(causal, t, NEG)
            sbuf[slot][hh][...] = t
            cm_ref[slot, hh] = jnp.max(t, axis=0, keepdims=True)

    def update(k0, slot):
        for hh in range(2):
            m_prev = m_ref[hh]
            m_new = jnp.maximum(m_prev, cm_ref[slot, hh])
            alpha = jnp.exp2(m_prev - m_new)
            p = jnp.exp2(sbuf[slot][hh][...] - m_new).astype(BF16)
            vt = jnp.concatenate([vt_ref[hh * B_DH:(hh + 1) * B_DH, pl.ds(k0, tk)], ones], axis=0)
            acc_ref[hh] = alpha * acc_ref[hh] + jnp.dot(vt, p, preferred_element_type=F32)
            m_ref[hh] = m_new

    def pair(k0, next_causal):
        scores(k0 + tk, 1)
        update(k0, 0)
        scores(k0 + tq, 0, next_causal)
        update(k0 + tk, 1)

    @pl.when(qi == 0)
    def _():
        scores(q0, 0, diag_masks[0])

    @pl.when(qi > 0)
    def _():
        scores(0, 0)

        def body(jj, carry):
            pair(pl.multiple_of(jj * tq, tq), None)
            return carry

        lax.fori_loop(0, qi - 1, body, 0)
        pair(q0 - tq, diag_masks[0])

    scores(q0 + tk, 1, diag_masks[1])
    update(q0, 0)
    update(q0 + tk, 1)
    o_t = jnp.concatenate(
        [acc_ref[hh, 0:B_DH] * (1.0 / acc_ref[hh, B_DH:B_DH + 1]) for hh in range(2)], axis=0)
    o_ref[...] = o_t.T


def _fox_prompt(q, kb, vt, f_row, bsz, seq):
    tq = _tile(seq, 512)
    tk = tq // 2
    nq = seq // tq
    kern = functools.partial(_fox_kernel, tq=tq)
    return pl.pallas_call(
        kern,
        grid=(bsz, B_PAIRS, nq),
        in_specs=[
            pl.BlockSpec((tq, LANES), lambda b, p, qi: (b * nq + qi, p)),
            pl.BlockSpec((seq, LANES), lambda b, p, qi: (b, p)),
            pl.BlockSpec((LANES, seq), lambda b, p, qi: (p, b)),
            pl.BlockSpec((None, None, 2, seq), lambda b, p, qi: (b, p, 0, 0)),
        ],
        out_specs=pl.BlockSpec((tq, LANES), lambda b, p, qi: (b * nq + qi, p)),
        out_shape=jax.ShapeDtypeStruct((bsz * seq, B_W), F32),
        scratch_shapes=[
            pltpu.VMEM((2, LANES, tq), BF16),
            pltpu.VMEM((2, 1, tq), F32),
            pltpu.VMEM((2, 2, 1, tq), F32),
            pltpu.VMEM((2, ACC_ROWS, tq), F32),
        ] + [pltpu.VMEM((tk, tq), F32)] * 4,
        compiler_params=_params("parallel", "parallel", "arbitrary"),
        name="fox_prompt",
    )(q, kb, vt, f_row)


def _fox_dec_kernel(q_ref, kt_ref, vt_ref, kn_ref, vn_ref, fn_ref, fc_ref, o_ref,
                    fqc_ref, m_ref, l_ref, a_ref, acc_ref, s_ref, p_ref, *, sq, nkb):
    kj = pl.program_id(1)

    @pl.when(kj == 0)
    def _():
        r = lax.broadcasted_iota(jnp.int32, (sq, LANES), 0)
        c = lax.broadcasted_iota(jnp.int32, (sq, LANES), 1)
        for h in range(B_HEADS):
            fqc_ref[h] = jnp.sum(jnp.where(r == c, fn_ref[h:h + 1, :], 0.0), axis=1, keepdims=True)
        m_ref[...] = jnp.full_like(m_ref, NEG)
        l_ref[...] = jnp.zeros_like(l_ref)
        acc_ref[...] = jnp.zeros_like(acc_ref)

    def attend(width, score, f_keys, values, valid):
        for h in range(B_HEADS):
            s_ref[h, :, 0:width] = score(h)
        for h in range(B_HEADS):
            t = s_ref[h, :, 0:width] + (fqc_ref[h] - f_keys(h)) * LOG2E
            if valid is not None:
                t = jnp.where(valid, t, NEG)
            m_prev = m_ref[h]
            m_new = jnp.maximum(m_prev, jnp.max(t, axis=1, keepdims=True))
            alpha = jnp.exp2(m_prev - m_new)
            p = jnp.exp2(t - m_new)
            l_ref[h] = alpha * l_ref[h] + jnp.sum(p, axis=1, keepdims=True)
            m_ref[h] = m_new
            a_ref[h] = alpha
            p_ref[h, :, 0:width] = p.astype(BF16)
        for h in range(B_HEADS):
            acc_ref[h] = a_ref[h] * acc_ref[h] + values(h, p_ref[h, :, 0:width])

    @pl.when(kj < nkb)
    def _():
        attend(
            kt_ref.shape[-1],
            lambda h: jnp.dot(q_ref[:, h * B_DH:(h + 1) * B_DH], kt_ref[h].astype(BF16),
                              preferred_element_type=F32),
            lambda h: fc_ref[h:h + 1, :],
            lambda h, p: lax.dot_general(p, vt_ref[h].astype(BF16), NT, preferred_element_type=F32),
            None)

    @pl.when(kj == nkb)
    def _():
        i = lax.broadcasted_iota(jnp.int32, (sq, LANES), 0)
        j = lax.broadcasted_iota(jnp.int32, (sq, LANES), 1)
        head = lambda h: slice(h * B_DH, (h + 1) * B_DH)
        attend(
            LANES,
            lambda h: lax.dot_general(q_ref[:, head(h)], kn_ref[:, head(h)], NT, preferred_element_type=F32),
            lambda h: fn_ref[h:h + 1, :],
            lambda h, p: jnp.dot(p, vn_ref[:, head(h)], preferred_element_type=F32),
            j <= i)
        for h in range(B_HEADS):
            o_ref[:, head(h)] = acc_ref[h] * (1.0 / l_ref[h])


def _fox_decode(q, cache_kt, cache_vt, kn, vn, f_new, f_cache, bsz, sq):
    past = cache_kt.shape[3]
    tk = _tile(past, 1024)
    nkb = past // tk
    cmap = lambda b, kj: (b, 0, 0, jnp.minimum(kj, nkb - 1))
    kern = functools.partial(_fox_dec_kernel, sq=sq, nkb=nkb)
    return pl.pallas_call(
        kern,
        grid=(bsz, nkb + 1),
        in_specs=[
            pl.BlockSpec((None, sq, B_W), lambda b, kj: (b, 0, 0)),
            pl.BlockSpec((None, B_HEADS, B_DH, tk), cmap),
            pl.BlockSpec((None, B_HEADS, B_DH, tk), cmap),
            pl.BlockSpec((None, LANES, B_W), lambda b, kj: (b, 0, 0)),
            pl.BlockSpec((None, LANES, B_W), lambda b, kj: (b, 0, 0)),
            pl.BlockSpec((None, B_HEADS, LANES), lambda b, kj: (b, 0, 0)),
            pl.BlockSpec((None, B_HEADS, tk), lambda b, kj: (b, 0, jnp.minimum(kj, nkb - 1))),
        ],
        out_specs=pl.BlockSpec((None, sq, B_W), lambda b, kj: (b, 0, 0)),
        out_shape=jax.ShapeDtypeStruct((bsz, sq, B_W), F32),
        scratch_shapes=[
            pltpu.VMEM((B_HEADS, sq, 1), F32),
            pltpu.VMEM((B_HEADS, sq, 1), F32),
            pltpu.VMEM((B_HEADS, sq, 1), F32),
            pltpu.VMEM((B_HEADS, sq, 1), F32),
            pltpu.VMEM((B_HEADS, sq, B_DH), F32),
            pltpu.VMEM((B_HEADS, sq, tk), F32),
            pltpu.VMEM((B_HEADS, sq, tk), BF16),
        ],
        compiler_params=_params("parallel", "arbitrary"),
        name="fox_decode",
    )(q, cache_kt, cache_vt, kn, vn, f_new, f_cache)


def _b_post_kernel(o_ref, z_ref, h_ref, p_ref, wo_ref, pw_ref, pn_ref, pg_ref, out_ref):
    z = z_ref[...]
    y = o_ref[...] * (z * _sigmoid(z))
    h2 = h_ref[...] + jnp.dot(y.astype(BF16), wo_ref[...], preferred_element_type=F32)
    out_ref[...] = _ple(h2, p_ref, pw_ref, pn_ref, pg_ref)


def _b_post(o, z, h, p, w_out, ple_w, ple_norm, ple_wg):
    t = h.shape[0]
    tm = _tile(t, 512)
    row = lambda w: pl.BlockSpec((tm, w), lambda i: (i, 0))
    return pl.pallas_call(
        _b_post_kernel,
        grid=(t // tm,),
        in_specs=[row(B_W), row(B_W), row(D_MODEL), row(PLE_DIM),
                  _const_spec((B_W, D_MODEL)), _const_spec((PLE_DIM, D_MODEL)),
                  _const_spec((1, D_MODEL)), _const_spec((D_MODEL, D_MODEL))],
        out_specs=row(D_MODEL),
        out_shape=jax.ShapeDtypeStruct((t, D_MODEL), F32),
        compiler_params=_params("parallel"),
        name="b_post",
    )(o, z, h, p, w_out, ple_w, ple_norm, ple_wg)


def _prep_weights(a_norm, a_w_in, a_b_i, a_b_f, a_hnorm, a_w_out, kv_norm, kv_w, kv_b_f, k_norm,
                  b_norm, b_w_in, q_norm, b_w_out, ple_w, ple_norm, ple_w_g):
    head_of_lane = jnp.arange(B_W, dtype=jnp.int32) // B_DH
    return dict(
        a_norm=a_norm[0].reshape(1, D_MODEL),
        a_w_main=a_w_in[0][:, :A_MAIN].astype(BF16),
        a_w_gt=_split(a_w_in[0][:, A_MAIN:].T),
        a_bias=jnp.concatenate([a_b_i[0], a_b_f[0]]).reshape(2 * A_HEADS, 1),
        a_hnorm=a_hnorm[0].reshape(1, A_V),
        a_w_out=a_w_out[0].astype(BF16),
        kv_norm=kv_norm.reshape(1, D_MODEL),
        kv_w=kv_w[:, :2 * B_W].astype(BF16),
        kv_wf=_split(kv_w[:, 2 * B_W:]),
        kv_wft=_split(kv_w[:, 2 * B_W:].T),
        kv_bf=kv_b_f.reshape(1, B_HEADS),
        kv_bft=kv_b_f.reshape(B_HEADS, 1),
        k_norm=jnp.tile(k_norm, B_HEADS).reshape(1, B_W),
        gmat=(head_of_lane[:, None] == head_of_lane[None, :]).astype(BF16),
        b_norm=b_norm[0].reshape(1, D_MODEL),
        b_w_in=b_w_in[0].astype(BF16),
        q_norm=jnp.tile(q_norm[0], B_HEADS).reshape(1, B_W),
        b_w_out=b_w_out[0].astype(BF16),
        ple_w=ple_w.astype(BF16),
        ple_norm=ple_norm.reshape(-1, 1, D_MODEL),
        ple_wg=ple_w_g.astype(BF16),
    )


def _trunk(x, p, c0, n0, m0, past, w):
    bsz, seq, _ = x.shape
    t = bsz * seq
    xf = x.reshape(t, D_MODEL)
    p0 = p[0].reshape(t, PLE_DIM)
    p1 = p[1].reshape(t, PLE_DIM)

    u, gt = _a_inproj(xf, w["a_norm"], w["a_w_main"], w["a_w_gt"])
    n0r = n0.reshape(bsz, A_HEADS, 1, A_DK)
    m0r = jnp.broadcast_to(m0.reshape(bsz, A_HEADS, 1, 1), (bsz, A_HEADS, 1, LANES))
    hs, c_new, n_new, m_new = _mlstm(u, gt, w["a_bias"], c0, n0r, m0r, bsz, seq)
    h1 = _a_post(hs, u, xf, p0, w["a_hnorm"], w["a_w_out"], w["ple_w"][0], w["ple_norm"][0], w["ple_wg"][0])

    k, v, kb, vb, vt, lf, lft = _kv_proj(h1, w["kv_norm"], w["kv_w"], w["kv_wf"], w["kv_wft"],
                                     w["kv_bf"], w["kv_bft"], w["k_norm"], w["gmat"])
    q, z = _q_proj(h1, w["b_norm"], w["b_w_in"], w["q_norm"], w["gmat"])
    if past is None:
        f_row, ka = _cumsum_aug(lft, bsz, seq)
        f_row = f_row.reshape(bsz, B_PAIRS, 2, seq)
        bound = (8.0 * LOG2E * 1.01) * jnp.max(jnp.abs(w["q_norm"])) * jnp.max(jnp.abs(w["k_norm"])) + 0.05
        o = lax.cond(
            bound <= MAX_BOUND,
            lambda: _fox_prompt_bounded(q, kb, ka, vt, f_row, bound.reshape(1, 1), bsz, seq),
            lambda: _fox_prompt(q, kb, vt, f_row, bsz, seq))
    else:
        cache_k, cache_v, cache_lf = past
        plen = cache_k.shape[1]
        assert seq <= LANES
        width = -(-(plen + seq) // LANES) * LANES
        lf_rows = jnp.concatenate(
            [jnp.swapaxes(cache_lf.astype(F32), 1, 2),
             jnp.swapaxes(lft.reshape(B_HEADS, bsz, seq), 0, 1),
             jnp.zeros((bsz, B_HEADS, width - plen - seq), F32)], axis=2)
        f_all = _cumsum_rows(lf_rows, bsz, width)
        f_new = jnp.pad(f_all[..., plen:plen + seq], ((0, 0), (0, 0), (0, LANES - seq)))
        pad_new = lambda a: jnp.pad(a.reshape(bsz, seq, B_W), ((0, 0), (0, LANES - seq), (0, 0)))
        to_t = lambda a: jnp.transpose(a.astype(F32), (0, 2, 3, 1))
        o = _fox_decode(q.reshape(bsz, seq, B_W), to_t(cache_k), to_t(cache_v), pad_new(kb), pad_new(vb),
                        f_new, f_all[..., :plen], bsz, seq).reshape(t, B_W)
    y = _b_post(o, z, h1, p1, w["b_w_out"], w["ple_w"][1], w["ple_norm"][1], w["ple_wg"][1])

    return (y.reshape(bsz, seq, D_MODEL),
            c_new[None], n_new.reshape(1, bsz, A_HEADS, A_DK), m_new[None, :, :, 0, 0],
            k.reshape(bsz, seq, B_HEADS, B_DH), v.reshape(bsz, seq, B_HEADS, B_DH),
            lf.reshape(bsz, seq, B_HEADS))


def kernel(x_prompt, x_sample, cache_k, cache_v, cache_logf, state_C, state_n, state_m, p_prompt, p_sample,
           a_norm, a_w_in, a_b_i, a_b_f, a_hnorm, a_w_out, kv_norm, kv_w, kv_b_f, k_norm,
           b_norm, b_w_in, q_norm, b_w_out, ple_w, ple_norm, ple_w_g):
    w = _prep_weights(a_norm, a_w_in, a_b_i, a_b_f, a_hnorm, a_w_out, kv_norm, kv_w, kv_b_f, k_norm,
                      b_norm, b_w_in, q_norm, b_w_out, ple_w, ple_norm, ple_w_g)
    bsz = x_prompt.shape[0]
    c0 = jnp.zeros((bsz, A_HEADS, A_DK, A_DV), F32)
    n0 = jnp.zeros((bsz, A_HEADS, A_DK), F32)
    m0 = jnp.zeros((bsz, A_HEADS), F32)
    prompt = _trunk(x_prompt, p_prompt, c0, n0, m0, None, w)
    sample = _trunk(x_sample, p_sample, state_C[0].astype(F32), state_n[0].astype(F32),
                    state_m[0].astype(F32), (cache_k, cache_v, cache_logf), w)
    return (prompt[0], sample[0]) + prompt[1:] + sample[1:]
```

```python
import functools

import jax
import jax.numpy as jnp
from jax import lax
from jax.experimental import pallas as pl
from jax.experimental.pallas import tpu as pltpu

D_MODEL = 1024
A_HEADS = 8
A_DK = 128
A_DV = 256
A_QK = A_HEADS * A_DK
A_V = A_HEADS * A_DV
A_MAIN = 2 * A_QK + 3 * A_V
B_HEADS = 16
B_DH = 64
B_W = B_HEADS * B_DH
B_PAIRS = B_HEADS // 2
PLE_DIM = 256
EPS = 1e-6
NEG = -1e30
LOG2E = 1.4426950408889634
ACC_ROWS = B_DH + 16
BIAS_LANES = 6
MAX_BOUND = 60.0
PAIR_UNROLL = 4

LANES = 128
VMEM_LIMIT = 56 * 1024 * 1024

F32 = jnp.float32
BF16 = jnp.bfloat16
NN = (((1,), (0,)), ((), ()))
NT = (((1,), (1,)), ((), ()))
TN = (((0,), (0,)), ((), ()))


def _params(*sem):
    return pltpu.CompilerParams(dimension_semantics=sem, vmem_limit_bytes=VMEM_LIMIT)


def _tile(n, pref):
    t = min(n, pref)
    while n % t:
        t //= 2
    return t


def _rms(x, g):
    ms = jnp.mean(x * x, axis=-1, keepdims=True)
    return x * lax.rsqrt(ms + EPS) * g


def _log_sigmoid(x):
    return jnp.minimum(x, 0.0) - jnp.log1p(jnp.exp(-jnp.abs(x)))


def _sigmoid(x):
    return 1.0 / (1.0 + jnp.exp(-x))


def _diag_to_col(row, eye):
    n = eye.shape[0]
    return jnp.sum(jnp.where(eye, jnp.broadcast_to(row, (n, n)), 0.0), axis=1, keepdims=True)


def _seg_cumsum_lanes(x, seg):
    w = x.shape[-1]
    pos = lax.broadcasted_iota(jnp.int32, x.shape, x.ndim - 1) % seg
    k = 1
    while k < min(seg, w):
        x = x + jnp.where(pos >= k, pltpu.roll(x, k, x.ndim - 1), 0.0)
        k *= 2
    return x


def _split(x):
    hi = x.astype(BF16)
    return hi, (x - hi.astype(F32)).astype(BF16)


def _dot_split(a, b, dims):
    dot = functools.partial(lax.dot_general, dimension_numbers=dims, preferred_element_type=F32)
    return dot(a[0], b[0]) + (dot(a[0], b[1]) + dot(a[1], b[0]))


def _split3(x):
    hi = x.astype(BF16)
    r = x - hi.astype(F32)
    mid = r.astype(BF16)
    return hi, mid, (r - mid.astype(F32)).astype(BF16)


def _a_inproj_kernel(x_ref, g_ref, w_ref, wgh_ref, wgl_ref, u_ref, gt_ref, xn_ref):
    @pl.when(pl.program_id(1) == 0)
    def _():
        xn = _rms(x_ref[...], g_ref[...])
        xs = _split(xn)
        xn_ref[...] = xs[0]
        gt_ref[...] = _dot_split((wgh_ref[...], wgl_ref[...]), xs, NT)

    u_ref[...] = jnp.dot(xn_ref[...], w_ref[...], preferred_element_type=F32).astype(BF16)


def _a_inproj(x, g, w_main, w_gt):
    t = x.shape[0]
    tm = _tile(t, 1024)
    tn = 1024
    gate_spec = pl.BlockSpec((2 * A_HEADS, D_MODEL), lambda i, j: (0, 0))
    return pl.pallas_call(
        _a_inproj_kernel,
        grid=(t // tm, A_MAIN // tn),
        in_specs=[
            pl.BlockSpec((tm, D_MODEL), lambda i, j: (i, 0)),
            pl.BlockSpec((1, D_MODEL), lambda i, j: (0, 0)),
            pl.BlockSpec((D_MODEL, tn), lambda i, j: (0, j)),
            gate_spec,
            gate_spec,
        ],
        out_specs=[
            pl.BlockSpec((tm, tn), lambda i, j: (i, j)),
            pl.BlockSpec((2 * A_HEADS, tm), lambda i, j: (0, i)),
        ],
        out_shape=[
            jax.ShapeDtypeStruct((t, A_MAIN), BF16),
            jax.ShapeDtypeStruct((2 * A_HEADS, t), F32),
        ],
        scratch_shapes=[pltpu.VMEM((tm, D_MODEL), BF16)],
        compiler_params=_params("parallel", "arbitrary"),
        name="a_inproj",
    )(x, g, w_main, *w_gt)


def _mlstm_kernel(q_ref, k_ref, v_ref, g_ref, bias_ref, c0_ref, m0_ref,
                  h_ref, c_ref, m_ref, gate_ref, *, n_seq, seq_len, n_chunks):
    R = LANES

    @pl.when(pl.program_id(1) == 0)
    def _():
        c_ref[...] = c0_ref[...]
        m_ref[...] = m0_ref[...]

    row = lax.broadcasted_iota(jnp.int32, (R, R), 0)
    col = lax.broadcasted_iota(jnp.int32, (R, R), 1)
    eye = row == col
    valid = (col <= row) & ((row // seq_len) == (col // seq_len))
    row1 = lax.broadcasted_iota(jnp.int32, (R, 1), 0)
    lane1 = lax.broadcasted_iota(jnp.int32, (1, R), 1)
    bias = bias_ref[...]
    scale = A_DK ** -0.5
    ones = jnp.ones((R, LANES), BF16)
    g = g_ref[...]
    gate_ref[0:A_HEADS, :] = g[0:A_HEADS] + bias[0:A_HEADS]
    gate_ref[A_HEADS:, :] = _seg_cumsum_lanes(_log_sigmoid(g[A_HEADS:] + bias[A_HEADS:]), seq_len)

    def chunk(r0):
        rows = pl.ds(r0, R)
        ig_all = gate_ref[0:A_HEADS, rows]
        b_all = gate_ref[A_HEADS:, rows]
        for h in range(A_HEADS):
            qb = q_ref[rows, h * A_DK:(h + 1) * A_DK]
            kb = k_ref[rows, h * A_DK:(h + 1) * A_DK]
            vb = jnp.concatenate([v_ref[rows, h * A_DV:(h + 1) * A_DV], ones], axis=1)
            b_row = b_all[h:h + 1]
            ig_row = ig_all[h:h + 1]
            b_col = _diag_to_col(b_row, eye)
            m_prev = [m_ref[s, h][:, 0:1] for s in range(n_seq)]
            m_prev_col = m_prev[0]
            for s in range(1, n_seq):
                m_prev_col = jnp.where(row1 >= s * seq_len, m_prev[s], m_prev_col)
            d = jnp.where(valid, b_col - b_row + ig_row, NEG)
            inter = b_col + m_prev_col
            m_t = jnp.maximum(inter, jnp.max(d, axis=1, keepdims=True))
            p = jnp.exp(d - m_t)
            w_inter = jnp.exp(inter - m_t)
            qk = lax.dot_general(qb, kb, NT, preferred_element_type=F32) * scale
            s_mat = qk * p
            qc = [jnp.dot(qb[s * seq_len:(s + 1) * seq_len], c_ref[s, h].astype(BF16),
                          preferred_element_type=F32) for s in range(n_seq)]
            qc = qc[0] if n_seq == 1 else jnp.concatenate(qc, axis=0)
            nd = w_inter * qc + jnp.dot(s_mat.astype(BF16), vb, preferred_element_type=F32)
            den = nd[:, A_DV:A_DV + 1]
            h_ref[rows, h * A_DV:(h + 1) * A_DV] = nd[:, 0:A_DV] * (1.0 / jnp.maximum(jnp.abs(den), jnp.exp(-m_t)))
            b_last = [b_row[:, (s + 1) * seq_len - 1:(s + 1) * seq_len] for s in range(n_seq)]
            m_new = [m_t[(s + 1) * seq_len - 1:(s + 1) * seq_len] for s in range(n_seq)]
            b_last_row, m_new_row = b_last[0], m_new[0]
            for s in range(1, n_seq):
                b_last_row = jnp.where(lane1 >= s * seq_len, b_last[s], b_last_row)
                m_new_row = jnp.where(lane1 >= s * seq_len, m_new[s], m_new_row)
            w_s_row = jnp.exp(b_last_row - b_row + ig_row - m_new_row)
            ksb = (kb.astype(F32) * (_diag_to_col(w_s_row, eye) * scale)).astype(BF16)
            for s in range(n_seq):
                sl = slice(s * seq_len, (s + 1) * seq_len)
                w_old = jnp.exp(b_last[s] + m_prev[s] - m_new[s])
                c_ref[s, h] = w_old * c_ref[s, h] + lax.dot_general(
                    ksb[sl], vb[sl], TN, preferred_element_type=F32)
                m_ref[s, h] = jnp.broadcast_to(m_new[s], (1, LANES))

    if n_chunks == 1:
        chunk(0)
    else:
        def body(ci, carry):
            chunk(pl.multiple_of(ci * R, R))
            return carry
        lax.fori_loop(0, n_chunks, body, 0)


def _mlstm(u, gt, bias, c0, m0, bsz, seq):
    if seq % LANES == 0:
        seq_len, n_seq = LANES, 1
        n_chunks = _tile(seq // LANES, 4)
    else:
        assert LANES % seq == 0 and bsz % (LANES // seq) == 0, (bsz, seq)
        seq_len, n_seq, n_chunks = seq, LANES // seq, 1
    tok = n_chunks * LANES
    steps = seq // (n_chunks * seq_len)
    t = bsz * seq
    tok_map = lambda b, c: (b * steps + c, 0)
    state4 = lambda shape: pl.BlockSpec((n_seq,) + shape, lambda b, c: (b, 0, 0, 0))
    kern = functools.partial(_mlstm_kernel, n_seq=n_seq, seq_len=seq_len, n_chunks=n_chunks)
    return pl.pallas_call(
        kern,
        grid=(bsz // n_seq, steps),
        in_specs=[
            pl.BlockSpec((tok, A_QK), tok_map),
            pl.BlockSpec((tok, A_QK), lambda b, c: (b * steps + c, 1)),
            pl.BlockSpec((tok, A_V), lambda b, c: (b * steps + c, 1)),
            pl.BlockSpec((2 * A_HEADS, tok), lambda b, c: (0, b * steps + c)),
            pl.BlockSpec((2 * A_HEADS, 1), lambda b, c: (0, 0)),
            state4((A_HEADS, A_DK, A_DV + LANES)),
            state4((A_HEADS, 1, LANES)),
        ],
        out_specs=[
            pl.BlockSpec((tok, A_V), tok_map),
            state4((A_HEADS, A_DK, A_DV + LANES)),
            state4((A_HEADS, 1, LANES)),
        ],
        out_shape=[
            jax.ShapeDtypeStruct((t, A_V), F32),
            jax.ShapeDtypeStruct((bsz, A_HEADS, A_DK, A_DV + LANES), F32),
            jax.ShapeDtypeStruct((bsz, A_HEADS, 1, LANES), F32),
        ],
        scratch_shapes=[pltpu.VMEM((2 * A_HEADS, tok), F32)],
        compiler_params=_params("parallel", "arbitrary"),
        name="mlstm",
    )(u, u, u, gt, bias, c0, m0)


def _ple(h, p_ref, pw_ref, pn_ref, pg_ref):
    pp = jnp.dot(p_ref[...].astype(BF16), pw_ref[...], preferred_element_type=F32)
    hn = _rms(h, pn_ref[...])
    gate = _sigmoid(jnp.dot(hn.astype(BF16), pg_ref[...], preferred_element_type=F32))
    return h + pp * gate


def _a_post_kernel(hs_ref, o_ref, z_ref, x_ref, p_ref, hg_ref, wo_ref, pw_ref, pn_ref, pg_ref,
                   out_ref, y_ref):
    for h in range(A_HEADS):
        sl = slice(h * A_DV, (h + 1) * A_DV)
        y = _rms(hs_ref[:, sl], hg_ref[:, sl]) * _sigmoid(o_ref[:, sl].astype(F32))
        z = z_ref[:, sl].astype(F32)
        y_ref[:, sl] = (y * (z * _sigmoid(z))).astype(BF16)
    h1 = x_ref[...] + jnp.dot(y_ref[...], wo_ref[...], preferred_element_type=F32)
    out_ref[...] = _ple(h1, p_ref, pw_ref, pn_ref, pg_ref)


def _const_spec(shape):
    return pl.BlockSpec(shape, lambda i: (0,) * len(shape))


def _a_post(hs, u, x, p, hnorm, w_out, ple_w, ple_norm, ple_wg):
    t = x.shape[0]
    tm = _tile(t, 256)
    return pl.pallas_call(
        _a_post_kernel,
        grid=(t // tm,),
        in_specs=[
            pl.BlockSpec((tm, A_V), lambda i: (i, 0)),
            pl.BlockSpec((tm, A_V), lambda i: (i, 2)),
            pl.BlockSpec((tm, A_V), lambda i: (i, 3)),
            pl.BlockSpec((tm, D_MODEL), lambda i: (i, 0)),
            pl.BlockSpec((tm, PLE_DIM), lambda i: (i, 0)),
            _const_spec((1, A_V)),
            _const_spec((A_V, D_MODEL)),
            _const_spec((PLE_DIM, D_MODEL)),
            _const_spec((1, D_MODEL)),
            _const_spec((D_MODEL, D_MODEL)),
        ],
        out_specs=pl.BlockSpec((tm, D_MODEL), lambda i: (i, 0)),
        out_shape=jax.ShapeDtypeStruct((t, D_MODEL), F32),
        scratch_shapes=[pltpu.VMEM((tm, A_V), BF16)],
        compiler_params=_params("parallel"),
        name="a_post",
    )(hs, u, u, x, p, hnorm, w_out, ple_w, ple_norm, ple_wg)


def _head_ms(x, gm_ref):
    return jnp.dot((x * x).astype(BF16), gm_ref[...], preferred_element_type=F32) * (1.0 / B_DH)


def _kv_kernel(h_ref, g_ref, w_ref, wfh_ref, wfl_ref, wfth_ref, wftl_ref, bf_ref, bft_ref, kg_ref, gm_ref,
               k_ref, v_ref, kb_ref, vb_ref, vt_ref, lf_ref, lft_ref):
    xs = _split(_rms(h_ref[...], g_ref[...]))
    kv = jnp.dot(xs[0], w_ref[...], preferred_element_type=F32)
    k = kv[:, :B_W]
    v = kv[:, B_W:]
    kn = k * lax.rsqrt(_head_ms(k, gm_ref) + EPS) * kg_ref[...]
    k_ref[...] = kn
    v_ref[...] = v
    kb_ref[...] = kn.astype(BF16)
    vb_ref[...] = v.astype(BF16)
    vt_ref[...] = v.T.astype(BF16)
    fg = _dot_split(xs, (wfh_ref[...], wfl_ref[...]), NN)
    lf_ref[...] = _log_sigmoid(fg + bf_ref[...])
    fgt = _dot_split((wfth_ref[...], wftl_ref[...]), xs, NT)
    lft_ref[...] = _log_sigmoid(fgt + bft_ref[...])


def _kv_proj(h, g, w_kv, w_f, w_ft, b_f, b_ft, k_g, gmat):
    t = h.shape[0]
    tm = _tile(t, 512)
    row = lambda w: pl.BlockSpec((tm, w), lambda i: (i, 0))
    return pl.pallas_call(
        _kv_kernel,
        grid=(t // tm,),
        in_specs=[
            row(D_MODEL),
            _const_spec((1, D_MODEL)),
            _const_spec((D_MODEL, 2 * B_W)),
            _const_spec((D_MODEL, B_HEADS)),
            _const_spec((D_MODEL, B_HEADS)),
            _const_spec((B_HEADS, D_MODEL)),
            _const_spec((B_HEADS, D_MODEL)),
            _const_spec((1, B_HEADS)),
            _const_spec((B_HEADS, 1)),
            _const_spec((1, B_W)),
            _const_spec((B_W, B_W)),
        ],
        out_specs=[row(B_W), row(B_W), row(B_W), row(B_W),
                   pl.BlockSpec((B_W, tm), lambda i: (0, i)), row(B_HEADS),
                   pl.BlockSpec((B_HEADS, tm), lambda i: (0, i))],
        out_shape=[
            jax.ShapeDtypeStruct((t, B_W), F32),
            jax.ShapeDtypeStruct((t, B_W), F32),
            jax.ShapeDtypeStruct((t, B_W), BF16),
            jax.ShapeDtypeStruct((t, B_W), BF16),
            jax.ShapeDtypeStruct((B_W, t), BF16),
            jax.ShapeDtypeStruct((t, B_HEADS), F32),
            jax.ShapeDtypeStruct((B_HEADS, t), F32),
        ],
        compiler_params=_params("parallel"),
        name="kv_proj",
    )(h, g, w_kv, *w_f, *w_ft, b_f, b_ft, k_g, gmat)


def _q_kernel(h_ref, g_ref, w_ref, qg_ref, gm_ref, q_ref, z_ref):
    xn = _rms(h_ref[...], g_ref[...])
    qz = jnp.dot(xn.astype(BF16), w_ref[...], preferred_element_type=F32)
    q = qz[:, :B_W]
    z_ref[...] = qz[:, B_W:]
    qn = q * lax.rsqrt(_head_ms(q, gm_ref) + EPS) * qg_ref[...]
    q_ref[...] = (qn * (B_DH ** -0.5 * LOG2E)).astype(BF16)


def _q_proj(h, g, w_in, q_g, gmat):
    t = h.shape[0]
    tm = _tile(t, 512)
    row = lambda w: pl.BlockSpec((tm, w), lambda i: (i, 0))
    return pl.pallas_call(
        _q_kernel,
        grid=(t // tm,),
        in_specs=[row(D_MODEL), _const_spec((1, D_MODEL)), _const_spec((D_MODEL, 2 * B_W)),
                  _const_spec((1, B_W)), _const_spec((B_W, B_W))],
        out_specs=[row(B_W), row(B_W)],
        out_shape=[jax.ShapeDtypeStruct((t, B_W), BF16), jax.ShapeDtypeStruct((t, B_W), F32)],
        compiler_params=_params("parallel"),
        name="q_proj",
    )(h, g, w_in, q_g, gmat)


def _cumsum_kernel(x_ref, f_ref, carry_ref):
    @pl.when(pl.program_id(1) == 0)
    def _():
        carry_ref[...] = jnp.zeros_like(carry_ref)

    w = x_ref.shape[-1]
    f = _seg_cumsum_lanes(x_ref[...], w) + carry_ref[:, 0:1]
    f_ref[...] = f
    carry_ref[...] = jnp.broadcast_to(f[:, w - 1:w], carry_ref.shape)


def _cumsum_rows(x, bsz, width):
    tb = _tile(width, 2048) if width % LANES == 0 and (width // LANES) & (width // LANES - 1) == 0 else width
    nb = width // tb
    if x.ndim == 2:
        in_spec = pl.BlockSpec((B_HEADS, tb), lambda b, j: (0, b * nb + j))
    else:
        in_spec = pl.BlockSpec((None, B_HEADS, tb), lambda b, j: (b, 0, j))
    return pl.pallas_call(
        _cumsum_kernel,
        grid=(bsz, nb),
        in_specs=[in_spec],
        out_specs=pl.BlockSpec((None, B_HEADS, tb), lambda b, j: (b, 0, j)),
        out_shape=jax.ShapeDtypeStruct((bsz, B_HEADS, width), F32),
        scratch_shapes=[pltpu.VMEM((B_HEADS, LANES), F32)],
        compiler_params=_params("parallel", "arbitrary"),
        name="logf_cumsum",
    )(x)


def _cumsum_aug_kernel(x_ref, place_ref, f_ref, ka_ref, carry_ref):
    @pl.when(pl.program_id(1) == 0)
    def _():
        carry_ref[...] = jnp.zeros_like(carry_ref)

    w = x_ref.shape[-1]
    f = _seg_cumsum_lanes(x_ref[...], w) + carry_ref[:, 0:1]
    f_ref[...] = f
    carry_ref[...] = jnp.broadcast_to(f[:, w - 1:w], carry_ref.shape)
    rows = jnp.concatenate(_split3(f * LOG2E) + (jnp.ones((B_HEADS, w), BF16),), axis=0)
    ka_ref[...] = lax.dot_general(rows, place_ref[...], TN, preferred_element_type=F32).astype(BF16)


def _bias_placement():
    r = jnp.arange(4 * B_HEADS)[:, None]
    c = jnp.arange(B_W)[None, :]
    part, head = r // B_HEADS, r % B_HEADS
    lane0 = (head // 2) * LANES + BIAS_LANES * (head % 2)
    is_part = (part < 3) & (c == lane0 + part)
    is_one = (part == 3) & (c >= lane0 + 3) & (c < lane0 + 6)
    return (is_part | is_one).astype(BF16)


def _cumsum_aug(x, bsz, width):
    tb = _tile(width, 2048)
    nb = width // tb
    return pl.pallas_call(
        _cumsum_aug_kernel,
        grid=(bsz, nb),
        in_specs=[pl.BlockSpec((B_HEADS, tb), lambda b, j: (0, b * nb + j)),
                  pl.BlockSpec((4 * B_HEADS, B_W), lambda b, j: (0, 0))],
        out_specs=[pl.BlockSpec((None, B_HEADS, tb), lambda b, j: (b, 0, j)),
                   pl.BlockSpec((tb, B_W), lambda b, j: (b * nb + j, 0))],
        out_shape=[jax.ShapeDtypeStruct((bsz, B_HEADS, width), F32),
                   jax.ShapeDtypeStruct((bsz * width, B_W), BF16)],
        scratch_shapes=[pltpu.VMEM((B_HEADS, LANES), F32)],
        compiler_params=_params("parallel", "arbitrary"),
        name="logf_cumsum_aug",
    )(x, _bias_placement())


def _fox_bounded_kernel(bound_ref, q_ref, k_ref, ka_ref, vt_ref, f_ref, o_ref, qt_ref, acc_ref,
                        p00, p01, p10, p11, *, tq):
    tk = tq // 2
    pbuf = ((p00, p01), (p10, p11))
    qi = pl.program_id(2)
    q0 = pl.multiple_of(qi * tq, tq)
    feat = lax.broadcasted_iota(jnp.int32, (LANES, tq), 0)
    qt = q_ref[...].astype(F32).T
    beta = _split3(f_ref[:, pl.ds(q0, tq)] * LOG2E - bound_ref[...])
    for hh in range(2):
        lo = BIAS_LANES * hh
        qt_ref[hh, 0:LANES, :] = jnp.where((feat >= hh * B_DH) & (feat < (hh + 1) * B_DH), qt, 0.0).astype(BF16)
        aug = jnp.where((feat >= lo) & (feat < lo + 3), -1.0, 0.0)
        for part in range(3):
            aug = jnp.where(feat == lo + 3 + part, beta[part][hh:hh + 1].astype(F32), aug)
        qt_ref[hh, LANES:2 * LANES, :] = aug.astype(BF16)
    acc_ref[...] = jnp.zeros_like(acc_ref)
    ones = jnp.ones((ACC_ROWS - B_DH, tk), BF16)
    kpos = lax.broadcasted_iota(jnp.int32, (tk, tq), 0)
    qpos = lax.broadcasted_iota(jnp.int32, (tk, tq), 1)
    diag_masks = (kpos <= qpos, (kpos + tk) <= qpos)

    def weights(k0, slot, causal=None):
        lhs = jnp.concatenate([k_ref[pl.ds(k0, tk), :], ka_ref[pl.ds(k0, tk), :]], axis=1)
        for hh in range(2):
            e = jnp.dot(lhs, qt_ref[hh], preferred_element_type=F32)
            if causal is not None:
                e = jnp.where(causal, e, NEG)
            pbuf[slot][hh][...] = jnp.exp2(e).astype(BF16)

    def accumulate(k0, slot):
        for hh in range(2):
            vt = jnp.concatenate([vt_ref[hh * B_DH:(hh + 1) * B_DH, pl.ds(k0, tk)], ones], axis=0)
            acc_ref[hh] += jnp.dot(vt, pbuf[slot][hh][...], preferred_element_type=F32)

    def pair(k0, next_causal):
        weights(k0 + tk, 1)
        accumulate(k0, 0)
        weights(k0 + tq, 0, next_causal)
        accumulate(k0 + tk, 1)

    def finish():
        weights(q0 + tk, 1, diag_masks[1])
        accumulate(q0, 0)
        accumulate(q0 + tk, 1)
        o_t = jnp.concatenate(
            [acc_ref[hh, 0:B_DH] * (1.0 / acc_ref[hh, B_DH:B_DH + 1]) for hh in range(2)], axis=0)
        o_ref[...] = o_t.T

    @pl.when(qi == 0)
    def _():
        weights(q0, 0, diag_masks[0])
        finish()

    @pl.when(qi > 0)
    def _():
        weights(0, 0)
        n_pairs = qi - 1

        def unrolled(jj, carry):
            for u in range(PAIR_UNROLL):
                pair(pl.multiple_of((jj * PAIR_UNROLL + u) * tq, tq), None)
            return carry

        def single(jj, carry):
            pair(pl.multiple_of(jj * tq, tq), None)
            return carry

        lax.fori_loop(0, n_pairs // PAIR_UNROLL, unrolled, 0)
        lax.fori_loop((n_pairs // PAIR_UNROLL) * PAIR_UNROLL, n_pairs, single, 0)
        pair(q0 - tq, diag_masks[0])
        finish()


def _fox_prompt_bounded(q, kb, ka, vt, f_row, bound, bsz, seq):
    tq = _tile(seq, 512)
    nq = seq // tq
    kern = functools.partial(_fox_bounded_kernel, tq=tq)
    return pl.pallas_call(
        kern,
        grid=(bsz, B_PAIRS, nq),
        in_specs=[
            pl.BlockSpec((1, 1), lambda b, p, qi: (0, 0)),
            pl.BlockSpec((tq, LANES), lambda b, p, qi: (b * nq + qi, p)),
            pl.BlockSpec((seq, LANES), lambda b, p, qi: (b, p)),
            pl.BlockSpec((seq, LANES), lambda b, p, qi: (b, p)),
            pl.BlockSpec((LANES, seq), lambda b, p, qi: (p, b)),
            pl.BlockSpec((None, None, 2, seq), lambda b, p, qi: (b, p, 0, 0)),
        ],
        out_specs=pl.BlockSpec((tq, LANES), lambda b, p, qi: (b * nq + qi, p)),
        out_shape=jax.ShapeDtypeStruct((bsz * seq, B_W), F32),
        scratch_shapes=[
            pltpu.VMEM((2, 2 * LANES, tq), BF16),
            pltpu.VMEM((2, ACC_ROWS, tq), F32),
        ] + [pltpu.VMEM((tq // 2, tq), BF16)] * 4,
        compiler_params=_params("parallel", "parallel", "arbitrary"),
        name="fox_prompt_bounded",
    )(bound, q, kb, ka, vt, f_row)


def _fox_kernel(q_ref, k_ref, vt_ref, f_ref, o_ref, qt_ref, m_ref, cm_ref, acc_ref,
                s00, s01, s10, s11, *, tq):
    tk = tq // 2
    sbuf = ((s00, s01), (s10, s11))
    qi = pl.program_id(2)
    q0 = pl.multiple_of(qi * tq, tq)
    feat = lax.broadcasted_iota(jnp.int32, (LANES, tq), 0)
    qt = q_ref[...].astype(F32).T
    qt_ref[0] = jnp.where(feat < B_DH, qt, 0.0).astype(BF16)
    qt_ref[1] = jnp.where(feat >= B_DH, qt, 0.0).astype(BF16)
    m_ref[...] = jnp.full_like(m_ref, NEG)
    acc_ref[...] = jnp.zeros_like(acc_ref)
    f_base = f_ref[:, pl.ds(q0, tq)][:, tq - 1:tq]
    ones = jnp.ones((ACC_ROWS - B_DH, tk), BF16)
    kpos = lax.broadcasted_iota(jnp.int32, (tk, tq), 0)
    qpos = lax.broadcasted_iota(jnp.int32, (tk, tq), 1)
    diag_masks = (kpos <= qpos, (kpos + tk) <= qpos)

    def scores(k0, slot, causal=None):
        kb = k_ref[pl.ds(k0, tk), :]
        fk = (f_ref[:, pl.ds(k0, tk)] - f_base) * LOG2E
        for hh in range(2):
            fk_col = jnp.concatenate(
                [jnp.broadcast_to(fk[hh:hh + 1, c * LANES:(c + 1) * LANES], (LANES, LANES)).T
                 for c in range(tk // LANES)], axis=0)
            t = (jnp.dot(kb, qt_ref[hh], preferred_element_type=F32)
                 - pltpu.repeat(fk_col, tq // LANES, axis=1))
            if causal is not None:
                t = jnp.where(causal, t, NEG)
            sbuf[slot][hh][...] = t
            cm_ref[slot, hh] = jnp.max(t, axis=0, keepdims=True)

    def update(k0, slot):
        for hh in range(2):
            m_prev = m_ref[hh]
            m_new = jnp.maximum(m_prev, cm_ref[slot, hh])
            alpha = jnp.exp2(m_prev - m_new)
            p = jnp.exp2(sbuf[slot][hh][...] - m_new).astype(BF16)
            vt = jnp.concatenate([vt_ref[hh * B_DH:(hh + 1) * B_DH, pl.ds(k0, tk)], ones], axis=0)
            acc_ref[hh] = alpha * acc_ref[hh] + jnp.dot(vt, p, preferred_element_type=F32)
            m_ref[hh] = m_new

    def pair(k0, next_causal):
        scores(k0 + tk, 1)
        update(k0, 0)
        scores(k0 + tq, 0, next_causal)
        update(k0 + tk, 1)

    @pl.when(qi == 0)
    def _():
        scores(q0, 0, diag_masks[0])

    @pl.when(qi > 0)
    def _():
        scores(0, 0)

        def body(jj, carry):
            pair(pl.multiple_of(jj * tq, tq), None)
            return carry

        lax.fori_loop(0, qi - 1, body, 0)
        pair(q0 - tq, diag_masks[0])

    scores(q0 + tk, 1, diag_masks[1])
    update(q0, 0)
    update(q0 + tk, 1)
    o_t = jnp.concatenate(
        [acc_ref[hh, 0:B_DH] * (1.0 / acc_ref[hh, B_DH:B_DH + 1]) for hh in range(2)], axis=0)
    o_ref[...] = o_t.T


def _fox_prompt(q, kb, vt, f_row, bsz, seq):
    tq = _tile(seq, 512)
    tk = tq // 2
    nq = seq // tq
    kern = functools.partial(_fox_kernel, tq=tq)
    return pl.pallas_call(
        kern,
        grid=(bsz, B_PAIRS, nq),
        in_specs=[
            pl.BlockSpec((tq, LANES), lambda b, p, qi: (b * nq + qi, p)),
            pl.BlockSpec((seq, LANES), lambda b, p, qi: (b, p)),
            pl.BlockSpec((LANES, seq), lambda b, p, qi: (p, b)),
            pl.BlockSpec((None, None, 2, seq), lambda b, p, qi: (b, p, 0, 0)),
        ],
        out_specs=pl.BlockSpec((tq, LANES), lambda b, p, qi: (b * nq + qi, p)),
        out_shape=jax.ShapeDtypeStruct((bsz * seq, B_W), F32),
        scratch_shapes=[
            pltpu.VMEM((2, LANES, tq), BF16),
            pltpu.VMEM((2, 1, tq), F32),
            pltpu.VMEM((2, 2, 1, tq), F32),
            pltpu.VMEM((2, ACC_ROWS, tq), F32),
        ] + [pltpu.VMEM((tk, tq), F32)] * 4,
        compiler_params=_params("parallel", "parallel", "arbitrary"),
        name="fox_prompt",
    )(q, kb, vt, f_row)


def _fox_dec_kernel(q_ref, kt_ref, vt_ref, kn_ref, vn_ref, fn_ref, fc_ref, o_ref,
                    fqc_ref, m_ref, l_ref, a_ref, acc_ref, s_ref, p_ref, *, sq, nkb):
    kj = pl.program_id(1)

    @pl.when(kj == 0)
    def _():
        r = lax.broadcasted_iota(jnp.int32, (sq, LANES), 0)
        c = lax.broadcasted_iota(jnp.int32, (sq, LANES), 1)
        for h in range(B_HEADS):
            fqc_ref[h] = jnp.sum(jnp.where(r == c, fn_ref[h:h + 1, :], 0.0), axis=1, keepdims=True)
        m_ref[...] = jnp.full_like(m_ref, NEG)
        l_ref[...] = jnp.zeros_like(l_ref)
        acc_ref[...] = jnp.zeros_like(acc_ref)

    def attend(width, score, f_keys, values, valid):
        for h in range(B_HEADS):
            s_ref[h, :, 0:width] = score(h)
        for h in range(B_HEADS):
            t = s_ref[h, :, 0:width] + (fqc_ref[h] - f_keys(h)) * LOG2E
            if valid is not None:
                t = jnp.where(valid, t, NEG)
            m_prev = m_ref[h]
            m_new = jnp.maximum(m_prev, jnp.max(t, axis=1, keepdims=True))
            alpha = jnp.exp2(m_prev - m_new)
            p = jnp.exp2(t - m_new)
            l_ref[h] = alpha * l_ref[h] + jnp.sum(p, axis=1, keepdims=True)
            m_ref[h] = m_new
            a_ref[h] = alpha
            p_ref[h, :, 0:width] = p.astype(BF16)
        for h in range(B_HEADS):
            acc_ref[h] = a_ref[h] * acc_ref[h] + values(h, p_ref[h, :, 0:width])

    @pl.when(kj < nkb)
    def _():
        attend(
            kt_ref.shape[-1],
            lambda h: jnp.dot(q_ref[:, h * B_DH:(h + 1) * B_DH], kt_ref[h].astype(BF16),
                              preferred_element_type=F32),
            lambda h: fc_ref[h:h + 1, :],
            lambda h, p: lax.dot_general(p, vt_ref[h].astype(BF16), NT, preferred_element_type=F32),
            None)

    @pl.when(kj == nkb)
    def _():
        i = lax.broadcasted_iota(jnp.int32, (sq, LANES), 0)
        j = lax.broadcasted_iota(jnp.int32, (sq, LANES), 1)
        head = lambda h: slice(h * B_DH, (h + 1) * B_DH)
        attend(
            LANES,
            lambda h: lax.dot_general(q_ref[:, head(h)], kn_ref[:, head(h)], NT, preferred_element_type=F32),
            lambda h: fn_ref[h:h + 1, :],
            lambda h, p: jnp.dot(p, vn_ref[:, head(h)], preferred_element_type=F32),
            j <= i)
        for h in range(B_HEADS):
            o_ref[:, head(h)] = acc_ref[h] * (1.0 / l_ref[h])


def _fox_decode(q, cache_kt, cache_vt, kn, vn, f_new, f_cache, bsz, sq):
    past = cache_kt.shape[3]
    tk = _tile(past, 2048)
    nkb = past // tk
    cmap = lambda b, kj: (b, 0, 0, jnp.minimum(kj, nkb - 1))
    kern = functools.partial(_fox_dec_kernel, sq=sq, nkb=nkb)
    return pl.pallas_call(
        kern,
        grid=(bsz, nkb + 1),
        in_specs=[
            pl.BlockSpec((None, sq, B_W), lambda b, kj: (b, 0, 0)),
            pl.BlockSpec((None, B_HEADS, B_DH, tk), cmap),
            pl.BlockSpec((None, B_HEADS, B_DH, tk), cmap),
            pl.BlockSpec((None, LANES, B_W), lambda b, kj: (b, 0, 0)),
            pl.BlockSpec((None, LANES, B_W), lambda b, kj: (b, 0, 0)),
            pl.BlockSpec((None, B_HEADS, LANES), lambda b, kj: (b, 0, 0)),
            pl.BlockSpec((None, B_HEADS, tk), lambda b, kj: (b, 0, jnp.minimum(kj, nkb - 1))),
        ],
        out_specs=pl.BlockSpec((None, sq, B_W), lambda b, kj: (b, 0, 0)),
        out_shape=jax.ShapeDtypeStruct((bsz, sq, B_W), F32),
        scratch_shapes=[
            pltpu.VMEM((B_HEADS, sq, 1), F32),
            pltpu.VMEM((B_HEADS, sq, 1), F32),
            pltpu.VMEM((B_HEADS, sq, 1), F32),
            pltpu.VMEM((B_HEADS, sq, 1), F32),
            pltpu.VMEM((B_HEADS, sq, B_DH), F32),
            pltpu.VMEM((B_HEADS, sq, tk), F32),
            pltpu.VMEM((B_HEADS, sq, tk), BF16),
        ],
        compiler_params=_params("parallel", "arbitrary"),
        name="fox_decode",
    )(q, cache_kt, cache_vt, kn, vn, f_new, f_cache)


def _b_post_kernel(o_ref, z_ref, h_ref, p_ref, wo_ref, pw_ref, pn_ref, pg_ref, out_ref):
    z = z_ref[...]
    y = o_ref[...] * (z * _sigmoid(z))
    h2 = h_ref[...] + jnp.dot(y.astype(BF16), wo_ref[...], preferred_element_type=F32)
    out_ref[...] = _ple(h2, p_ref, pw_ref, pn_ref, pg_ref)


def _b_post(o, z, h, p, w_out, ple_w, ple_norm, ple_wg):
    t = h.shape[0]
    tm = _tile(t, 512)
    row = lambda w: pl.BlockSpec((tm, w), lambda i: (i, 0))
    return pl.pallas_call(
        _b_post_kernel,
        grid=(t // tm,),
        in_specs=[row(B_W), row(B_W), row(D_MODEL), row(PLE_DIM),
                  _const_spec((B_W, D_MODEL)), _const_spec((PLE_DIM, D_MODEL)),
                  _const_spec((1, D_MODEL)), _const_spec((D_MODEL, D_MODEL))],
        out_specs=row(D_MODEL),
        out_shape=jax.ShapeDtypeStruct((t, D_MODEL), F32),
        compiler_params=_params("parallel"),
        name="b_post",
    )(o, z, h, p, w_out, ple_w, ple_norm, ple_wg)


def _prep_weights(a_norm, a_w_in, a_b_i, a_b_f, a_hnorm, a_w_out, kv_norm, kv_w, kv_b_f, k_norm,
                  b_norm, b_w_in, q_norm, b_w_out, ple_w, ple_norm, ple_w_g):
    head_of_lane = jnp.arange(B_W, dtype=jnp.int32) // B_DH
    return dict(
        a_norm=a_norm[0].reshape(1, D_MODEL),
        a_w_main=a_w_in[0][:, :A_MAIN].astype(BF16),
        a_w_gt=_split(a_w_in[0][:, A_MAIN:].T),
        a_bias=jnp.concatenate([a_b_i[0], a_b_f[0]]).reshape(2 * A_HEADS, 1),
        a_hnorm=a_hnorm[0].reshape(1, A_V),
        a_w_out=a_w_out[0].astype(BF16),
        kv_norm=kv_norm.reshape(1, D_MODEL),
        kv_w=kv_w[:, :2 * B_W].astype(BF16),
        kv_wf=_split(kv_w[:, 2 * B_W:]),
        kv_wft=_split(kv_w[:, 2 * B_W:].T),
        kv_bf=kv_b_f.reshape(1, B_HEADS),
        kv_bft=kv_b_f.reshape(B_HEADS, 1),
        k_norm=jnp.tile(k_norm, B_HEADS).reshape(1, B_W),
        gmat=(head_of_lane[:, None] == head_of_lane[None, :]).astype(BF16),
        b_norm=b_norm[0].reshape(1, D_MODEL),
        b_w_in=b_w_in[0].astype(BF16),
        q_norm=jnp.tile(q_norm[0], B_HEADS).reshape(1, B_W),
        b_w_out=b_w_out[0].astype(BF16),
        ple_w=ple_w.astype(BF16),
        ple_norm=ple_norm.reshape(-1, 1, D_MODEL),
        ple_wg=ple_w_g.astype(BF16),
    )


def _trunk(x, p, c0, n0, m0, past, w):
    bsz, seq, _ = x.shape
    t = bsz * seq
    xf = x.reshape(t, D_MODEL)
    p0 = p[0].reshape(t, PLE_DIM)
    p1 = p[1].reshape(t, PLE_DIM)

    u, gt = _a_inproj(xf, w["a_norm"], w["a_w_main"], w["a_w_gt"])
    c0x = jnp.concatenate([c0, jnp.broadcast_to(n0[..., None], (bsz, A_HEADS, A_DK, LANES))], axis=-1)
    m0r = jnp.broadcast_to(m0.reshape(bsz, A_HEADS, 1, 1), (bsz, A_HEADS, 1, LANES))
    hs, c_ext, m_new = _mlstm(u, gt, w["a_bias"], c0x, m0r, bsz, seq)
    c_new, n_new = c_ext[..., :A_DV], c_ext[..., A_DV]
    h1 = _a_post(hs, u, xf, p0, w["a_hnorm"], w["a_w_out"], w["ple_w"][0], w["ple_norm"][0], w["ple_wg"][0])

    k, v, kb, vb, vt, lf, lft = _kv_proj(h1, w["kv_norm"], w["kv_w"], w["kv_wf"], w["kv_wft"],
                                     w["kv_bf"], w["kv_bft"], w["k_norm"], w["gmat"])
    q, z = _q_proj(h1, w["b_norm"], w["b_w_in"], w["q_norm"], w["gmat"])
    if past is None:
        f_row, ka = _cumsum_aug(lft, bsz, seq)
        f_row = f_row.reshape(bsz, B_PAIRS, 2, seq)
        bound = (8.0 * LOG2E * 1.01) * jnp.max(jnp.abs(w["q_norm"])) * jnp.max(jnp.abs(w["k_norm"])) + 0.05
        o = lax.cond(
            bound <= MAX_BOUND,
            lambda: _fox_prompt_bounded(q, kb, ka, vt, f_row, bound.reshape(1, 1), bsz, seq),
            lambda: _fox_prompt(q, kb, vt, f_row, bsz, seq))
    else:
        cache_k, cache_v, cache_lf = past
        plen = cache_k.shape[1]
        assert seq <= LANES
        width = -(-(plen + seq) // LANES) * LANES
        lf_rows = jnp.concatenate(
            [jnp.swapaxes(cache_lf.astype(F32), 1, 2),
             jnp.swapaxes(lft.reshape(B_HEADS, bsz, seq), 0, 1),
             jnp.zeros((bsz, B_HEADS, width - plen - seq), F32)], axis=2)
        f_all = _cumsum_rows(lf_rows, bsz, width)
        f_new = jnp.pad(f_all[..., plen:plen + seq], ((0, 0), (0, 0), (0, LANES - seq)))
        pad_new = lambda a: jnp.pad(a.reshape(bsz, seq, B_W), ((0, 0), (0, LANES - seq), (0, 0)))
        to_t = lambda a: jnp.transpose(a.astype(F32), (0, 2, 3, 1))
        o = _fox_decode(q.reshape(bsz, seq, B_W), to_t(cache_k), to_t(cache_v), pad_new(kb), pad_new(vb),
                        f_new, f_all[..., :plen], bsz, seq).reshape(t, B_W)
    y = _b_post(o, z, h1, p1, w["b_w_out"], w["ple_w"][1], w["ple_norm"][1], w["ple_wg"][1])

    return (y.reshape(bsz, seq, D_MODEL),
            c_new[None], n_new.reshape(1, bsz, A_HEADS, A_DK), m_new[None, :, :, 0, 0],
            k.reshape(bsz, seq, B_HEADS, B_DH), v.reshape(bsz, seq, B_HEADS, B_DH),
            lf.reshape(bsz, seq, B_HEADS))


def kernel(x_prompt, x_sample, cache_k, cache_v, cache_logf, state_C, state_n, state_m, p_prompt, p_sample,
           a_norm, a_w_in, a_b_i, a_b_f, a_hnorm, a_w_out, kv_norm, kv_w, kv_b_f, k_norm,
           b_norm, b_w_in, q_norm, b_w_out, ple_w, ple_norm, ple_w_g):
    w = _prep_weights(a_norm, a_w_in, a_b_i, a_b_f, a_hnorm, a_w_out, kv_norm, kv_w, kv_b_f, k_norm,
                      b_norm, b_w_in, q_norm, b_w_out, ple_w, ple_norm, ple_w_g)
    bsz = x_prompt.shape[0]
    c0 = jnp.zeros((bsz, A_HEADS, A_DK, A_DV), F32)
    n0 = jnp.zeros((bsz, A_HEADS, A_DK), F32)
    m0 = jnp.zeros((bsz, A_HEADS), F32)
    prompt = _trunk(x_prompt, p_prompt, c0, n0, m0, None, w)
    sample = _trunk(x_sample, p_sample, state_C[0].astype(F32), state_n[0].astype(F32),
                    state_m[0].astype(F32), (cache_k, cache_v, cache_logf), w)
    return (prompt[0], sample[0]) + prompt[1:] + sample[1:]
```

```python
import functools

import jax
import jax.numpy as jnp
from jax import lax
from jax.experimental import pallas as pl
from jax.experimental.pallas import tpu as pltpu

D_MODEL = 1024
A_HEADS = 8
A_DK = 128
A_DV = 256
A_QK = A_HEADS * A_DK
A_V = A_HEADS * A_DV
A_MAIN = 2 * A_QK + 3 * A_V
B_HEADS = 16
B_DH = 64
B_W = B_HEADS * B_DH
B_PAIRS = B_HEADS // 2
PLE_DIM = 256
EPS = 1e-6
NEG = -1e30
LOG2E = 1.4426950408889634
ACC_ROWS = B_DH + 16
BIAS_LANES = 6
MAX_BOUND = 60.0
PAIR_UNROLL = 4

LANES = 128
VMEM_LIMIT = 56 * 1024 * 1024

F32 = jnp.float32
BF16 = jnp.bfloat16
NN = (((1,), (0,)), ((), ()))
NT = (((1,), (1,)), ((), ()))
TN = (((0,), (0,)), ((), ()))


def _params(*sem):
    return pltpu.CompilerParams(dimension_semantics=sem, vmem_limit_bytes=VMEM_LIMIT)


def _tile(n, pref):
    t = min(n, pref)
    while n % t:
        t //= 2
    return t


def _rms(x, g):
    ms = jnp.mean(x * x, axis=-1, keepdims=True)
    return x * lax.rsqrt(ms + EPS) * g


def _log_sigmoid(x):
    return jnp.minimum(x, 0.0) - jnp.log1p(jnp.exp(-jnp.abs(x)))


def _sigmoid(x):
    return 0.5 * jnp.tanh(0.5 * x) + 0.5


def _diag_to_col(row, eye):
    n = eye.shape[0]
    return jnp.sum(jnp.where(eye, jnp.broadcast_to(row, (n, n)), 0.0), axis=1, keepdims=True)


def _seg_cumsum_lanes(x, seg):
    w = x.shape[-1]
    pos = lax.broadcasted_iota(jnp.int32, x.shape, x.ndim - 1) % seg
    k = 1
    while k < min(seg, w):
        x = x + jnp.where(pos >= k, pltpu.roll(x, k, x.ndim - 1), 0.0)
        k *= 2
    return x


def _split(x):
    hi = x.astype(BF16)
    return hi, (x - hi.astype(F32)).astype(BF16)


def _dot_split(a, b, dims):
    dot = functools.partial(lax.dot_general, dimension_numbers=dims, preferred_element_type=F32)
    return dot(a[0], b[0]) + (dot(a[0], b[1]) + dot(a[1], b[0]))


def _split3(x):
    hi = x.astype(BF16)
    r = x - hi.astype(F32)
    mid = r.astype(BF16)
    return hi, mid, (r - mid.astype(F32)).astype(BF16)


def _a_inproj_kernel(x_ref, g_ref, w_ref, wgh_ref, wgl_ref, u_ref, gt_ref, xn_ref):
    @pl.when(pl.program_id(1) == 0)
    def _():
        xn = _rms(x_ref[...], g_ref[...])
        xs = _split(xn)
        xn_ref[...] = xs[0]
        gt_ref[...] = _dot_split((wgh_ref[...], wgl_ref[...]), xs, NT)

    u_ref[...] = jnp.dot(xn_ref[...], w_ref[...], preferred_element_type=F32).astype(BF16)


def _a_inproj(x, g, w_main, w_gt):
    t = x.shape[0]
    tm = _tile(t, 1024)
    tn = 1024
    gate_spec = pl.BlockSpec((2 * A_HEADS, D_MODEL), lambda i, j: (0, 0))
    return pl.pallas_call(
        _a_inproj_kernel,
        grid=(t // tm, A_MAIN // tn),
        in_specs=[
            pl.BlockSpec((tm, D_MODEL), lambda i, j: (i, 0)),
            pl.BlockSpec((1, D_MODEL), lambda i, j: (0, 0)),
            pl.BlockSpec((D_MODEL, tn), lambda i, j: (0, j)),
            gate_spec,
            gate_spec,
        ],
        out_specs=[
            pl.BlockSpec((tm, tn), lambda i, j: (i, j)),
            pl.BlockSpec((2 * A_HEADS, tm), lambda i, j: (0, i)),
        ],
        out_shape=[
            jax.ShapeDtypeStruct((t, A_MAIN), BF16),
            jax.ShapeDtypeStruct((2 * A_HEADS, t), F32),
        ],
        scratch_shapes=[pltpu.VMEM((tm, D_MODEL), BF16)],
        compiler_params=_params("parallel", "arbitrary"),
        name="a_inproj",
    )(x, g, w_main, *w_gt)


def _mlstm_kernel(q_ref, k_ref, v_ref, g_ref, bias_ref, c0_ref, m0_ref,
                  h_ref, c_ref, m_ref, gate_ref, *, n_seq, seq_len, n_chunks):
    R = LANES

    @pl.when(pl.program_id(1) == 0)
    def _():
        c_ref[...] = c0_ref[...]
        m_ref[...] = m0_ref[...]

    row = lax.broadcasted_iota(jnp.int32, (R, R), 0)
    col = lax.broadcasted_iota(jnp.int32, (R, R), 1)
    eye = row == col
    valid = (col <= row) & ((row // seq_len) == (col // seq_len))
    row1 = lax.broadcasted_iota(jnp.int32, (R, 1), 0)
    lane1 = lax.broadcasted_iota(jnp.int32, (1, R), 1)
    bias = bias_ref[...]
    scale = A_DK ** -0.5
    ones = jnp.ones((R, LANES), BF16)
    g = g_ref[...]
    gate_ref[0:A_HEADS, :] = g[0:A_HEADS] + bias[0:A_HEADS]
    gate_ref[A_HEADS:, :] = _seg_cumsum_lanes(_log_sigmoid(g[A_HEADS:] + bias[A_HEADS:]), seq_len)

    def chunk(r0):
        rows = pl.ds(r0, R)
        ig_all = gate_ref[0:A_HEADS, rows]
        b_all = gate_ref[A_HEADS:, rows]
        for h in range(A_HEADS):
            qb = q_ref[rows, h * A_DK:(h + 1) * A_DK]
            kb = k_ref[rows, h * A_DK:(h + 1) * A_DK]
            vb = jnp.concatenate([v_ref[rows, h * A_DV:(h + 1) * A_DV], ones], axis=1)
            b_row = b_all[h:h + 1]
            ig_row = ig_all[h:h + 1]
            b_col = _diag_to_col(b_row, eye)
            m_prev = [m_ref[s, h][:, 0:1] for s in range(n_seq)]
            m_prev_col = m_prev[0]
            for s in range(1, n_seq):
                m_prev_col = jnp.where(row1 >= s * seq_len, m_prev[s], m_prev_col)
            d = jnp.where(valid, b_col - b_row + ig_row, NEG)
            inter = b_col + m_prev_col
            m_t = jnp.maximum(inter, jnp.max(d, axis=1, keepdims=True))
            p = jnp.exp(d - m_t)
            w_inter = jnp.exp(inter - m_t)
            qk = lax.dot_general(qb, kb, NT, preferred_element_type=F32) * scale
            s_mat = qk * p
            qc = [jnp.dot(qb[s * seq_len:(s + 1) * seq_len], c_ref[s, h].astype(BF16),
                          preferred_element_type=F32) for s in range(n_seq)]
            qc = qc[0] if n_seq == 1 else jnp.concatenate(qc, axis=0)
            nd = w_inter * qc + jnp.dot(s_mat.astype(BF16), vb, preferred_element_type=F32)
            den = nd[:, A_DV:A_DV + 1]
            h_ref[rows, h * A_DV:(h + 1) * A_DV] = nd[:, 0:A_DV] * (1.0 / jnp.maximum(jnp.abs(den), jnp.exp(-m_t)))
            b_last = [b_row[:, (s + 1) * seq_len - 1:(s + 1) * seq_len] for s in range(n_seq)]
            m_new = [m_t[(s + 1) * seq_len - 1:(s + 1) * seq_len] for s in range(n_seq)]
            b_last_row, m_new_row = b_last[0], m_new[0]
            for s in range(1, n_seq):
                b_last_row = jnp.where(lane1 >= s * seq_len, b_last[s], b_last_row)
                m_new_row = jnp.where(lane1 >= s * seq_len, m_new[s], m_new_row)
            w_s_row = jnp.exp(b_last_row - b_row + ig_row - m_new_row)
            ksb = (kb.astype(F32) * (_diag_to_col(w_s_row, eye) * scale)).astype(BF16)
            for s in range(n_seq):
                sl = slice(s * seq_len, (s + 1) * seq_len)
                w_old = jnp.exp(b_last[s] + m_prev[s] - m_new[s])
                c_ref[s, h] = w_old * c_ref[s, h] + lax.dot_general(
                    ksb[sl], vb[sl], TN, preferred_element_type=F32)
                m_ref[s, h] = jnp.broadcast_to(m_new[s], (1, LANES))

    if n_chunks == 1:
        chunk(0)
    else:
        def body(ci, carry):
            chunk(pl.multiple_of(ci * R, R))
            return carry
        lax.fori_loop(0, n_chunks, body, 0)


def _mlstm(u, gt, bias, c0, m0, bsz, seq):
    if seq % LANES == 0:
        seq_len, n_seq = LANES, 1
        n_chunks = _tile(seq // LANES, 4)
    else:
        assert LANES % seq == 0 and bsz % (LANES // seq) == 0, (bsz, seq)
        seq_len, n_seq, n_chunks = seq, LANES // seq, 1
    tok = n_chunks * LANES
    steps = seq // (n_chunks * seq_len)
    t = bsz * seq
    tok_map = lambda b, c: (b * steps + c, 0)
    state4 = lambda shape: pl.BlockSpec((n_seq,) + shape, lambda b, c: (b, 0, 0, 0))
    kern = functools.partial(_mlstm_kernel, n_seq=n_seq, seq_len=seq_len, n_chunks=n_chunks)
    return pl.pallas_call(
        kern,
        grid=(bsz // n_seq, steps),
        in_specs=[
            pl.BlockSpec((tok, A_QK), tok_map),
            pl.BlockSpec((tok, A_QK), lambda b, c: (b * steps + c, 1)),
            pl.BlockSpec((tok, A_V), lambda b, c: (b * steps + c, 1)),
            pl.BlockSpec((2 * A_HEADS, tok), lambda b, c: (0, b * steps + c)),
            pl.BlockSpec((2 * A_HEADS, 1), lambda b, c: (0, 0)),
            state4((A_HEADS, A_DK, A_DV + LANES)),
            state4((A_HEADS, 1, LANES)),
        ],
        out_specs=[
            pl.BlockSpec((tok, A_V), tok_map),
            state4((A_HEADS, A_DK, A_DV + LANES)),
            state4((A_HEADS, 1, LANES)),
        ],
        out_shape=[
            jax.ShapeDtypeStruct((t, A_V), F32),
            jax.ShapeDtypeStruct((bsz, A_HEADS, A_DK, A_DV + LANES), F32),
            jax.ShapeDtypeStruct((bsz, A_HEADS, 1, LANES), F32),
        ],
        scratch_shapes=[pltpu.VMEM((2 * A_HEADS, tok), F32)],
        compiler_params=_params("parallel", "arbitrary"),
        name="mlstm",
    )(u, u, u, gt, bias, c0, m0)


def _ple(h, p_ref, pw_ref, pn_ref, pg_ref):
    pp = jnp.dot(p_ref[...].astype(BF16), pw_ref[...], preferred_element_type=F32)
    hn = _rms(h, pn_ref[...])
    gate = _sigmoid(jnp.dot(hn.astype(BF16), pg_ref[...], preferred_element_type=F32))
    return h + pp * gate


def _a_post_kernel(hs_ref, o_ref, z_ref, x_ref, p_ref, hg_ref, wo_ref, pw_ref, pn_ref, pg_ref,
                   out_ref, y_ref):
    for h in range(A_HEADS):
        sl = slice(h * A_DV, (h + 1) * A_DV)
        y = _rms(hs_ref[:, sl], hg_ref[:, sl]) * _sigmoid(o_ref[:, sl].astype(F32))
        z = z_ref[:, sl].astype(F32)
        y_ref[:, sl] = (y * (z * _sigmoid(z))).astype(BF16)
    h1 = x_ref[...] + jnp.dot(y_ref[...], wo_ref[...], preferred_element_type=F32)
    out_ref[...] = _ple(h1, p_ref, pw_ref, pn_ref, pg_ref)


def _const_spec(shape):
    return pl.BlockSpec(shape, lambda i: (0,) * len(shape))


def _a_post(hs, u, x, p, hnorm, w_out, ple_w, ple_norm, ple_wg):
    t = x.shape[0]
    tm = _tile(t, 256)
    return pl.pallas_call(
        _a_post_kernel,
        grid=(t // tm,),
        in_specs=[
            pl.BlockSpec((tm, A_V), lambda i: (i, 0)),
            pl.BlockSpec((tm, A_V), lambda i: (i, 2)),
            pl.BlockSpec((tm, A_V), lambda i: (i, 3)),
            pl.BlockSpec((tm, D_MODEL), lambda i: (i, 0)),
            pl.BlockSpec((tm, PLE_DIM), lambda i: (i, 0)),
            _const_spec((1, A_V)),
            _const_spec((A_V, D_MODEL)),
            _const_spec((PLE_DIM, D_MODEL)),
            _const_spec((1, D_MODEL)),
            _const_spec((D_MODEL, D_MODEL)),
        ],
        out_specs=pl.BlockSpec((tm, D_MODEL), lambda i: (i, 0)),
        out_shape=jax.ShapeDtypeStruct((t, D_MODEL), F32),
        scratch_shapes=[pltpu.VMEM((tm, A_V), BF16)],
        compiler_params=_params("parallel"),
        name="a_post",
    )(hs, u, u, x, p, hnorm, w_out, ple_w, ple_norm, ple_wg)


def _head_ms(x, gm_ref):
    return jnp.dot((x * x).astype(BF16), gm_ref[...], preferred_element_type=F32) * (1.0 / B_DH)


def _kv_kernel(h_ref, g_ref, w_ref, wfh_ref, wfl_ref, wfth_ref, wftl_ref, bf_ref, bft_ref, kg_ref, gm_ref,
               k_ref, v_ref, kb_ref, vb_ref, vt_ref, lf_ref, lft_ref):
    xs = _split(_rms(h_ref[...], g_ref[...]))
    kv = jnp.dot(xs[0], w_ref[...], preferred_element_type=F32)
    k = kv[:, :B_W]
    v = kv[:, B_W:]
    kn = k * lax.rsqrt(_head_ms(k, gm_ref) + EPS) * kg_ref[...]
    k_ref[...] = kn
    v_ref[...] = v
    kb_ref[...] = kn.astype(BF16)
    vb_ref[...] = v.astype(BF16)
    vt_ref[...] = v.T.astype(BF16)
    fg = _dot_split(xs, (wfh_ref[...], wfl_ref[...]), NN)
    lf_ref[...] = _log_sigmoid(fg + bf_ref[...])
    fgt = _dot_split((wfth_ref[...], wftl_ref[...]), xs, NT)
    lft_ref[...] = _log_sigmoid(fgt + bft_ref[...])


def _kv_proj(h, g, w_kv, w_f, w_ft, b_f, b_ft, k_g, gmat):
    t = h.shape[0]
    tm = _tile(t, 512)
    row = lambda w: pl.BlockSpec((tm, w), lambda i: (i, 0))
    return pl.pallas_call(
        _kv_kernel,
        grid=(t // tm,),
        in_specs=[
            row(D_MODEL),
            _const_spec((1, D_MODEL)),
            _const_spec((D_MODEL, 2 * B_W)),
            _const_spec((D_MODEL, B_HEADS)),
            _const_spec((D_MODEL, B_HEADS)),
            _const_spec((B_HEADS, D_MODEL)),
            _const_spec((B_HEADS, D_MODEL)),
            _const_spec((1, B_HEADS)),
            _const_spec((B_HEADS, 1)),
            _const_spec((1, B_W)),
            _const_spec((B_W, B_W)),
        ],
        out_specs=[row(B_W), row(B_W), row(B_W), row(B_W),
                   pl.BlockSpec((B_W, tm), lambda i: (0, i)), row(B_HEADS),
                   pl.BlockSpec((B_HEADS, tm), lambda i: (0, i))],
        out_shape=[
            jax.ShapeDtypeStruct((t, B_W), F32),
            jax.ShapeDtypeStruct((t, B_W), F32),
            jax.ShapeDtypeStruct((t, B_W), BF16),
            jax.ShapeDtypeStruct((t, B_W), BF16),
            jax.ShapeDtypeStruct((B_W, t), BF16),
            jax.ShapeDtypeStruct((t, B_HEADS), F32),
            jax.ShapeDtypeStruct((B_HEADS, t), F32),
        ],
        compiler_params=_params("parallel"),
        name="kv_proj",
    )(h, g, w_kv, *w_f, *w_ft, b_f, b_ft, k_g, gmat)


def _q_kernel(h_ref, g_ref, w_ref, qg_ref, gm_ref, q_ref, z_ref):
    xn = _rms(h_ref[...], g_ref[...])
    qz = jnp.dot(xn.astype(BF16), w_ref[...], preferred_element_type=F32)
    q = qz[:, :B_W]
    z_ref[...] = qz[:, B_W:]
    qn = q * lax.rsqrt(_head_ms(q, gm_ref) + EPS) * qg_ref[...]
    q_ref[...] = (qn * (B_DH ** -0.5 * LOG2E)).astype(BF16)


def _q_proj(h, g, w_in, q_g, gmat):
    t = h.shape[0]
    tm = _tile(t, 512)
    row = lambda w: pl.BlockSpec((tm, w), lambda i: (i, 0))
    return pl.pallas_call(
        _q_kernel,
        grid=(t // tm,),
        in_specs=[row(D_MODEL), _const_spec((1, D_MODEL)), _const_spec((D_MODEL, 2 * B_W)),
                  _const_spec((1, B_W)), _const_spec((B_W, B_W))],
        out_specs=[row(B_W), row(B_W)],
        out_shape=[jax.ShapeDtypeStruct((t, B_W), BF16), jax.ShapeDtypeStruct((t, B_W), F32)],
        compiler_params=_params("parallel"),
        name="q_proj",
    )(h, g, w_in, q_g, gmat)


def _cumsum_kernel(x_ref, f_ref, carry_ref):
    @pl.when(pl.program_id(1) == 0)
    def _():
        carry_ref[...] = jnp.zeros_like(carry_ref)

    w = x_ref.shape[-1]
    f = _seg_cumsum_lanes(x_ref[...], w) + carry_ref[:, 0:1]
    f_ref[...] = f
    carry_ref[...] = jnp.broadcast_to(f[:, w - 1:w], carry_ref.shape)


def _cumsum_rows(x, bsz, width):
    tb = _tile(width, 2048) if width % LANES == 0 and (width // LANES) & (width // LANES - 1) == 0 else width
    nb = width // tb
    if x.ndim == 2:
        in_spec = pl.BlockSpec((B_HEADS, tb), lambda b, j: (0, b * nb + j))
    else:
        in_spec = pl.BlockSpec((None, B_HEADS, tb), lambda b, j: (b, 0, j))
    return pl.pallas_call(
        _cumsum_kernel,
        grid=(bsz, nb),
        in_specs=[in_spec],
        out_specs=pl.BlockSpec((None, B_HEADS, tb), lambda b, j: (b, 0, j)),
        out_shape=jax.ShapeDtypeStruct((bsz, B_HEADS, width), F32),
        scratch_shapes=[pltpu.VMEM((B_HEADS, LANES), F32)],
        compiler_params=_params("parallel", "arbitrary"),
        name="logf_cumsum",
    )(x)


def _cumsum_aug_kernel(x_ref, place_ref, f_ref, ka_ref, carry_ref):
    @pl.when(pl.program_id(1) == 0)
    def _():
        carry_ref[...] = jnp.zeros_like(carry_ref)

    w = x_ref.shape[-1]
    f = _seg_cumsum_lanes(x_ref[...], w) + carry_ref[:, 0:1]
    f_ref[...] = f
    carry_ref[...] = jnp.broadcast_to(f[:, w - 1:w], carry_ref.shape)
    rows = jnp.concatenate(_split3(f * LOG2E) + (jnp.ones((B_HEADS, w), BF16),), axis=0)
    ka_ref[...] = lax.dot_general(rows, place_ref[...], TN, preferred_element_type=F32).astype(BF16)


def _bias_placement():
    r = jnp.arange(4 * B_HEADS)[:, None]
    c = jnp.arange(B_W)[None, :]
    part, head = r // B_HEADS, r % B_HEADS
    lane0 = (head // 2) * LANES + BIAS_LANES * (head % 2)
    is_part = (part < 3) & (c == lane0 + part)
    is_one = (part == 3) & (c >= lane0 + 3) & (c < lane0 + 6)
    return (is_part | is_one).astype(BF16)


def _cumsum_aug(x, bsz, width):
    tb = _tile(width, 2048)
    nb = width // tb
    return pl.pallas_call(
        _cumsum_aug_kernel,
        grid=(bsz, nb),
        in_specs=[pl.BlockSpec((B_HEADS, tb), lambda b, j: (0, b * nb + j)),
                  pl.BlockSpec((4 * B_HEADS, B_W), lambda b, j: (0, 0))],
        out_specs=[pl.BlockSpec((None, B_HEADS, tb), lambda b, j: (b, 0, j)),
                   pl.BlockSpec((tb, B_W), lambda b, j: (b * nb + j, 0))],
        out_shape=[jax.ShapeDtypeStruct((bsz, B_HEADS, width), F32),
                   jax.ShapeDtypeStruct((bsz * width, B_W), BF16)],
        scratch_shapes=[pltpu.VMEM((B_HEADS, LANES), F32)],
        compiler_params=_params("parallel", "arbitrary"),
        name="logf_cumsum_aug",
    )(x, _bias_placement())


def _fox_bounded_kernel(bound_ref, q_ref, k_ref, ka_ref, vt_ref, f_ref, o_ref, qt_ref, acc_ref, l_ref,
                        p00, p01, p10, p11, *, tq):
    tk = tq // 2
    pbuf = ((p00, p01), (p10, p11))
    qi = pl.program_id(2)
    q0 = pl.multiple_of(qi * tq, tq)
    feat = lax.broadcasted_iota(jnp.int32, (LANES, tq), 0)
    qt = q_ref[...].astype(F32).T
    beta = _split3(f_ref[:, pl.ds(q0, tq)] * LOG2E - bound_ref[...])
    for hh in range(2):
        lo = BIAS_LANES * hh
        qt_ref[hh, 0:LANES, :] = jnp.where((feat >= hh * B_DH) & (feat < (hh + 1) * B_DH), qt, 0.0).astype(BF16)
        aug = jnp.where((feat >= lo) & (feat < lo + 3), -1.0, 0.0)
        for part in range(3):
            aug = jnp.where(feat == lo + 3 + part, beta[part][hh:hh + 1].astype(F32), aug)
        qt_ref[hh, LANES:2 * LANES, :] = aug.astype(BF16)
    acc_ref[...] = jnp.zeros_like(acc_ref)
    l_ref[...] = jnp.zeros_like(l_ref)
    kpos = lax.broadcasted_iota(jnp.int32, (tk, tq), 0)
    qpos = lax.broadcasted_iota(jnp.int32, (tk, tq), 1)
    diag_masks = (kpos <= qpos, (kpos + tk) <= qpos)

    def weights(k0, slot, causal=None):
        lhs = jnp.concatenate([k_ref[pl.ds(k0, tk), :], ka_ref[pl.ds(k0, tk), :]], axis=1)
        for hh in range(2):
            e = jnp.dot(lhs, qt_ref[hh], preferred_element_type=F32)
            if causal is not None:
                e = jnp.where(causal, e, NEG)
            p = jnp.exp2(e)
            pbuf[slot][hh][...] = p.astype(BF16)
            l_ref[hh] += jnp.sum(p, axis=0, keepdims=True)

    def accumulate(k0, slot):
        for hh in range(2):
            vt = vt_ref[hh * B_DH:(hh + 1) * B_DH, pl.ds(k0, tk)]
            acc_ref[hh] += jnp.dot(vt, pbuf[slot][hh][...], preferred_element_type=F32)

    def pair(k0, next_causal):
        weights(k0 + tk, 1)
        accumulate(k0, 0)
        weights(k0 + tq, 0, next_causal)
        accumulate(k0 + tk, 1)

    def finish():
        weights(q0 + tk, 1, diag_masks[1])
        accumulate(q0, 0)
        accumulate(q0 + tk, 1)
        o_t = jnp.concatenate(
            [acc_ref[hh] * (1.0 / l_ref[hh]) for hh in range(2)], axis=0)
        o_ref[...] = o_t.T

    @pl.when(qi == 0)
    def _():
        weights(q0, 0, diag_masks[0])
        finish()

    @pl.when(qi > 0)
    def _():
        weights(0, 0)
        n_pairs = qi - 1

        def unrolled(jj, carry):
            for u in range(PAIR_UNROLL):
                pair(pl.multiple_of((jj * PAIR_UNROLL + u) * tq, tq), None)
            return carry

        def single(jj, carry):
            pair(pl.multiple_of(jj * tq, tq), None)
            return carry

        lax.fori_loop(0, n_pairs // PAIR_UNROLL, unrolled, 0)
        lax.fori_loop((n_pairs // PAIR_UNROLL) * PAIR_UNROLL, n_pairs, single, 0)
        pair(q0 - tq, diag_masks[0])
        finish()


def _fox_prompt_bounded(q, kb, ka, vt, f_row, bound, bsz, seq):
    tq = _tile(seq, 512)
    nq = seq // tq
    kern = functools.partial(_fox_bounded_kernel, tq=tq)
    return pl.pallas_call(
        kern,
        grid=(bsz, B_PAIRS, nq),
        in_specs=[
            pl.BlockSpec((1, 1), lambda b, p, qi: (0, 0)),
            pl.BlockSpec((tq, LANES), lambda b, p, qi: (b * nq + qi, p)),
            pl.BlockSpec((seq, LANES), lambda b, p, qi: (b, p)),
            pl.BlockSpec((seq, LANES), lambda b, p, qi: (b, p)),
            pl.BlockSpec((LANES, seq), lambda b, p, qi: (p, b)),
            pl.BlockSpec((None, None, 2, seq), lambda b, p, qi: (b, p, 0, 0)),
        ],
        out_specs=pl.BlockSpec((tq, LANES), lambda b, p, qi: (b * nq + qi, p)),
        out_shape=jax.ShapeDtypeStruct((bsz * seq, B_W), F32),
        scratch_shapes=[
            pltpu.VMEM((2, 2 * LANES, tq), BF16),
            pltpu.VMEM((2, B_DH, tq), F32),
            pltpu.VMEM((2, 1, tq), F32),
        ] + [pltpu.VMEM((tq // 2, tq), BF16)] * 4,
        compiler_params=_params("parallel", "parallel", "arbitrary"),
        name="fox_prompt_bounded",
    )(bound, q, kb, ka, vt, f_row)


def _fox_kernel(q_ref, k_ref, vt_ref, f_ref, o_ref, qt_ref, m_ref, cm_ref, acc_ref,
                s00, s01, s10, s11, *, tq):
    tk = tq // 2
    sbuf = ((s00, s01), (s10, s11))
    qi = pl.program_id(2)
    q0 = pl.multiple_of(qi * tq, tq)
    feat = lax.broadcasted_iota(jnp.int32, (LANES, tq), 0)
    qt = q_ref[...].astype(F32).T
    qt_ref[0] = jnp.where(feat < B_DH, qt, 0.0).astype(BF16)
    qt_ref[1] = jnp.where(feat >= B_DH, qt, 0.0).astype(BF16)
    m_ref[...] = jnp.full_like(m_ref, NEG)
    acc_ref[...] = jnp.zeros_like(acc_ref)
    f_base = f_ref[:, pl.ds(q0, tq)][:, tq - 1:tq]
    ones = jnp.ones((ACC_ROWS - B_DH, tk), BF16)
    kpos = lax.broadcasted_iota(jnp.int32, (tk, tq), 0)
    qpos = lax.broadcasted_iota(jnp.int32, (tk, tq), 1)
    diag_masks = (kpos <= qpos, (kpos + tk) <= qpos)

    def scores(k0, slot, causal=None):
        kb = k_ref[pl.ds(k0, tk), :]
        fk = (f_ref[:, pl.ds(k0, tk)] - f_base) * LOG2E
        for hh in range(2):
            fk_col = jnp.concatenate(
                [jnp.broadcast_to(fk[hh:hh + 1, c * LANES:(c + 1) * LANES], (LANES, LANES)).T
                 for c in range(tk // LANES)], axis=0)
            t = (jnp.dot(kb, qt_ref[hh], preferred_element_type=F32)
                 - pltpu.repeat(fk_col, tq // LANES, axis=1))
            if causal is not None:
                t = jnp.where(causal, t, NEG)
            sbuf[slot][hh][...] = t
            cm_ref[slot, hh] = jnp.max(t, axis=0, keepdims=True)

    def update(k0, slot):
        for hh in range(2):
            m_prev = m_ref[hh]
            m_new = jnp.maximum(m_prev, cm_ref[slot, hh])
            alpha = jnp.exp2(m_prev - m_new)
            p = jnp.exp2(sbuf[slot][hh][...] - m_new).astype(BF16)
            vt = jnp.concatenate([vt_ref[hh * B_DH:(hh + 1) * B_DH, pl.ds(k0, tk)], ones], axis=0)
            acc_ref[hh] = alpha * acc_ref[hh] + jnp.dot(vt, p, preferred_element_type=F32)
            m_ref[hh] = m_new

    def pair(k0, next_causal):
        scores(k0 + tk, 1)
        update(k0, 0)
        scores(k0 + tq, 0, next_causal)
        update(k0 + tk, 1)

    @pl.when(qi == 0)
    def _():
        scores(q0, 0, diag_masks[0])

    @pl.when(qi > 0)
    def _():
        scores(0, 0)

        def body(jj, carry):
            pair(pl.multiple_of(jj * tq, tq), None)
            return carry

        lax.fori_loop(0, qi - 1, body, 0)
        pair(q0 - tq, diag_masks[0])

    scores(q0 + tk, 1, diag_masks[1])
    update(q0, 0)
    update(q0 + tk, 1)
    o_t = jnp.concatenate(
        [acc_ref[hh, 0:B_DH] * (1.0 / acc_ref[hh, B_DH:B_DH + 1]) for hh in range(2)], axis=0)
    o_ref[...] = o_t.T


def _fox_prompt(q, kb, vt, f_row, bsz, seq):
    tq = _tile(seq, 512)
    tk = tq // 2
    nq = seq // tq
    kern = functools.partial(_fox_kernel, tq=tq)
    return pl.pallas_call(
        kern,
        grid=(bsz, B_PAIRS, nq),
        in_specs=[
            pl.BlockSpec((tq, LANES), lambda b, p, qi: (b * nq + qi, p)),
            pl.BlockSpec((seq, LANES), lambda b, p, qi: (b, p)),
            pl.BlockSpec((LANES, seq), lambda b, p, qi: (p, b)),
            pl.BlockSpec((None, None, 2, seq), lambda b, p, qi: (b, p, 0, 0)),
        ],
        out_specs=pl.BlockSpec((tq, LANES), lambda b, p, qi: (b * nq + qi, p)),
        out_shape=jax.ShapeDtypeStruct((bsz * seq, B_W), F32),
        scratch_shapes=[
            pltpu.VMEM((2, LANES, tq), BF16),
            pltpu.VMEM((2, 1, tq), F32),
            pltpu.VMEM((2, 2, 1, tq), F32),
            pltpu.VMEM((2, ACC_ROWS, tq), F32),
        ] + [pltpu.VMEM((tk, tq), F32)] * 4,
        compiler_params=_params("parallel", "parallel", "arbitrary"),
        name="fox_prompt",
    )(q, kb, vt, f_row)


def _fox_dec_kernel(q_ref, kt_ref, vt_ref, kn_ref, vn_ref, fn_ref, fc_ref, o_ref,
                    fqc_ref, m_ref, l_ref, a_ref, acc_ref, s_ref, p_ref, *, sq, nkb):
    kj = pl.program_id(1)

    @pl.when(kj == 0)
    def _():
        r = lax.broadcasted_iota(jnp.int32, (sq, LANES), 0)
        c = lax.broadcasted_iota(jnp.int32, (sq, LANES), 1)
        for h in range(B_HEADS):
            fqc_ref[h] = jnp.sum(jnp.where(r == c, fn_ref[h:h + 1, :], 0.0), axis=1, keepdims=True)
        m_ref[...] = jnp.full_like(m_ref, NEG)
        l_ref[...] = jnp.zeros_like(l_ref)
        acc_ref[...] = jnp.zeros_like(acc_ref)

    def attend(width, score, f_keys, values, valid):
        for h in range(B_HEADS):
            s_ref[h, :, 0:width] = score(h)
        for h in range(B_HEADS):
            t = s_ref[h, :, 0:width] + (fqc_ref[h] - f_keys(h)) * LOG2E
            if valid is not None:
                t = jnp.where(valid, t, NEG)
            m_prev = m_ref[h]
            m_new = jnp.maximum(m_prev, jnp.max(t, axis=1, keepdims=True))
            alpha = jnp.exp2(m_prev - m_new)
            p = jnp.exp2(t - m_new)
            l_ref[h] = alpha * l_ref[h] + jnp.sum(p, axis=1, keepdims=True)
            m_ref[h] = m_new
            a_ref[h] = alpha
            p_ref[h, :, 0:width] = p.astype(BF16)
        for h in range(B_HEADS):
            acc_ref[h] = a_ref[h] * acc_ref[h] + values(h, p_ref[h, :, 0:width])

    @pl.when(kj < nkb)
    def _():
        attend(
            kt_ref.shape[-1],
            lambda h: jnp.dot(q_ref[:, h * B_DH:(h + 1) * B_DH], kt_ref[h].astype(BF16),
                              preferred_element_type=F32),
            lambda h: fc_ref[h:h + 1, :],
            lambda h, p: lax.dot_general(p, vt_ref[h].astype(BF16), NT, preferred_element_type=F32),
            None)

    @pl.when(kj == nkb)
    def _():
        i = lax.broadcasted_iota(jnp.int32, (sq, LANES), 0)
        j = lax.broadcasted_iota(jnp.int32, (sq, LANES), 1)
        head = lambda h: slice(h * B_DH, (h + 1) * B_DH)
        attend(
            LANES,
            lambda h: lax.dot_general(q_ref[:, head(h)], kn_ref[:, head(h)], NT, preferred_element_type=F32),
            lambda h: fn_ref[h:h + 1, :],
            lambda h, p: jnp.dot(p, vn_ref[:, head(h)], preferred_element_type=F32),
            j <= i)
        for h in range(B_HEADS):
            o_ref[:, head(h)] = acc_ref[h] * (1.0 / l_ref[h])


def _fox_decode(q, cache_kt, cache_vt, kn, vn, f_new, f_cache, bsz, sq):
    past = cache_kt.shape[3]
    tk = _tile(past, 2048)
    nkb = past // tk
    cmap = lambda b, kj: (b, 0, 0, jnp.minimum(kj, nkb - 1))
    kern = functools.partial(_fox_dec_kernel, sq=sq, nkb=nkb)
    return pl.pallas_call(
        kern,
        grid=(bsz, nkb + 1),
        in_specs=[
            pl.BlockSpec((None, sq, B_W), lambda b, kj: (b, 0, 0)),
            pl.BlockSpec((None, B_HEADS, B_DH, tk), cmap),
            pl.BlockSpec((None, B_HEADS, B_DH, tk), cmap),
            pl.BlockSpec((None, LANES, B_W), lambda b, kj: (b, 0, 0)),
            pl.BlockSpec((None, LANES, B_W), lambda b, kj: (b, 0, 0)),
            pl.BlockSpec((None, B_HEADS, LANES), lambda b, kj: (b, 0, 0)),
            pl.BlockSpec((None, B_HEADS, tk), lambda b, kj: (b, 0, jnp.minimum(kj, nkb - 1))),
        ],
        out_specs=pl.BlockSpec((None, sq, B_W), lambda b, kj: (b, 0, 0)),
        out_shape=jax.ShapeDtypeStruct((bsz, sq, B_W), F32),
        scratch_shapes=[
            pltpu.VMEM((B_HEADS, sq, 1), F32),
            pltpu.VMEM((B_HEADS, sq, 1), F32),
            pltpu.VMEM((B_HEADS, sq, 1), F32),
            pltpu.VMEM((B_HEADS, sq, 1), F32),
            pltpu.VMEM((B_HEADS, sq, B_DH), F32),
            pltpu.VMEM((B_HEADS, sq, tk), F32),
            pltpu.VMEM((B_HEADS, sq, tk), BF16),
        ],
        compiler_params=_params("parallel", "arbitrary"),
        name="fox_decode",
    )(q, cache_kt, cache_vt, kn, vn, f_new, f_cache)


def _b_post_kernel(o_ref, z_ref, h_ref, p_ref, wo_ref, pw_ref, pn_ref, pg_ref, out_ref):
    z = z_ref[...]
    y = o_ref[...] * (z * _sigmoid(z))
    h2 = h_ref[...] + jnp.dot(y.astype(BF16), wo_ref[...], preferred_element_type=F32)
    out_ref[...] = _ple(h2, p_ref, pw_ref, pn_ref, pg_ref)


def _b_post(o, z, h, p, w_out, ple_w, ple_norm, ple_wg):
    t = h.shape[0]
    tm = _tile(t, 512)
    row = lambda w: pl.BlockSpec((tm, w), lambda i: (i, 0))
    return pl.pallas_call(
        _b_post_kernel,
        grid=(t // tm,),
        in_specs=[row(B_W), row(B_W), row(D_MODEL), row(PLE_DIM),
                  _const_spec((B_W, D_MODEL)), _const_spec((PLE_DIM, D_MODEL)),
                  _const_spec((1, D_MODEL)), _const_spec((D_MODEL, D_MODEL))],
        out_specs=row(D_MODEL),
        out_shape=jax.ShapeDtypeStruct((t, D_MODEL), F32),
        compiler_params=_params("parallel"),
        name="b_post",
    )(o, z, h, p, w_out, ple_w, ple_norm, ple_wg)


def _prep_weights(a_norm, a_w_in, a_b_i, a_b_f, a_hnorm, a_w_out, kv_norm, kv_w, kv_b_f, k_norm,
                  b_norm, b_w_in, q_norm, b_w_out, ple_w, ple_norm, ple_w_g):
    head_of_lane = jnp.arange(B_W, dtype=jnp.int32) // B_DH
    return dict(
        a_norm=a_norm[0].reshape(1, D_MODEL),
        a_w_main=a_w_in[0][:, :A_MAIN].astype(BF16),
        a_w_gt=_split(a_w_in[0][:, A_MAIN:].T),
        a_bias=jnp.concatenate([a_b_i[0], a_b_f[0]]).reshape(2 * A_HEADS, 1),
        a_hnorm=a_hnorm[0].reshape(1, A_V),
        a_w_out=a_w_out[0].astype(BF16),
        kv_norm=kv_norm.reshape(1, D_MODEL),
        kv_w=kv_w[:, :2 * B_W].astype(BF16),
        kv_wf=_split(kv_w[:, 2 * B_W:]),
        kv_wft=_split(kv_w[:, 2 * B_W:].T),
        kv_bf=kv_b_f.reshape(1, B_HEADS),
        kv_bft=kv_b_f.reshape(B_HEADS, 1),
        k_norm=jnp.tile(k_norm, B_HEADS).reshape(1, B_W),
        gmat=(head_of_lane[:, None] == head_of_lane[None, :]).astype(BF16),
        b_norm=b_norm[0].reshape(1, D_MODEL),
        b_w_in=b_w_in[0].astype(BF16),
        q_norm=jnp.tile(q_norm[0], B_HEADS).reshape(1, B_W),
        b_w_out=b_w_out[0].astype(BF16),
        ple_w=ple_w.astype(BF16),
        ple_norm=ple_norm.reshape(-1, 1, D_MODEL),
        ple_wg=ple_w_g.astype(BF16),
    )


def _trunk(x, p, c0, n0, m0, past, w):
    bsz, seq, _ = x.shape
    t = bsz * seq
    xf = x.reshape(t, D_MODEL)
    p0 = p[0].reshape(t, PLE_DIM)
    p1 = p[1].reshape(t, PLE_DIM)

    u, gt = _a_inproj(xf, w["a_norm"], w["a_w_main"], w["a_w_gt"])
    c0x = jnp.concatenate([c0, jnp.broadcast_to(n0[..., None], (bsz, A_HEADS, A_DK, LANES))], axis=-1)
    m0r = jnp.broadcast_to(m0.reshape(bsz, A_HEADS, 1, 1), (bsz, A_HEADS, 1, LANES))
    hs, c_ext, m_new = _mlstm(u, gt, w["a_bias"], c0x, m0r, bsz, seq)
    c_new, n_new = c_ext[..., :A_DV], c_ext[..., A_DV]
    h1 = _a_post(hs, u, xf, p0, w["a_hnorm"], w["a_w_out"], w["ple_w"][0], w["ple_norm"][0], w["ple_wg"][0])

    k, v, kb, vb, vt, lf, lft = _kv_proj(h1, w["kv_norm"], w["kv_w"], w["kv_wf"], w["kv_wft"],
                                     w["kv_bf"], w["kv_bft"], w["k_norm"], w["gmat"])
    q, z = _q_proj(h1, w["b_norm"], w["b_w_in"], w["q_norm"], w["gmat"])
    if past is None:
        f_row, ka = _cumsum_aug(lft, bsz, seq)
        f_row = f_row.reshape(bsz, B_PAIRS, 2, seq)
        bound = (8.0 * LOG2E * 1.01) * jnp.max(jnp.abs(w["q_norm"])) * jnp.max(jnp.abs(w["k_norm"])) + 0.05
        o = lax.cond(
            bound <= MAX_BOUND,
            lambda: _fox_prompt_bounded(q, kb, ka, vt, f_row, bound.reshape(1, 1), bsz, seq),
            lambda: _fox_prompt(q, kb, vt, f_row, bsz, seq))
    else:
        cache_k, cache_v, cache_lf = past
        plen = cache_k.shape[1]
        assert seq <= LANES
        width = -(-(plen + seq) // LANES) * LANES
        lf_rows = jnp.concatenate(
            [jnp.swapaxes(cache_lf.astype(F32), 1, 2),
             jnp.swapaxes(lft.reshape(B_HEADS, bsz, seq), 0, 1),
             jnp.zeros((bsz, B_HEADS, width - plen - seq), F32)], axis=2)
        f_all = _cumsum_rows(lf_rows, bsz, width)
        f_new = jnp.pad(f_all[..., plen:plen + seq], ((0, 0), (0, 0), (0, LANES - seq)))
        pad_new = lambda a: jnp.pad(a.reshape(bsz, seq, B_W), ((0, 0), (0, LANES - seq), (0, 0)))
        to_t = lambda a: jnp.transpose(a.astype(F32), (0, 2, 3, 1))
        o = _fox_decode(q.reshape(bsz, seq, B_W), to_t(cache_k), to_t(cache_v), pad_new(kb), pad_new(vb),
                        f_new, f_all[..., :plen], bsz, seq).reshape(t, B_W)
    y = _b_post(o, z, h1, p1, w["b_w_out"], w["ple_w"][1], w["ple_norm"][1], w["ple_wg"][1])

    return (y.reshape(bsz, seq, D_MODEL),
            c_new[None], n_new.reshape(1, bsz, A_HEADS, A_DK), m_new[None, :, :, 0, 0],
            k.reshape(bsz, seq, B_HEADS, B_DH), v.reshape(bsz, seq, B_HEADS, B_DH),
            lf.reshape(bsz, seq, B_HEADS))


def kernel(x_prompt, x_sample, cache_k, cache_v, cache_logf, state_C, state_n, state_m, p_prompt, p_sample,
           a_norm, a_w_in, a_b_i, a_b_f, a_hnorm, a_w_out, kv_norm, kv_w, kv_b_f, k_norm,
           b_norm, b_w_in, q_norm, b_w_out, ple_w, ple_norm, ple_w_g):
    w = _prep_weights(a_norm, a_w_in, a_b_i, a_b_f, a_hnorm, a_w_out, kv_norm, kv_w, kv_b_f, k_norm,
                      b_norm, b_w_in, q_norm, b_w_out, ple_w, ple_norm, ple_w_g)
    bsz = x_prompt.shape[0]
    c0 = jnp.zeros((bsz, A_HEADS, A_DK, A_DV), F32)
    n0 = jnp.zeros((bsz, A_HEADS, A_DK), F32)
    m0 = jnp.zeros((bsz, A_HEADS), F32)
    prompt = _trunk(x_prompt, p_prompt, c0, n0, m0, None, w)
    sample = _trunk(x_sample, p_sample, state_C[0].astype(F32), state_n[0].astype(F32),
                    state_m[0].astype(F32), (cache_k, cache_v, cache_logf), w)
    return (prompt[0], sample[0]) + prompt[1:] + sample[1:]
```

```python
import functools

import jax
import jax.numpy as jnp
from jax import lax
from jax.experimental import pallas as pl
from jax.experimental.pallas import tpu as pltpu

D_MODEL = 1024
A_HEADS = 8
A_DK = 128
A_DV = 256
A_QK = A_HEADS * A_DK
A_V = A_HEADS * A_DV
A_MAIN = 2 * A_QK + 3 * A_V
B_HEADS = 16
B_DH = 64
B_W = B_HEADS * B_DH
B_PAIRS = B_HEADS // 2
PLE_DIM = 256
EPS = 1e-6
NEG = -1e30
LOG2E = 1.4426950408889634
ACC_ROWS = B_DH + 16
BIAS_LANES = 6
MAX_BOUND = 60.0
PAIR_UNROLL = 4

LANES = 128
VMEM_LIMIT = 56 * 1024 * 1024

F32 = jnp.float32
BF16 = jnp.bfloat16
NN = (((1,), (0,)), ((), ()))
NT = (((1,), (1,)), ((), ()))
TN = (((0,), (0,)), ((), ()))


def _params(*sem):
    return pltpu.CompilerParams(dimension_semantics=sem, vmem_limit_bytes=VMEM_LIMIT)


def _tile(n, pref):
    t = min(n, pref)
    while n % t:
        t //= 2
    return t


def _rms(x, g):
    ms = jnp.mean(x * x, axis=-1, keepdims=True)
    return x * lax.rsqrt(ms + EPS) * g


def _log_sigmoid(x):
    return jnp.minimum(x, 0.0) - jnp.log1p(jnp.exp(-jnp.abs(x)))


def _sigmoid(x):
    return 0.5 * jnp.tanh(0.5 * x) + 0.5


def _diag_to_col(row, eye):
    n = eye.shape[0]
    return jnp.sum(jnp.where(eye, jnp.broadcast_to(row, (n, n)), 0.0), axis=1, keepdims=True)


def _seg_cumsum_lanes(x, seg):
    w = x.shape[-1]
    pos = lax.broadcasted_iota(jnp.int32, x.shape, x.ndim - 1) % seg
    k = 1
    while k < min(seg, w):
        x = x + jnp.where(pos >= k, pltpu.roll(x, k, x.ndim - 1), 0.0)
        k *= 2
    return x


def _split(x):
    hi = x.astype(BF16)
    return hi, (x - hi.astype(F32)).astype(BF16)


def _dot_split(a, b, dims):
    dot = functools.partial(lax.dot_general, dimension_numbers=dims, preferred_element_type=F32)
    return dot(a[0], b[0]) + (dot(a[0], b[1]) + dot(a[1], b[0]))


def _split3(x):
    hi = x.astype(BF16)
    r = x - hi.astype(F32)
    mid = r.astype(BF16)
    return hi, mid, (r - mid.astype(F32)).astype(BF16)


def _a_inproj_kernel(x_ref, g_ref, w_ref, wgh_ref, wgl_ref, u_ref, gt_ref, xn_ref):
    @pl.when(pl.program_id(1) == 0)
    def _():
        xn = _rms(x_ref[...], g_ref[...])
        xs = _split(xn)
        xn_ref[...] = xs[0]
        gt_ref[...] = _dot_split((wgh_ref[...], wgl_ref[...]), xs, NT)

    u_ref[...] = jnp.dot(xn_ref[...], w_ref[...], preferred_element_type=F32).astype(BF16)


def _a_inproj(x, g, w_main, w_gt):
    t = x.shape[0]
    tm = _tile(t, 1024)
    tn = 1024
    gate_spec = pl.BlockSpec((2 * A_HEADS, D_MODEL), lambda i, j: (0, 0))
    return pl.pallas_call(
        _a_inproj_kernel,
        grid=(t // tm, A_MAIN // tn),
        in_specs=[
            pl.BlockSpec((tm, D_MODEL), lambda i, j: (i, 0)),
            pl.BlockSpec((1, D_MODEL), lambda i, j: (0, 0)),
            pl.BlockSpec((D_MODEL, tn), lambda i, j: (0, j)),
            gate_spec,
            gate_spec,
        ],
        out_specs=[
            pl.BlockSpec((tm, tn), lambda i, j: (i, j)),
            pl.BlockSpec((2 * A_HEADS, tm), lambda i, j: (0, i)),
        ],
        out_shape=[
            jax.ShapeDtypeStruct((t, A_MAIN), BF16),
            jax.ShapeDtypeStruct((2 * A_HEADS, t), F32),
        ],
        scratch_shapes=[pltpu.VMEM((tm, D_MODEL), BF16)],
        compiler_params=_params("parallel", "arbitrary"),
        name="a_inproj",
    )(x, g, w_main, *w_gt)


def _mlstm_kernel(q_ref, k_ref, v_ref, g_ref, bias_ref, c0_ref, n0_ref, m0_ref,
                  h_ref, c_out_ref, n_out_ref, m_ref, c_ref, gate_ref, *, n_seq, seq_len, n_chunks, steps):
    R = LANES

    @pl.when(pl.program_id(1) == 0)
    def _():
        c_ref[:, :, :, 0:A_DV] = c0_ref[...]
        c_ref[:, :, :, A_DV:] = jnp.broadcast_to(n0_ref[...], (n_seq, A_HEADS, A_DK, LANES))
        m_ref[...] = m0_ref[...]

    row = lax.broadcasted_iota(jnp.int32, (R, R), 0)
    col = lax.broadcasted_iota(jnp.int32, (R, R), 1)
    eye = row == col
    valid = (col <= row) & ((row // seq_len) == (col // seq_len))
    row1 = lax.broadcasted_iota(jnp.int32, (R, 1), 0)
    lane1 = lax.broadcasted_iota(jnp.int32, (1, R), 1)
    bias = bias_ref[...]
    scale = A_DK ** -0.5
    ones = jnp.ones((R, LANES), BF16)
    g = g_ref[...]
    gate_ref[0:A_HEADS, :] = g[0:A_HEADS] + bias[0:A_HEADS]
    gate_ref[A_HEADS:, :] = _seg_cumsum_lanes(_log_sigmoid(g[A_HEADS:] + bias[A_HEADS:]), seq_len)

    def chunk(r0):
        rows = pl.ds(r0, R)
        ig_all = gate_ref[0:A_HEADS, rows]
        b_all = gate_ref[A_HEADS:, rows]
        for h in range(A_HEADS):
            qb = q_ref[rows, h * A_DK:(h + 1) * A_DK]
            kb = k_ref[rows, h * A_DK:(h + 1) * A_DK]
            vb = jnp.concatenate([v_ref[rows, h * A_DV:(h + 1) * A_DV], ones], axis=1)
            b_row = b_all[h:h + 1]
            ig_row = ig_all[h:h + 1]
            b_col = _diag_to_col(b_row, eye)
            m_prev = [m_ref[s, h][:, 0:1] for s in range(n_seq)]
            m_prev_col = m_prev[0]
            for s in range(1, n_seq):
                m_prev_col = jnp.where(row1 >= s * seq_len, m_prev[s], m_prev_col)
            d = jnp.where(valid, b_col - b_row + ig_row, NEG)
            inter = b_col + m_prev_col
            m_t = jnp.maximum(inter, jnp.max(d, axis=1, keepdims=True))
            p = jnp.exp(d - m_t)
            w_inter = jnp.exp(inter - m_t)
            qk = lax.dot_general(qb, kb, NT, preferred_element_type=F32) * scale
            s_mat = qk * p
            qc = [jnp.dot(qb[s * seq_len:(s + 1) * seq_len], c_ref[s, h].astype(BF16),
                          preferred_element_type=F32) for s in range(n_seq)]
            qc = qc[0] if n_seq == 1 else jnp.concatenate(qc, axis=0)
            nd = w_inter * qc + jnp.dot(s_mat.astype(BF16), vb, preferred_element_type=F32)
            den = nd[:, A_DV:A_DV + 1]
            h_ref[rows, h * A_DV:(h + 1) * A_DV] = nd[:, 0:A_DV] * (1.0 / jnp.maximum(jnp.abs(den), jnp.exp(-m_t)))
            b_last = [b_row[:, (s + 1) * seq_len - 1:(s + 1) * seq_len] for s in range(n_seq)]
            m_new = [m_t[(s + 1) * seq_len - 1:(s + 1) * seq_len] for s in range(n_seq)]
            b_last_row, m_new_row = b_last[0], m_new[0]
            for s in range(1, n_seq):
                b_last_row = jnp.where(lane1 >= s * seq_len, b_last[s], b_last_row)
                m_new_row = jnp.where(lane1 >= s * seq_len, m_new[s], m_new_row)
            w_s_row = jnp.exp(b_last_row - b_row + ig_row - m_new_row)
            ksb = (kb.astype(F32) * (_diag_to_col(w_s_row, eye) * scale)).astype(BF16)
            for s in range(n_seq):
                sl = slice(s * seq_len, (s + 1) * seq_len)
                w_old = jnp.exp(b_last[s] + m_prev[s] - m_new[s])
                c_ref[s, h] = w_old * c_ref[s, h] + lax.dot_general(
                    ksb[sl], vb[sl], TN, preferred_element_type=F32)
                m_ref[s, h] = jnp.broadcast_to(m_new[s], (1, LANES))

    if n_chunks == 1:
        chunk(0)
    else:
        def body(ci, carry):
            chunk(pl.multiple_of(ci * R, R))
            return carry
        lax.fori_loop(0, n_chunks, body, 0)

    @pl.when(pl.program_id(1) == steps - 1)
    def _():
        c_out_ref[...] = c_ref[:, :, :, 0:A_DV]
        n_out_ref[...] = c_ref[:, :, :, A_DV:A_DV + 1]


def _mlstm(u, gt, bias, c0, n0, m0, bsz, seq):
    if seq % LANES == 0:
        seq_len, n_seq = LANES, 1
        n_chunks = _tile(seq // LANES, 4)
    else:
        assert LANES % seq == 0 and bsz % (LANES // seq) == 0, (bsz, seq)
        seq_len, n_seq, n_chunks = seq, LANES // seq, 1
    tok = n_chunks * LANES
    steps = seq // (n_chunks * seq_len)
    t = bsz * seq
    tok_map = lambda b, c: (b * steps + c, 0)
    state4 = lambda shape: pl.BlockSpec((n_seq,) + shape, lambda b, c: (b, 0, 0, 0))
    kern = functools.partial(_mlstm_kernel, n_seq=n_seq, seq_len=seq_len, n_chunks=n_chunks, steps=steps)
    return pl.pallas_call(
        kern,
        grid=(bsz // n_seq, steps),
        in_specs=[
            pl.BlockSpec((tok, A_QK), tok_map),
            pl.BlockSpec((tok, A_QK), lambda b, c: (b * steps + c, 1)),
            pl.BlockSpec((tok, A_V), lambda b, c: (b * steps + c, 1)),
            pl.BlockSpec((2 * A_HEADS, tok), lambda b, c: (0, b * steps + c)),
            pl.BlockSpec((2 * A_HEADS, 1), lambda b, c: (0, 0)),
            state4((A_HEADS, A_DK, A_DV)),
            state4((A_HEADS, A_DK, 1)),
            state4((A_HEADS, 1, LANES)),
        ],
        out_specs=[
            pl.BlockSpec((tok, A_V), tok_map),
            state4((A_HEADS, A_DK, A_DV)),
            state4((A_HEADS, A_DK, 1)),
            state4((A_HEADS, 1, LANES)),
        ],
        out_shape=[
            jax.ShapeDtypeStruct((t, A_V), F32),
            jax.ShapeDtypeStruct((bsz, A_HEADS, A_DK, A_DV), F32),
            jax.ShapeDtypeStruct((bsz, A_HEADS, A_DK, 1), F32),
            jax.ShapeDtypeStruct((bsz, A_HEADS, 1, LANES), F32),
        ],
        scratch_shapes=[pltpu.VMEM((n_seq, A_HEADS, A_DK, A_DV + LANES), F32),
                        pltpu.VMEM((2 * A_HEADS, tok), F32)],
        compiler_params=_params("parallel", "arbitrary"),
        name="mlstm",
    )(u, u, u, gt, bias, c0, n0, m0)


def _ple(h, p_ref, pw_ref, pn_ref, pg_ref):
    pp = jnp.dot(p_ref[...].astype(BF16), pw_ref[...], preferred_element_type=F32)
    hn = _rms(h, pn_ref[...])
    gate = _sigmoid(jnp.dot(hn.astype(BF16), pg_ref[...], preferred_element_type=F32))
    return h + pp * gate


def _a_post_kernel(hs_ref, o_ref, z_ref, x_ref, p_ref, hg_ref, wo_ref, pw_ref, pn_ref, pg_ref,
                   out_ref, y_ref):
    for h in range(A_HEADS):
        sl = slice(h * A_DV, (h + 1) * A_DV)
        y = _rms(hs_ref[:, sl], hg_ref[:, sl]) * _sigmoid(o_ref[:, sl].astype(F32))
        z = z_ref[:, sl].astype(F32)
        y_ref[:, sl] = (y * (z * _sigmoid(z))).astype(BF16)
    h1 = x_ref[...] + jnp.dot(y_ref[...], wo_ref[...], preferred_element_type=F32)
    out_ref[...] = _ple(h1, p_ref, pw_ref, pn_ref, pg_ref)


def _const_spec(shape):
    return pl.BlockSpec(shape, lambda i: (0,) * len(shape))


def _a_post(hs, u, x, p, layer, hnorm, w_out, ple_w, ple_norm, ple_wg):
    t = x.shape[0]
    tm = _tile(t, 256)
    return pl.pallas_call(
        _a_post_kernel,
        grid=(t // tm,),
        in_specs=[
            pl.BlockSpec((tm, A_V), lambda i: (i, 0)),
            pl.BlockSpec((tm, A_V), lambda i: (i, 2)),
            pl.BlockSpec((tm, A_V), lambda i: (i, 3)),
            pl.BlockSpec((tm, D_MODEL), lambda i: (i, 0)),
            pl.BlockSpec((None, tm, PLE_DIM), lambda i: (layer, i, 0)),
            _const_spec((1, A_V)),
            _const_spec((A_V, D_MODEL)),
            _const_spec((PLE_DIM, D_MODEL)),
            _const_spec((1, D_MODEL)),
            _const_spec((D_MODEL, D_MODEL)),
        ],
        out_specs=pl.BlockSpec((tm, D_MODEL), lambda i: (i, 0)),
        out_shape=jax.ShapeDtypeStruct((t, D_MODEL), F32),
        scratch_shapes=[pltpu.VMEM((tm, A_V), BF16)],
        compiler_params=_params("parallel"),
        name="a_post",
    )(hs, u, u, x, p, hnorm, w_out, ple_w, ple_norm, ple_wg)


def _head_ms(x, gm_ref):
    return jnp.dot((x * x).astype(BF16), gm_ref[...], preferred_element_type=F32) * (1.0 / B_DH)


def _kv_kernel(h_ref, g_ref, w_ref, wfh_ref, wfl_ref, wfth_ref, wftl_ref, bf_ref, bft_ref, kg_ref, gm_ref,
               k_ref, v_ref, kb_ref, vb_ref, vt_ref, lf_ref, lft_ref):
    xs = _split(_rms(h_ref[...], g_ref[...]))
    kv = jnp.dot(xs[0], w_ref[...], preferred_element_type=F32)
    k = kv[:, :B_W]
    v = kv[:, B_W:]
    kn = k * lax.rsqrt(_head_ms(k, gm_ref) + EPS) * kg_ref[...]
    k_ref[...] = kn
    v_ref[...] = v
    kb_ref[...] = kn.astype(BF16)
    vb_ref[...] = v.astype(BF16)
    vt_ref[...] = v.T.astype(BF16)
    fg = _dot_split(xs, (wfh_ref[...], wfl_ref[...]), NN)
    lf_ref[...] = _log_sigmoid(fg + bf_ref[...])
    fgt = _dot_split((wfth_ref[...], wftl_ref[...]), xs, NT)
    lft_ref[...] = _log_sigmoid(fgt + bft_ref[...])


def _kv_proj(h, g, w_kv, w_f, w_ft, b_f, b_ft, k_g, gmat):
    t = h.shape[0]
    tm = _tile(t, 512)
    row = lambda w: pl.BlockSpec((tm, w), lambda i: (i, 0))
    return pl.pallas_call(
        _kv_kernel,
        grid=(t // tm,),
        in_specs=[
            row(D_MODEL),
            _const_spec((1, D_MODEL)),
            _const_spec((D_MODEL, 2 * B_W)),
            _const_spec((D_MODEL, B_HEADS)),
            _const_spec((D_MODEL, B_HEADS)),
            _const_spec((B_HEADS, D_MODEL)),
            _const_spec((B_HEADS, D_MODEL)),
            _const_spec((1, B_HEADS)),
            _const_spec((B_HEADS, 1)),
            _const_spec((1, B_W)),
            _const_spec((B_W, B_W)),
        ],
        out_specs=[row(B_W), row(B_W), row(B_W), row(B_W),
                   pl.BlockSpec((B_W, tm), lambda i: (0, i)), row(B_HEADS),
                   pl.BlockSpec((B_HEADS, tm), lambda i: (0, i))],
        out_shape=[
            jax.ShapeDtypeStruct((t, B_W), F32),
            jax.ShapeDtypeStruct((t, B_W), F32),
            jax.ShapeDtypeStruct((t, B_W), BF16),
            jax.ShapeDtypeStruct((t, B_W), BF16),
            jax.ShapeDtypeStruct((B_W, t), BF16),
            jax.ShapeDtypeStruct((t, B_HEADS), F32),
            jax.ShapeDtypeStruct((B_HEADS, t), F32),
        ],
        compiler_params=_params("parallel"),
        name="kv_proj",
    )(h, g, w_kv, *w_f, *w_ft, b_f, b_ft, k_g, gmat)


def _q_kernel(h_ref, g_ref, w_ref, qg_ref, gm_ref, q_ref, z_ref):
    xn = _rms(h_ref[...], g_ref[...])
    qz = jnp.dot(xn.astype(BF16), w_ref[...], preferred_element_type=F32)
    q = qz[:, :B_W]
    z_ref[...] = qz[:, B_W:]
    qn = q * lax.rsqrt(_head_ms(q, gm_ref) + EPS) * qg_ref[...]
    q_ref[...] = (qn * (B_DH ** -0.5 * LOG2E)).astype(BF16)


def _q_proj(h, g, w_in, q_g, gmat):
    t = h.shape[0]
    tm = _tile(t, 512)
    row = lambda w: pl.BlockSpec((tm, w), lambda i: (i, 0))
    return pl.pallas_call(
        _q_kernel,
        grid=(t // tm,),
        in_specs=[row(D_MODEL), _const_spec((1, D_MODEL)), _const_spec((D_MODEL, 2 * B_W)),
                  _const_spec((1, B_W)), _const_spec((B_W, B_W))],
        out_specs=[row(B_W), row(B_W)],
        out_shape=[jax.ShapeDtypeStruct((t, B_W), BF16), jax.ShapeDtypeStruct((t, B_W), F32)],
        compiler_params=_params("parallel"),
        name="q_proj",
    )(h, g, w_in, q_g, gmat)


def _cumsum_kernel(x_ref, f_ref, carry_ref):
    @pl.when(pl.program_id(1) == 0)
    def _():
        carry_ref[...] = jnp.zeros_like(carry_ref)

    w = x_ref.shape[-1]
    f = _seg_cumsum_lanes(x_ref[...], w) + carry_ref[:, 0:1]
    f_ref[...] = f
    carry_ref[...] = jnp.broadcast_to(f[:, w - 1:w], carry_ref.shape)


def _cumsum_rows(x, bsz, width):
    tb = _tile(width, 2048) if width % LANES == 0 and (width // LANES) & (width // LANES - 1) == 0 else width
    nb = width // tb
    if x.ndim == 2:
        in_spec = pl.BlockSpec((B_HEADS, tb), lambda b, j: (0, b * nb + j))
    else:
        in_spec = pl.BlockSpec((None, B_HEADS, tb), lambda b, j: (b, 0, j))
    return pl.pallas_call(
        _cumsum_kernel,
        grid=(bsz, nb),
        in_specs=[in_spec],
        out_specs=pl.BlockSpec((None, B_HEADS, tb), lambda b, j: (b, 0, j)),
        out_shape=jax.ShapeDtypeStruct((bsz, B_HEADS, width), F32),
        scratch_shapes=[pltpu.VMEM((B_HEADS, LANES), F32)],
        compiler_params=_params("parallel", "arbitrary"),
        name="logf_cumsum",
    )(x)


def _cumsum_aug_kernel(x_ref, place_ref, f_ref, ka_ref, carry_ref):
    @pl.when(pl.program_id(1) == 0)
    def _():
        carry_ref[...] = jnp.zeros_like(carry_ref)

    w = x_ref.shape[-1]
    f = _seg_cumsum_lanes(x_ref[...], w) + carry_ref[:, 0:1]
    f_ref[...] = f
    carry_ref[...] = jnp.broadcast_to(f[:, w - 1:w], carry_ref.shape)
    rows = jnp.concatenate(_split3(f * LOG2E) + (jnp.ones((B_HEADS, w), BF16),), axis=0)
    ka_ref[...] = lax.dot_general(rows, place_ref[...], TN, preferred_element_type=F32).astype(BF16)


def _bias_placement():
    r = jnp.arange(4 * B_HEADS)[:, None]
    c = jnp.arange(B_W)[None, :]
    part, head = r // B_HEADS, r % B_HEADS
    lane0 = (head // 2) * LANES + BIAS_LANES * (head % 2)
    is_part = (part < 3) & (c == lane0 + part)
    is_one = (part == 3) & (c >= lane0 + 3) & (c < lane0 + 6)
    return (is_part | is_one).astype(BF16)


def _cumsum_aug(x, bsz, width):
    tb = _tile(width, 2048)
    nb = width // tb
    return pl.pallas_call(
        _cumsum_aug_kernel,
        grid=(bsz, nb),
        in_specs=[pl.BlockSpec((B_HEADS, tb), lambda b, j: (0, b * nb + j)),
                  pl.BlockSpec((4 * B_HEADS, B_W), lambda b, j: (0, 0))],
        out_specs=[pl.BlockSpec((None, B_HEADS, tb), lambda b, j: (b, 0, j)),
                   pl.BlockSpec((tb, B_W), lambda b, j: (b * nb + j, 0))],
        out_shape=[jax.ShapeDtypeStruct((bsz, B_HEADS, width), F32),
                   jax.ShapeDtypeStruct((bsz * width, B_W), BF16)],
        scratch_shapes=[pltpu.VMEM((B_HEADS, LANES), F32)],
        compiler_params=_params("parallel", "arbitrary"),
        name="logf_cumsum_aug",
    )(x, _bias_placement())


def _fox_bounded_kernel(bound_ref, q_ref, k_ref, ka_ref, vt_ref, f_ref, o_ref, qt_ref, acc_ref,
                        p00, p01, p10, p11, *, tq):
    tk = tq // 2
    pbuf = ((p00, p01), (p10, p11))
    qi = pl.program_id(2)
    q0 = pl.multiple_of(qi * tq, tq)
    feat = lax.broadcasted_iota(jnp.int32, (LANES, tq), 0)
    qt = q_ref[...].astype(F32).T
    beta = _split3(f_ref[:, pl.ds(q0, tq)] * LOG2E - bound_ref[...])
    for hh in range(2):
        lo = BIAS_LANES * hh
        qt_ref[hh, 0:LANES, :] = jnp.where((feat >= hh * B_DH) & (feat < (hh + 1) * B_DH), qt, 0.0).astype(BF16)
        aug = jnp.where((feat >= lo) & (feat < lo + 3), -1.0, 0.0)
        for part in range(3):
            aug = jnp.where(feat == lo + 3 + part, beta[part][hh:hh + 1].astype(F32), aug)
        qt_ref[hh, LANES:2 * LANES, :] = aug.astype(BF16)
    acc_ref[...] = jnp.zeros_like(acc_ref)
    ones = jnp.ones((ACC_ROWS - B_DH, tk), BF16)
    kpos = lax.broadcasted_iota(jnp.int32, (tk, tq), 0)
    qpos = lax.broadcasted_iota(jnp.int32, (tk, tq), 1)
    diag_masks = (kpos <= qpos, (kpos + tk) <= qpos)

    def weights(k0, slot, causal=None):
        lhs = jnp.concatenate([k_ref[pl.ds(k0, tk), :], ka_ref[pl.ds(k0, tk), :]], axis=1)
        for hh in range(2):
            e = jnp.dot(lhs, qt_ref[hh], preferred_element_type=F32)
            if causal is not None:
                e = jnp.where(causal, e, NEG)
            pbuf[slot][hh][...] = jnp.exp2(e).astype(BF16)

    def accumulate(k0, slot):
        for hh in range(2):
            vt = jnp.concatenate([vt_ref[hh * B_DH:(hh + 1) * B_DH, pl.ds(k0, tk)], ones], axis=0)
            acc_ref[hh] += jnp.dot(vt, pbuf[slot][hh][...], preferred_element_type=F32)

    def pair(k0, next_causal):
        weights(k0 + tk, 1)
        accumulate(k0, 0)
        weights(k0 + tq, 0, next_causal)
        accumulate(k0 + tk, 1)

    def finish():
        weights(q0 + tk, 1, diag_masks[1])
        accumulate(q0, 0)
        accumulate(q0 + tk, 1)
        o_t = jnp.concatenate(
            [acc_ref[hh, 0:B_DH] * (1.0 / acc_ref[hh, B_DH:B_DH + 1]) for hh in range(2)], axis=0)
        o_ref[...] = o_t.T

    @pl.when(qi == 0)
    def _():
        weights(q0, 0, diag_masks[0])
        finish()

    @pl.when(qi > 0)
    def _():
        weights(0, 0)
        n_pairs = qi - 1

        def unrolled(jj, carry):
            for u in range(PAIR_UNROLL):
                pair(pl.multiple_of((jj * PAIR_UNROLL + u) * tq, tq), None)
            return carry

        def single(jj, carry):
            pair(pl.multiple_of(jj * tq, tq), None)
            return carry

        lax.fori_loop(0, n_pairs // PAIR_UNROLL, unrolled, 0)
        lax.fori_loop((n_pairs // PAIR_UNROLL) * PAIR_UNROLL, n_pairs, single, 0)
        pair(q0 - tq, diag_masks[0])
        finish()


def _fox_prompt_bounded(q, kb, ka, vt, f_row, bound, bsz, seq):
    tq = _tile(seq, 512)
    nq = seq // tq
    kern = functools.partial(_fox_bounded_kernel, tq=tq)
    return pl.pallas_call(
        kern,
        grid=(bsz, B_PAIRS, nq),
        in_specs=[
            pl.BlockSpec((1, 1), lambda b, p, qi: (0, 0)),
            pl.BlockSpec((tq, LANES), lambda b, p, qi: (b * nq + qi, p)),
            pl.BlockSpec((seq, LANES), lambda b, p, qi: (b, p)),
            pl.BlockSpec((seq, LANES), lambda b, p, qi: (b, p)),
            pl.BlockSpec((LANES, seq), lambda b, p, qi: (p, b)),
            pl.BlockSpec((None, None, 2, seq), lambda b, p, qi: (b, p, 0, 0)),
        ],
        out_specs=pl.BlockSpec((tq, LANES), lambda b, p, qi: (b * nq + qi, p)),
        out_shape=jax.ShapeDtypeStruct((bsz * seq, B_W), F32),
        scratch_shapes=[
            pltpu.VMEM((2, 2 * LANES, tq), BF16),
            pltpu.VMEM((2, ACC_ROWS, tq), F32),
        ] + [pltpu.VMEM((tq // 2, tq), BF16)] * 4,
        compiler_params=_params("parallel", "parallel", "arbitrary"),
        name="fox_prompt_bounded",
    )(bound, q, kb, ka, vt, f_row)


def _fox_kernel(q_ref, k_ref, vt_ref, f_ref, o_ref, qt_ref, m_ref, cm_ref, acc_ref,
                s00, s01, s10, s11, *, tq):
    tk = tq // 2
    sbuf = ((s00, s01), (s10, s11))
    qi = pl.program_id(2)
    q0 = pl.multiple_of(qi * tq, tq)
    feat = lax.broadcasted_iota(jnp.int32, (LANES, tq), 0)
    qt = q_ref[...].astype(F32).T
    qt_ref[0] = jnp.where(feat < B_DH, qt, 0.0).astype(BF16)
    qt_ref[1] = jnp.where(feat >= B_DH, qt, 0.0).astype(BF16)
    m_ref[...] = jnp.full_like(m_ref, NEG)
    acc_ref[...] = jnp.zeros_like(acc_ref)
    f_base = f_ref[:, pl.ds(q0, tq)][:, tq - 1:tq]
    ones = jnp.ones((ACC_ROWS - B_DH, tk), BF16)
    kpos = lax.broadcasted_iota(jnp.int32, (tk, tq), 0)
    qpos = lax.broadcasted_iota(jnp.int32, (tk, tq), 1)
    diag_masks = (kpos <= qpos, (kpos + tk) <= qpos)

    def scores(k0, slot, causal=None):
        kb = k_ref[pl.ds(k0, tk), :]
        fk = (f_ref[:, pl.ds(k0, tk)] - f_base) * LOG2E
        for hh in range(2):
            fk_col = jnp.concatenate(
                [jnp.broadcast_to(fk[hh:hh + 1, c * LANES:(c + 1) * LANES], (LANES, LANES)).T
                 for c in range(tk // LANES)], axis=0)
            t = (jnp.dot(kb, qt_ref[hh], preferred_element_type=F32)
                 - pltpu.repeat(fk_col, tq // LANES, axis=1))
            if causal is not None:
                t = jnp.where(causal, t, NEG)
            sbuf[slot][hh][...] = t
            cm_ref[slot, hh] = jnp.max(t, axis=0, keepdims=True)

    def update(k0, slot):
        for hh in range(2):
            m_prev = m_ref[hh]
            m_new = jnp.maximum(m_prev, cm_ref[slot, hh])
            alpha = jnp.exp2(m_prev - m_new)
            p = jnp.exp2(sbuf[slot][hh][...] - m_new).astype(BF16)
            vt = jnp.concatenate([vt_ref[hh * B_DH:(hh + 1) * B_DH, pl.ds(k0, tk)], ones], axis=0)
            acc_ref[hh] = alpha * acc_ref[hh] + jnp.dot(vt, p, preferred_element_type=F32)
            m_ref[hh] = m_new

    def pair(k0, next_causal):
        scores(k0 + tk, 1)
        update(k0, 0)
        scores(k0 + tq, 0, next_causal)
        update(k0 + tk, 1)

    @pl.when(qi == 0)
    def _():
        scores(q0, 0, diag_masks[0])

    @pl.when(qi > 0)
    def _():
        scores(0, 0)

        def body(jj, carry):
            pair(pl.multiple_of(jj * tq, tq), None)
            return carry

        lax.fori_loop(0, qi - 1, body, 0)
        pair(q0 - tq, diag_masks[0])

    scores(q0 + tk, 1, diag_masks[1])
    update(q0, 0)
    update(q0 + tk, 1)
    o_t = jnp.concatenate(
        [acc_ref[hh, 0:B_DH] * (1.0 / acc_ref[hh, B_DH:B_DH + 1]) for hh in range(2)], axis=0)
    o_ref[...] = o_t.T


def _fox_prompt(q, kb, vt, f_row, bsz, seq):
    tq = _tile(seq, 512)
    tk = tq // 2
    nq = seq // tq
    kern = functools.partial(_fox_kernel, tq=tq)
    return pl.pallas_call(
        kern,
        grid=(bsz, B_PAIRS, nq),
        in_specs=[
            pl.BlockSpec((tq, LANES), lambda b, p, qi: (b * nq + qi, p)),
            pl.BlockSpec((seq, LANES), lambda b, p, qi: (b, p)),
            pl.BlockSpec((LANES, seq), lambda b, p, qi: (p, b)),
            pl.BlockSpec((None, None, 2, seq), lambda b, p, qi: (b, p, 0, 0)),
        ],
        out_specs=pl.BlockSpec((tq, LANES), lambda b, p, qi: (b * nq + qi, p)),
        out_shape=jax.ShapeDtypeStruct((bsz * seq, B_W), F32),
        scratch_shapes=[
            pltpu.VMEM((2, LANES, tq), BF16),
            pltpu.VMEM((2, 1, tq), F32),
            pltpu.VMEM((2, 2, 1, tq), F32),
            pltpu.VMEM((2, ACC_ROWS, tq), F32),
        ] + [pltpu.VMEM((tk, tq), F32)] * 4,
        compiler_params=_params("parallel", "parallel", "arbitrary"),
        name="fox_prompt",
    )(q, kb, vt, f_row)


def _fox_dec_kernel(q_ref, kt_ref, vt_ref, kn_ref, vn_ref, fn_ref, fc_ref, o_ref,
                    fqc_ref, m_ref, l_ref, a_ref, acc_ref, s_ref, p_ref, *, sq, nkb):
    kj = pl.program_id(1)

    @pl.when(kj == 0)
    def _():
        r = lax.broadcasted_iota(jnp.int32, (sq, LANES), 0)
        c = lax.broadcasted_iota(jnp.int32, (sq, LANES), 1)
        for h in range(B_HEADS):
            fqc_ref[h] = jnp.sum(jnp.where(r == c, fn_ref[h:h + 1, :], 0.0), axis=1, keepdims=True)
        m_ref[...] = jnp.full_like(m_ref, NEG)
        l_ref[...] = jnp.zeros_like(l_ref)
        acc_ref[...] = jnp.zeros_like(acc_ref)

    def attend(width, score, f_keys, values, valid):
        for h in range(B_HEADS):
            s_ref[h, :, 0:width] = score(h)
        for h in range(B_HEADS):
            t = s_ref[h, :, 0:width] + (fqc_ref[h] - f_keys(h)) * LOG2E
            if valid is not None:
                t = jnp.where(valid, t, NEG)
            m_prev = m_ref[h]
            m_new = jnp.maximum(m_prev, jnp.max(t, axis=1, keepdims=True))
            alpha = jnp.exp2(m_prev - m_new)
            p = jnp.exp2(t - m_new)
            l_ref[h] = alpha * l_ref[h] + jnp.sum(p, axis=1, keepdims=True)
            m_ref[h] = m_new
            a_ref[h] = alpha
            p_ref[h, :, 0:width] = p.astype(BF16)
        for h in range(B_HEADS):
            acc_ref[h] = a_ref[h] * acc_ref[h] + values(h, p_ref[h, :, 0:width])

    @pl.when(kj < nkb)
    def _():
        attend(
            kt_ref.shape[-1],
            lambda h: jnp.dot(q_ref[:, h * B_DH:(h + 1) * B_DH], kt_ref[h].astype(BF16),
                              preferred_element_type=F32),
            lambda h: fc_ref[h:h + 1, :],
            lambda h, p: lax.dot_general(p, vt_ref[h].astype(BF16), NT, preferred_element_type=F32),
            None)

    @pl.when(kj == nkb)
    def _():
        i = lax.broadcasted_iota(jnp.int32, (sq, LANES), 0)
        j = lax.broadcasted_iota(jnp.int32, (sq, LANES), 1)
        head = lambda h: slice(h * B_DH, (h + 1) * B_DH)
        attend(
            LANES,
            lambda h: lax.dot_general(q_ref[:, head(h)], kn_ref[:, head(h)], NT, preferred_element_type=F32),
            lambda h: fn_ref[h:h + 1, :],
            lambda h, p: jnp.dot(p, vn_ref[:, head(h)], preferred_element_type=F32),
            j <= i)
        for h in range(B_HEADS):
            o_ref[:, head(h)] = acc_ref[h] * (1.0 / l_ref[h])


def _fox_decode(q, cache_kt, cache_vt, kn, vn, f_new, f_cache, bsz, sq):
    past = cache_kt.shape[3]
    tk = _tile(past, 2048)
    nkb = past // tk
    cmap = lambda b, kj: (b, 0, 0, jnp.minimum(kj, nkb - 1))
    kern = functools.partial(_fox_dec_kernel, sq=sq, nkb=nkb)
    return pl.pallas_call(
        kern,
        grid=(bsz, nkb + 1),
        in_specs=[
            pl.BlockSpec((None, sq, B_W), lambda b, kj: (b, 0, 0)),
            pl.BlockSpec((None, B_HEADS, B_DH, tk), cmap),
            pl.BlockSpec((None, B_HEADS, B_DH, tk), cmap),
            pl.BlockSpec((None, LANES, B_W), lambda b, kj: (b, 0, 0)),
            pl.BlockSpec((None, LANES, B_W), lambda b, kj: (b, 0, 0)),
            pl.BlockSpec((None, B_HEADS, LANES), lambda b, kj: (b, 0, 0)),
            pl.BlockSpec((None, B_HEADS, tk), lambda b, kj: (b, 0, jnp.minimum(kj, nkb - 1))),
        ],
        out_specs=pl.BlockSpec((None, sq, B_W), lambda b, kj: (b, 0, 0)),
        out_shape=jax.ShapeDtypeStruct((bsz, sq, B_W), F32),
        scratch_shapes=[
            pltpu.VMEM((B_HEADS, sq, 1), F32),
            pltpu.VMEM((B_HEADS, sq, 1), F32),
            pltpu.VMEM((B_HEADS, sq, 1), F32),
            pltpu.VMEM((B_HEADS, sq, 1), F32),
            pltpu.VMEM((B_HEADS, sq, B_DH), F32),
            pltpu.VMEM((B_HEADS, sq, tk), F32),
            pltpu.VMEM((B_HEADS, sq, tk), BF16),
        ],
        compiler_params=_params("parallel", "arbitrary"),
        name="fox_decode",
    )(q, cache_kt, cache_vt, kn, vn, f_new, f_cache)


def _b_post_kernel(o_ref, z_ref, h_ref, p_ref, wo_ref, pw_ref, pn_ref, pg_ref, out_ref):
    z = z_ref[...]
    y = o_ref[...] * (z * _sigmoid(z))
    h2 = h_ref[...] + jnp.dot(y.astype(BF16), wo_ref[...], preferred_element_type=F32)
    out_ref[...] = _ple(h2, p_ref, pw_ref, pn_ref, pg_ref)


def _b_post(o, z, h, p, layer, w_out, ple_w, ple_norm, ple_wg):
    t = h.shape[0]
    tm = _tile(t, 512)
    row = lambda w: pl.BlockSpec((tm, w), lambda i: (i, 0))
    return pl.pallas_call(
        _b_post_kernel,
        grid=(t // tm,),
        in_specs=[row(B_W), row(B_W), row(D_MODEL), pl.BlockSpec((None, tm, PLE_DIM), lambda i: (layer, i, 0)),
                  _const_spec((B_W, D_MODEL)), _const_spec((PLE_DIM, D_MODEL)),
                  _const_spec((1, D_MODEL)), _const_spec((D_MODEL, D_MODEL))],
        out_specs=row(D_MODEL),
        out_shape=jax.ShapeDtypeStruct((t, D_MODEL), F32),
        compiler_params=_params("parallel"),
        name="b_post",
    )(o, z, h, p, w_out, ple_w, ple_norm, ple_wg)


def _prep_weights(a_norm, a_w_in, a_b_i, a_b_f, a_hnorm, a_w_out, kv_norm, kv_w, kv_b_f, k_norm,
                  b_norm, b_w_in, q_norm, b_w_out, ple_w, ple_norm, ple_w_g):
    head_of_lane = jnp.arange(B_W, dtype=jnp.int32) // B_DH
    return dict(
        a_norm=a_norm[0].reshape(1, D_MODEL),
        a_w_main=a_w_in[0][:, :A_MAIN].astype(BF16),
        a_w_gt=_split(a_w_in[0][:, A_MAIN:].T),
        a_bias=jnp.concatenate([a_b_i[0], a_b_f[0]]).reshape(2 * A_HEADS, 1),
        a_hnorm=a_hnorm[0].reshape(1, A_V),
        a_w_out=a_w_out[0].astype(BF16),
        kv_norm=kv_norm.reshape(1, D_MODEL),
        kv_w=kv_w[:, :2 * B_W].astype(BF16),
        kv_wf=_split(kv_w[:, 2 * B_W:]),
        kv_wft=_split(kv_w[:, 2 * B_W:].T),
        kv_bf=kv_b_f.reshape(1, B_HEADS),
        kv_bft=kv_b_f.reshape(B_HEADS, 1),
        k_norm=jnp.tile(k_norm, B_HEADS).reshape(1, B_W),
        gmat=(head_of_lane[:, None] == head_of_lane[None, :]).astype(BF16),
        b_norm=b_norm[0].reshape(1, D_MODEL),
        b_w_in=b_w_in[0].astype(BF16),
        q_norm=jnp.tile(q_norm[0], B_HEADS).reshape(1, B_W),
        b_w_out=b_w_out[0].astype(BF16),
        ple_w=ple_w.astype(BF16),
        ple_norm=ple_norm.reshape(-1, 1, D_MODEL),
        ple_wg=ple_w_g.astype(BF16),
    )


def _trunk(x, p, c0, n0, m0, past, w):
    bsz, seq, _ = x.shape
    t = bsz * seq
    xf = x.reshape(t, D_MODEL)
    pf = p.reshape(p.shape[0], t, PLE_DIM)

    u, gt = _a_inproj(xf, w["a_norm"], w["a_w_main"], w["a_w_gt"])
    m0r = jnp.broadcast_to(m0.reshape(bsz, A_HEADS, 1, 1), (bsz, A_HEADS, 1, LANES))
    hs, c_new, n_new, m_new = _mlstm(u, gt, w["a_bias"], c0, n0[..., None], m0r, bsz, seq)
    h1 = _a_post(hs, u, xf, pf, 0, w["a_hnorm"], w["a_w_out"], w["ple_w"][0], w["ple_norm"][0], w["ple_wg"][0])

    k, v, kb, vb, vt, lf, lft = _kv_proj(h1, w["kv_norm"], w["kv_w"], w["kv_wf"], w["kv_wft"],
                                     w["kv_bf"], w["kv_bft"], w["k_norm"], w["gmat"])
    q, z = _q_proj(h1, w["b_norm"], w["b_w_in"], w["q_norm"], w["gmat"])
    if past is None:
        f_row, ka = _cumsum_aug(lft, bsz, seq)
        f_row = f_row.reshape(bsz, B_PAIRS, 2, seq)
        bound = (8.0 * LOG2E * 1.01) * jnp.max(jnp.abs(w["q_norm"])) * jnp.max(jnp.abs(w["k_norm"])) + 0.05
        o = lax.cond(
            bound <= MAX_BOUND,
            lambda: _fox_prompt_bounded(q, kb, ka, vt, f_row, bound.reshape(1, 1), bsz, seq),
            lambda: _fox_prompt(q, kb, vt, f_row, bsz, seq))
    else:
        cache_k, cache_v, cache_lf = past
        plen = cache_k.shape[1]
        assert seq <= LANES
        width = -(-(plen + seq) // LANES) * LANES
        lf_rows = jnp.concatenate(
            [jnp.swapaxes(cache_lf.astype(F32), 1, 2),
             jnp.swapaxes(lft.reshape(B_HEADS, bsz, seq), 0, 1),
             jnp.zeros((bsz, B_HEADS, width - plen - seq), F32)], axis=2)
        f_all = _cumsum_rows(lf_rows, bsz, width)
        f_new = jnp.pad(f_all[..., plen:plen + seq], ((0, 0), (0, 0), (0, LANES - seq)))
        pad_new = lambda a: jnp.pad(a.reshape(bsz, seq, B_W), ((0, 0), (0, LANES - seq), (0, 0)))
        to_t = lambda a: jnp.transpose(a.astype(F32), (0, 2, 3, 1))
        o = _fox_decode(q.reshape(bsz, seq, B_W), to_t(cache_k), to_t(cache_v), pad_new(kb), pad_new(vb),
                        f_new, f_all[..., :plen], bsz, seq).reshape(t, B_W)
    y = _b_post(o, z, h1, pf, 1, w["b_w_out"], w["ple_w"][1], w["ple_norm"][1], w["ple_wg"][1])

    return (y.reshape(bsz, seq, D_MODEL),
            c_new[None], n_new.reshape(1, bsz, A_HEADS, A_DK), m_new[None, :, :, 0, 0],
            k.reshape(bsz, seq, B_HEADS, B_DH), v.reshape(bsz, seq, B_HEADS, B_DH),
            lf.reshape(bsz, seq, B_HEADS))


def kernel(x_prompt, x_sample, cache_k, cache_v, cache_logf, state_C, state_n, state_m, p_prompt, p_sample,
           a_norm, a_w_in, a_b_i, a_b_f, a_hnorm, a_w_out, kv_norm, kv_w, kv_b_f, k_norm,
           b_norm, b_w_in, q_norm, b_w_out, ple_w, ple_norm, ple_w_g):
    w = _prep_weights(a_norm, a_w_in, a_b_i, a_b_f, a_hnorm, a_w_out, kv_norm, kv_w, kv_b_f, k_norm,
                      b_norm, b_w_in, q_norm, b_w_out, ple_w, ple_norm, ple_w_g)
    bsz = x_prompt.shape[0]
    c0 = jnp.zeros((bsz, A_HEADS, A_DK, A_DV), F32)
    n0 = jnp.zeros((bsz, A_HEADS, A_DK), F32)
    m0 = jnp.zeros((bsz, A_HEADS), F32)
    prompt = _trunk(x_prompt, p_prompt, c0, n0, m0, None, w)
    sample = _trunk(x_sample, p_sample, state_C[0].astype(F32), state_n[0].astype(F32),
                    state_m[0].astype(F32), (cache_k, cache_v, cache_logf), w)
    return (prompt[0], sample[0]) + prompt[1:] + sample[1:]
```

```python
import functools

import jax
import jax.numpy as jnp
from jax import lax
from jax.experimental import pallas as pl
from jax.experimental.pallas import tpu as pltpu

D_MODEL = 1024
A_HEADS = 8
A_DK = 128
A_DV = 256
A_QK = A_HEADS * A_DK
A_V = A_HEADS * A_DV
A_MAIN = 2 * A_QK + 3 * A_V
B_HEADS = 16
B_DH = 64
B_W = B_HEADS * B_DH
B_PAIRS = B_HEADS // 2
PLE_DIM = 256
EPS = 1e-6
NEG = -1e30
LOG2E = 1.4426950408889634
ACC_ROWS = B_DH + 16
BIAS_LANES = 6
MAX_BOUND = 60.0
PAIR_UNROLL = 4

LANES = 128
VMEM_LIMIT = 56 * 1024 * 1024

F32 = jnp.float32
BF16 = jnp.bfloat16
NN = (((1,), (0,)), ((), ()))
NT = (((1,), (1,)), ((), ()))
TN = (((0,), (0,)), ((), ()))


def _params(*sem):
    return pltpu.CompilerParams(dimension_semantics=sem, vmem_limit_bytes=VMEM_LIMIT)


def _tile(n, pref):
    t = min(n, pref)
    while n % t:
        t //= 2
    return t


def _rms(x, g):
    ms = jnp.mean(x * x, axis=-1, keepdims=True)
    return x * lax.rsqrt(ms + EPS) * g


def _log_sigmoid(x):
    return jnp.minimum(x, 0.0) - jnp.log1p(jnp.exp(-jnp.abs(x)))


def _sigmoid(x):
    return 0.5 * jnp.tanh(0.5 * x) + 0.5


def _diag_to_col(row, eye):
    n = eye.shape[0]
    return jnp.sum(jnp.where(eye, jnp.broadcast_to(row, (n, n)), 0.0), axis=1, keepdims=True)


def _seg_cumsum_lanes(x, seg):
    w = x.shape[-1]
    pos = lax.broadcasted_iota(jnp.int32, x.shape, x.ndim - 1) % seg
    k = 1
    while k < min(seg, w):
        x = x + jnp.where(pos >= k, pltpu.roll(x, k, x.ndim - 1), 0.0)
        k *= 2
    return x


def _split(x):
    hi = x.astype(BF16)
    return hi, (x - hi.astype(F32)).astype(BF16)


def _dot_split(a, b, dims):
    dot = functools.partial(lax.dot_general, dimension_numbers=dims, preferred_element_type=F32)
    return dot(a[0], b[0]) + (dot(a[0], b[1]) + dot(a[1], b[0]))


def _split3(x):
    hi = x.astype(BF16)
    r = x - hi.astype(F32)
    mid = r.astype(BF16)
    return hi, mid, (r - mid.astype(F32)).astype(BF16)


def _a_inproj_kernel(x_ref, g_ref, w_ref, wgh_ref, wgl_ref, u_ref, gt_ref, xn_ref):
    @pl.when(pl.program_id(1) == 0)
    def _():
        xn = _rms(x_ref[...], g_ref[...])
        xs = _split(xn)
        xn_ref[...] = xs[0]
        gt_ref[...] = _dot_split((wgh_ref[...], wgl_ref[...]), xs, NT)

    u_ref[...] = jnp.dot(xn_ref[...], w_ref[...], preferred_element_type=F32).astype(BF16)


def _a_inproj(x, g, w_main, w_gt):
    t = x.shape[0]
    tm = _tile(t, 1024)
    tn = 1024
    gate_spec = pl.BlockSpec((2 * A_HEADS, D_MODEL), lambda i, j: (0, 0))
    return pl.pallas_call(
        _a_inproj_kernel,
        grid=(t // tm, A_MAIN // tn),
        in_specs=[
            pl.BlockSpec((tm, D_MODEL), lambda i, j: (i, 0)),
            pl.BlockSpec((1, D_MODEL), lambda i, j: (0, 0)),
            pl.BlockSpec((D_MODEL, tn), lambda i, j: (0, j)),
            gate_spec,
            gate_spec,
        ],
        out_specs=[
            pl.BlockSpec((tm, tn), lambda i, j: (i, j)),
            pl.BlockSpec((2 * A_HEADS, tm), lambda i, j: (0, i)),
        ],
        out_shape=[
            jax.ShapeDtypeStruct((t, A_MAIN), BF16),
            jax.ShapeDtypeStruct((2 * A_HEADS, t), F32),
        ],
        scratch_shapes=[pltpu.VMEM((tm, D_MODEL), BF16)],
        compiler_params=_params("parallel", "arbitrary"),
        name="a_inproj",
    )(x, g, w_main, *w_gt)


def _mlstm_kernel(q_ref, k_ref, v_ref, g_ref, bias_ref, c0_ref, n0_ref, m0_ref,
                  h_ref, c_out_ref, n_out_ref, m_ref, c_ref, gate_ref, *, n_seq, seq_len, n_chunks, steps):
    R = LANES

    @pl.when(pl.program_id(1) == 0)
    def _():
        c_ref[:, :, :, 0:A_DV] = c0_ref[...]
        c_ref[:, :, :, A_DV:] = jnp.broadcast_to(n0_ref[...], (n_seq, A_HEADS, A_DK, LANES))
        m_ref[...] = m0_ref[...]

    row = lax.broadcasted_iota(jnp.int32, (R, R), 0)
    col = lax.broadcasted_iota(jnp.int32, (R, R), 1)
    eye = row == col
    valid = (col <= row) & ((row // seq_len) == (col // seq_len))
    row1 = lax.broadcasted_iota(jnp.int32, (R, 1), 0)
    lane1 = lax.broadcasted_iota(jnp.int32, (1, R), 1)
    bias = bias_ref[...]
    scale = A_DK ** -0.5
    ones = jnp.ones((R, LANES), BF16)
    g = g_ref[...]
    gate_ref[0:A_HEADS, :] = g[0:A_HEADS] + bias[0:A_HEADS]
    gate_ref[A_HEADS:, :] = _seg_cumsum_lanes(_log_sigmoid(g[A_HEADS:] + bias[A_HEADS:]), seq_len)

    def chunk(r0):
        rows = pl.ds(r0, R)
        ig_all = gate_ref[0:A_HEADS, rows]
        b_all = gate_ref[A_HEADS:, rows]
        for h in range(A_HEADS):
            qb = q_ref[rows, h * A_DK:(h + 1) * A_DK]
            kb = k_ref[rows, h * A_DK:(h + 1) * A_DK]
            vb = jnp.concatenate([v_ref[rows, h * A_DV:(h + 1) * A_DV], ones], axis=1)
            b_row = b_all[h:h + 1]
            ig_row = ig_all[h:h + 1]
            b_col = _diag_to_col(b_row, eye)
            m_prev = [m_ref[s, h][:, 0:1] for s in range(n_seq)]
            m_prev_col = m_prev[0]
            for s in range(1, n_seq):
                m_prev_col = jnp.where(row1 >= s * seq_len, m_prev[s], m_prev_col)
            d = jnp.where(valid, b_col - b_row + ig_row, NEG)
            inter = b_col + m_prev_col
            m_t = jnp.maximum(inter, jnp.max(d, axis=1, keepdims=True))
            p = jnp.exp(d - m_t)
            w_inter = jnp.exp(inter - m_t)
            qk = lax.dot_general(qb, kb, NT, preferred_element_type=F32) * scale
            s_mat = qk * p
            qc = [jnp.dot(qb[s * seq_len:(s + 1) * seq_len], c_ref[s, h].astype(BF16),
                          preferred_element_type=F32) for s in range(n_seq)]
            qc = qc[0] if n_seq == 1 else jnp.concatenate(qc, axis=0)
            nd = w_inter * qc + jnp.dot(s_mat.astype(BF16), vb, preferred_element_type=F32)
            den = nd[:, A_DV:A_DV + 1]
            h_ref[rows, h * A_DV:(h + 1) * A_DV] = nd[:, 0:A_DV] * (1.0 / jnp.maximum(jnp.abs(den), jnp.exp(-m_t)))
            b_last = [b_row[:, (s + 1) * seq_len - 1:(s + 1) * seq_len] for s in range(n_seq)]
            m_new = [m_t[(s + 1) * seq_len - 1:(s + 1) * seq_len] for s in range(n_seq)]
            b_last_row, m_new_row = b_last[0], m_new[0]
            for s in range(1, n_seq):
                b_last_row = jnp.where(lane1 >= s * seq_len, b_last[s], b_last_row)
                m_new_row = jnp.where(lane1 >= s * seq_len, m_new[s], m_new_row)
            w_s_row = jnp.exp(b_last_row - b_row + ig_row - m_new_row)
            ksb = (kb.astype(F32) * (_diag_to_col(w_s_row, eye) * scale)).astype(BF16)
            for s in range(n_seq):
                sl = slice(s * seq_len, (s + 1) * seq_len)
                w_old = jnp.exp(b_last[s] + m_prev[s] - m_new[s])
                c_ref[s, h] = w_old * c_ref[s, h] + lax.dot_general(
                    ksb[sl], vb[sl], TN, preferred_element_type=F32)
                m_ref[s, h] = jnp.broadcast_to(m_new[s], (1, LANES))

    if n_chunks == 1:
        chunk(0)
    else:
        def body(ci, carry):
            chunk(pl.multiple_of(ci * R, R))
            return carry
        lax.fori_loop(0, n_chunks, body, 0)

    @pl.when(pl.program_id(1) == steps - 1)
    def _():
        c_out_ref[...] = c_ref[:, :, :, 0:A_DV]
        n_out_ref[...] = c_ref[:, :, :, A_DV:A_DV + 1]


def _mlstm(u, gt, bias, c0, n0, m0, bsz, seq):
    if seq % LANES == 0:
        seq_len, n_seq = LANES, 1
        n_chunks = _tile(seq // LANES, 4)
    else:
        assert LANES % seq == 0 and bsz % (LANES // seq) == 0, (bsz, seq)
        seq_len, n_seq, n_chunks = seq, LANES // seq, 1
    tok = n_chunks * LANES
    steps = seq // (n_chunks * seq_len)
    t = bsz * seq
    tok_map = lambda b, c: (b * steps + c, 0)
    state4 = lambda shape: pl.BlockSpec((n_seq,) + shape, lambda b, c: (b, 0, 0, 0))
    kern = functools.partial(_mlstm_kernel, n_seq=n_seq, seq_len=seq_len, n_chunks=n_chunks, steps=steps)
    return pl.pallas_call(
        kern,
        grid=(bsz // n_seq, steps),
        in_specs=[
            pl.BlockSpec((tok, A_QK), tok_map),
            pl.BlockSpec((tok, A_QK), lambda b, c: (b * steps + c, 1)),
            pl.BlockSpec((tok, A_V), lambda b, c: (b * steps + c, 1)),
            pl.BlockSpec((2 * A_HEADS, tok), lambda b, c: (0, b * steps + c)),
            pl.BlockSpec((2 * A_HEADS, 1), lambda b, c: (0, 0)),
            state4((A_HEADS, A_DK, A_DV)),
            state4((A_HEADS, A_DK, 1)),
            state4((A_HEADS, 1, LANES)),
        ],
        out_specs=[
            pl.BlockSpec((tok, A_V), tok_map),
            state4((A_HEADS, A_DK, A_DV)),
            state4((A_HEADS, A_DK, 1)),
            state4((A_HEADS, 1, LANES)),
        ],
        out_shape=[
            jax.ShapeDtypeStruct((t, A_V), F32),
            jax.ShapeDtypeStruct((bsz, A_HEADS, A_DK, A_DV), F32),
            jax.ShapeDtypeStruct((bsz, A_HEADS, A_DK, 1), F32),
            jax.ShapeDtypeStruct((bsz, A_HEADS, 1, LANES), F32),
        ],
        scratch_shapes=[pltpu.VMEM((n_seq, A_HEADS, A_DK, A_DV + LANES), F32),
                        pltpu.VMEM((2 * A_HEADS, tok), F32)],
        compiler_params=_params("parallel", "arbitrary"),
        name="mlstm",
    )(u, u, u, gt, bias, c0, n0, m0)


def _ple(h, p_ref, pw_ref, pn_ref, pg_ref):
    pp = jnp.dot(p_ref[...].astype(BF16), pw_ref[...], preferred_element_type=F32)
    hn = _rms(h, pn_ref[...])
    gate = _sigmoid(jnp.dot(hn.astype(BF16), pg_ref[...], preferred_element_type=F32))
    return h + pp * gate


def _a_post_kernel(hs_ref, o_ref, z_ref, x_ref, p_ref, hg_ref, wo_ref, pw_ref, pn_ref, pg_ref,
                   out_ref, y_ref):
    for h in range(A_HEADS):
        sl = slice(h * A_DV, (h + 1) * A_DV)
        y = _rms(hs_ref[:, sl], hg_ref[:, sl]) * _sigmoid(o_ref[:, sl].astype(F32))
        z = z_ref[:, sl].astype(F32)
        y_ref[:, sl] = (y * (z * _sigmoid(z))).astype(BF16)
    h1 = x_ref[...] + jnp.dot(y_ref[...], wo_ref[...], preferred_element_type=F32)
    out_ref[...] = _ple(h1, p_ref, pw_ref, pn_ref, pg_ref)


def _const_spec(shape):
    return pl.BlockSpec(shape, lambda i: (0,) * len(shape))


def _a_post(hs, u, x, p, layer, hnorm, w_out, ple_w, ple_norm, ple_wg):
    t = x.shape[0]
    tm = _tile(t, 256)
    return pl.pallas_call(
        _a_post_kernel,
        grid=(t // tm,),
        in_specs=[
            pl.BlockSpec((tm, A_V), lambda i: (i, 0)),
            pl.BlockSpec((tm, A_V), lambda i: (i, 2)),
            pl.BlockSpec((tm, A_V), lambda i: (i, 3)),
            pl.BlockSpec((tm, D_MODEL), lambda i: (i, 0)),
            pl.BlockSpec((None, tm, PLE_DIM), lambda i: (layer, i, 0)),
            _const_spec((1, A_V)),
            _const_spec((A_V, D_MODEL)),
            _const_spec((PLE_DIM, D_MODEL)),
            _const_spec((1, D_MODEL)),
            _const_spec((D_MODEL, D_MODEL)),
        ],
        out_specs=pl.BlockSpec((tm, D_MODEL), lambda i: (i, 0)),
        out_shape=jax.ShapeDtypeStruct((t, D_MODEL), F32),
        scratch_shapes=[pltpu.VMEM((tm, A_V), BF16)],
        compiler_params=_params("parallel"),
        name="a_post",
    )(hs, u, u, x, p, hnorm, w_out, ple_w, ple_norm, ple_wg)


def _head_ms(x, gm_ref):
    return jnp.dot((x * x).astype(BF16), gm_ref[...], preferred_element_type=F32) * (1.0 / B_DH)


def _kv_kernel(h_ref, g_ref, w_ref, wfh_ref, wfl_ref, wfth_ref, wftl_ref, bf_ref, bft_ref, kg_ref, gm_ref,
               k_ref, v_ref, kb_ref, vb_ref, vt_ref, lf_ref, lft_ref):
    xs = _split(_rms(h_ref[...], g_ref[...]))
    kv = jnp.dot(xs[0], w_ref[...], preferred_element_type=F32)
    k = kv[:, :B_W]
    v = kv[:, B_W:]
    kn = k * lax.rsqrt(_head_ms(k, gm_ref) + EPS) * kg_ref[...]
    k_ref[...] = pltpu.einshape("m(hd)->mhd", kn, h=B_HEADS)
    v_ref[...] = pltpu.einshape("m(hd)->mhd", v, h=B_HEADS)
    kb_ref[...] = kn.astype(BF16)
    vb_ref[...] = v.astype(BF16)
    vt_ref[...] = v.T.astype(BF16)
    fg = _dot_split(xs, (wfh_ref[...], wfl_ref[...]), NN)
    lf_ref[...] = _log_sigmoid(fg + bf_ref[...])
    fgt = _dot_split((wfth_ref[...], wftl_ref[...]), xs, NT)
    lft_ref[...] = _log_sigmoid(fgt + bft_ref[...])


def _kv_proj(h, g, w_kv, w_f, w_ft, b_f, b_ft, k_g, gmat):
    t = h.shape[0]
    tm = _tile(t, 512)
    row = lambda w: pl.BlockSpec((tm, w), lambda i: (i, 0))
    return pl.pallas_call(
        _kv_kernel,
        grid=(t // tm,),
        in_specs=[
            row(D_MODEL),
            _const_spec((1, D_MODEL)),
            _const_spec((D_MODEL, 2 * B_W)),
            _const_spec((D_MODEL, B_HEADS)),
            _const_spec((D_MODEL, B_HEADS)),
            _const_spec((B_HEADS, D_MODEL)),
            _const_spec((B_HEADS, D_MODEL)),
            _const_spec((1, B_HEADS)),
            _const_spec((B_HEADS, 1)),
            _const_spec((1, B_W)),
            _const_spec((B_W, B_W)),
        ],
        out_specs=[pl.BlockSpec((tm, B_HEADS, B_DH), lambda i: (i, 0, 0)),
                   pl.BlockSpec((tm, B_HEADS, B_DH), lambda i: (i, 0, 0)), row(B_W), row(B_W),
                   pl.BlockSpec((B_W, tm), lambda i: (0, i)), row(B_HEADS),
                   pl.BlockSpec((B_HEADS, tm), lambda i: (0, i))],
        out_shape=[
            jax.ShapeDtypeStruct((t, B_HEADS, B_DH), F32),
            jax.ShapeDtypeStruct((t, B_HEADS, B_DH), F32),
            jax.ShapeDtypeStruct((t, B_W), BF16),
            jax.ShapeDtypeStruct((t, B_W), BF16),
            jax.ShapeDtypeStruct((B_W, t), BF16),
            jax.ShapeDtypeStruct((t, B_HEADS), F32),
            jax.ShapeDtypeStruct((B_HEADS, t), F32),
        ],
        compiler_params=_params("parallel"),
        name="kv_proj",
    )(h, g, w_kv, *w_f, *w_ft, b_f, b_ft, k_g, gmat)


def _q_kernel(h_ref, g_ref, w_ref, qg_ref, gm_ref, q_ref, z_ref):
    xn = _rms(h_ref[...], g_ref[...])
    qz = jnp.dot(xn.astype(BF16), w_ref[...], preferred_element_type=F32)
    q = qz[:, :B_W]
    z_ref[...] = qz[:, B_W:]
    qn = q * lax.rsqrt(_head_ms(q, gm_ref) + EPS) * qg_ref[...]
    q_ref[...] = (qn * (B_DH ** -0.5 * LOG2E)).astype(BF16)


def _q_proj(h, g, w_in, q_g, gmat):
    t = h.shape[0]
    tm = _tile(t, 512)
    row = lambda w: pl.BlockSpec((tm, w), lambda i: (i, 0))
    return pl.pallas_call(
        _q_kernel,
        grid=(t // tm,),
        in_specs=[row(D_MODEL), _const_spec((1, D_MODEL)), _const_spec((D_MODEL, 2 * B_W)),
                  _const_spec((1, B_W)), _const_spec((B_W, B_W))],
        out_specs=[row(B_W), row(B_W)],
        out_shape=[jax.ShapeDtypeStruct((t, B_W), BF16), jax.ShapeDtypeStruct((t, B_W), F32)],
        compiler_params=_params("parallel"),
        name="q_proj",
    )(h, g, w_in, q_g, gmat)


def _cumsum_kernel(x_ref, f_ref, carry_ref):
    @pl.when(pl.program_id(1) == 0)
    def _():
        carry_ref[...] = jnp.zeros_like(carry_ref)

    w = x_ref.shape[-1]
    f = _seg_cumsum_lanes(x_ref[...], w) + carry_ref[:, 0:1]
    f_ref[...] = f
    carry_ref[...] = jnp.broadcast_to(f[:, w - 1:w], carry_ref.shape)


def _cumsum_rows(x, bsz, width):
    tb = _tile(width, 2048) if width % LANES == 0 and (width // LANES) & (width // LANES - 1) == 0 else width
    nb = width // tb
    if x.ndim == 2:
        in_spec = pl.BlockSpec((B_HEADS, tb), lambda b, j: (0, b * nb + j))
    else:
        in_spec = pl.BlockSpec((None, B_HEADS, tb), lambda b, j: (b, 0, j))
    return pl.pallas_call(
        _cumsum_kernel,
        grid=(bsz, nb),
        in_specs=[in_spec],
        out_specs=pl.BlockSpec((None, B_HEADS, tb), lambda b, j: (b, 0, j)),
        out_shape=jax.ShapeDtypeStruct((bsz, B_HEADS, width), F32),
        scratch_shapes=[pltpu.VMEM((B_HEADS, LANES), F32)],
        compiler_params=_params("parallel", "arbitrary"),
        name="logf_cumsum",
    )(x)


def _cumsum_aug_kernel(x_ref, place_ref, f_ref, ka_ref, carry_ref):
    @pl.when(pl.program_id(1) == 0)
    def _():
        carry_ref[...] = jnp.zeros_like(carry_ref)

    w = x_ref.shape[-1]
    f = _seg_cumsum_lanes(x_ref[...], w) + carry_ref[:, 0:1]
    f_ref[...] = f
    carry_ref[...] = jnp.broadcast_to(f[:, w - 1:w], carry_ref.shape)
    rows = jnp.concatenate(_split3(f * LOG2E) + (jnp.ones((B_HEADS, w), BF16),), axis=0)
    ka_ref[...] = lax.dot_general(rows, place_ref[...], TN, preferred_element_type=F32).astype(BF16)


def _bias_placement():
    r = jnp.arange(4 * B_HEADS)[:, None]
    c = jnp.arange(B_W)[None, :]
    part, head = r // B_HEADS, r % B_HEADS
    lane0 = (head // 2) * LANES + BIAS_LANES * (head % 2)
    is_part = (part < 3) & (c == lane0 + part)
    is_one = (part == 3) & (c >= lane0 + 3) & (c < lane0 + 6)
    return (is_part | is_one).astype(BF16)


def _cumsum_aug(x, bsz, width):
    tb = _tile(width, 2048)
    nb = width // tb
    return pl.pallas_call(
        _cumsum_aug_kernel,
        grid=(bsz, nb),
        in_specs=[pl.BlockSpec((B_HEADS, tb), lambda b, j: (0, b * nb + j)),
                  pl.BlockSpec((4 * B_HEADS, B_W), lambda b, j: (0, 0))],
        out_specs=[pl.BlockSpec((None, B_HEADS, tb), lambda b, j: (b, 0, j)),
                   pl.BlockSpec((tb, B_W), lambda b, j: (b * nb + j, 0))],
        out_shape=[jax.ShapeDtypeStruct((bsz, B_HEADS, width), F32),
                   jax.ShapeDtypeStruct((bsz * width, B_W), BF16)],
        scratch_shapes=[pltpu.VMEM((B_HEADS, LANES), F32)],
        compiler_params=_params("parallel", "arbitrary"),
        name="logf_cumsum_aug",
    )(x, _bias_placement())


def _fox_bounded_kernel(bound_ref, q_ref, k_ref, ka_ref, vt_ref, f_ref, o_ref, qt_ref, acc_ref,
                        p00, p01, p10, p11, *, tq):
    tk = tq // 2
    pbuf = ((p00, p01), (p10, p11))
    qi = pl.program_id(2)
    q0 = pl.multiple_of(qi * tq, tq)
    feat = lax.broadcasted_iota(jnp.int32, (LANES, tq), 0)
    qt = q_ref[...].astype(F32).T
    beta = _split3(f_ref[:, pl.ds(q0, tq)] * LOG2E - bound_ref[...])
    for hh in range(2):
        lo = BIAS_LANES * hh
        qt_ref[hh, 0:LANES, :] = jnp.where((feat >= hh * B_DH) & (feat < (hh + 1) * B_DH), qt, 0.0).astype(BF16)
        aug = jnp.where((feat >= lo) & (feat < lo + 3), -1.0, 0.0)
        for part in range(3):
            aug = jnp.where(feat == lo + 3 + part, beta[part][hh:hh + 1].astype(F32), aug)
        qt_ref[hh, LANES:2 * LANES, :] = aug.astype(BF16)
    acc_ref[...] = jnp.zeros_like(acc_ref)
    ones = jnp.ones((ACC_ROWS - B_DH, tk), BF16)
    kpos = lax.broadcasted_iota(jnp.int32, (tk, tq), 0)
    qpos = lax.broadcasted_iota(jnp.int32, (tk, tq), 1)
    diag_masks = (kpos <= qpos, (kpos + tk) <= qpos)

    def weights(k0, slot, causal=None):
        lhs = jnp.concatenate([k_ref[pl.ds(k0, tk), :], ka_ref[pl.ds(k0, tk), :]], axis=1)
        for hh in range(2):
            e = jnp.dot(lhs, qt_ref[hh], preferred_element_type=F32)
            if causal is not None:
                e = jnp.where(causal, e, NEG)
            pbuf[slot][hh][...] = jnp.exp2(e).astype(BF16)

    def accumulate(k0, slot):
        for hh in range(2):
            vt = jnp.concatenate([vt_ref[hh * B_DH:(hh + 1) * B_DH, pl.ds(k0, tk)], ones], axis=0)
            acc_ref[hh] += jnp.dot(vt, pbuf[slot][hh][...], preferred_element_type=F32)

    def pair(k0, next_causal):
        weights(k0 + tk, 1)
        accumulate(k0, 0)
        weights(k0 + tq, 0, next_causal)
        accumulate(k0 + tk, 1)

    def finish():
        weights(q0 + tk, 1, diag_masks[1])
        accumulate(q0, 0)
        accumulate(q0 + tk, 1)
        o_t = jnp.concatenate(
            [acc_ref[hh, 0:B_DH] * (1.0 / acc_ref[hh, B_DH:B_DH + 1]) for hh in range(2)], axis=0)
        o_ref[...] = o_t.T

    @pl.when(qi == 0)
    def _():
        weights(q0, 0, diag_masks[0])
        finish()

    @pl.when(qi > 0)
    def _():
        weights(0, 0)
        n_pairs = qi - 1

        def unrolled(jj, carry):
            for u in range(PAIR_UNROLL):
                pair(pl.multiple_of((jj * PAIR_UNROLL + u) * tq, tq), None)
            return carry

        def single(jj, carry):
            pair(pl.multiple_of(jj * tq, tq), None)
            return carry

        lax.fori_loop(0, n_pairs // PAIR_UNROLL, unrolled, 0)
        lax.fori_loop((n_pairs // PAIR_UNROLL) * PAIR_UNROLL, n_pairs, single, 0)
        pair(q0 - tq, diag_masks[0])
        finish()


def _fox_prompt_bounded(q, kb, ka, vt, f_row, bound, bsz, seq):
    tq = _tile(seq, 1024)
    nq = seq // tq
    kern = functools.partial(_fox_bounded_kernel, tq=tq)
    return pl.pallas_call(
        kern,
        grid=(bsz, B_PAIRS, nq),
        in_specs=[
            pl.BlockSpec((1, 1), lambda b, p, qi: (0, 0)),
            pl.BlockSpec((tq, LANES), lambda b, p, qi: (b * nq + qi, p)),
            pl.BlockSpec((seq, LANES), lambda b, p, qi: (b, p)),
            pl.BlockSpec((seq, LANES), lambda b, p, qi: (b, p)),
            pl.BlockSpec((LANES, seq), lambda b, p, qi: (p, b)),
            pl.BlockSpec((None, None, 2, seq), lambda b, p, qi: (b, p, 0, 0)),
        ],
        out_specs=pl.BlockSpec((tq, LANES), lambda b, p, qi: (b * nq + qi, p)),
        out_shape=jax.ShapeDtypeStruct((bsz * seq, B_W), F32),
        scratch_shapes=[
            pltpu.VMEM((2, 2 * LANES, tq), BF16),
            pltpu.VMEM((2, ACC_ROWS, tq), F32),
        ] + [pltpu.VMEM((tq // 2, tq), BF16)] * 4,
        compiler_params=_params("parallel", "parallel", "arbitrary"),
        name="fox_prompt_bounded",
    )(bound, q, kb, ka, vt, f_row)


def _fox_kernel(q_ref, k_ref, vt_ref, f_ref, o_ref, qt_ref, m_ref, cm_ref, acc_ref,
                s00, s01, s10, s11, *, tq):
    tk = tq // 2
    sbuf = ((s00, s01), (s10, s11))
    qi = pl.program_id(2)
    q0 = pl.multiple_of(qi * tq, tq)
    feat = lax.broadcasted_iota(jnp.int32, (LANES, tq), 0)
    qt = q_ref[...].astype(F32).T
    qt_ref[0] = jnp.where(feat < B_DH, qt, 0.0).astype(BF16)
    qt_ref[1] = jnp.where(feat >= B_DH, qt, 0.0).astype(BF16)
    m_ref[...] = jnp.full_like(m_ref, NEG)
    acc_ref[...] = jnp.zeros_like(acc_ref)
    f_base = f_ref[:, pl.ds(q0, tq)][:, tq - 1:tq]
    ones = jnp.ones((ACC_ROWS - B_DH, tk), BF16)
    kpos = lax.broadcasted_iota(jnp.int32, (tk, tq), 0)
    qpos = lax.broadcasted_iota(jnp.int32, (tk, tq), 1)
    diag_masks = (kpos <= qpos, (kpos + tk) <= qpos)

    def scores(k0, slot, causal=None):
        kb = k_ref[pl.ds(k0, tk), :]
        fk = (f_ref[:, pl.ds(k0, tk)] - f_base) * LOG2E
        for hh in range(2):
            fk_col = jnp.concatenate(
                [jnp.broadcast_to(fk[hh:hh + 1, c * LANES:(c + 1) * LANES], (LANES, LANES)).T
                 for c in range(tk // LANES)], axis=0)
            t = (jnp.dot(kb, qt_ref[hh], preferred_element_type=F32)
                 - pltpu.repeat(fk_col, tq // LANES, axis=1))
            if causal is not None:
                t = jnp.where(causal, t, NEG)
            sbuf[slot][hh][...] = t
            cm_ref[slot, hh] = jnp.max(t, axis=0, keepdims=True)

    def update(k0, slot):
        for hh in range(2):
            m_prev = m_ref[hh]
            m_new = jnp.maximum(m_prev, cm_ref[slot, hh])
            alpha = jnp.exp2(m_prev - m_new)
            p = jnp.exp2(sbuf[slot][hh][...] - m_new).astype(BF16)
            vt = jnp.concatenate([vt_ref[hh * B_DH:(hh + 1) * B_DH, pl.ds(k0, tk)], ones], axis=0)
            acc_ref[hh] = alpha * acc_ref[hh] + jnp.dot(vt, p, preferred_element_type=F32)
            m_ref[hh] = m_new

    def pair(k0, next_causal):
        scores(k0 + tk, 1)
        update(k0, 0)
        scores(k0 + tq, 0, next_causal)
        update(k0 + tk, 1)

    @pl.when(qi == 0)
    def _():
        scores(q0, 0, diag_masks[0])

    @pl.when(qi > 0)
    def _():
        scores(0, 0)

        def body(jj, carry):
            pair(pl.multiple_of(jj * tq, tq), None)
            return carry

        lax.fori_loop(0, qi - 1, body, 0)
        pair(q0 - tq, diag_masks[0])

    scores(q0 + tk, 1, diag_masks[1])
    update(q0, 0)
    update(q0 + tk, 1)
    o_t = jnp.concatenate(
        [acc_ref[hh, 0:B_DH] * (1.0 / acc_ref[hh, B_DH:B_DH + 1]) for hh in range(2)], axis=0)
    o_ref[...] = o_t.T


def _fox_prompt(q, kb, vt, f_row, bsz, seq):
    tq = _tile(seq, 512)
    tk = tq // 2
    nq = seq // tq
    kern = functools.partial(_fox_kernel, tq=tq)
    return pl.pallas_call(
        kern,
        grid=(bsz, B_PAIRS, nq),
        in_specs=[
            pl.BlockSpec((tq, LANES), lambda b, p, qi: (b * nq + qi, p)),
            pl.BlockSpec((seq, LANES), lambda b, p, qi: (b, p)),
            pl.BlockSpec((LANES, seq), lambda b, p, qi: (p, b)),
            pl.BlockSpec((None, None, 2, seq), lambda b, p, qi: (b, p, 0, 0)),
        ],
        out_specs=pl.BlockSpec((tq, LANES), lambda b, p, qi: (b * nq + qi, p)),
        out_shape=jax.ShapeDtypeStruct((bsz * seq, B_W), F32),
        scratch_shapes=[
            pltpu.VMEM((2, LANES, tq), BF16),
            pltpu.VMEM((2, 1, tq), F32),
            pltpu.VMEM((2, 2, 1, tq), F32),
            pltpu.VMEM((2, ACC_ROWS, tq), F32),
        ] + [pltpu.VMEM((tk, tq), F32)] * 4,
        compiler_params=_params("parallel", "parallel", "arbitrary"),
        name="fox_prompt",
    )(q, kb, vt, f_row)


def _fox_dec_kernel(q_ref, kt_ref, vt_ref, kn_ref, vn_ref, fn_ref, fc_ref, o_ref,
                    fqc_ref, m_ref, l_ref, a_ref, acc_ref, s_ref, p_ref, *, sq, nkb):
    kj = pl.program_id(1)

    @pl.when(kj == 0)
    def _():
        r = lax.broadcasted_iota(jnp.int32, (sq, LANES), 0)
        c = lax.broadcasted_iota(jnp.int32, (sq, LANES), 1)
        for h in range(B_HEADS):
            fqc_ref[h] = jnp.sum(jnp.where(r == c, fn_ref[h:h + 1, :], 0.0), axis=1, keepdims=True)
        m_ref[...] = jnp.full_like(m_ref, NEG)
        l_ref[...] = jnp.zeros_like(l_ref)
        acc_ref[...] = jnp.zeros_like(acc_ref)

    def attend(width, score, f_keys, values, valid):
        for h in range(B_HEADS):
            s_ref[h, :, 0:width] = score(h)
        for h in range(B_HEADS):
            t = s_ref[h, :, 0:width] + (fqc_ref[h] - f_keys(h)) * LOG2E
            if valid is not None:
                t = jnp.where(valid, t, NEG)
            m_prev = m_ref[h]
            m_new = jnp.maximum(m_prev, jnp.max(t, axis=1, keepdims=True))
            alpha = jnp.exp2(m_prev - m_new)
            p = jnp.exp2(t - m_new)
            l_ref[h] = alpha * l_ref[h] + jnp.sum(p, axis=1, keepdims=True)
            m_ref[h] = m_new
            a_ref[h] = alpha
            p_ref[h, :, 0:width] = p.astype(BF16)
        for h in range(B_HEADS):
            acc_ref[h] = a_ref[h] * acc_ref[h] + values(h, p_ref[h, :, 0:width])

    @pl.when(kj < nkb)
    def _():
        attend(
            kt_ref.shape[-1],
            lambda h: jnp.dot(q_ref[:, h * B_DH:(h + 1) * B_DH], kt_ref[h].astype(BF16),
                              preferred_element_type=F32),
            lambda h: fc_ref[h:h + 1, :],
            lambda h, p: lax.dot_general(p, vt_ref[h].astype(BF16), NT, preferred_element_type=F32),
            None)

    @pl.when(kj == nkb)
    def _():
        i = lax.broadcasted_iota(jnp.int32, (sq, LANES), 0)
        j = lax.broadcasted_iota(jnp.int32, (sq, LANES), 1)
        head = lambda h: slice(h * B_DH, (h + 1) * B_DH)
        attend(
            LANES,
            lambda h: lax.dot_general(q_ref[:, head(h)], kn_ref[:, head(h)], NT, preferred_element_type=F32),
            lambda h: fn_ref[h:h + 1, :],
            lambda h, p: jnp.dot(p, vn_ref[:, head(h)], preferred_element_type=F32),
            j <= i)
        for h in range(B_HEADS):
            o_ref[:, head(h)] = acc_ref[h] * (1.0 / l_ref[h])


def _fox_decode(q, cache_kt, cache_vt, kn, vn, f_new, f_cache, bsz, sq):
    past = cache_kt.shape[3]
    tk = _tile(past, 2048)
    nkb = past // tk
    cmap = lambda b, kj: (b, 0, 0, jnp.minimum(kj, nkb - 1))
    kern = functools.partial(_fox_dec_kernel, sq=sq, nkb=nkb)
    return pl.pallas_call(
        kern,
        grid=(bsz, nkb + 1),
        in_specs=[
            pl.BlockSpec((None, sq, B_W), lambda b, kj: (b, 0, 0)),
            pl.BlockSpec((None, B_HEADS, B_DH, tk), cmap),
            pl.BlockSpec((None, B_HEADS, B_DH, tk), cmap),
            pl.BlockSpec((None, LANES, B_W), lambda b, kj: (b, 0, 0)),
            pl.BlockSpec((None, LANES, B_W), lambda b, kj: (b, 0, 0)),
            pl.BlockSpec((None, B_HEADS, LANES), lambda b, kj: (b, 0, 0)),
            pl.BlockSpec((None, B_HEADS, tk), lambda b, kj: (b, 0, jnp.minimum(kj, nkb - 1))),
        ],
        out_specs=pl.BlockSpec((None, sq, B_W), lambda b, kj: (b, 0, 0)),
        out_shape=jax.ShapeDtypeStruct((bsz, sq, B_W), F32),
        scratch_shapes=[
            pltpu.VMEM((B_HEADS, sq, 1), F32),
            pltpu.VMEM((B_HEADS, sq, 1), F32),
            pltpu.VMEM((B_HEADS, sq, 1), F32),
            pltpu.VMEM((B_HEADS, sq, 1), F32),
            pltpu.VMEM((B_HEADS, sq, B_DH), F32),
            pltpu.VMEM((B_HEADS, sq, tk), F32),
            pltpu.VMEM((B_HEADS, sq, tk), BF16),
        ],
        compiler_params=_params("parallel", "arbitrary"),
        name="fox_decode",
    )(q, cache_kt, cache_vt, kn, vn, f_new, f_cache)


def _b_post_kernel(o_ref, z_ref, h_ref, p_ref, wo_ref, pw_ref, pn_ref, pg_ref, out_ref):
    z = z_ref[...]
    y = o_ref[...] * (z * _sigmoid(z))
    h2 = h_ref[...] + jnp.dot(y.astype(BF16), wo_ref[...], preferred_element_type=F32)
    out_ref[...] = _ple(h2, p_ref, pw_ref, pn_ref, pg_ref)


def _b_post(o, z, h, p, layer, w_out, ple_w, ple_norm, ple_wg):
    t = h.shape[0]
    tm = _tile(t, 512)
    row = lambda w: pl.BlockSpec((tm, w), lambda i: (i, 0))
    return pl.pallas_call(
        _b_post_kernel,
        grid=(t // tm,),
        in_specs=[row(B_W), row(B_W), row(D_MODEL), pl.BlockSpec((None, tm, PLE_DIM), lambda i: (layer, i, 0)),
                  _const_spec((B_W, D_MODEL)), _const_spec((PLE_DIM, D_MODEL)),
                  _const_spec((1, D_MODEL)), _const_spec((D_MODEL, D_MODEL))],
        out_specs=row(D_MODEL),
        out_shape=jax.ShapeDtypeStruct((t, D_MODEL), F32),
        compiler_params=_params("parallel"),
        name="b_post",
    )(o, z, h, p, w_out, ple_w, ple_norm, ple_wg)


def _prep_weights(a_norm, a_w_in, a_b_i, a_b_f, a_hnorm, a_w_out, kv_norm, kv_w, kv_b_f, k_norm,
                  b_norm, b_w_in, q_norm, b_w_out, ple_w, ple_norm, ple_w_g):
    head_of_lane = jnp.arange(B_W, dtype=jnp.int32) // B_DH
    return dict(
        a_norm=a_norm[0].reshape(1, D_MODEL),
        a_w_main=a_w_in[0][:, :A_MAIN].astype(BF16),
        a_w_gt=_split(a_w_in[0][:, A_MAIN:].T),
        a_bias=jnp.concatenate([a_b_i[0], a_b_f[0]]).reshape(2 * A_HEADS, 1),
        a_hnorm=a_hnorm[0].reshape(1, A_V),
        a_w_out=a_w_out[0].astype(BF16),
        kv_norm=kv_norm.reshape(1, D_MODEL),
        kv_w=kv_w[:, :2 * B_W].astype(BF16),
        kv_wf=_split(kv_w[:, 2 * B_W:]),
        kv_wft=_split(kv_w[:, 2 * B_W:].T),
        kv_bf=kv_b_f.reshape(1, B_HEADS),
        kv_bft=kv_b_f.reshape(B_HEADS, 1),
        k_norm=jnp.tile(k_norm, B_HEADS).reshape(1, B_W),
        gmat=(head_of_lane[:, None] == head_of_lane[None, :]).astype(BF16),
        b_norm=b_norm[0].reshape(1, D_MODEL),
        b_w_in=b_w_in[0].astype(BF16),
        q_norm=jnp.tile(q_norm[0], B_HEADS).reshape(1, B_W),
        b_w_out=b_w_out[0].astype(BF16),
        ple_w=ple_w.astype(BF16),
        ple_norm=ple_norm.reshape(-1, 1, D_MODEL),
        ple_wg=ple_w_g.astype(BF16),
    )


def _trunk(x, p, c0, n0, m0, past, w):
    bsz, seq, _ = x.shape
    t = bsz * seq
    xf = x.reshape(t, D_MODEL)
    pf = p.reshape(p.shape[0], t, PLE_DIM)

    u, gt = _a_inproj(xf, w["a_norm"], w["a_w_main"], w["a_w_gt"])
    m0r = jnp.broadcast_to(m0.reshape(bsz, A_HEADS, 1, 1), (bsz, A_HEADS, 1, LANES))
    hs, c_new, n_new, m_new = _mlstm(u, gt, w["a_bias"], c0, n0[..., None], m0r, bsz, seq)
    h1 = _a_post(hs, u, xf, pf, 0, w["a_hnorm"], w["a_w_out"], w["ple_w"][0], w["ple_norm"][0], w["ple_wg"][0])

    k, v, kb, vb, vt, lf, lft = _kv_proj(h1, w["kv_norm"], w["kv_w"], w["kv_wf"], w["kv_wft"],
                                     w["kv_bf"], w["kv_bft"], w["k_norm"], w["gmat"])
    q, z = _q_proj(h1, w["b_norm"], w["b_w_in"], w["q_norm"], w["gmat"])
    if past is None:
        f_row, ka = _cumsum_aug(lft, bsz, seq)
        f_row = f_row.reshape(bsz, B_PAIRS, 2, seq)
        bound = (8.0 * LOG2E * 1.01) * jnp.max(jnp.abs(w["q_norm"])) * jnp.max(jnp.abs(w["k_norm"])) + 0.05
        o = lax.cond(
            bound <= MAX_BOUND,
            lambda: _fox_prompt_bounded(q, kb, ka, vt, f_row, bound.reshape(1, 1), bsz, seq),
            lambda: _fox_prompt(q, kb, vt, f_row, bsz, seq))
    else:
        cache_k, cache_v, cache_lf = past
        plen = cache_k.shape[1]
        assert seq <= LANES
        width = -(-(plen + seq) // LANES) * LANES
        lf_rows = jnp.concatenate(
            [jnp.swapaxes(cache_lf.astype(F32), 1, 2),
             jnp.swapaxes(lft.reshape(B_HEADS, bsz, seq), 0, 1),
             jnp.zeros((bsz, B_HEADS, width - plen - seq), F32)], axis=2)
        f_all = _cumsum_rows(lf_rows, bsz, width)
        f_new = jnp.pad(f_all[..., plen:plen + seq], ((0, 0), (0, 0), (0, LANES - seq)))
        pad_new = lambda a: jnp.pad(a.reshape(bsz, seq, B_W), ((0, 0), (0, LANES - seq), (0, 0)))
        to_t = lambda a: jnp.transpose(a.astype(F32), (0, 2, 3, 1))
        o = _fox_decode(q.reshape(bsz, seq, B_W), to_t(cache_k), to_t(cache_v), pad_new(kb), pad_new(vb),
                        f_new, f_all[..., :plen], bsz, seq).reshape(t, B_W)
    y = _b_post(o, z, h1, pf, 1, w["b_w_out"], w["ple_w"][1], w["ple_norm"][1], w["ple_wg"][1])

    return (y.reshape(bsz, seq, D_MODEL),
            c_new[None], n_new.reshape(1, bsz, A_HEADS, A_DK), m_new[None, :, :, 0, 0],
            k.reshape(bsz, seq, B_HEADS, B_DH), v.reshape(bsz, seq, B_HEADS, B_DH),
            lf.reshape(bsz, seq, B_HEADS))


def kernel(x_prompt, x_sample, cache_k, cache_v, cache_logf, state_C, state_n, state_m, p_prompt, p_sample,
           a_norm, a_w_in, a_b_i, a_b_f, a_hnorm, a_w_out, kv_norm, kv_w, kv_b_f, k_norm,
           b_norm, b_w_in, q_norm, b_w_out, ple_w, ple_norm, ple_w_g):
    w = _prep_weights(a_norm, a_w_in, a_b_i, a_b_f, a_hnorm, a_w_out, kv_norm, kv_w, kv_b_f, k_norm,
                      b_norm, b_w_in, q_norm, b_w_out, ple_w, ple_norm, ple_w_g)
    bsz = x_prompt.shape[0]
    c0 = jnp.zeros((bsz, A_HEADS, A_DK, A_DV), F32)
    n0 = jnp.zeros((bsz, A_HEADS, A_DK), F32)
    m0 = jnp.zeros((bsz, A_HEADS), F32)
    prompt = _trunk(x_prompt, p_prompt, c0, n0, m0, None, w)
    sample = _trunk(x_sample, p_sample, state_C[0].astype(F32), state_n[0].astype(F32),
                    state_m[0].astype(F32), (cache_k, cache_v, cache_logf), w)
    return (prompt[0], sample[0]) + prompt[1:] + sample[1:]
```

```python
import functools

import jax
import jax.numpy as jnp
from jax import lax
from jax.experimental import pallas as pl
from jax.experimental.pallas import tpu as pltpu

D_MODEL = 1024
A_HEADS = 8
A_DK = 128
A_DV = 256
A_QK = A_HEADS * A_DK
A_V = A_HEADS * A_DV
A_MAIN = 2 * A_QK + 3 * A_V
B_HEADS = 16
B_DH = 64
B_W = B_HEADS * B_DH
B_PAIRS = B_HEADS // 2
PLE_DIM = 256
EPS = 1e-6
NEG = -1e30
LOG2E = 1.4426950408889634
ACC_ROWS = B_DH + 16
BIAS_LANES = 6
MAX_BOUND = 60.0
PAIR_UNROLL = 4
MLSTM_ROWS = 128

LANES = 128
VMEM_LIMIT = 56 * 1024 * 1024

F32 = jnp.float32
BF16 = jnp.bfloat16
NN = (((1,), (0,)), ((), ()))
NT = (((1,), (1,)), ((), ()))
TN = (((0,), (0,)), ((), ()))


def _params(*sem):
    return pltpu.CompilerParams(dimension_semantics=sem, vmem_limit_bytes=VMEM_LIMIT)


def _tile(n, pref):
    t = min(n, pref)
    while n % t:
        t //= 2
    return t


def _rms(x, g):
    ms = jnp.mean(x * x, axis=-1, keepdims=True)
    return x * lax.rsqrt(ms + EPS) * g


def _log_sigmoid(x):
    return jnp.minimum(x, 0.0) - jnp.log1p(jnp.exp(-jnp.abs(x)))


def _sigmoid(x):
    return 0.5 * jnp.tanh(0.5 * x) + 0.5


def _diag_to_col(row, eye):
    n = eye.shape[0]
    return jnp.sum(jnp.where(eye, jnp.broadcast_to(row, (n, n)), 0.0), axis=1, keepdims=True)


def _seg_cummax_lanes(x, seg):
    w = x.shape[-1]
    pos = lax.broadcasted_iota(jnp.int32, x.shape, x.ndim - 1) % seg
    k = 1
    while k < min(seg, w):
        x = jnp.maximum(x, jnp.where(pos >= k, pltpu.roll(x, k, x.ndim - 1), NEG))
        k *= 2
    return x


def _seg_cumsum_lanes(x, seg):
    w = x.shape[-1]
    pos = lax.broadcasted_iota(jnp.int32, x.shape, x.ndim - 1) % seg
    k = 1
    while k < min(seg, w):
        x = x + jnp.where(pos >= k, pltpu.roll(x, k, x.ndim - 1), 0.0)
        k *= 2
    return x


def _split(x):
    hi = x.astype(BF16)
    return hi, (x - hi.astype(F32)).astype(BF16)


def _dot_split(a, b, dims):
    dot = functools.partial(lax.dot_general, dimension_numbers=dims, preferred_element_type=F32)
    return dot(a[0], b[0]) + (dot(a[0], b[1]) + dot(a[1], b[0]))


def _split3(x):
    hi = x.astype(BF16)
    r = x - hi.astype(F32)
    mid = r.astype(BF16)
    return hi, mid, (r - mid.astype(F32)).astype(BF16)


def _a_inproj_kernel(x_ref, g_ref, w_ref, wgh_ref, wgl_ref, u_ref, gt_ref, xn_ref):
    @pl.when(pl.program_id(1) == 0)
    def _():
        xn = _rms(x_ref[...], g_ref[...])
        xs = _split(xn)
        xn_ref[...] = xs[0]
        gt_ref[...] = _dot_split((wgh_ref[...], wgl_ref[...]), xs, NT)

    u_ref[...] = jnp.dot(xn_ref[...], w_ref[...], preferred_element_type=F32).astype(BF16)


def _a_inproj(x, g, w_main, w_gt):
    t = x.shape[0]
    tm = _tile(t, 1024)
    tn = 1024
    gate_spec = pl.BlockSpec((2 * A_HEADS, D_MODEL), lambda i, j: (0, 0))
    return pl.pallas_call(
        _a_inproj_kernel,
        grid=(t // tm, A_MAIN // tn),
        in_specs=[
            pl.BlockSpec((tm, D_MODEL), lambda i, j: (i, 0)),
            pl.BlockSpec((1, D_MODEL), lambda i, j: (0, 0)),
            pl.BlockSpec((D_MODEL, tn), lambda i, j: (0, j)),
            gate_spec,
            gate_spec,
        ],
        out_specs=[
            pl.BlockSpec((tm, tn), lambda i, j: (i, j)),
            pl.BlockSpec((2 * A_HEADS, tm), lambda i, j: (0, i)),
        ],
        out_shape=[
            jax.ShapeDtypeStruct((t, A_MAIN), BF16),
            jax.ShapeDtypeStruct((2 * A_HEADS, t), F32),
        ],
        scratch_shapes=[pltpu.VMEM((tm, D_MODEL), BF16)],
        compiler_params=_params("parallel", "arbitrary"),
        name="a_inproj",
    )(x, g, w_main, *w_gt)


def _mlstm_kernel(q_ref, k_ref, v_ref, g_ref, bias_ref, c0_ref, n0_ref, m0_ref,
                  h_ref, c_out_ref, n_out_ref, m_ref, c_ref, gate_ref, *, rows, n_seq, seq_len, n_chunks, steps):
    R = rows

    @pl.when(pl.program_id(1) == 0)
    def _():
        c_ref[:, :, :, 0:A_DV] = c0_ref[...]
        c_ref[:, :, :, A_DV:] = jnp.broadcast_to(n0_ref[...], (n_seq, A_HEADS, A_DK, LANES))
        m_ref[...] = m0_ref[...]

    assert R == LANES
    row = lax.broadcasted_iota(jnp.int32, (R, R), 0)
    col = lax.broadcasted_iota(jnp.int32, (R, R), 1)
    valid = (col <= row) & ((row // seq_len) == (col // seq_len))
    lane8 = lax.broadcasted_iota(jnp.int32, (A_HEADS, R), 1)
    bias = bias_ref[...]
    scale = A_DK ** -0.5
    ones = jnp.ones((R, LANES), BF16)
    g = g_ref[...]
    ig_blk = g[0:A_HEADS] + bias[0:A_HEADS]
    b_blk = _seg_cumsum_lanes(_log_sigmoid(g[A_HEADS:] + bias[A_HEADS:]), seq_len)
    gate_ref[0:A_HEADS, :] = ig_blk
    gate_ref[A_HEADS:2 * A_HEADS, :] = b_blk
    gate_ref[2 * A_HEADS:, :] = _seg_cummax_lanes(ig_blk - b_blk, seq_len)

    def per_seq_rows(vals):
        out = vals[0]
        for s in range(1, n_seq):
            out = jnp.where(lane8 >= s * seq_len, vals[s], out)
        return jnp.broadcast_to(out, (A_HEADS, R))

    def chunk(r0):
        rows = pl.ds(r0, R)
        ig = gate_ref[0:A_HEADS, rows]
        b = gate_ref[A_HEADS:2 * A_HEADS, rows]
        cmax = gate_ref[2 * A_HEADS:, rows]
        m_prev = [m_ref[s, :, 0, 0:1] for s in range(n_seq)]
        m_prev_rows = per_seq_rows(m_prev)
        m_t = b + jnp.maximum(m_prev_rows, cmax)
        a = b - m_t
        ends = [(s + 1) * seq_len - 1 for s in range(n_seq)]
        b_last = [b[:, e:e + 1] for e in ends]
        m_new = [m_t[:, e:e + 1] for e in ends]
        w_s = jnp.exp(per_seq_rows(b_last) - b + ig - per_seq_rows(m_new)) * scale
        w_old = [jnp.exp(b_last[s] + m_prev[s] - m_new[s]) for s in range(n_seq)]
        for s in range(n_seq):
            m_ref[s, :, 0, :] = jnp.broadcast_to(m_new[s], (A_HEADS, LANES))
        cols = jnp.concatenate([a, jnp.exp(a + m_prev_rows), jnp.exp(-m_t), w_s,
                                jnp.zeros((R - 4 * A_HEADS, R), F32)], axis=0).T
        c_rows = ig - b
        for h in range(A_HEADS):
            qb = q_ref[rows, h * A_DK:(h + 1) * A_DK]
            kb = k_ref[rows, h * A_DK:(h + 1) * A_DK]
            vb = jnp.concatenate([v_ref[rows, h * A_DV:(h + 1) * A_DV], ones], axis=1)
            a_col = cols[:, h:h + 1]
            w_inter = cols[:, A_HEADS + h:A_HEADS + h + 1]
            e_m = cols[:, 2 * A_HEADS + h:2 * A_HEADS + h + 1]
            w_s_col = cols[:, 3 * A_HEADS + h:3 * A_HEADS + h + 1]
            p = jnp.exp(jnp.where(valid, a_col + c_rows[h:h + 1], NEG))
            qk = lax.dot_general(qb, kb, NT, preferred_element_type=F32) * scale
            s_mat = qk * p
            qc = [jnp.dot(qb[s * seq_len:(s + 1) * seq_len], c_ref[s, h].astype(BF16),
                          preferred_element_type=F32) for s in range(n_seq)]
            qc = qc[0] if n_seq == 1 else jnp.concatenate(qc, axis=0)
            nd = w_inter * qc + jnp.dot(s_mat.astype(BF16), vb, preferred_element_type=F32)
            den = nd[:, A_DV:A_DV + 1]
            h_ref[rows, h * A_DV:(h + 1) * A_DV] = nd[:, 0:A_DV] * (1.0 / jnp.maximum(jnp.abs(den), e_m))
            ksb = (kb.astype(F32) * w_s_col).astype(BF16)
            for s in range(n_seq):
                sl = slice(s * seq_len, (s + 1) * seq_len)
                c_ref[s, h] = w_old[s][h:h + 1] * c_ref[s, h] + lax.dot_general(
                    ksb[sl], vb[sl], TN, preferred_element_type=F32)

    if n_chunks == 1:
        chunk(0)
    else:
        def body(ci, carry):
            chunk(pl.multiple_of(ci * R, R))
            return carry
        lax.fori_loop(0, n_chunks, body, 0)

    @pl.when(pl.program_id(1) == steps - 1)
    def _():
        c_out_ref[...] = c_ref[:, :, :, 0:A_DV]
        n_out_ref[...] = c_ref[:, :, :, A_DV:A_DV + 1]


def _mlstm(u, gt, bias, c0, n0, m0, bsz, seq):
    if seq % LANES == 0:
        rows = _tile(seq, MLSTM_ROWS)
        seq_len, n_seq = rows, 1
        n_chunks = _tile(seq // rows, 1024 // rows)
    else:
        assert LANES % seq == 0 and bsz % (LANES // seq) == 0, (bsz, seq)
        rows, seq_len, n_seq, n_chunks = LANES, seq, LANES // seq, 1
    tok = n_chunks * rows
    steps = seq // (n_chunks * seq_len)
    t = bsz * seq
    tok_map = lambda b, c: (b * steps + c, 0)
    state4 = lambda shape: pl.BlockSpec((n_seq,) + shape, lambda b, c: (b, 0, 0, 0))
    kern = functools.partial(_mlstm_kernel, rows=rows, n_seq=n_seq, seq_len=seq_len, n_chunks=n_chunks, steps=steps)
    return pl.pallas_call(
        kern,
        grid=(bsz // n_seq, steps),
        in_specs=[
            pl.BlockSpec((tok, A_QK), tok_map),
            pl.BlockSpec((tok, A_QK), lambda b, c: (b * steps + c, 1)),
            pl.BlockSpec((tok, A_V), lambda b, c: (b * steps + c, 1)),
            pl.BlockSpec((2 * A_HEADS, tok), lambda b, c: (0, b * steps + c)),
            pl.BlockSpec((2 * A_HEADS, 1), lambda b, c: (0, 0)),
            state4((A_HEADS, A_DK, A_DV)),
            state4((A_HEADS, A_DK, 1)),
            state4((A_HEADS, 1, LANES)),
        ],
        out_specs=[
            pl.BlockSpec((tok, A_V), tok_map),
            state4((A_HEADS, A_DK, A_DV)),
            state4((A_HEADS, A_DK, 1)),
            state4((A_HEADS, 1, LANES)),
        ],
        out_shape=[
            jax.ShapeDtypeStruct((t, A_V), F32),
            jax.ShapeDtypeStruct((bsz, A_HEADS, A_DK, A_DV), F32),
            jax.ShapeDtypeStruct((bsz, A_HEADS, A_DK, 1), F32),
            jax.ShapeDtypeStruct((bsz, A_HEADS, 1, LANES), F32),
        ],
        scratch_shapes=[pltpu.VMEM((n_seq, A_HEADS, A_DK, A_DV + LANES), F32),
                        pltpu.VMEM((3 * A_HEADS, tok), F32)],
        compiler_params=_params("parallel", "arbitrary"),
        name="mlstm",
    )(u, u, u, gt, bias, c0, n0, m0)


def _ple(h, p_ref, pw_ref, pn_ref, pg_ref):
    pp = jnp.dot(p_ref[...].astype(BF16), pw_ref[...], preferred_element_type=F32)
    hn = _rms(h, pn_ref[...])
    gate = _sigmoid(jnp.dot(hn.astype(BF16), pg_ref[...], preferred_element_type=F32))
    return h + pp * gate


def _a_post_kernel(hs_ref, o_ref, z_ref, x_ref, p_ref, hg_ref, wo_ref, pw_ref, pn_ref, pg_ref,
                   out_ref, y_ref):
    for h in range(A_HEADS):
        sl = slice(h * A_DV, (h + 1) * A_DV)
        y = _rms(hs_ref[:, sl], hg_ref[:, sl]) * _sigmoid(o_ref[:, sl].astype(F32))
        z = z_ref[:, sl].astype(F32)
        y_ref[:, sl] = (y * (z * _sigmoid(z))).astype(BF16)
    h1 = x_ref[...] + jnp.dot(y_ref[...], wo_ref[...], preferred_element_type=F32)
    out_ref[...] = _ple(h1, p_ref, pw_ref, pn_ref, pg_ref)


def _const_spec(shape):
    return pl.BlockSpec(shape, lambda i: (0,) * len(shape))


def _a_post(hs, u, x, p, layer, hnorm, w_out, ple_w, ple_norm, ple_wg):
    t = x.shape[0]
    tm = _tile(t, 256)
    return pl.pallas_call(
        _a_post_kernel,
        grid=(t // tm,),
        in_specs=[
            pl.BlockSpec((tm, A_V), lambda i: (i, 0)),
            pl.BlockSpec((tm, A_V), lambda i: (i, 2)),
            pl.BlockSpec((tm, A_V), lambda i: (i, 3)),
            pl.BlockSpec((tm, D_MODEL), lambda i: (i, 0)),
            pl.BlockSpec((None, tm, PLE_DIM), lambda i: (layer, i, 0)),
            _const_spec((1, A_V)),
            _const_spec((A_V, D_MODEL)),
            _const_spec((PLE_DIM, D_MODEL)),
            _const_spec((1, D_MODEL)),
            _const_spec((D_MODEL, D_MODEL)),
        ],
        out_specs=pl.BlockSpec((tm, D_MODEL), lambda i: (i, 0)),
        out_shape=jax.ShapeDtypeStruct((t, D_MODEL), F32),
        scratch_shapes=[pltpu.VMEM((tm, A_V), BF16)],
        compiler_params=_params("parallel"),
        name="a_post",
    )(hs, u, u, x, p, hnorm, w_out, ple_w, ple_norm, ple_wg)


def _head_ms(x, gm_ref):
    return jnp.dot((x * x).astype(BF16), gm_ref[...], preferred_element_type=F32) * (1.0 / B_DH)


def _kv_kernel(h_ref, g_ref, w_ref, wfh_ref, wfl_ref, wfth_ref, wftl_ref, bf_ref, bft_ref, kg_ref, gm_ref,
               k_ref, v_ref, kb_ref, vb_ref, vt_ref, lf_ref, lft_ref):
    xs = _split(_rms(h_ref[...], g_ref[...]))
    kv = jnp.dot(xs[0], w_ref[...], preferred_element_type=F32)
    k = kv[:, :B_W]
    v = kv[:, B_W:]
    kn = k * lax.rsqrt(_head_ms(k, gm_ref) + EPS) * kg_ref[...]
    k_ref[...] = pltpu.einshape("m(hd)->mhd", kn, h=B_HEADS)
    v_ref[...] = pltpu.einshape("m(hd)->mhd", v, h=B_HEADS)
    kb_ref[...] = kn.astype(BF16)
    vb_ref[...] = v.astype(BF16)
    vt_ref[...] = v.T.astype(BF16)
    fg = _dot_split(xs, (wfh_ref[...], wfl_ref[...]), NN)
    lf_ref[...] = _log_sigmoid(fg + bf_ref[...])
    fgt = _dot_split((wfth_ref[...], wftl_ref[...]), xs, NT)
    lft_ref[...] = _log_sigmoid(fgt + bft_ref[...])


def _kv_proj(h, g, w_kv, w_f, w_ft, b_f, b_ft, k_g, gmat):
    t = h.shape[0]
    tm = _tile(t, 512)
    row = lambda w: pl.BlockSpec((tm, w), lambda i: (i, 0))
    return pl.pallas_call(
        _kv_kernel,
        grid=(t // tm,),
        in_specs=[
            row(D_MODEL),
            _const_spec((1, D_MODEL)),
            _const_spec((D_MODEL, 2 * B_W)),
            _const_spec((D_MODEL, B_HEADS)),
            _const_spec((D_MODEL, B_HEADS)),
            _const_spec((B_HEADS, D_MODEL)),
            _const_spec((B_HEADS, D_MODEL)),
            _const_spec((1, B_HEADS)),
            _const_spec((B_HEADS, 1)),
            _const_spec((1, B_W)),
            _const_spec((B_W, B_W)),
        ],
        out_specs=[pl.BlockSpec((tm, B_HEADS, B_DH), lambda i: (i, 0, 0)),
                   pl.BlockSpec((tm, B_HEADS, B_DH), lambda i: (i, 0, 0)), row(B_W), row(B_W),
                   pl.BlockSpec((B_W, tm), lambda i: (0, i)), row(B_HEADS),
                   pl.BlockSpec((B_HEADS, tm), lambda i: (0, i))],
        out_shape=[
            jax.ShapeDtypeStruct((t, B_HEADS, B_DH), F32),
            jax.ShapeDtypeStruct((t, B_HEADS, B_DH), F32),
            jax.ShapeDtypeStruct((t, B_W), BF16),
            jax.ShapeDtypeStruct((t, B_W), BF16),
            jax.ShapeDtypeStruct((B_W, t), BF16),
            jax.ShapeDtypeStruct((t, B_HEADS), F32),
            jax.ShapeDtypeStruct((B_HEADS, t), F32),
        ],
        compiler_params=_params("parallel"),
        name="kv_proj",
    )(h, g, w_kv, *w_f, *w_ft, b_f, b_ft, k_g, gmat)


def _q_kernel(h_ref, g_ref, w_ref, qg_ref, gm_ref, q_ref, z_ref):
    xn = _rms(h_ref[...], g_ref[...])
    qz = jnp.dot(xn.astype(BF16), w_ref[...], preferred_element_type=F32)
    q = qz[:, :B_W]
    z_ref[...] = qz[:, B_W:]
    qn = q * lax.rsqrt(_head_ms(q, gm_ref) + EPS) * qg_ref[...]
    q_ref[...] = (qn * (B_DH ** -0.5 * LOG2E)).astype(BF16)


def _q_proj(h, g, w_in, q_g, gmat):
    t = h.shape[0]
    tm = _tile(t, 512)
    row = lambda w: pl.BlockSpec((tm, w), lambda i: (i, 0))
    return pl.pallas_call(
        _q_kernel,
        grid=(t // tm,),
        in_specs=[row(D_MODEL), _const_spec((1, D_MODEL)), _const_spec((D_MODEL, 2 * B_W)),
                  _const_spec((1, B_W)), _const_spec((B_W, B_W))],
        out_specs=[row(B_W), row(B_W)],
        out_shape=[jax.ShapeDtypeStruct((t, B_W), BF16), jax.ShapeDtypeStruct((t, B_W), F32)],
        compiler_params=_params("parallel"),
        name="q_proj",
    )(h, g, w_in, q_g, gmat)


def _cumsum_kernel(x_ref, f_ref, carry_ref):
    @pl.when(pl.program_id(1) == 0)
    def _():
        carry_ref[...] = jnp.zeros_like(carry_ref)

    w = x_ref.shape[-1]
    f = _seg_cumsum_lanes(x_ref[...], w) + carry_ref[:, 0:1]
    f_ref[...] = f
    carry_ref[...] = jnp.broadcast_to(f[:, w - 1:w], carry_ref.shape)


def _cumsum_rows(x, bsz, width):
    tb = _tile(width, 2048) if width % LANES == 0 and (width // LANES) & (width // LANES - 1) == 0 else width
    nb = width // tb
    if x.ndim == 2:
        in_spec = pl.BlockSpec((B_HEADS, tb), lambda b, j: (0, b * nb + j))
    else:
        in_spec = pl.BlockSpec((None, B_HEADS, tb), lambda b, j: (b, 0, j))
    return pl.pallas_call(
        _cumsum_kernel,
        grid=(bsz, nb),
        in_specs=[in_spec],
        out_specs=pl.BlockSpec((None, B_HEADS, tb), lambda b, j: (b, 0, j)),
        out_shape=jax.ShapeDtypeStruct((bsz, B_HEADS, width), F32),
        scratch_shapes=[pltpu.VMEM((B_HEADS, LANES), F32)],
        compiler_params=_params("parallel", "arbitrary"),
        name="logf_cumsum",
    )(x)


def _cumsum_aug_kernel(x_ref, place_ref, f_ref, ka_ref, carry_ref):
    @pl.when(pl.program_id(1) == 0)
    def _():
        carry_ref[...] = jnp.zeros_like(carry_ref)

    w = x_ref.shape[-1]
    f = _seg_cumsum_lanes(x_ref[...], w) + carry_ref[:, 0:1]
    f_ref[...] = f
    carry_ref[...] = jnp.broadcast_to(f[:, w - 1:w], carry_ref.shape)
    rows = jnp.concatenate(_split3(f * LOG2E) + (jnp.ones((B_HEADS, w), BF16),), axis=0)
    ka_ref[...] = lax.dot_general(rows, place_ref[...], TN, preferred_element_type=F32).astype(BF16)


def _bias_placement():
    r = jnp.arange(4 * B_HEADS)[:, None]
    c = jnp.arange(B_W)[None, :]
    part, head = r // B_HEADS, r % B_HEADS
    lane0 = (head // 2) * LANES + BIAS_LANES * (head % 2)
    is_part = (part < 3) & (c == lane0 + part)
    is_one = (part == 3) & (c >= lane0 + 3) & (c < lane0 + 6)
    return (is_part | is_one).astype(BF16)


def _cumsum_aug(x, bsz, width):
    tb = _tile(width, 2048)
    nb = width // tb
    return pl.pallas_call(
        _cumsum_aug_kernel,
        grid=(bsz, nb),
        in_specs=[pl.BlockSpec((B_HEADS, tb), lambda b, j: (0, b * nb + j)),
                  pl.BlockSpec((4 * B_HEADS, B_W), lambda b, j: (0, 0))],
        out_specs=[pl.BlockSpec((None, B_HEADS, tb), lambda b, j: (b, 0, j)),
                   pl.BlockSpec((tb, B_W), lambda b, j: (b * nb + j, 0))],
        out_shape=[jax.ShapeDtypeStruct((bsz, B_HEADS, width), F32),
                   jax.ShapeDtypeStruct((bsz * width, B_W), BF16)],
        scratch_shapes=[pltpu.VMEM((B_HEADS, LANES), F32)],
        compiler_params=_params("parallel", "arbitrary"),
        name="logf_cumsum_aug",
    )(x, _bias_placement())


def _fox_bounded_kernel(bound_ref, q_ref, k_ref, ka_ref, vt_ref, f_ref, o_ref, qt_ref, acc_ref,
                        p00, p01, p10, p11, *, tq):
    tk = tq // 2
    pbuf = ((p00, p01), (p10, p11))
    qi = pl.program_id(2)
    q0 = pl.multiple_of(qi * tq, tq)
    feat = lax.broadcasted_iota(jnp.int32, (LANES, tq), 0)
    qt = q_ref[...].astype(F32).T
    beta = _split3(f_ref[:, pl.ds(q0, tq)] * LOG2E - bound_ref[...])
    for hh in range(2):
        lo = BIAS_LANES * hh
        qt_ref[hh, 0:LANES, :] = jnp.where((feat >= hh * B_DH) & (feat < (hh + 1) * B_DH), qt, 0.0).astype(BF16)
        aug = jnp.where((feat >= lo) & (feat < lo + 3), -1.0, 0.0)
        for part in range(3):
            aug = jnp.where(feat == lo + 3 + part, beta[part][hh:hh + 1].astype(F32), aug)
        qt_ref[hh, LANES:2 * LANES, :] = aug.astype(BF16)
    acc_ref[...] = jnp.zeros_like(acc_ref)
    ones = jnp.ones((ACC_ROWS - B_DH, tk), BF16)
    kpos = lax.broadcasted_iota(jnp.int32, (tk, tq), 0)
    qpos = lax.broadcasted_iota(jnp.int32, (tk, tq), 1)
    diag_masks = (kpos <= qpos, (kpos + tk) <= qpos)

    def weights(k0, slot, causal=None):
        lhs = jnp.concatenate([k_ref[pl.ds(k0, tk), :], ka_ref[pl.ds(k0, tk), :]], axis=1)
        for hh in range(2):
            e = jnp.dot(lhs, qt_ref[hh], preferred_element_type=F32)
            if causal is not None:
                e = jnp.where(causal, e, NEG)
            pbuf[slot][hh][...] = jnp.exp2(e).astype(BF16)

    def accumulate(k0, slot):
        for hh in range(2):
            vt = jnp.concatenate([vt_ref[hh * B_DH:(hh + 1) * B_DH, pl.ds(k0, tk)], ones], axis=0)
            acc_ref[hh] += jnp.dot(vt, pbuf[slot][hh][...], preferred_element_type=F32)

    def pair(k0, next_causal):
        weights(k0 + tk, 1)
        accumulate(k0, 0)
        weights(k0 + tq, 0, next_causal)
        accumulate(k0 + tk, 1)

    def finish():
        weights(q0 + tk, 1, diag_masks[1])
        accumulate(q0, 0)
        accumulate(q0 + tk, 1)
        o_t = jnp.concatenate(
            [acc_ref[hh, 0:B_DH] * (1.0 / acc_ref[hh, B_DH:B_DH + 1]) for hh in range(2)], axis=0)
        o_ref[...] = o_t.T

    @pl.when(qi == 0)
    def _():
        weights(q0, 0, diag_masks[0])
        finish()

    @pl.when(qi > 0)
    def _():
        weights(0, 0)
        n_pairs = qi - 1

        def unrolled(jj, carry):
            for u in range(PAIR_UNROLL):
                pair(pl.multiple_of((jj * PAIR_UNROLL + u) * tq, tq), None)
            return carry

        def single(jj, carry):
            pair(pl.multiple_of(jj * tq, tq), None)
            return carry

        lax.fori_loop(0, n_pairs // PAIR_UNROLL, unrolled, 0)
        lax.fori_loop((n_pairs // PAIR_UNROLL) * PAIR_UNROLL, n_pairs, single, 0)
        pair(q0 - tq, diag_masks[0])
        finish()


def _fox_prompt_bounded(q, kb, ka, vt, f_row, bound, bsz, seq):
    tq = _tile(seq, 1024)
    nq = seq // tq
    kern = functools.partial(_fox_bounded_kernel, tq=tq)
    return pl.pallas_call(
        kern,
        grid=(bsz, B_PAIRS, nq),
        in_specs=[
            pl.BlockSpec((1, 1), lambda b, p, qi: (0, 0)),
            pl.BlockSpec((tq, LANES), lambda b, p, qi: (b * nq + qi, p)),
            pl.BlockSpec((seq, LANES), lambda b, p, qi: (b, p)),
            pl.BlockSpec((seq, LANES), lambda b, p, qi: (b, p)),
            pl.BlockSpec((LANES, seq), lambda b, p, qi: (p, b)),
            pl.BlockSpec((None, None, 2, seq), lambda b, p, qi: (b, p, 0, 0)),
        ],
        out_specs=pl.BlockSpec((tq, LANES), lambda b, p, qi: (b * nq + qi, p)),
        out_shape=jax.ShapeDtypeStruct((bsz * seq, B_W), F32),
        scratch_shapes=[
            pltpu.VMEM((2, 2 * LANES, tq), BF16),
            pltpu.VMEM((2, ACC_ROWS, tq), F32),
        ] + [pltpu.VMEM((tq // 2, tq), BF16)] * 4,
        compiler_params=_params("parallel", "parallel", "arbitrary"),
        name="fox_prompt_bounded",
    )(bound, q, kb, ka, vt, f_row)


def _fox_kernel(q_ref, k_ref, vt_ref, f_ref, o_ref, qt_ref, m_ref, cm_ref, acc_ref,
                s00, s01, s10, s11, *, tq):
    tk = tq // 2
    sbuf = ((s00, s01), (s10, s11))
    qi = pl.program_id(2)
    q0 = pl.multiple_of(qi * tq, tq)
    feat = lax.broadcasted_iota(jnp.int32, (LANES, tq), 0)
    qt = q_ref[...].astype(F32).T
    qt_ref[0] = jnp.where(feat < B_DH, qt, 0.0).astype(BF16)
    qt_ref[1] = jnp.where(feat >= B_DH, qt, 0.0).astype(BF16)
    m_ref[...] = jnp.full_like(m_ref, NEG)
    acc_ref[...] = jnp.zeros_like(acc_ref)
    f_base = f_ref[:, pl.ds(q0, tq)][:, tq - 1:tq]
    ones = jnp.ones((ACC_ROWS - B_DH, tk), BF16)
    kpos = lax.broadcasted_iota(jnp.int32, (tk, tq), 0)
    qpos = lax.broadcasted_iota(jnp.int32, (tk, tq), 1)
    diag_masks = (kpos <= qpos, (kpos + tk) <= qpos)

    def scores(k0, slot, causal=None):
        kb = k_ref[pl.ds(k0, tk), :]
        fk = (f_ref[:, pl.ds(k0, tk)] - f_base) * LOG2E
        for hh in range(2):
            fk_col = jnp.concatenate(
                [jnp.broadcast_to(fk[hh:hh + 1, c * LANES:(c + 1) * LANES], (LANES, LANES)).T
                 for c in range(tk // LANES)], axis=0)
            t = (jnp.dot(kb, qt_ref[hh], preferred_element_type=F32)
                 - pltpu.repeat(fk_col, tq // LANES, axis=1))
            if causal is not None:
                t = jnp.where(causal, t, NEG)
            sbuf[slot][hh][...] = t
            cm_ref[slot, hh] = jnp.max(t, axis=0, keepdims=True)

    def update(k0, slot):
        for hh in range(2):
            m_prev = m_ref[hh]
            m_new = jnp.maximum(m_prev, cm_ref[slot, hh])
            alpha = jnp.exp2(m_prev - m_new)
            p = jnp.exp2(sbuf[slot][hh][...] - m_new).astype(BF16)
            vt = jnp.concatenate([vt_ref[hh * B_DH:(hh + 1) * B_DH, pl.ds(k0, tk)], ones], axis=0)
            acc_ref[hh] = alpha * acc_ref[hh] + jnp.dot(vt, p, preferred_element_type=F32)
            m_ref[hh] = m_new

    def pair(k0, next_causal):
        scores(k0 + tk, 1)
        update(k0, 0)
        scores(k0 + tq, 0, next_causal)
        update(k0 + tk, 1)

    @pl.when(qi == 0)
    def _():
        scores(q0, 0, diag_masks[0])

    @pl.when(qi > 0)
    def _():
        scores(0, 0)

        def body(jj, carry):
            pair(pl.multiple_of(jj * tq, tq), None)
            return carry

        lax.fori_loop(0, qi - 1, body, 0)
        pair(q0 - tq, diag_masks[0])

    scores(q0 + tk, 1, diag_masks[1])
    update(q0, 0)
    update(q0 + tk, 1)
    o_t = jnp.concatenate(
        [acc_ref[hh, 0:B_DH] * (1.0 / acc_ref[hh, B_DH:B_DH + 1]) for hh in range(2)], axis=0)
    o_ref[...] = o_t.T


def _fox_prompt(q, kb, vt, f_row, bsz, seq):
    tq = _tile(seq, 512)
    tk = tq // 2
    nq = seq // tq
    kern = functools.partial(_fox_kernel, tq=tq)
    return pl.pallas_call(
        kern,
        grid=(bsz, B_PAIRS, nq),
        in_specs=[
            pl.BlockSpec((tq, LANES), lambda b, p, qi: (b * nq + qi, p)),
            pl.BlockSpec((seq, LANES), lambda b, p, qi: (b, p)),
            pl.BlockSpec((LANES, seq), lambda b, p, qi: (p, b)),
            pl.BlockSpec((None, None, 2, seq), lambda b, p, qi: (b, p, 0, 0)),
        ],
        out_specs=pl.BlockSpec((tq, LANES), lambda b, p, qi: (b * nq + qi, p)),
        out_shape=jax.ShapeDtypeStruct((bsz * seq, B_W), F32),
        scratch_shapes=[
            pltpu.VMEM((2, LANES, tq), BF16),
            pltpu.VMEM((2, 1, tq), F32),
            pltpu.VMEM((2, 2, 1, tq), F32),
            pltpu.VMEM((2, ACC_ROWS, tq), F32),
        ] + [pltpu.VMEM((tk, tq), F32)] * 4,
        compiler_params=_params("parallel", "parallel", "arbitrary"),
        name="fox_prompt",
    )(q, kb, vt, f_row)


def _fox_dec_kernel(q_ref, kt_ref, vt_ref, kn_ref, vn_ref, fn_ref, fc_ref, o_ref,
                    fqc_ref, m_ref, l_ref, a_ref, acc_ref, s_ref, p_ref, *, sq, nkb):
    kj = pl.program_id(1)

    @pl.when(kj == 0)
    def _():
        r = lax.broadcasted_iota(jnp.int32, (sq, LANES), 0)
        c = lax.broadcasted_iota(jnp.int32, (sq, LANES), 1)
        for h in range(B_HEADS):
            fqc_ref[h] = jnp.sum(jnp.where(r == c, fn_ref[h:h + 1, :], 0.0), axis=1, keepdims=True)
        m_ref[...] = jnp.full_like(m_ref, NEG)
        l_ref[...] = jnp.zeros_like(l_ref)
        acc_ref[...] = jnp.zeros_like(acc_ref)

    def attend(width, score, f_keys, values, valid):
        for h in range(B_HEADS):
            s_ref[h, :, 0:width] = score(h)
        for h in range(B_HEADS):
            t = s_ref[h, :, 0:width] + (fqc_ref[h] - f_keys(h)) * LOG2E
            if valid is not None:
                t = jnp.where(valid, t, NEG)
            m_prev = m_ref[h]
            m_new = jnp.maximum(m_prev, jnp.max(t, axis=1, keepdims=True))
            alpha = jnp.exp2(m_prev - m_new)
            p = jnp.exp2(t - m_new)
            l_ref[h] = alpha * l_ref[h] + jnp.sum(p, axis=1, keepdims=True)
            m_ref[h] = m_new
            a_ref[h] = alpha
            p_ref[h, :, 0:width] = p.astype(BF16)
        for h in range(B_HEADS):
            acc_ref[h] = a_ref[h] * acc_ref[h] + values(h, p_ref[h, :, 0:width])

    @pl.when(kj < nkb)
    def _():
        attend(
            kt_ref.shape[-1],
            lambda h: jnp.dot(q_ref[:, h * B_DH:(h + 1) * B_DH], kt_ref[h].astype(BF16),
                              preferred_element_type=F32),
            lambda h: fc_ref[h:h + 1, :],
            lambda h, p: lax.dot_general(p, vt_ref[h].astype(BF16), NT, preferred_element_type=F32),
            None)

    @pl.when(kj == nkb)
    def _():
        i = lax.broadcasted_iota(jnp.int32, (sq, LANES), 0)
        j = lax.broadcasted_iota(jnp.int32, (sq, LANES), 1)
        head = lambda h: slice(h * B_DH, (h + 1) * B_DH)
        attend(
            LANES,
            lambda h: lax.dot_general(q_ref[:, head(h)], kn_ref[:, head(h)], NT, preferred_element_type=F32),
            lambda h: fn_ref[h:h + 1, :],
            lambda h, p: jnp.dot(p, vn_ref[:, head(h)], preferred_element_type=F32),
            j <= i)
        for h in range(B_HEADS):
            o_ref[:, head(h)] = acc_ref[h] * (1.0 / l_ref[h])


def _fox_decode(q, cache_kt, cache_vt, kn, vn, f_new, f_cache, bsz, sq):
    past = cache_kt.shape[3]
    tk = _tile(past, 2048)
    nkb = past // tk
    cmap = lambda b, kj: (b, 0, 0, jnp.minimum(kj, nkb - 1))
    kern = functools.partial(_fox_dec_kernel, sq=sq, nkb=nkb)
    return pl.pallas_call(
        kern,
        grid=(bsz, nkb + 1),
        in_specs=[
            pl.BlockSpec((None, sq, B_W), lambda b, kj: (b, 0, 0)),
            pl.BlockSpec((None, B_HEADS, B_DH, tk), cmap),
            pl.BlockSpec((None, B_HEADS, B_DH, tk), cmap),
            pl.BlockSpec((None, LANES, B_W), lambda b, kj: (b, 0, 0)),
            pl.BlockSpec((None, LANES, B_W), lambda b, kj: (b, 0, 0)),
            pl.BlockSpec((None, B_HEADS, LANES), lambda b, kj: (b, 0, 0)),
            pl.BlockSpec((None, B_HEADS, tk), lambda b, kj: (b, 0, jnp.minimum(kj, nkb - 1))),
        ],
        out_specs=pl.BlockSpec((None, sq, B_W), lambda b, kj: (b, 0, 0)),
        out_shape=jax.ShapeDtypeStruct((bsz, sq, B_W), F32),
        scratch_shapes=[
            pltpu.VMEM((B_HEADS, sq, 1), F32),
            pltpu.VMEM((B_HEADS, sq, 1), F32),
            pltpu.VMEM((B_HEADS, sq, 1), F32),
            pltpu.VMEM((B_HEADS, sq, 1), F32),
            pltpu.VMEM((B_HEADS, sq, B_DH), F32),
            pltpu.VMEM((B_HEADS, sq, tk), F32),
            pltpu.VMEM((B_HEADS, sq, tk), BF16),
        ],
        compiler_params=_params("parallel", "arbitrary"),
        name="fox_decode",
    )(q, cache_kt, cache_vt, kn, vn, f_new, f_cache)


def _b_post_kernel(o_ref, z_ref, h_ref, p_ref, wo_ref, pw_ref, pn_ref, pg_ref, out_ref):
    z = z_ref[...]
    y = o_ref[...] * (z * _sigmoid(z))
    h2 = h_ref[...] + jnp.dot(y.astype(BF16), wo_ref[...], preferred_element_type=F32)
    out_ref[...] = _ple(h2, p_ref, pw_ref, pn_ref, pg_ref)


def _b_post(o, z, h, p, layer, w_out, ple_w, ple_norm, ple_wg):
    t = h.shape[0]
    tm = _tile(t, 512)
    row = lambda w: pl.BlockSpec((tm, w), lambda i: (i, 0))
    return pl.pallas_call(
        _b_post_kernel,
        grid=(t // tm,),
        in_specs=[row(B_W), row(B_W), row(D_MODEL), pl.BlockSpec((None, tm, PLE_DIM), lambda i: (layer, i, 0)),
                  _const_spec((B_W, D_MODEL)), _const_spec((PLE_DIM, D_MODEL)),
                  _const_spec((1, D_MODEL)), _const_spec((D_MODEL, D_MODEL))],
        out_specs=row(D_MODEL),
        out_shape=jax.ShapeDtypeStruct((t, D_MODEL), F32),
        compiler_params=_params("parallel"),
        name="b_post",
    )(o, z, h, p, w_out, ple_w, ple_norm, ple_wg)


def _prep_weights(a_norm, a_w_in, a_b_i, a_b_f, a_hnorm, a_w_out, kv_norm, kv_w, kv_b_f, k_norm,
                  b_norm, b_w_in, q_norm, b_w_out, ple_w, ple_norm, ple_w_g):
    head_of_lane = jnp.arange(B_W, dtype=jnp.int32) // B_DH
    return dict(
        a_norm=a_norm[0].reshape(1, D_MODEL),
        a_w_main=a_w_in[0][:, :A_MAIN].astype(BF16),
        a_w_gt=_split(a_w_in[0][:, A_MAIN:].T),
        a_bias=jnp.concatenate([a_b_i[0], a_b_f[0]]).reshape(2 * A_HEADS, 1),
        a_hnorm=a_hnorm[0].reshape(1, A_V),
        a_w_out=a_w_out[0].astype(BF16),
        kv_norm=kv_norm.reshape(1, D_MODEL),
        kv_w=kv_w[:, :2 * B_W].astype(BF16),
        kv_wf=_split(kv_w[:, 2 * B_W:]),
        kv_wft=_split(kv_w[:, 2 * B_W:].T),
        kv_bf=kv_b_f.reshape(1, B_HEADS),
        kv_bft=kv_b_f.reshape(B_HEADS, 1),
        k_norm=jnp.tile(k_norm, B_HEADS).reshape(1, B_W),
        gmat=(head_of_lane[:, None] == head_of_lane[None, :]).astype(BF16),
        b_norm=b_norm[0].reshape(1, D_MODEL),
        b_w_in=b_w_in[0].astype(BF16),
        q_norm=jnp.tile(q_norm[0], B_HEADS).reshape(1, B_W),
        b_w_out=b_w_out[0].astype(BF16),
        ple_w=ple_w.astype(BF16),
        ple_norm=ple_norm.reshape(-1, 1, D_MODEL),
        ple_wg=ple_w_g.astype(BF16),
    )


def _trunk(x, p, c0, n0, m0, past, w):
    bsz, seq, _ = x.shape
    t = bsz * seq
    xf = x.reshape(t, D_MODEL)
    pf = p.reshape(p.shape[0], t, PLE_DIM)

    u, gt = _a_inproj(xf, w["a_norm"], w["a_w_main"], w["a_w_gt"])
    m0r = jnp.broadcast_to(m0.reshape(bsz, A_HEADS, 1, 1), (bsz, A_HEADS, 1, LANES))
    hs, c_new, n_new, m_new = _mlstm(u, gt, w["a_bias"], c0, n0[..., None], m0r, bsz, seq)
    h1 = _a_post(hs, u, xf, pf, 0, w["a_hnorm"], w["a_w_out"], w["ple_w"][0], w["ple_norm"][0], w["ple_wg"][0])

    k, v, kb, vb, vt, lf, lft = _kv_proj(h1, w["kv_norm"], w["kv_w"], w["kv_wf"], w["kv_wft"],
                                     w["kv_bf"], w["kv_bft"], w["k_norm"], w["gmat"])
    q, z = _q_proj(h1, w["b_norm"], w["b_w_in"], w["q_norm"], w["gmat"])
    if past is None:
        f_row, ka = _cumsum_aug(lft, bsz, seq)
        f_row = f_row.reshape(bsz, B_PAIRS, 2, seq)
        bound = (8.0 * LOG2E * 1.01) * jnp.max(jnp.abs(w["q_norm"])) * jnp.max(jnp.abs(w["k_norm"])) + 0.05
        o = lax.cond(
            bound <= MAX_BOUND,
            lambda: _fox_prompt_bounded(q, kb, ka, vt, f_row, bound.reshape(1, 1), bsz, seq),
            lambda: _fox_prompt(q, kb, vt, f_row, bsz, seq))
    else:
        cache_k, cache_v, cache_lf = past
        plen = cache_k.shape[1]
        assert seq <= LANES
        width = -(-(plen + seq) // LANES) * LANES
        lf_rows = jnp.concatenate(
            [jnp.swapaxes(cache_lf.astype(F32), 1, 2),
             jnp.swapaxes(lft.reshape(B_HEADS, bsz, seq), 0, 1),
             jnp.zeros((bsz, B_HEADS, width - plen - seq), F32)], axis=2)
        f_all = _cumsum_rows(lf_rows, bsz, width)
        f_new = jnp.pad(f_all[..., plen:plen + seq], ((0, 0), (0, 0), (0, LANES - seq)))
        pad_new = lambda a: jnp.pad(a.reshape(bsz, seq, B_W), ((0, 0), (0, LANES - seq), (0, 0)))
        to_t = lambda a: jnp.transpose(a.astype(F32), (0, 2, 3, 1))
        o = _fox_decode(q.reshape(bsz, seq, B_W), to_t(cache_k), to_t(cache_v), pad_new(kb), pad_new(vb),
                        f_new, f_all[..., :plen], bsz, seq).reshape(t, B_W)
    y = _b_post(o, z, h1, pf, 1, w["b_w_out"], w["ple_w"][1], w["ple_norm"][1], w["ple_wg"][1])

    return (y.reshape(bsz, seq, D_MODEL),
            c_new[None], n_new.reshape(1, bsz, A_HEADS, A_DK), m_new[None, :, :, 0, 0],
            k.reshape(bsz, seq, B_HEADS, B_DH), v.reshape(bsz, seq, B_HEADS, B_DH),
            lf.reshape(bsz, seq, B_HEADS))


def kernel(x_prompt, x_sample, cache_k, cache_v, cache_logf, state_C, state_n, state_m, p_prompt, p_sample,
           a_norm, a_w_in, a_b_i, a_b_f, a_hnorm, a_w_out, kv_norm, kv_w, kv_b_f, k_norm,
           b_norm, b_w_in, q_norm, b_w_out, ple_w, ple_norm, ple_w_g):
    w = _prep_weights(a_norm, a_w_in, a_b_i, a_b_f, a_hnorm, a_w_out, kv_norm, kv_w, kv_b_f, k_norm,
                      b_norm, b_w_in, q_norm, b_w_out, ple_w, ple_norm, ple_w_g)
    bsz = x_prompt.shape[0]
    c0 = jnp.zeros((bsz, A_HEADS, A_DK, A_DV), F32)
    n0 = jnp.zeros((bsz, A_HEADS, A_DK), F32)
    m0 = jnp.zeros((bsz, A_HEADS), F32)
    prompt = _trunk(x_prompt, p_prompt, c0, n0, m0, None, w)
    sample = _trunk(x_sample, p_sample, state_C[0].astype(F32), state_n[0].astype(F32),
                    state_m[0].astype(F32), (cache_k, cache_v, cache_logf), w)
    return (prompt[0], sample[0]) + prompt[1:] + sample[1:]
```

```python
import functools

import jax
import jax.numpy as jnp
from jax import lax
from jax.experimental import pallas as pl
from jax.experimental.pallas import tpu as pltpu

D_MODEL = 1024
A_HEADS = 8
A_DK = 128
A_DV = 256
A_QK = A_HEADS * A_DK
A_V = A_HEADS * A_DV
A_MAIN = 2 * A_QK + 3 * A_V
B_HEADS = 16
B_DH = 64
B_W = B_HEADS * B_DH
B_PAIRS = B_HEADS // 2
PLE_DIM = 256
EPS = 1e-6
NEG = -1e30
LOG2E = 1.4426950408889634
ACC_ROWS = B_DH + 16
BIAS_LANES = 6
MAX_BOUND = 60.0
PAIR_UNROLL = 4
MLSTM_ROWS = 128

LANES = 128
VMEM_LIMIT = 56 * 1024 * 1024

F32 = jnp.float32
BF16 = jnp.bfloat16
NN = (((1,), (0,)), ((), ()))
NT = (((1,), (1,)), ((), ()))
TN = (((0,), (0,)), ((), ()))


def _params(*sem):
    return pltpu.CompilerParams(dimension_semantics=sem, vmem_limit_bytes=VMEM_LIMIT)


def _tile(n, pref):
    t = min(n, pref)
    while n % t:
        t //= 2
    return t


def _rms(x, g):
    ms = jnp.mean(x * x, axis=-1, keepdims=True)
    return x * lax.rsqrt(ms + EPS) * g


def _log_sigmoid(x):
    return jnp.minimum(x, 0.0) - jnp.log1p(jnp.exp(-jnp.abs(x)))


def _sigmoid(x):
    return 0.5 * jnp.tanh(0.5 * x) + 0.5


def _diag_to_col(row, eye):
    n = eye.shape[0]
    return jnp.sum(jnp.where(eye, jnp.broadcast_to(row, (n, n)), 0.0), axis=1, keepdims=True)


def _seg_cummax_lanes(x, seg):
    w = x.shape[-1]
    pos = lax.broadcasted_iota(jnp.int32, x.shape, x.ndim - 1) % seg
    k = 1
    while k < min(seg, w):
        x = jnp.maximum(x, jnp.where(pos >= k, pltpu.roll(x, k, x.ndim - 1), NEG))
        k *= 2
    return x


def _seg_cumsum_lanes(x, seg):
    w = x.shape[-1]
    pos = lax.broadcasted_iota(jnp.int32, x.shape, x.ndim - 1) % seg
    k = 1
    while k < min(seg, w):
        x = x + jnp.where(pos >= k, pltpu.roll(x, k, x.ndim - 1), 0.0)
        k *= 2
    return x


def _split(x):
    hi = x.astype(BF16)
    return hi, (x - hi.astype(F32)).astype(BF16)


def _dot_split(a, b, dims):
    dot = functools.partial(lax.dot_general, dimension_numbers=dims, preferred_element_type=F32)
    return dot(a[0], b[0]) + (dot(a[0], b[1]) + dot(a[1], b[0]))


def _split3(x):
    hi = x.astype(BF16)
    r = x - hi.astype(F32)
    mid = r.astype(BF16)
    return hi, mid, (r - mid.astype(F32)).astype(BF16)


def _a_inproj_kernel(x_ref, g_ref, w_ref, wgh_ref, wgl_ref, u_ref, gt_ref, xn_ref):
    @pl.when(pl.program_id(1) == 0)
    def _():
        xn = _rms(x_ref[...], g_ref[...])
        xs = _split(xn)
        xn_ref[...] = xs[0]
        gt_ref[...] = _dot_split((wgh_ref[...], wgl_ref[...]), xs, NT)

    u_ref[...] = jnp.dot(xn_ref[...], w_ref[...], preferred_element_type=F32).astype(BF16)


def _a_inproj(x, g, w_main, w_gt):
    t = x.shape[0]
    tm = _tile(t, 1024)
    tn = 1024
    gate_spec = pl.BlockSpec((2 * A_HEADS, D_MODEL), lambda i, j: (0, 0))
    return pl.pallas_call(
        _a_inproj_kernel,
        grid=(t // tm, A_MAIN // tn),
        in_specs=[
            pl.BlockSpec((tm, D_MODEL), lambda i, j: (i, 0)),
            pl.BlockSpec((1, D_MODEL), lambda i, j: (0, 0)),
            pl.BlockSpec((D_MODEL, tn), lambda i, j: (0, j)),
            gate_spec,
            gate_spec,
        ],
        out_specs=[
            pl.BlockSpec((tm, tn), lambda i, j: (i, j)),
            pl.BlockSpec((2 * A_HEADS, tm), lambda i, j: (0, i)),
        ],
        out_shape=[
            jax.ShapeDtypeStruct((t, A_MAIN), BF16),
            jax.ShapeDtypeStruct((2 * A_HEADS, t), F32),
        ],
        scratch_shapes=[pltpu.VMEM((tm, D_MODEL), BF16)],
        compiler_params=_params("parallel", "arbitrary"),
        name="a_inproj",
    )(x, g, w_main, *w_gt)


def _mlstm_kernel(q_ref, k_ref, v_ref, g_ref, bias_ref, c0_ref, n0_ref, m0_ref,
                  h_ref, c_out_ref, n_out_ref, m_ref, c_ref, gate_ref, *, rows, n_seq, seq_len, n_chunks, steps):
    R = rows

    @pl.when(pl.program_id(1) == 0)
    def _():
        c_ref[:, :, :, 0:A_DV] = c0_ref[...]
        c_ref[:, :, :, A_DV:] = jnp.broadcast_to(n0_ref[...], (n_seq, A_HEADS, A_DK, LANES))
        m_ref[...] = m0_ref[...]

    assert R == LANES
    row = lax.broadcasted_iota(jnp.int32, (R, R), 0)
    col = lax.broadcasted_iota(jnp.int32, (R, R), 1)
    valid = (col <= row) & ((row // seq_len) == (col // seq_len))
    lane8 = lax.broadcasted_iota(jnp.int32, (A_HEADS, R), 1)
    bias = bias_ref[...]
    scale = A_DK ** -0.5
    ones = jnp.ones((R, LANES), BF16)
    g = g_ref[...]
    ig_blk = g[0:A_HEADS] + bias[0:A_HEADS]
    b_blk = _seg_cumsum_lanes(_log_sigmoid(g[A_HEADS:] + bias[A_HEADS:]), seq_len)
    gate_ref[0:A_HEADS, :] = ig_blk
    gate_ref[A_HEADS:2 * A_HEADS, :] = b_blk
    gate_ref[2 * A_HEADS:, :] = _seg_cummax_lanes(ig_blk - b_blk, seq_len)

    def per_seq_rows(vals):
        out = vals[0]
        for s in range(1, n_seq):
            out = jnp.where(lane8 >= s * seq_len, vals[s], out)
        return jnp.broadcast_to(out, (A_HEADS, R))

    def chunk(r0):
        rows = pl.ds(r0, R)
        ig = gate_ref[0:A_HEADS, rows]
        b = gate_ref[A_HEADS:2 * A_HEADS, rows]
        cmax = gate_ref[2 * A_HEADS:, rows]
        m_prev = [m_ref[s, :, 0, 0:1] for s in range(n_seq)]
        m_prev_rows = per_seq_rows(m_prev)
        m_t = b + jnp.maximum(m_prev_rows, cmax)
        a = b - m_t
        ends = [(s + 1) * seq_len - 1 for s in range(n_seq)]
        b_last = [b[:, e:e + 1] for e in ends]
        m_new = [m_t[:, e:e + 1] for e in ends]
        w_s = jnp.exp(per_seq_rows(b_last) - b + ig - per_seq_rows(m_new)) * scale
        w_old = [jnp.exp(b_last[s] + m_prev[s] - m_new[s]) for s in range(n_seq)]
        for s in range(n_seq):
            m_ref[s, :, 0, :] = jnp.broadcast_to(m_new[s], (A_HEADS, LANES))
        cols = jnp.concatenate([a, jnp.exp(a + m_prev_rows), jnp.exp(-m_t), w_s,
                                jnp.zeros((R - 4 * A_HEADS, R), F32)], axis=0).T
        c_rows = ig - b
        for h in range(A_HEADS):
            qb = q_ref[rows, h * A_DK:(h + 1) * A_DK]
            kb = k_ref[rows, h * A_DK:(h + 1) * A_DK]
            vb = jnp.concatenate([v_ref[rows, h * A_DV:(h + 1) * A_DV], ones], axis=1)
            a_col = cols[:, h:h + 1]
            w_inter = cols[:, A_HEADS + h:A_HEADS + h + 1]
            e_m = cols[:, 2 * A_HEADS + h:2 * A_HEADS + h + 1]
            w_s_col = cols[:, 3 * A_HEADS + h:3 * A_HEADS + h + 1]
            p = jnp.exp(jnp.where(valid, a_col + c_rows[h:h + 1], NEG))
            qk = lax.dot_general(qb, kb, NT, preferred_element_type=F32) * scale
            s_mat = qk * p
            qc = [jnp.dot(qb[s * seq_len:(s + 1) * seq_len], c_ref[s, h].astype(BF16),
                          preferred_element_type=F32) for s in range(n_seq)]
            qc = qc[0] if n_seq == 1 else jnp.concatenate(qc, axis=0)
            nd = w_inter * qc + jnp.dot(s_mat.astype(BF16), vb, preferred_element_type=F32)
            den = nd[:, A_DV:A_DV + 1]
            h_ref[rows, h * A_DV:(h + 1) * A_DV] = nd[:, 0:A_DV] * (1.0 / jnp.maximum(jnp.abs(den), e_m))
            ksb = (kb.astype(F32) * w_s_col).astype(BF16)
            for s in range(n_seq):
                sl = slice(s * seq_len, (s + 1) * seq_len)
                c_ref[s, h] = w_old[s][h:h + 1] * c_ref[s, h] + lax.dot_general(
                    ksb[sl], vb[sl], TN, preferred_element_type=F32)

    if n_chunks == 1:
        chunk(0)
    else:
        def body(ci, carry):
            chunk(pl.multiple_of(ci * R, R))
            return carry
        lax.fori_loop(0, n_chunks, body, 0)

    @pl.when(pl.program_id(1) == steps - 1)
    def _():
        c_out_ref[...] = c_ref[:, :, :, 0:A_DV]
        n_out_ref[...] = c_ref[:, :, :, A_DV:A_DV + 1]


def _mlstm(u, gt, bias, c0, n0, m0, bsz, seq):
    if seq % LANES == 0:
        rows = _tile(seq, MLSTM_ROWS)
        seq_len, n_seq = rows, 1
        n_chunks = _tile(seq // rows, 1024 // rows)
    else:
        assert LANES % seq == 0 and bsz % (LANES // seq) == 0, (bsz, seq)
        rows, seq_len, n_seq, n_chunks = LANES, seq, LANES // seq, 1
    tok = n_chunks * rows
    steps = seq // (n_chunks * seq_len)
    t = bsz * seq
    tok_map = lambda b, c: (b * steps + c, 0)
    state4 = lambda shape: pl.BlockSpec((n_seq,) + shape, lambda b, c: (b, 0, 0, 0))
    kern = functools.partial(_mlstm_kernel, rows=rows, n_seq=n_seq, seq_len=seq_len, n_chunks=n_chunks, steps=steps)
    return pl.pallas_call(
        kern,
        grid=(bsz // n_seq, steps),
        in_specs=[
            pl.BlockSpec((tok, A_QK), tok_map),
            pl.BlockSpec((tok, A_QK), lambda b, c: (b * steps + c, 1)),
            pl.BlockSpec((tok, A_V), lambda b, c: (b * steps + c, 1)),
            pl.BlockSpec((2 * A_HEADS, tok), lambda b, c: (0, b * steps + c)),
            pl.BlockSpec((2 * A_HEADS, 1), lambda b, c: (0, 0)),
            state4((A_HEADS, A_DK, A_DV)),
            state4((A_HEADS, A_DK, 1)),
            state4((A_HEADS, 1, LANES)),
        ],
        out_specs=[
            pl.BlockSpec((tok, A_V), tok_map),
            state4((A_HEADS, A_DK, A_DV)),
            state4((A_HEADS, A_DK, 1)),
            state4((A_HEADS, 1, LANES)),
        ],
        out_shape=[
            jax.ShapeDtypeStruct((t, A_V), F32),
            jax.ShapeDtypeStruct((bsz, A_HEADS, A_DK, A_DV), F32),
            jax.ShapeDtypeStruct((bsz, A_HEADS, A_DK, 1), F32),
            jax.ShapeDtypeStruct((bsz, A_HEADS, 1, LANES), F32),
        ],
        scratch_shapes=[pltpu.VMEM((n_seq, A_HEADS, A_DK, A_DV + LANES), F32),
                        pltpu.VMEM((3 * A_HEADS, tok), F32)],
        compiler_params=_params("parallel", "arbitrary"),
        name="mlstm",
    )(u, u, u, gt, bias, c0, n0, m0)


def _ple(h, p_ref, pw_ref, pn_ref, pg_ref):
    pp = jnp.dot(p_ref[...].astype(BF16), pw_ref[...], preferred_element_type=F32)
    hn = _rms(h, pn_ref[...])
    gate = _sigmoid(jnp.dot(hn.astype(BF16), pg_ref[...], preferred_element_type=F32))
    return h + pp * gate


def _a_post_kernel(hs_ref, o_ref, z_ref, x_ref, p_ref, hg_ref, wo_ref, pw_ref, pn_ref, pg_ref,
                   out_ref, y_ref):
    for h in range(A_HEADS):
        sl = slice(h * A_DV, (h + 1) * A_DV)
        y = _rms(hs_ref[:, sl], hg_ref[:, sl]) * _sigmoid(o_ref[:, sl].astype(F32))
        z = z_ref[:, sl].astype(F32)
        y_ref[:, sl] = (y * (z * _sigmoid(z))).astype(BF16)
    h1 = x_ref[...] + jnp.dot(y_ref[...], wo_ref[...], preferred_element_type=F32)
    out_ref[...] = _ple(h1, p_ref, pw_ref, pn_ref, pg_ref)


def _const_spec(shape):
    return pl.BlockSpec(shape, lambda i: (0,) * len(shape))


def _a_post(hs, u, x, p, layer, hnorm, w_out, ple_w, ple_norm, ple_wg):
    t = x.shape[0]
    tm = _tile(t, 256)
    return pl.pallas_call(
        _a_post_kernel,
        grid=(t // tm,),
        in_specs=[
            pl.BlockSpec((tm, A_V), lambda i: (i, 0)),
            pl.BlockSpec((tm, A_V), lambda i: (i, 2)),
            pl.BlockSpec((tm, A_V), lambda i: (i, 3)),
            pl.BlockSpec((tm, D_MODEL), lambda i: (i, 0)),
            pl.BlockSpec((None, tm, PLE_DIM), lambda i: (layer, i, 0)),
            _const_spec((1, A_V)),
            _const_spec((A_V, D_MODEL)),
            _const_spec((PLE_DIM, D_MODEL)),
            _const_spec((1, D_MODEL)),
            _const_spec((D_MODEL, D_MODEL)),
        ],
        out_specs=pl.BlockSpec((tm, D_MODEL), lambda i: (i, 0)),
        out_shape=jax.ShapeDtypeStruct((t, D_MODEL), F32),
        scratch_shapes=[pltpu.VMEM((tm, A_V), BF16)],
        compiler_params=_params("parallel"),
        name="a_post",
    )(hs, u, u, x, p, hnorm, w_out, ple_w, ple_norm, ple_wg)


def _head_ms(x, gm_ref):
    return jnp.dot((x * x).astype(BF16), gm_ref[...], preferred_element_type=F32) * (1.0 / B_DH)


def _kv_kernel(h_ref, g_ref, w_ref, wfh_ref, wfl_ref, wfth_ref, wftl_ref, bf_ref, bft_ref, kg_ref, gm_ref,
               k_ref, v_ref, kb_ref, vb_ref, vt_ref, lf_ref, lft_ref):
    xs = _split(_rms(h_ref[...], g_ref[...]))
    kv = jnp.dot(xs[0], w_ref[...], preferred_element_type=F32)
    k = kv[:, :B_W]
    v = kv[:, B_W:]
    kn = k * lax.rsqrt(_head_ms(k, gm_ref) + EPS) * kg_ref[...]
    k_ref[...] = pltpu.einshape("m(hd)->mhd", kn, h=B_HEADS)
    v_ref[...] = pltpu.einshape("m(hd)->mhd", v, h=B_HEADS)
    kb_ref[...] = kn.astype(BF16)
    vb_ref[...] = v.astype(BF16)
    vt_ref[...] = v.T.astype(BF16)
    fg = _dot_split(xs, (wfh_ref[...], wfl_ref[...]), NN)
    lf_ref[...] = _log_sigmoid(fg + bf_ref[...])
    fgt = _dot_split((wfth_ref[...], wftl_ref[...]), xs, NT)
    lft_ref[...] = _log_sigmoid(fgt + bft_ref[...])


def _kv_proj(h, g, w_kv, w_f, w_ft, b_f, b_ft, k_g, gmat):
    t = h.shape[0]
    tm = _tile(t, 512)
    row = lambda w: pl.BlockSpec((tm, w), lambda i: (i, 0))
    return pl.pallas_call(
        _kv_kernel,
        grid=(t // tm,),
        in_specs=[
            row(D_MODEL),
            _const_spec((1, D_MODEL)),
            _const_spec((D_MODEL, 2 * B_W)),
            _const_spec((D_MODEL, B_HEADS)),
            _const_spec((D_MODEL, B_HEADS)),
            _const_spec((B_HEADS, D_MODEL)),
            _const_spec((B_HEADS, D_MODEL)),
            _const_spec((1, B_HEADS)),
            _const_spec((B_HEADS, 1)),
            _const_spec((1, B_W)),
            _const_spec((B_W, B_W)),
        ],
        out_specs=[pl.BlockSpec((tm, B_HEADS, B_DH), lambda i: (i, 0, 0)),
                   pl.BlockSpec((tm, B_HEADS, B_DH), lambda i: (i, 0, 0)), row(B_W), row(B_W),
                   pl.BlockSpec((B_W, tm), lambda i: (0, i)), row(B_HEADS),
                   pl.BlockSpec((B_HEADS, tm), lambda i: (0, i))],
        out_shape=[
            jax.ShapeDtypeStruct((t, B_HEADS, B_DH), F32),
            jax.ShapeDtypeStruct((t, B_HEADS, B_DH), F32),
            jax.ShapeDtypeStruct((t, B_W), BF16),
            jax.ShapeDtypeStruct((t, B_W), BF16),
            jax.ShapeDtypeStruct((B_W, t), BF16),
            jax.ShapeDtypeStruct((t, B_HEADS), F32),
            jax.ShapeDtypeStruct((B_HEADS, t), F32),
        ],
        compiler_params=_params("parallel"),
        name="kv_proj",
    )(h, g, w_kv, *w_f, *w_ft, b_f, b_ft, k_g, gmat)


def _q_kernel(h_ref, g_ref, w_ref, qg_ref, gm_ref, q_ref, z_ref):
    xn = _rms(h_ref[...], g_ref[...])
    qz = jnp.dot(xn.astype(BF16), w_ref[...], preferred_element_type=F32)
    q = qz[:, :B_W]
    z_ref[...] = qz[:, B_W:]
    qn = q * lax.rsqrt(_head_ms(q, gm_ref) + EPS) * qg_ref[...]
    q_ref[...] = (qn * (B_DH ** -0.5 * LOG2E)).astype(BF16)


def _q_proj(h, g, w_in, q_g, gmat):
    t = h.shape[0]
    tm = _tile(t, 512)
    row = lambda w: pl.BlockSpec((tm, w), lambda i: (i, 0))
    return pl.pallas_call(
        _q_kernel,
        grid=(t // tm,),
        in_specs=[row(D_MODEL), _const_spec((1, D_MODEL)), _const_spec((D_MODEL, 2 * B_W)),
                  _const_spec((1, B_W)), _const_spec((B_W, B_W))],
        out_specs=[row(B_W), row(B_W)],
        out_shape=[jax.ShapeDtypeStruct((t, B_W), BF16), jax.ShapeDtypeStruct((t, B_W), F32)],
        compiler_params=_params("parallel"),
        name="q_proj",
    )(h, g, w_in, q_g, gmat)


def _cumsum_kernel(x_ref, f_ref, carry_ref):
    @pl.when(pl.program_id(1) == 0)
    def _():
        carry_ref[...] = jnp.zeros_like(carry_ref)

    w = x_ref.shape[-1]
    f = _seg_cumsum_lanes(x_ref[...], w) + carry_ref[:, 0:1]
    f_ref[...] = f
    carry_ref[...] = jnp.broadcast_to(f[:, w - 1:w], carry_ref.shape)


def _cumsum_rows(x, bsz, width):
    tb = _tile(width, 2048) if width % LANES == 0 and (width // LANES) & (width // LANES - 1) == 0 else width
    nb = width // tb
    if x.ndim == 2:
        in_spec = pl.BlockSpec((B_HEADS, tb), lambda b, j: (0, b * nb + j))
    else:
        in_spec = pl.BlockSpec((None, B_HEADS, tb), lambda b, j: (b, 0, j))
    return pl.pallas_call(
        _cumsum_kernel,
        grid=(bsz, nb),
        in_specs=[in_spec],
        out_specs=pl.BlockSpec((None, B_HEADS, tb), lambda b, j: (b, 0, j)),
        out_shape=jax.ShapeDtypeStruct((bsz, B_HEADS, width), F32),
        scratch_shapes=[pltpu.VMEM((B_HEADS, LANES), F32)],
        compiler_params=_params("parallel", "arbitrary"),
        name="logf_cumsum",
    )(x)


def _cumsum_aug_kernel(x_ref, place_ref, f_ref, ka_ref, carry_ref):
    @pl.when(pl.program_id(1) == 0)
    def _():
        carry_ref[...] = jnp.zeros_like(carry_ref)

    w = x_ref.shape[-1]
    f = _seg_cumsum_lanes(x_ref[...], w) + carry_ref[:, 0:1]
    f_ref[...] = f
    carry_ref[...] = jnp.broadcast_to(f[:, w - 1:w], carry_ref.shape)
    rows = jnp.concatenate(_split3(f * LOG2E) + (jnp.ones((B_HEADS, w), BF16),), axis=0)
    ka_ref[...] = lax.dot_general(rows, place_ref[...], TN, preferred_element_type=F32).astype(BF16)


def _bias_placement():
    r = jnp.arange(4 * B_HEADS)[:, None]
    c = jnp.arange(B_W)[None, :]
    part, head = r // B_HEADS, r % B_HEADS
    lane0 = (head // 2) * LANES + BIAS_LANES * (head % 2)
    is_part = (part < 3) & (c == lane0 + part)
    is_one = (part == 3) & (c >= lane0 + 3) & (c < lane0 + 6)
    return (is_part | is_one).astype(BF16)


def _cumsum_aug(x, bsz, width):
    tb = _tile(width, 2048)
    nb = width // tb
    return pl.pallas_call(
        _cumsum_aug_kernel,
        grid=(bsz, nb),
        in_specs=[pl.BlockSpec((B_HEADS, tb), lambda b, j: (0, b * nb + j)),
                  pl.BlockSpec((4 * B_HEADS, B_W), lambda b, j: (0, 0))],
        out_specs=[pl.BlockSpec((None, B_HEADS, tb), lambda b, j: (b, 0, j)),
                   pl.BlockSpec((tb, B_W), lambda b, j: (b * nb + j, 0))],
        out_shape=[jax.ShapeDtypeStruct((bsz, B_HEADS, width), F32),
                   jax.ShapeDtypeStruct((bsz * width, B_W), BF16)],
        scratch_shapes=[pltpu.VMEM((B_HEADS, LANES), F32)],
        compiler_params=_params("parallel", "arbitrary"),
        name="logf_cumsum_aug",
    )(x, _bias_placement())


def _fox_bounded_kernel(bound_ref, q_ref, k_ref, ka_ref, vt_ref, f_ref, o_ref, qt_ref, acc_ref,
                        p00, p01, p10, p11, *, tq):
    tk = tq // 2
    pbuf = ((p00, p01), (p10, p11))
    qi = pl.program_id(2)
    q0 = pl.multiple_of(qi * tq, tq)
    feat = lax.broadcasted_iota(jnp.int32, (LANES, tq), 0)
    qt = q_ref[...].astype(F32).T
    beta = _split3(f_ref[:, pl.ds(q0, tq)] * LOG2E - bound_ref[...])
    for hh in range(2):
        lo = BIAS_LANES * hh
        qt_ref[hh, 0:LANES, :] = jnp.where((feat >= hh * B_DH) & (feat < (hh + 1) * B_DH), qt, 0.0).astype(BF16)
        aug = jnp.where((feat >= lo) & (feat < lo + 3), -1.0, 0.0)
        for part in range(3):
            aug = jnp.where(feat == lo + 3 + part, beta[part][hh:hh + 1].astype(F32), aug)
        qt_ref[hh, LANES:2 * LANES, :] = aug.astype(BF16)
    acc_ref[...] = jnp.zeros_like(acc_ref)
    ones = jnp.ones((ACC_ROWS - B_DH, tk), BF16)
    kpos = lax.broadcasted_iota(jnp.int32, (tk, tq), 0)
    qpos = lax.broadcasted_iota(jnp.int32, (tk, tq), 1)
    diag_masks = (kpos <= qpos, (kpos + tk) <= qpos)

    def weights(k0, slot, causal=None):
        lhs = jnp.concatenate([k_ref[pl.ds(k0, tk), :], ka_ref[pl.ds(k0, tk), :]], axis=1)
        for hh in range(2):
            e = jnp.dot(lhs, qt_ref[hh], preferred_element_type=F32)
            if causal is not None:
                e = jnp.where(causal, e, NEG)
            pbuf[slot][hh][...] = jnp.exp2(e).astype(BF16)

    def accumulate(k0, slot):
        for hh in range(2):
            vt = jnp.concatenate([vt_ref[hh * B_DH:(hh + 1) * B_DH, pl.ds(k0, tk)], ones], axis=0)
            acc_ref[hh] += jnp.dot(vt, pbuf[slot][hh][...], preferred_element_type=F32)

    def pair(k0, next_causal):
        weights(k0 + tk, 1)
        accumulate(k0, 0)
        weights(k0 + tq, 0, next_causal)
        accumulate(k0 + tk, 1)

    def finish():
        weights(q0 + tk, 1, diag_masks[1])
        accumulate(q0, 0)
        accumulate(q0 + tk, 1)
        o_t = jnp.concatenate(
            [acc_ref[hh, 0:B_DH] * (1.0 / acc_ref[hh, B_DH:B_DH + 1]) for hh in range(2)], axis=0)
        o_ref[...] = o_t.T

    @pl.when(qi == 0)
    def _():
        weights(q0, 0, diag_masks[0])
        finish()

    @pl.when(qi > 0)
    def _():
        weights(0, 0)
        n_pairs = qi - 1

        def unrolled(jj, carry):
            for u in range(PAIR_UNROLL):
                pair(pl.multiple_of((jj * PAIR_UNROLL + u) * tq, tq), None)
            return carry

        def single(jj, carry):
            pair(pl.multiple_of(jj * tq, tq), None)
            return carry

        lax.fori_loop(0, n_pairs // PAIR_UNROLL, unrolled, 0)
        lax.fori_loop((n_pairs // PAIR_UNROLL) * PAIR_UNROLL, n_pairs, single, 0)
        pair(q0 - tq, diag_masks[0])
        finish()


def _fox_prompt_bounded(q, kb, ka, vt, f_row, bound, bsz, seq):
    tq = _tile(seq, 1024)
    nq = seq // tq
    kern = functools.partial(_fox_bounded_kernel, tq=tq)
    return pl.pallas_call(
        kern,
        grid=(bsz, B_PAIRS, nq),
        in_specs=[
            pl.BlockSpec((1, 1), lambda b, p, qi: (0, 0)),
            pl.BlockSpec((tq, LANES), lambda b, p, qi: (b * nq + qi, p)),
            pl.BlockSpec((seq, LANES), lambda b, p, qi: (b, p)),
            pl.BlockSpec((seq, LANES), lambda b, p, qi: (b, p)),
            pl.BlockSpec((LANES, seq), lambda b, p, qi: (p, b)),
            pl.BlockSpec((None, None, 2, seq), lambda b, p, qi: (b, p, 0, 0)),
        ],
        out_specs=pl.BlockSpec((tq, LANES), lambda b, p, qi: (b * nq + qi, p)),
        out_shape=jax.ShapeDtypeStruct((bsz * seq, B_W), F32),
        scratch_shapes=[
            pltpu.VMEM((2, 2 * LANES, tq), BF16),
            pltpu.VMEM((2, ACC_ROWS, tq), F32),
        ] + [pltpu.VMEM((tq // 2, tq), BF16)] * 4,
        compiler_params=_params("parallel", "parallel", "arbitrary"),
        name="fox_prompt_bounded",
    )(bound, q, kb, ka, vt, f_row)


def _fox_kernel(q_ref, k_ref, vt_ref, f_ref, o_ref, qt_ref, m_ref, cm_ref, acc_ref,
                s00, s01, s10, s11, *, tq):
    tk = tq // 2
    sbuf = ((s00, s01), (s10, s11))
    qi = pl.program_id(2)
    q0 = pl.multiple_of(qi * tq, tq)
    feat = lax.broadcasted_iota(jnp.int32, (LANES, tq), 0)
    qt = q_ref[...].astype(F32).T
    qt_ref[0] = jnp.where(feat < B_DH, qt, 0.0).astype(BF16)
    qt_ref[1] = jnp.where(feat >= B_DH, qt, 0.0).astype(BF16)
    m_ref[...] = jnp.full_like(m_ref, NEG)
    acc_ref[...] = jnp.zeros_like(acc_ref)
    f_base = f_ref[:, pl.ds(q0, tq)][:, tq - 1:tq]
    ones = jnp.ones((ACC_ROWS - B_DH, tk), BF16)
    kpos = lax.broadcasted_iota(jnp.int32, (tk, tq), 0)
    qpos = lax.broadcasted_iota(jnp.int32, (tk, tq), 1)
    diag_masks = (kpos <= qpos, (kpos + tk) <= qpos)

    def scores(k0, slot, causal=None):
        kb = k_ref[pl.ds(k0, tk), :]
        fk = (f_ref[:, pl.ds(k0, tk)] - f_base) * LOG2E
        for hh in range(2):
            fk_col = jnp.concatenate(
                [jnp.broadcast_to(fk[hh:hh + 1, c * LANES:(c + 1) * LANES], (LANES, LANES)).T
                 for c in range(tk // LANES)], axis=0)
            t = (jnp.dot(kb, qt_ref[hh], preferred_element_type=F32)
                 - pltpu.repeat(fk_col, tq // LANES, axis=1))
            if causal is not None:
                t = jnp.where(causal, t, NEG)
            sbuf[slot][hh][...] = t
            cm_ref[slot, hh] = jnp.max(t, axis=0, keepdims=True)

    def update(k0, slot):
        for hh in range(2):
            m_prev = m_ref[hh]
            m_new = jnp.maximum(m_prev, cm_ref[slot, hh])
            alpha = jnp.exp2(m_prev - m_new)
            p = jnp.exp2(sbuf[slot][hh][...] - m_new).astype(BF16)
            vt = jnp.concatenate([vt_ref[hh * B_DH:(hh + 1) * B_DH, pl.ds(k0, tk)], ones], axis=0)
            acc_ref[hh] = alpha * acc_ref[hh] + jnp.dot(vt, p, preferred_element_type=F32)
            m_ref[hh] = m_new

    def pair(k0, next_causal):
        scores(k0 + tk, 1)
        update(k0, 0)
        scores(k0 + tq, 0, next_causal)
        update(k0 + tk, 1)

    @pl.when(qi == 0)
    def _():
        scores(q0, 0, diag_masks[0])

    @pl.when(qi > 0)
    def _():
        scores(0, 0)

        def body(jj, carry):
            pair(pl.multiple_of(jj * tq, tq), None)
            return carry

        lax.fori_loop(0, qi - 1, body, 0)
        pair(q0 - tq, diag_masks[0])

    scores(q0 + tk, 1, diag_masks[1])
    update(q0, 0)
    update(q0 + tk, 1)
    o_t = jnp.concatenate(
        [acc_ref[hh, 0:B_DH] * (1.0 / acc_ref[hh, B_DH:B_DH + 1]) for hh in range(2)], axis=0)
    o_ref[...] = o_t.T


def _fox_prompt(q, kb, vt, f_row, bsz, seq):
    tq = _tile(seq, 512)
    tk = tq // 2
    nq = seq // tq
    kern = functools.partial(_fox_kernel, tq=tq)
    return pl.pallas_call(
        kern,
        grid=(bsz, B_PAIRS, nq),
        in_specs=[
            pl.BlockSpec((tq, LANES), lambda b, p, qi: (b * nq + qi, p)),
            pl.BlockSpec((seq, LANES), lambda b, p, qi: (b, p)),
            pl.BlockSpec((LANES, seq), lambda b, p, qi: (p, b)),
            pl.BlockSpec((None, None, 2, seq), lambda b, p, qi: (b, p, 0, 0)),
        ],
        out_specs=pl.BlockSpec((tq, LANES), lambda b, p, qi: (b * nq + qi, p)),
        out_shape=jax.ShapeDtypeStruct((bsz * seq, B_W), F32),
        scratch_shapes=[
            pltpu.VMEM((2, LANES, tq), BF16),
            pltpu.VMEM((2, 1, tq), F32),
            pltpu.VMEM((2, 2, 1, tq), F32),
            pltpu.VMEM((2, ACC_ROWS, tq), F32),
        ] + [pltpu.VMEM((tk, tq), F32)] * 4,
        compiler_params=_params("parallel", "parallel", "arbitrary"),
        name="fox_prompt",
    )(q, kb, vt, f_row)


def _fox_dec_kernel(bound_ref, q_ref, kt_ref, vt_ref, kn_ref, vn_ref, fn_ref, fc_ref, o_ref,
                    fqc_ref, m_ref, l_ref, a_ref, acc_ref, s_ref, p_ref, *, sq, nkb, bounded):
    kj = pl.program_id(1)

    @pl.when(kj == 0)
    def _():
        r = lax.broadcasted_iota(jnp.int32, (sq, LANES), 0)
        c = lax.broadcasted_iota(jnp.int32, (sq, LANES), 1)
        for h in range(B_HEADS):
            fqc_ref[h] = jnp.sum(jnp.where(r == c, fn_ref[h:h + 1, :], 0.0), axis=1, keepdims=True)
        m_ref[...] = jnp.full_like(m_ref, NEG)
        l_ref[...] = jnp.zeros_like(l_ref)
        acc_ref[...] = jnp.zeros_like(acc_ref)

    def attend(width, score, f_keys, values, valid):
        for h in range(B_HEADS):
            s_ref[h, :, 0:width] = score(h)
        for h in range(B_HEADS):
            t = s_ref[h, :, 0:width] + (fqc_ref[h] - f_keys(h)) * LOG2E
            if valid is not None:
                t = jnp.where(valid, t, NEG)
            if bounded:
                p = jnp.exp2(t - bound_ref[...])
                l_ref[h] += jnp.sum(p, axis=1, keepdims=True)
            else:
                m_prev = m_ref[h]
                m_new = jnp.maximum(m_prev, jnp.max(t, axis=1, keepdims=True))
                alpha = jnp.exp2(m_prev - m_new)
                p = jnp.exp2(t - m_new)
                l_ref[h] = alpha * l_ref[h] + jnp.sum(p, axis=1, keepdims=True)
                m_ref[h] = m_new
                a_ref[h] = alpha
            p_ref[h, :, 0:width] = p.astype(BF16)
        for h in range(B_HEADS):
            pv = values(h, p_ref[h, :, 0:width])
            acc_ref[h] = acc_ref[h] + pv if bounded else a_ref[h] * acc_ref[h] + pv

    @pl.when(kj < nkb)
    def _():
        attend(
            kt_ref.shape[-1],
            lambda h: jnp.dot(q_ref[:, h * B_DH:(h + 1) * B_DH], kt_ref[h].astype(BF16),
                              preferred_element_type=F32),
            lambda h: fc_ref[h:h + 1, :],
            lambda h, p: lax.dot_general(p, vt_ref[h].astype(BF16), NT, preferred_element_type=F32),
            None)

    @pl.when(kj == nkb)
    def _():
        i = lax.broadcasted_iota(jnp.int32, (sq, LANES), 0)
        j = lax.broadcasted_iota(jnp.int32, (sq, LANES), 1)
        head = lambda h: slice(h * B_DH, (h + 1) * B_DH)
        attend(
            LANES,
            lambda h: lax.dot_general(q_ref[:, head(h)], kn_ref[:, head(h)], NT, preferred_element_type=F32),
            lambda h: fn_ref[h:h + 1, :],
            lambda h, p: jnp.dot(p, vn_ref[:, head(h)], preferred_element_type=F32),
            j <= i)
        for h in range(B_HEADS):
            o_ref[:, head(h)] = acc_ref[h] * (1.0 / l_ref[h])


def _fox_decode(q, cache_kt, cache_vt, kn, vn, f_new, f_cache, bound, bounded, bsz, sq):
    past = cache_kt.shape[3]
    tk = _tile(past, 2048)
    nkb = past // tk
    cmap = lambda b, kj: (b, 0, 0, jnp.minimum(kj, nkb - 1))
    kern = functools.partial(_fox_dec_kernel, sq=sq, nkb=nkb, bounded=bounded)
    return pl.pallas_call(
        kern,
        grid=(bsz, nkb + 1),
        in_specs=[
            pl.BlockSpec((1, 1), lambda b, kj: (0, 0)),
            pl.BlockSpec((None, sq, B_W), lambda b, kj: (b, 0, 0)),
            pl.BlockSpec((None, B_HEADS, B_DH, tk), cmap),
            pl.BlockSpec((None, B_HEADS, B_DH, tk), cmap),
            pl.BlockSpec((None, LANES, B_W), lambda b, kj: (b, 0, 0)),
            pl.BlockSpec((None, LANES, B_W), lambda b, kj: (b, 0, 0)),
            pl.BlockSpec((None, B_HEADS, LANES), lambda b, kj: (b, 0, 0)),
            pl.BlockSpec((None, B_HEADS, tk), lambda b, kj: (b, 0, jnp.minimum(kj, nkb - 1))),
        ],
        out_specs=pl.BlockSpec((None, sq, B_W), lambda b, kj: (b, 0, 0)),
        out_shape=jax.ShapeDtypeStruct((bsz, sq, B_W), F32),
        scratch_shapes=[
            pltpu.VMEM((B_HEADS, sq, 1), F32),
            pltpu.VMEM((B_HEADS, sq, 1), F32),
            pltpu.VMEM((B_HEADS, sq, 1), F32),
            pltpu.VMEM((B_HEADS, sq, 1), F32),
            pltpu.VMEM((B_HEADS, sq, B_DH), F32),
            pltpu.VMEM((B_HEADS, sq, tk), F32),
            pltpu.VMEM((B_HEADS, sq, tk), BF16),
        ],
        compiler_params=_params("parallel", "arbitrary"),
        name="fox_decode",
    )(bound, q, cache_kt, cache_vt, kn, vn, f_new, f_cache)


def _b_post_kernel(o_ref, z_ref, h_ref, p_ref, wo_ref, pw_ref, pn_ref, pg_ref, out_ref):
    z = z_ref[...]
    y = o_ref[...] * (z * _sigmoid(z))
    h2 = h_ref[...] + jnp.dot(y.astype(BF16), wo_ref[...], preferred_element_type=F32)
    out_ref[...] = _ple(h2, p_ref, pw_ref, pn_ref, pg_ref)


def _b_post(o, z, h, p, layer, w_out, ple_w, ple_norm, ple_wg):
    t = h.shape[0]
    tm = _tile(t, 512)
    row = lambda w: pl.BlockSpec((tm, w), lambda i: (i, 0))
    return pl.pallas_call(
        _b_post_kernel,
        grid=(t // tm,),
        in_specs=[row(B_W), row(B_W), row(D_MODEL), pl.BlockSpec((None, tm, PLE_DIM), lambda i: (layer, i, 0)),
                  _const_spec((B_W, D_MODEL)), _const_spec((PLE_DIM, D_MODEL)),
                  _const_spec((1, D_MODEL)), _const_spec((D_MODEL, D_MODEL))],
        out_specs=row(D_MODEL),
        out_shape=jax.ShapeDtypeStruct((t, D_MODEL), F32),
        compiler_params=_params("parallel"),
        name="b_post",
    )(o, z, h, p, w_out, ple_w, ple_norm, ple_wg)


def _prep_weights(a_norm, a_w_in, a_b_i, a_b_f, a_hnorm, a_w_out, kv_norm, kv_w, kv_b_f, k_norm,
                  b_norm, b_w_in, q_norm, b_w_out, ple_w, ple_norm, ple_w_g):
    head_of_lane = jnp.arange(B_W, dtype=jnp.int32) // B_DH
    return dict(
        a_norm=a_norm[0].reshape(1, D_MODEL),
        a_w_main=a_w_in[0][:, :A_MAIN].astype(BF16),
        a_w_gt=_split(a_w_in[0][:, A_MAIN:].T),
        a_bias=jnp.concatenate([a_b_i[0], a_b_f[0]]).reshape(2 * A_HEADS, 1),
        a_hnorm=a_hnorm[0].reshape(1, A_V),
        a_w_out=a_w_out[0].astype(BF16),
        kv_norm=kv_norm.reshape(1, D_MODEL),
        kv_w=kv_w[:, :2 * B_W].astype(BF16),
        kv_wf=_split(kv_w[:, 2 * B_W:]),
        kv_wft=_split(kv_w[:, 2 * B_W:].T),
        kv_bf=kv_b_f.reshape(1, B_HEADS),
        kv_bft=kv_b_f.reshape(B_HEADS, 1),
        k_norm=jnp.tile(k_norm, B_HEADS).reshape(1, B_W),
        gmat=(head_of_lane[:, None] == head_of_lane[None, :]).astype(BF16),
        b_norm=b_norm[0].reshape(1, D_MODEL),
        b_w_in=b_w_in[0].astype(BF16),
        q_norm=jnp.tile(q_norm[0], B_HEADS).reshape(1, B_W),
        b_w_out=b_w_out[0].astype(BF16),
        ple_w=ple_w.astype(BF16),
        ple_norm=ple_norm.reshape(-1, 1, D_MODEL),
        ple_wg=ple_w_g.astype(BF16),
    )


def _trunk(x, p, c0, n0, m0, past, w):
    bsz, seq, _ = x.shape
    t = bsz * seq
    xf = x.reshape(t, D_MODEL)
    pf = p.reshape(p.shape[0], t, PLE_DIM)

    u, gt = _a_inproj(xf, w["a_norm"], w["a_w_main"], w["a_w_gt"])
    m0r = jnp.broadcast_to(m0.reshape(bsz, A_HEADS, 1, 1), (bsz, A_HEADS, 1, LANES))
    hs, c_new, n_new, m_new = _mlstm(u, gt, w["a_bias"], c0, n0[..., None], m0r, bsz, seq)
    h1 = _a_post(hs, u, xf, pf, 0, w["a_hnorm"], w["a_w_out"], w["ple_w"][0], w["ple_norm"][0], w["ple_wg"][0])

    k, v, kb, vb, vt, lf, lft = _kv_proj(h1, w["kv_norm"], w["kv_w"], w["kv_wf"], w["kv_wft"],
                                     w["kv_bf"], w["kv_bft"], w["k_norm"], w["gmat"])
    q, z = _q_proj(h1, w["b_norm"], w["b_w_in"], w["q_norm"], w["gmat"])
    bound = (8.0 * LOG2E * 1.01) * jnp.max(jnp.abs(w["q_norm"])) * jnp.max(jnp.abs(w["k_norm"])) + 0.05
    bound11 = bound.reshape(1, 1)
    if past is None:
        f_row, ka = _cumsum_aug(lft, bsz, seq)
        f_row = f_row.reshape(bsz, B_PAIRS, 2, seq)
        o = lax.cond(
            bound <= MAX_BOUND,
            lambda: _fox_prompt_bounded(q, kb, ka, vt, f_row, bound11, bsz, seq),
            lambda: _fox_prompt(q, kb, vt, f_row, bsz, seq))
    else:
        cache_k, cache_v, cache_lf = past
        plen = cache_k.shape[1]
        assert seq <= LANES
        width = -(-(plen + seq) // LANES) * LANES
        lf_rows = jnp.concatenate(
            [jnp.swapaxes(cache_lf.astype(F32), 1, 2),
             jnp.swapaxes(lft.reshape(B_HEADS, bsz, seq), 0, 1),
             jnp.zeros((bsz, B_HEADS, width - plen - seq), F32)], axis=2)
        f_all = _cumsum_rows(lf_rows, bsz, width)
        f_new = jnp.pad(f_all[..., plen:plen + seq], ((0, 0), (0, 0), (0, LANES - seq)))
        pad_new = lambda a: jnp.pad(a.reshape(bsz, seq, B_W), ((0, 0), (0, LANES - seq), (0, 0)))
        to_t = lambda a: jnp.transpose(a.astype(F32), (0, 2, 3, 1))
        dec_args = (q.reshape(bsz, seq, B_W), to_t(cache_k), to_t(cache_v), pad_new(kb), pad_new(vb),
                    f_new, f_all[..., :plen], bound11)
        o = lax.cond(
            bound <= MAX_BOUND,
            lambda: _fox_decode(*dec_args, True, bsz, seq),
            lambda: _fox_decode(*dec_args, False, bsz, seq)).reshape(t, B_W)
    y = _b_post(o, z, h1, pf, 1, w["b_w_out"], w["ple_w"][1], w["ple_norm"][1], w["ple_wg"][1])

    return (y.reshape(bsz, seq, D_MODEL),
            c_new[None], n_new.reshape(1, bsz, A_HEADS, A_DK), m_new[None, :, :, 0, 0],
            k.reshape(bsz, seq, B_HEADS, B_DH), v.reshape(bsz, seq, B_HEADS, B_DH),
            lf.reshape(bsz, seq, B_HEADS))


def kernel(x_prompt, x_sample, cache_k, cache_v, cache_logf, state_C, state_n, state_m, p_prompt, p_sample,
           a_norm, a_w_in, a_b_i, a_b_f, a_hnorm, a_w_out, kv_norm, kv_w, kv_b_f, k_norm,
           b_norm, b_w_in, q_norm, b_w_out, ple_w, ple_norm, ple_w_g):
    w = _prep_weights(a_norm, a_w_in, a_b_i, a_b_f, a_hnorm, a_w_out, kv_norm, kv_w, kv_b_f, k_norm,
                      b_norm, b_w_in, q_norm, b_w_out, ple_w, ple_norm, ple_w_g)
    bsz = x_prompt.shape[0]
    c0 = jnp.zeros((bsz, A_HEADS, A_DK, A_DV), F32)
    n0 = jnp.zeros((bsz, A_HEADS, A_DK), F32)
    m0 = jnp.zeros((bsz, A_HEADS), F32)
    prompt = _trunk(x_prompt, p_prompt, c0, n0, m0, None, w)
    sample = _trunk(x_sample, p_sample, state_C[0].astype(F32), state_n[0].astype(F32),
                    state_m[0].astype(F32), (cache_k, cache_v, cache_logf), w)
    return (prompt[0], sample[0]) + prompt[1:] + sample[1:]
```

```python
import functools

import jax
import jax.numpy as jnp
from jax import lax
from jax.experimental import pallas as pl
from jax.experimental.pallas import tpu as pltpu

D_MODEL = 1024
A_HEADS = 8
A_DK = 128
A_DV = 256
A_QK = A_HEADS * A_DK
A_V = A_HEADS * A_DV
A_MAIN = 2 * A_QK + 3 * A_V
B_HEADS = 16
B_DH = 64
B_W = B_HEADS * B_DH
B_PAIRS = B_HEADS // 2
PLE_DIM = 256
EPS = 1e-6
NEG = -1e30
LOG2E = 1.4426950408889634
ACC_ROWS = B_DH + 16
BIAS_LANES = 6
MAX_BOUND = 60.0
PAIR_UNROLL = 4
MLSTM_ROWS = 128

LANES = 128
VMEM_LIMIT = 56 * 1024 * 1024

F32 = jnp.float32
BF16 = jnp.bfloat16
NN = (((1,), (0,)), ((), ()))
NT = (((1,), (1,)), ((), ()))
TN = (((0,), (0,)), ((), ()))


def _params(*sem):
    return pltpu.CompilerParams(dimension_semantics=sem, vmem_limit_bytes=VMEM_LIMIT)


def _tile(n, pref):
    t = min(n, pref)
    while n % t:
        t //= 2
    return t


def _rms(x, g):
    ms = jnp.mean(x * x, axis=-1, keepdims=True)
    return x * lax.rsqrt(ms + EPS) * g


def _log_sigmoid(x):
    return jnp.minimum(x, 0.0) - jnp.log1p(jnp.exp(-jnp.abs(x)))


def _sigmoid(x):
    return 0.5 * jnp.tanh(0.5 * x) + 0.5


def _diag_to_col(row, eye):
    n = eye.shape[0]
    return jnp.sum(jnp.where(eye, jnp.broadcast_to(row, (n, n)), 0.0), axis=1, keepdims=True)


def _seg_cummax_lanes(x, seg):
    w = x.shape[-1]
    pos = lax.broadcasted_iota(jnp.int32, x.shape, x.ndim - 1) % seg
    k = 1
    while k < min(seg, w):
        x = jnp.maximum(x, jnp.where(pos >= k, pltpu.roll(x, k, x.ndim - 1), NEG))
        k *= 2
    return x


def _seg_cumsum_lanes(x, seg):
    w = x.shape[-1]
    pos = lax.broadcasted_iota(jnp.int32, x.shape, x.ndim - 1) % seg
    k = 1
    while k < min(seg, w):
        x = x + jnp.where(pos >= k, pltpu.roll(x, k, x.ndim - 1), 0.0)
        k *= 2
    return x


def _split(x):
    hi = x.astype(BF16)
    return hi, (x - hi.astype(F32)).astype(BF16)


def _dot_split(a, b, dims):
    dot = functools.partial(lax.dot_general, dimension_numbers=dims, preferred_element_type=F32)
    return dot(a[0], b[0]) + (dot(a[0], b[1]) + dot(a[1], b[0]))


def _split3(x):
    hi = x.astype(BF16)
    r = x - hi.astype(F32)
    mid = r.astype(BF16)
    return hi, mid, (r - mid.astype(F32)).astype(BF16)


def _a_inproj_kernel(x_ref, g_ref, w_ref, wgh_ref, wgl_ref, u_ref, gt_ref, xn_ref):
    @pl.when(pl.program_id(1) == 0)
    def _():
        xn = _rms(x_ref[...], g_ref[...])
        xs = _split(xn)
        xn_ref[...] = xs[0]
        gt_ref[...] = _dot_split((wgh_ref[...], wgl_ref[...]), xs, NT)

    u_ref[...] = jnp.dot(xn_ref[...], w_ref[...], preferred_element_type=F32).astype(BF16)


def _a_inproj(x, g, w_main, w_gt):
    t = x.shape[0]
    tm = _tile(t, 1024)
    tn = 1024
    gate_spec = pl.BlockSpec((2 * A_HEADS, D_MODEL), lambda i, j: (0, 0))
    return pl.pallas_call(
        _a_inproj_kernel,
        grid=(t // tm, A_MAIN // tn),
        in_specs=[
            pl.BlockSpec((tm, D_MODEL), lambda i, j: (i, 0)),
            pl.BlockSpec((1, D_MODEL), lambda i, j: (0, 0)),
            pl.BlockSpec((D_MODEL, tn), lambda i, j: (0, j)),
            gate_spec,
            gate_spec,
        ],
        out_specs=[
            pl.BlockSpec((tm, tn), lambda i, j: (i, j)),
            pl.BlockSpec((2 * A_HEADS, tm), lambda i, j: (0, i)),
        ],
        out_shape=[
            jax.ShapeDtypeStruct((t, A_MAIN), BF16),
            jax.ShapeDtypeStruct((2 * A_HEADS, t), F32),
        ],
        scratch_shapes=[pltpu.VMEM((tm, D_MODEL), BF16)],
        compiler_params=_params("parallel", "arbitrary"),
        name="a_inproj",
    )(x, g, w_main, *w_gt)


def _mlstm_kernel(q_ref, k_ref, v_ref, g_ref, bias_ref, c0_ref, n0_ref, m0_ref,
                  h_ref, c_out_ref, n_out_ref, m_ref, c_ref, gate_ref, *, rows, n_seq, seq_len, n_chunks, steps):
    R = rows

    @pl.when(pl.program_id(1) == 0)
    def _():
        c_ref[:, :, :, 0:A_DV] = c0_ref[...]
        c_ref[:, :, :, A_DV:] = jnp.broadcast_to(n0_ref[...], (n_seq, A_HEADS, A_DK, LANES))
        m_ref[...] = m0_ref[...]

    assert R == LANES
    row = lax.broadcasted_iota(jnp.int32, (R, R), 0)
    col = lax.broadcasted_iota(jnp.int32, (R, R), 1)
    valid = (col <= row) & ((row // seq_len) == (col // seq_len))
    lane8 = lax.broadcasted_iota(jnp.int32, (A_HEADS, R), 1)
    bias = bias_ref[...]
    scale = A_DK ** -0.5
    ones = jnp.ones((R, LANES), BF16)
    g = g_ref[...]
    ig_blk = g[0:A_HEADS] + bias[0:A_HEADS]
    b_blk = _seg_cumsum_lanes(_log_sigmoid(g[A_HEADS:] + bias[A_HEADS:]), seq_len)
    gate_ref[0:A_HEADS, :] = ig_blk
    gate_ref[A_HEADS:2 * A_HEADS, :] = b_blk
    gate_ref[2 * A_HEADS:, :] = _seg_cummax_lanes(ig_blk - b_blk, seq_len)

    def per_seq_rows(vals):
        out = vals[0]
        for s in range(1, n_seq):
            out = jnp.where(lane8 >= s * seq_len, vals[s], out)
        return jnp.broadcast_to(out, (A_HEADS, R))

    def chunk(r0):
        rows = pl.ds(r0, R)
        ig = gate_ref[0:A_HEADS, rows]
        b = gate_ref[A_HEADS:2 * A_HEADS, rows]
        cmax = gate_ref[2 * A_HEADS:, rows]
        m_prev = [m_ref[s, :, 0, 0:1] for s in range(n_seq)]
        m_prev_rows = per_seq_rows(m_prev)
        m_t = b + jnp.maximum(m_prev_rows, cmax)
        a = b - m_t
        ends = [(s + 1) * seq_len - 1 for s in range(n_seq)]
        b_last = [b[:, e:e + 1] for e in ends]
        m_new = [m_t[:, e:e + 1] for e in ends]
        w_s = jnp.exp(per_seq_rows(b_last) - b + ig - per_seq_rows(m_new)) * scale
        w_old = [jnp.exp(b_last[s] + m_prev[s] - m_new[s]) for s in range(n_seq)]
        for s in range(n_seq):
            m_ref[s, :, 0, :] = jnp.broadcast_to(m_new[s], (A_HEADS, LANES))
        cols = jnp.concatenate([a, jnp.exp(a + m_prev_rows), jnp.exp(-m_t), w_s,
                                jnp.zeros((R - 4 * A_HEADS, R), F32)], axis=0).T
        c_rows = ig - b
        for h in range(A_HEADS):
            qb = q_ref[rows, h * A_DK:(h + 1) * A_DK]
            kb = k_ref[rows, h * A_DK:(h + 1) * A_DK]
            vb = jnp.concatenate([v_ref[rows, h * A_DV:(h + 1) * A_DV], ones], axis=1)
            a_col = cols[:, h:h + 1]
            w_inter = cols[:, A_HEADS + h:A_HEADS + h + 1]
            e_m = cols[:, 2 * A_HEADS + h:2 * A_HEADS + h + 1]
            w_s_col = cols[:, 3 * A_HEADS + h:3 * A_HEADS + h + 1]
            p = jnp.exp(jnp.where(valid, a_col + c_rows[h:h + 1], NEG))
            qk = lax.dot_general(qb, kb, NT, preferred_element_type=F32) * scale
            s_mat = qk * p
            qc = [jnp.dot(qb[s * seq_len:(s + 1) * seq_len], c_ref[s, h].astype(BF16),
                          preferred_element_type=F32) for s in range(n_seq)]
            qc = qc[0] if n_seq == 1 else jnp.concatenate(qc, axis=0)
            nd = w_inter * qc + jnp.dot(s_mat.astype(BF16), vb, preferred_element_type=F32)
            den = nd[:, A_DV:A_DV + 1]
            h_ref[rows, h * A_DV:(h + 1) * A_DV] = nd[:, 0:A_DV] * (1.0 / jnp.maximum(jnp.abs(den), e_m))
            ksb = (kb.astype(F32) * w_s_col).astype(BF16)
            for s in range(n_seq):
                sl = slice(s * seq_len, (s + 1) * seq_len)
                c_ref[s, h] = w_old[s][h:h + 1] * c_ref[s, h] + lax.dot_general(
                    ksb[sl], vb[sl], TN, preferred_element_type=F32)

    if n_chunks == 1:
        chunk(0)
    else:
        def body(ci, carry):
            chunk(pl.multiple_of(ci * R, R))
            return carry
        lax.fori_loop(0, n_chunks, body, 0)

    @pl.when(pl.program_id(1) == steps - 1)
    def _():
        c_out_ref[...] = c_ref[:, :, :, 0:A_DV]
        n_out_ref[...] = c_ref[:, :, :, A_DV:A_DV + 1]


def _mlstm(u, gt, bias, c0, n0, m0, bsz, seq):
    if seq % LANES == 0:
        rows = _tile(seq, MLSTM_ROWS)
        seq_len, n_seq = rows, 1
        n_chunks = _tile(seq // rows, 1024 // rows)
    else:
        assert LANES % seq == 0 and bsz % (LANES // seq) == 0, (bsz, seq)
        rows, seq_len, n_seq, n_chunks = LANES, seq, LANES // seq, 1
    tok = n_chunks * rows
    steps = seq // (n_chunks * seq_len)
    t = bsz * seq
    tok_map = lambda b, c: (b * steps + c, 0)
    state4 = lambda shape: pl.BlockSpec((n_seq,) + shape, lambda b, c: (b, 0, 0, 0))
    kern = functools.partial(_mlstm_kernel, rows=rows, n_seq=n_seq, seq_len=seq_len, n_chunks=n_chunks, steps=steps)
    return pl.pallas_call(
        kern,
        grid=(bsz // n_seq, steps),
        in_specs=[
            pl.BlockSpec((tok, A_QK), tok_map),
            pl.BlockSpec((tok, A_QK), lambda b, c: (b * steps + c, 1)),
            pl.BlockSpec((tok, A_V), lambda b, c: (b * steps + c, 1)),
            pl.BlockSpec((2 * A_HEADS, tok), lambda b, c: (0, b * steps + c)),
            pl.BlockSpec((2 * A_HEADS, 1), lambda b, c: (0, 0)),
            state4((A_HEADS, A_DK, A_DV)),
            state4((A_HEADS, A_DK, 1)),
            state4((A_HEADS, 1, LANES)),
        ],
        out_specs=[
            pl.BlockSpec((tok, A_V), tok_map),
            state4((A_HEADS, A_DK, A_DV)),
            state4((A_HEADS, A_DK, 1)),
            state4((A_HEADS, 1, LANES)),
        ],
        out_shape=[
            jax.ShapeDtypeStruct((t, A_V), F32),
            jax.ShapeDtypeStruct((bsz, A_HEADS, A_DK, A_DV), F32),
            jax.ShapeDtypeStruct((bsz, A_HEADS, A_DK, 1), F32),
            jax.ShapeDtypeStruct((bsz, A_HEADS, 1, LANES), F32),
        ],
        scratch_shapes=[pltpu.VMEM((n_seq, A_HEADS, A_DK, A_DV + LANES), F32),
                        pltpu.VMEM((3 * A_HEADS, tok), F32)],
        compiler_params=_params("parallel", "arbitrary"),
        name="mlstm",
    )(u, u, u, gt, bias, c0, n0, m0)


def _ple(h, p_ref, pw_ref, pn_ref, pg_ref):
    pp = jnp.dot(p_ref[...].astype(BF16), pw_ref[...], preferred_element_type=F32)
    hn = _rms(h, pn_ref[...])
    gate = _sigmoid(jnp.dot(hn.astype(BF16), pg_ref[...], preferred_element_type=F32))
    return h + pp * gate


def _a_post_kernel(hs_ref, o_ref, z_ref, x_ref, p_ref, hg_ref, wo_ref, pw_ref, pn_ref, pg_ref,
                   out_ref, y_ref):
    for h in range(A_HEADS):
        sl = slice(h * A_DV, (h + 1) * A_DV)
        y = _rms(hs_ref[:, sl], hg_ref[:, sl]) * _sigmoid(o_ref[:, sl].astype(F32))
        z = z_ref[:, sl].astype(F32)
        y_ref[:, sl] = (y * (z * _sigmoid(z))).astype(BF16)
    h1 = x_ref[...] + jnp.dot(y_ref[...], wo_ref[...], preferred_element_type=F32)
    out_ref[...] = _ple(h1, p_ref, pw_ref, pn_ref, pg_ref)


def _const_spec(shape):
    return pl.BlockSpec(shape, lambda i: (0,) * len(shape))


def _a_post(hs, u, x, p, layer, hnorm, w_out, ple_w, ple_norm, ple_wg):
    t = x.shape[0]
    tm = _tile(t, 256)
    return pl.pallas_call(
        _a_post_kernel,
        grid=(t // tm,),
        in_specs=[
            pl.BlockSpec((tm, A_V), lambda i: (i, 0)),
            pl.BlockSpec((tm, A_V), lambda i: (i, 2)),
            pl.BlockSpec((tm, A_V), lambda i: (i, 3)),
            pl.BlockSpec((tm, D_MODEL), lambda i: (i, 0)),
            pl.BlockSpec((None, tm, PLE_DIM), lambda i: (layer, i, 0)),
            _const_spec((1, A_V)),
            _const_spec((A_V, D_MODEL)),
            _const_spec((PLE_DIM, D_MODEL)),
            _const_spec((1, D_MODEL)),
            _const_spec((D_MODEL, D_MODEL)),
        ],
        out_specs=pl.BlockSpec((tm, D_MODEL), lambda i: (i, 0)),
        out_shape=jax.ShapeDtypeStruct((t, D_MODEL), F32),
        scratch_shapes=[pltpu.VMEM((tm, A_V), BF16)],
        compiler_params=_params("parallel"),
        name="a_post",
    )(hs, u, u, x, p, hnorm, w_out, ple_w, ple_norm, ple_wg)


def _head_ms(x, gm_ref):
    return jnp.dot((x * x).astype(BF16), gm_ref[...], preferred_element_type=F32) * (1.0 / B_DH)


def _kv_kernel(h_ref, g_ref, w_ref, wfh_ref, wfl_ref, wfth_ref, wftl_ref, bf_ref, bft_ref, kg_ref, gm_ref,
               k_ref, v_ref, kb_ref, vb_ref, vt_ref, lf_ref, lft_ref):
    xs = _split(_rms(h_ref[...], g_ref[...]))
    kv = jnp.dot(xs[0], w_ref[...], preferred_element_type=F32)
    k = kv[:, :B_W]
    v = kv[:, B_W:]
    kn = k * lax.rsqrt(_head_ms(k, gm_ref) + EPS) * kg_ref[...]
    k_ref[...] = pltpu.einshape("m(hd)->mhd", kn, h=B_HEADS)
    v_ref[...] = pltpu.einshape("m(hd)->mhd", v, h=B_HEADS)
    kb_ref[...] = kn.astype(BF16)
    vb_ref[...] = v.astype(BF16)
    vt_ref[...] = v.T.astype(BF16)
    fg = _dot_split(xs, (wfh_ref[...], wfl_ref[...]), NN)
    lf_ref[...] = _log_sigmoid(fg + bf_ref[...])
    fgt = _dot_split((wfth_ref[...], wftl_ref[...]), xs, NT)
    lft_ref[...] = _log_sigmoid(fgt + bft_ref[...])


def _kv_proj(h, g, w_kv, w_f, w_ft, b_f, b_ft, k_g, gmat):
    t = h.shape[0]
    tm = _tile(t, 512)
    row = lambda w: pl.BlockSpec((tm, w), lambda i: (i, 0))
    return pl.pallas_call(
        _kv_kernel,
        grid=(t // tm,),
        in_specs=[
            row(D_MODEL),
            _const_spec((1, D_MODEL)),
            _const_spec((D_MODEL, 2 * B_W)),
            _const_spec((D_MODEL, B_HEADS)),
            _const_spec((D_MODEL, B_HEADS)),
            _const_spec((B_HEADS, D_MODEL)),
            _const_spec((B_HEADS, D_MODEL)),
            _const_spec((1, B_HEADS)),
            _const_spec((B_HEADS, 1)),
            _const_spec((1, B_W)),
            _const_spec((B_W, B_W)),
        ],
        out_specs=[pl.BlockSpec((tm, B_HEADS, B_DH), lambda i: (i, 0, 0)),
                   pl.BlockSpec((tm, B_HEADS, B_DH), lambda i: (i, 0, 0)), row(B_W), row(B_W),
                   pl.BlockSpec((B_W, tm), lambda i: (0, i)), row(B_HEADS),
                   pl.BlockSpec((B_HEADS, tm), lambda i: (0, i))],
        out_shape=[
            jax.ShapeDtypeStruct((t, B_HEADS, B_DH), F32),
            jax.ShapeDtypeStruct((t, B_HEADS, B_DH), F32),
            jax.ShapeDtypeStruct((t, B_W), BF16),
            jax.ShapeDtypeStruct((t, B_W), BF16),
            jax.ShapeDtypeStruct((B_W, t), BF16),
            jax.ShapeDtypeStruct((t, B_HEADS), F32),
            jax.ShapeDtypeStruct((B_HEADS, t), F32),
        ],
        compiler_params=_params("parallel"),
        name="kv_proj",
    )(h, g, w_kv, *w_f, *w_ft, b_f, b_ft, k_g, gmat)


def _q_kernel(h_ref, g_ref, w_ref, qg_ref, gm_ref, q_ref, z_ref):
    xn = _rms(h_ref[...], g_ref[...])
    qz = jnp.dot(xn.astype(BF16), w_ref[...], preferred_element_type=F32)
    q = qz[:, :B_W]
    z_ref[...] = qz[:, B_W:]
    qn = q * lax.rsqrt(_head_ms(q, gm_ref) + EPS) * qg_ref[...]
    q_ref[...] = (qn * (B_DH ** -0.5 * LOG2E)).astype(BF16)


def _q_proj(h, g, w_in, q_g, gmat):
    t = h.shape[0]
    tm = _tile(t, 512)
    row = lambda w: pl.BlockSpec((tm, w), lambda i: (i, 0))
    return pl.pallas_call(
        _q_kernel,
        grid=(t // tm,),
        in_specs=[row(D_MODEL), _const_spec((1, D_MODEL)), _const_spec((D_MODEL, 2 * B_W)),
                  _const_spec((1, B_W)), _const_spec((B_W, B_W))],
        out_specs=[row(B_W), row(B_W)],
        out_shape=[jax.ShapeDtypeStruct((t, B_W), BF16), jax.ShapeDtypeStruct((t, B_W), F32)],
        compiler_params=_params("parallel"),
        name="q_proj",
    )(h, g, w_in, q_g, gmat)


def _cumsum_kernel(x_ref, f_ref, carry_ref):
    @pl.when(pl.program_id(1) == 0)
    def _():
        carry_ref[...] = jnp.zeros_like(carry_ref)

    w = x_ref.shape[-1]
    f = _seg_cumsum_lanes(x_ref[...], w) + carry_ref[:, 0:1]
    f_ref[...] = f
    carry_ref[...] = jnp.broadcast_to(f[:, w - 1:w], carry_ref.shape)


def _cumsum_rows(x, bsz, width):
    tb = _tile(width, 2048) if width % LANES == 0 and (width // LANES) & (width // LANES - 1) == 0 else width
    nb = width // tb
    if x.ndim == 2:
        in_spec = pl.BlockSpec((B_HEADS, tb), lambda b, j: (0, b * nb + j))
    else:
        in_spec = pl.BlockSpec((None, B_HEADS, tb), lambda b, j: (b, 0, j))
    return pl.pallas_call(
        _cumsum_kernel,
        grid=(bsz, nb),
        in_specs=[in_spec],
        out_specs=pl.BlockSpec((None, B_HEADS, tb), lambda b, j: (b, 0, j)),
        out_shape=jax.ShapeDtypeStruct((bsz, B_HEADS, width), F32),
        scratch_shapes=[pltpu.VMEM((B_HEADS, LANES), F32)],
        compiler_params=_params("parallel", "arbitrary"),
        name="logf_cumsum",
    )(x)


def _cumsum_aug_kernel(x_ref, place_ref, f_ref, ka_ref, carry_ref):
    @pl.when(pl.program_id(1) == 0)
    def _():
        carry_ref[...] = jnp.zeros_like(carry_ref)

    w = x_ref.shape[-1]
    f = _seg_cumsum_lanes(x_ref[...], w) + carry_ref[:, 0:1]
    f_ref[...] = f
    carry_ref[...] = jnp.broadcast_to(f[:, w - 1:w], carry_ref.shape)
    rows = jnp.concatenate(_split3(f * LOG2E) + (jnp.ones((B_HEADS, w), BF16),), axis=0)
    ka_ref[...] = lax.dot_general(rows, place_ref[...], TN, preferred_element_type=F32).astype(BF16)


def _bias_placement():
    r = jnp.arange(4 * B_HEADS)[:, None]
    c = jnp.arange(B_W)[None, :]
    part, head = r // B_HEADS, r % B_HEADS
    lane0 = (head // 2) * LANES + BIAS_LANES * (head % 2)
    is_part = (part < 3) & (c == lane0 + part)
    is_one = (part == 3) & (c >= lane0 + 3) & (c < lane0 + 6)
    return (is_part | is_one).astype(BF16)


def _cumsum_aug(x, bsz, width):
    tb = _tile(width, 2048)
    nb = width // tb
    return pl.pallas_call(
        _cumsum_aug_kernel,
        grid=(bsz, nb),
        in_specs=[pl.BlockSpec((B_HEADS, tb), lambda b, j: (0, b * nb + j)),
                  pl.BlockSpec((4 * B_HEADS, B_W), lambda b, j: (0, 0))],
        out_specs=[pl.BlockSpec((None, B_HEADS, tb), lambda b, j: (b, 0, j)),
                   pl.BlockSpec((tb, B_W), lambda b, j: (b * nb + j, 0))],
        out_shape=[jax.ShapeDtypeStruct((bsz, B_HEADS, width), F32),
                   jax.ShapeDtypeStruct((bsz * width, B_W), BF16)],
        scratch_shapes=[pltpu.VMEM((B_HEADS, LANES), F32)],
        compiler_params=_params("parallel", "arbitrary"),
        name="logf_cumsum_aug",
    )(x, _bias_placement())


def _fox_bounded_kernel(bound_ref, q_ref, k_ref, ka_ref, vt_ref, f_ref, o_ref, qt_ref, acc_ref,
                        p00, p01, p10, p11, *, tq):
    tk = tq // 2
    pbuf = ((p00, p01), (p10, p11))
    qi = pl.program_id(2)
    q0 = pl.multiple_of(qi * tq, tq)
    feat = lax.broadcasted_iota(jnp.int32, (LANES, tq), 0)
    qt = q_ref[...].astype(F32).T
    beta = _split3(f_ref[:, pl.ds(q0, tq)] * LOG2E - bound_ref[...])
    for hh in range(2):
        lo = BIAS_LANES * hh
        qt_ref[hh, 0:LANES, :] = jnp.where((feat >= hh * B_DH) & (feat < (hh + 1) * B_DH), qt, 0.0).astype(BF16)
        aug = jnp.where((feat >= lo) & (feat < lo + 3), -1.0, 0.0)
        for part in range(3):
            aug = jnp.where(feat == lo + 3 + part, beta[part][hh:hh + 1].astype(F32), aug)
        qt_ref[hh, LANES:2 * LANES, :] = aug.astype(BF16)
    acc_ref[...] = jnp.zeros_like(acc_ref)
    ones = jnp.ones((ACC_ROWS - B_DH, tk), BF16)
    kpos = lax.broadcasted_iota(jnp.int32, (tk, tq), 0)
    qpos = lax.broadcasted_iota(jnp.int32, (tk, tq), 1)
    diag_masks = (kpos <= qpos, (kpos + tk) <= qpos)

    def weights(k0, slot, causal=None):
        lhs = jnp.concatenate([k_ref[pl.ds(k0, tk), :], ka_ref[pl.ds(k0, tk), :]], axis=1)
        for hh in range(2):
            e = jnp.dot(lhs, qt_ref[hh], preferred_element_type=F32)
            if causal is not None:
                e = jnp.where(causal, e, NEG)
            pbuf[slot][hh][...] = jnp.exp2(e).astype(BF16)

    def accumulate(k0, slot):
        for hh in range(2):
            vt = jnp.concatenate([vt_ref[hh * B_DH:(hh + 1) * B_DH, pl.ds(k0, tk)], ones], axis=0)
            acc_ref[hh] += jnp.dot(vt, pbuf[slot][hh][...], preferred_element_type=F32)

    def pair(k0, next_causal):
        weights(k0 + tk, 1)
        accumulate(k0, 0)
        weights(k0 + tq, 0, next_causal)
        accumulate(k0 + tk, 1)

    def finish():
        weights(q0 + tk, 1, diag_masks[1])
        accumulate(q0, 0)
        accumulate(q0 + tk, 1)
        o_t = jnp.concatenate(
            [acc_ref[hh, 0:B_DH] * (1.0 / acc_ref[hh, B_DH:B_DH + 1]) for hh in range(2)], axis=0)
        o_ref[...] = o_t.T

    @pl.when(qi == 0)
    def _():
        weights(q0, 0, diag_masks[0])
        finish()

    @pl.when(qi > 0)
    def _():
        weights(0, 0)
        n_pairs = qi - 1

        def unrolled(jj, carry):
            for u in range(PAIR_UNROLL):
                pair(pl.multiple_of((jj * PAIR_UNROLL + u) * tq, tq), None)
            return carry

        def single(jj, carry):
            pair(pl.multiple_of(jj * tq, tq), None)
            return carry

        lax.fori_loop(0, n_pairs // PAIR_UNROLL, unrolled, 0)
        lax.fori_loop((n_pairs // PAIR_UNROLL) * PAIR_UNROLL, n_pairs, single, 0)
        pair(q0 - tq, diag_masks[0])
        finish()


def _fox_prompt_bounded(q, kb, ka, vt, f_row, bound, bsz, seq):
    tq = _tile(seq, 1024)
    nq = seq // tq
    kern = functools.partial(_fox_bounded_kernel, tq=tq)
    return pl.pallas_call(
        kern,
        grid=(bsz, B_PAIRS, nq),
        in_specs=[
            pl.BlockSpec((1, 1), lambda b, p, qi: (0, 0)),
            pl.BlockSpec((tq, LANES), lambda b, p, qi: (b * nq + qi, p)),
            pl.BlockSpec((seq, LANES), lambda b, p, qi: (b, p)),
            pl.BlockSpec((seq, LANES), lambda b, p, qi: (b, p)),
            pl.BlockSpec((LANES, seq), lambda b, p, qi: (p, b)),
            pl.BlockSpec((None, None, 2, seq), lambda b, p, qi: (b, p, 0, 0)),
        ],
        out_specs=pl.BlockSpec((tq, LANES), lambda b, p, qi: (b * nq + qi, p)),
        out_shape=jax.ShapeDtypeStruct((bsz * seq, B_W), F32),
        scratch_shapes=[
            pltpu.VMEM((2, 2 * LANES, tq), BF16),
            pltpu.VMEM((2, ACC_ROWS, tq), F32),
        ] + [pltpu.VMEM((tq // 2, tq), BF16)] * 4,
        compiler_params=_params("parallel", "parallel", "arbitrary"),
        name="fox_prompt_bounded",
    )(bound, q, kb, ka, vt, f_row)


def _fox_kernel(q_ref, k_ref, vt_ref, f_ref, o_ref, qt_ref, m_ref, cm_ref, acc_ref,
                s00, s01, s10, s11, *, tq):
    tk = tq // 2
    sbuf = ((s00, s01), (s10, s11))
    qi = pl.program_id(2)
    q0 = pl.multiple_of(qi * tq, tq)
    feat = lax.broadcasted_iota(jnp.int32, (LANES, tq), 0)
    qt = q_ref[...].astype(F32).T
    qt_ref[0] = jnp.where(feat < B_DH, qt, 0.0).astype(BF16)
    qt_ref[1] = jnp.where(feat >= B_DH, qt, 0.0).astype(BF16)
    m_ref[...] = jnp.full_like(m_ref, NEG)
    acc_ref[...] = jnp.zeros_like(acc_ref)
    f_base = f_ref[:, pl.ds(q0, tq)][:, tq - 1:tq]
    ones = jnp.ones((ACC_ROWS - B_DH, tk), BF16)
    kpos = lax.broadcasted_iota(jnp.int32, (tk, tq), 0)
    qpos = lax.broadcasted_iota(jnp.int32, (tk, tq), 1)
    diag_masks = (kpos <= qpos, (kpos + tk) <= qpos)

    def scores(k0, slot, causal=None):
        kb = k_ref[pl.ds(k0, tk), :]
        fk = (f_ref[:, pl.ds(k0, tk)] - f_base) * LOG2E
        for hh in range(2):
            fk_col = jnp.concatenate(
                [jnp.broadcast_to(fk[hh:hh + 1, c * LANES:(c + 1) * LANES], (LANES, LANES)).T
                 for c in range(tk // LANES)], axis=0)
            t = (jnp.dot(kb, qt_ref[hh], preferred_element_type=F32)
                 - pltpu.repeat(fk_col, tq // LANES, axis=1))
            if causal is not None:
                t = jnp.where(causal, t, NEG)
            sbuf[slot][hh][...] = t
            cm_ref[slot, hh] = jnp.max(t, axis=0, keepdims=True)

    def update(k0, slot):
        for hh in range(2):
            m_prev = m_ref[hh]
            m_new = jnp.maximum(m_prev, cm_ref[slot, hh])
            alpha = jnp.exp2(m_prev - m_new)
            p = jnp.exp2(sbuf[slot][hh][...] - m_new).astype(BF16)
            vt = jnp.concatenate([vt_ref[hh * B_DH:(hh + 1) * B_DH, pl.ds(k0, tk)], ones], axis=0)
            acc_ref[hh] = alpha * acc_ref[hh] + jnp.dot(vt, p, preferred_element_type=F32)
            m_ref[hh] = m_new

    def pair(k0, next_causal):
        scores(k0 + tk, 1)
        update(k0, 0)
        scores(k0 + tq, 0, next_causal)
        update(k0 + tk, 1)

    @pl.when(qi == 0)
    def _():
        scores(q0, 0, diag_masks[0])

    @pl.when(qi > 0)
    def _():
        scores(0, 0)

        def body(jj, carry):
            pair(pl.multiple_of(jj * tq, tq), None)
            return carry

        lax.fori_loop(0, qi - 1, body, 0)
        pair(q0 - tq, diag_masks[0])

    scores(q0 + tk, 1, diag_masks[1])
    update(q0, 0)
    update(q0 + tk, 1)
    o_t = jnp.concatenate(
        [acc_ref[hh, 0:B_DH] * (1.0 / acc_ref[hh, B_DH:B_DH + 1]) for hh in range(2)], axis=0)
    o_ref[...] = o_t.T


def _fox_prompt(q, kb, vt, f_row, bsz, seq):
    tq = _tile(seq, 512)
    tk = tq // 2
    nq = seq // tq
    kern = functools.partial(_fox_kernel, tq=tq)
    return pl.pallas_call(
        kern,
        grid=(bsz, B_PAIRS, nq),
        in_specs=[
            pl.BlockSpec((tq, LANES), lambda b, p, qi: (b * nq + qi, p)),
            pl.BlockSpec((seq, LANES), lambda b, p, qi: (b, p)),
            pl.BlockSpec((LANES, seq), lambda b, p, qi: (p, b)),
            pl.BlockSpec((None, None, 2, seq), lambda b, p, qi: (b, p, 0, 0)),
        ],
        out_specs=pl.BlockSpec((tq, LANES), lambda b, p, qi: (b * nq + qi, p)),
        out_shape=jax.ShapeDtypeStruct((bsz * seq, B_W), F32),
        scratch_shapes=[
            pltpu.VMEM((2, LANES, tq), BF16),
            pltpu.VMEM((2, 1, tq), F32),
            pltpu.VMEM((2, 2, 1, tq), F32),
            pltpu.VMEM((2, ACC_ROWS, tq), F32),
        ] + [pltpu.VMEM((tk, tq), F32)] * 4,
        compiler_params=_params("parallel", "parallel", "arbitrary"),
        name="fox_prompt",
    )(q, kb, vt, f_row)


def _fox_dec_kernel(bound_ref, q_ref, kt_ref, vt_ref, kn_ref, vn_ref, fn_ref, fc_ref, o_ref,
                    fqc_ref, m_ref, l_ref, a_ref, acc_ref, s_ref, p_ref, *, sq, nkb, bounded):
    kj = pl.program_id(1)
    head = lambda h: slice(h * B_DH, (h + 1) * B_DH)

    @pl.when(kj == 0)
    def _():
        r = lax.broadcasted_iota(jnp.int32, (sq, LANES), 0)
        c = lax.broadcasted_iota(jnp.int32, (sq, LANES), 1)
        for h in range(B_HEADS):
            fqc_ref[h] = jnp.sum(jnp.where(r == c, fn_ref[h:h + 1, :], 0.0), axis=1, keepdims=True)
        m_ref[...] = jnp.full_like(m_ref, NEG)
        l_ref[...] = jnp.zeros_like(l_ref)
        acc_ref[...] = jnp.zeros_like(acc_ref)

    def attend(width, score, f_keys, values, valid):
        for h in range(B_HEADS):
            s_ref[h, :, 0:width] = score(h)
        for h in range(B_HEADS):
            t = s_ref[h, :, 0:width] + (fqc_ref[h] - f_keys(h)) * LOG2E
            if valid is not None:
                t = jnp.where(valid, t, NEG)
            if bounded:
                p = jnp.exp2(t - bound_ref[...])
                l_ref[h] += jnp.sum(p, axis=1, keepdims=True)
            else:
                m_prev = m_ref[h]
                m_new = jnp.maximum(m_prev, jnp.max(t, axis=1, keepdims=True))
                alpha = jnp.exp2(m_prev - m_new)
                p = jnp.exp2(t - m_new)
                l_ref[h] = alpha * l_ref[h] + jnp.sum(p, axis=1, keepdims=True)
                m_ref[h] = m_new
                a_ref[h] = alpha
            p_ref[h, :, 0:width] = p.astype(BF16)
        for h in range(B_HEADS):
            pv = values(h, p_ref[h, :, 0:width])
            acc_ref[h] = acc_ref[h] + pv if bounded else a_ref[h] * acc_ref[h] + pv

    @pl.when(kj == 0)
    def _():
        i = lax.broadcasted_iota(jnp.int32, (sq, LANES), 0)
        j = lax.broadcasted_iota(jnp.int32, (sq, LANES), 1)
        attend(
            LANES,
            lambda h: lax.dot_general(q_ref[:, head(h)], kn_ref[:, head(h)], NT, preferred_element_type=F32),
            lambda h: fn_ref[h:h + 1, :],
            lambda h, p: jnp.dot(p, vn_ref[:, head(h)], preferred_element_type=F32),
            j <= i)

    @pl.when(kj > 0)
    def _():
        attend(
            kt_ref.shape[-1],
            lambda h: jnp.dot(q_ref[:, head(h)], kt_ref[h].astype(BF16), preferred_element_type=F32),
            lambda h: fc_ref[h:h + 1, :],
            lambda h, p: lax.dot_general(p, vt_ref[h].astype(BF16), NT, preferred_element_type=F32),
            None)

    @pl.when(kj == nkb)
    def _():
        for h in range(B_HEADS):
            o_ref[:, head(h)] = acc_ref[h] * (1.0 / l_ref[h])


def _fox_decode(q, cache_kt, cache_vt, kn, vn, f_new, f_cache, bound, bounded, bsz, sq):
    past = cache_kt.shape[3]
    tk = _tile(past, 2048)
    nkb = past // tk
    cmap = lambda b, kj: (b, 0, 0, jnp.maximum(kj - 1, 0))
    kern = functools.partial(_fox_dec_kernel, sq=sq, nkb=nkb, bounded=bounded)
    return pl.pallas_call(
        kern,
        grid=(bsz, nkb + 1),
        in_specs=[
            pl.BlockSpec((1, 1), lambda b, kj: (0, 0)),
            pl.BlockSpec((None, sq, B_W), lambda b, kj: (b, 0, 0)),
            pl.BlockSpec((None, B_HEADS, B_DH, tk), cmap),
            pl.BlockSpec((None, B_HEADS, B_DH, tk), cmap),
            pl.BlockSpec((None, LANES, B_W), lambda b, kj: (b, 0, 0)),
            pl.BlockSpec((None, LANES, B_W), lambda b, kj: (b, 0, 0)),
            pl.BlockSpec((None, B_HEADS, LANES), lambda b, kj: (b, 0, 0)),
            pl.BlockSpec((None, B_HEADS, tk), lambda b, kj: (b, 0, jnp.maximum(kj - 1, 0))),
        ],
        out_specs=pl.BlockSpec((None, sq, B_W), lambda b, kj: (b, 0, 0)),
        out_shape=jax.ShapeDtypeStruct((bsz, sq, B_W), F32),
        scratch_shapes=[
            pltpu.VMEM((B_HEADS, sq, 1), F32),
            pltpu.VMEM((B_HEADS, sq, 1), F32),
            pltpu.VMEM((B_HEADS, sq, 1), F32),
            pltpu.VMEM((B_HEADS, sq, 1), F32),
            pltpu.VMEM((B_HEADS, sq, B_DH), F32),
            pltpu.VMEM((B_HEADS, sq, tk), F32),
            pltpu.VMEM((B_HEADS, sq, tk), BF16),
        ],
        compiler_params=_params("parallel", "arbitrary"),
        name="fox_decode",
    )(bound, q, cache_kt, cache_vt, kn, vn, f_new, f_cache)


def _b_post_kernel(o_ref, z_ref, h_ref, p_ref, wo_ref, pw_ref, pn_ref, pg_ref, out_ref):
    z = z_ref[...]
    y = o_ref[...] * (z * _sigmoid(z))
    h2 = h_ref[...] + jnp.dot(y.astype(BF16), wo_ref[...], preferred_element_type=F32)
    out_ref[...] = _ple(h2, p_ref, pw_ref, pn_ref, pg_ref)


def _b_post(o, z, h, p, layer, w_out, ple_w, ple_norm, ple_wg):
    t = h.shape[0]
    tm = _tile(t, 512)
    row = lambda w: pl.BlockSpec((tm, w), lambda i: (i, 0))
    return pl.pallas_call(
        _b_post_kernel,
        grid=(t // tm,),
        in_specs=[row(B_W), row(B_W), row(D_MODEL), pl.BlockSpec((None, tm, PLE_DIM), lambda i: (layer, i, 0)),
                  _const_spec((B_W, D_MODEL)), _const_spec((PLE_DIM, D_MODEL)),
                  _const_spec((1, D_MODEL)), _const_spec((D_MODEL, D_MODEL))],
        out_specs=row(D_MODEL),
        out_shape=jax.ShapeDtypeStruct((t, D_MODEL), F32),
        compiler_params=_params("parallel"),
        name="b_post",
    )(o, z, h, p, w_out, ple_w, ple_norm, ple_wg)


def _prep_weights(a_norm, a_w_in, a_b_i, a_b_f, a_hnorm, a_w_out, kv_norm, kv_w, kv_b_f, k_norm,
                  b_norm, b_w_in, q_norm, b_w_out, ple_w, ple_norm, ple_w_g):
    head_of_lane = jnp.arange(B_W, dtype=jnp.int32) // B_DH
    return dict(
        a_norm=a_norm[0].reshape(1, D_MODEL),
        a_w_main=a_w_in[0][:, :A_MAIN].astype(BF16),
        a_w_gt=_split(a_w_in[0][:, A_MAIN:].T),
        a_bias=jnp.concatenate([a_b_i[0], a_b_f[0]]).reshape(2 * A_HEADS, 1),
        a_hnorm=a_hnorm[0].reshape(1, A_V),
        a_w_out=a_w_out[0].astype(BF16),
        kv_norm=kv_norm.reshape(1, D_MODEL),
        kv_w=kv_w[:, :2 * B_W].astype(BF16),
        kv_wf=_split(kv_w[:, 2 * B_W:]),
        kv_wft=_split(kv_w[:, 2 * B_W:].T),
        kv_bf=kv_b_f.reshape(1, B_HEADS),
        kv_bft=kv_b_f.reshape(B_HEADS, 1),
        k_norm=jnp.tile(k_norm, B_HEADS).reshape(1, B_W),
        gmat=(head_of_lane[:, None] == head_of_lane[None, :]).astype(BF16),
        b_norm=b_norm[0].reshape(1, D_MODEL),
        b_w_in=b_w_in[0].astype(BF16),
        q_norm=jnp.tile(q_norm[0], B_HEADS).reshape(1, B_W),
        b_w_out=b_w_out[0].astype(BF16),
        ple_w=ple_w.astype(BF16),
        ple_norm=ple_norm.reshape(-1, 1, D_MODEL),
        ple_wg=ple_w_g.astype(BF16),
    )


def _trunk(x, p, c0, n0, m0, past, w):
    bsz, seq, _ = x.shape
    t = bsz * seq
    xf = x.reshape(t, D_MODEL)
    pf = p.reshape(p.shape[0], t, PLE_DIM)

    u, gt = _a_inproj(xf, w["a_norm"], w["a_w_main"], w["a_w_gt"])
    m0r = jnp.broadcast_to(m0.reshape(bsz, A_HEADS, 1, 1), (bsz, A_HEADS, 1, LANES))
    hs, c_new, n_new, m_new = _mlstm(u, gt, w["a_bias"], c0, n0[..., None], m0r, bsz, seq)
    h1 = _a_post(hs, u, xf, pf, 0, w["a_hnorm"], w["a_w_out"], w["ple_w"][0], w["ple_norm"][0], w["ple_wg"][0])

    k, v, kb, vb, vt, lf, lft = _kv_proj(h1, w["kv_norm"], w["kv_w"], w["kv_wf"], w["kv_wft"],
                                     w["kv_bf"], w["kv_bft"], w["k_norm"], w["gmat"])
    q, z = _q_proj(h1, w["b_norm"], w["b_w_in"], w["q_norm"], w["gmat"])
    bound = (8.0 * LOG2E * 1.01) * jnp.max(jnp.abs(w["q_norm"])) * jnp.max(jnp.abs(w["k_norm"])) + 0.05
    bound11 = bound.reshape(1, 1)
    if past is None:
        f_row, ka = _cumsum_aug(lft, bsz, seq)
        f_row = f_row.reshape(bsz, B_PAIRS, 2, seq)
        o = lax.cond(
            bound <= MAX_BOUND,
            lambda: _fox_prompt_bounded(q, kb, ka, vt, f_row, bound11, bsz, seq),
            lambda: _fox_prompt(q, kb, vt, f_row, bsz, seq))
    else:
        cache_k, cache_v, cache_lf = past
        plen = cache_k.shape[1]
        assert seq <= LANES
        width = -(-(plen + seq) // LANES) * LANES
        lf_rows = jnp.concatenate(
            [jnp.swapaxes(cache_lf.astype(F32), 1, 2),
             jnp.swapaxes(lft.reshape(B_HEADS, bsz, seq), 0, 1),
             jnp.zeros((bsz, B_HEADS, width - plen - seq), F32)], axis=2)
        f_all = _cumsum_rows(lf_rows, bsz, width)
        f_new = jnp.pad(f_all[..., plen:plen + seq], ((0, 0), (0, 0), (0, LANES - seq)))
        pad_new = lambda a: jnp.pad(a.reshape(bsz, seq, B_W), ((0, 0), (0, LANES - seq), (0, 0)))
        to_t = lambda a: jnp.transpose(a.astype(F32), (0, 2, 3, 1))
        dec_args = (q.reshape(bsz, seq, B_W), to_t(cache_k), to_t(cache_v), pad_new(kb), pad_new(vb),
                    f_new, f_all[..., :plen], bound11)
        o = lax.cond(
            bound <= MAX_BOUND,
            lambda: _fox_decode(*dec_args, True, bsz, seq),
            lambda: _fox_decode(*dec_args, False, bsz, seq)).reshape(t, B_W)
    y = _b_post(o, z, h1, pf, 1, w["b_w_out"], w["ple_w"][1], w["ple_norm"][1], w["ple_wg"][1])

    return (y.reshape(bsz, seq, D_MODEL),
            c_new[None], n_new.reshape(1, bsz, A_HEADS, A_DK), m_new[None, :, :, 0, 0],
            k.reshape(bsz, seq, B_HEADS, B_DH), v.reshape(bsz, seq, B_HEADS, B_DH),
            lf.reshape(bsz, seq, B_HEADS))


def kernel(x_prompt, x_sample, cache_k, cache_v, cache_logf, state_C, state_n, state_m, p_prompt, p_sample,
           a_norm, a_w_in, a_b_i, a_b_f, a_hnorm, a_w_out, kv_norm, kv_w, kv_b_f, k_norm,
           b_norm, b_w_in, q_norm, b_w_out, ple_w, ple_norm, ple_w_g):
    w = _prep_weights(a_norm, a_w_in, a_b_i, a_b_f, a_hnorm, a_w_out, kv_norm, kv_w, kv_b_f, k_norm,
                      b_norm, b_w_in, q_norm, b_w_out, ple_w, ple_norm, ple_w_g)
    bsz = x_prompt.shape[0]
    c0 = jnp.zeros((bsz, A_HEADS, A_DK, A_DV), F32)
    n0 = jnp.zeros((bsz, A_HEADS, A_DK), F32)
    m0 = jnp.zeros((bsz, A_HEADS), F32)
    prompt = _trunk(x_prompt, p_prompt, c0, n0, m0, None, w)
    sample = _trunk(x_sample, p_sample, state_C[0].astype(F32), state_n[0].astype(F32),
                    state_m[0].astype(F32), (cache_k, cache_v, cache_logf), w)
    return (prompt[0], sample[0]) + prompt[1:] + sample[1:]
```

```python
import functools

import jax
import jax.numpy as jnp
from jax import lax
from jax.experimental import pallas as pl
from jax.experimental.pallas import tpu as pltpu

D_MODEL = 1024
A_HEADS = 8
A_DK = 128
A_DV = 256
A_QK = A_HEADS * A_DK
A_V = A_HEADS * A_DV
A_MAIN = 2 * A_QK + 3 * A_V
B_HEADS = 16
B_DH = 64
B_W = B_HEADS * B_DH
B_PAIRS = B_HEADS // 2
PLE_DIM = 256
EPS = 1e-6
NEG = -1e30
LOG2E = 1.4426950408889634
ACC_ROWS = B_DH + 16
BIAS_LANES = 6
MAX_BOUND = 60.0
PAIR_UNROLL = 4

LANES = 128
VMEM_LIMIT = 56 * 1024 * 1024

F32 = jnp.float32
BF16 = jnp.bfloat16
NN = (((1,), (0,)), ((), ()))
NT = (((1,), (1,)), ((), ()))
TN = (((0,), (0,)), ((), ()))


def _params(*sem):
    return pltpu.CompilerParams(dimension_semantics=sem, vmem_limit_bytes=VMEM_LIMIT)


def _tile(n, pref):
    t = min(n, pref)
    while n % t:
        t //= 2
    return t


def _rms(x, g):
    ms = jnp.mean(x * x, axis=-1, keepdims=True)
    return x * lax.rsqrt(ms + EPS) * g


def _log_sigmoid(x):
    return jnp.minimum(x, 0.0) - jnp.log1p(jnp.exp(-jnp.abs(x)))


def _sigmoid(x):
    return 0.5 * jnp.tanh(0.5 * x) + 0.5


def _seg_cummax_lanes(x, seg):
    w = x.shape[-1]
    pos = lax.broadcasted_iota(jnp.int32, x.shape, x.ndim - 1) % seg
    k = 1
    while k < min(seg, w):
        x = jnp.maximum(x, jnp.where(pos >= k, pltpu.roll(x, k, x.ndim - 1), NEG))
        k *= 2
    return x


def _seg_cumsum_lanes(x, seg):
    w = x.shape[-1]
    pos = lax.broadcasted_iota(jnp.int32, x.shape, x.ndim - 1) % seg
    k = 1
    while k < min(seg, w):
        x = x + jnp.where(pos >= k, pltpu.roll(x, k, x.ndim - 1), 0.0)
        k *= 2
    return x


def _split(x):
    hi = x.astype(BF16)
    return hi, (x - hi.astype(F32)).astype(BF16)


def _dot_split(a, b, dims):
    dot = functools.partial(lax.dot_general, dimension_numbers=dims, preferred_element_type=F32)
    return dot(a[0], b[0]) + (dot(a[0], b[1]) + dot(a[1], b[0]))


def _split3(x):
    hi = x.astype(BF16)
    r = x - hi.astype(F32)
    mid = r.astype(BF16)
    return hi, mid, (r - mid.astype(F32)).astype(BF16)


def _a_inproj_kernel(x_ref, g_ref, w_ref, wgh_ref, wgl_ref, u_ref, gt_ref, xn_ref):
    @pl.when(pl.program_id(1) == 0)
    def _():
        xn = _rms(x_ref[...], g_ref[...])
        xs = _split(xn)
        xn_ref[...] = xs[0]
        gt_ref[...] = _dot_split((wgh_ref[...], wgl_ref[...]), xs, NT)

    u_ref[...] = jnp.dot(xn_ref[...], w_ref[...], preferred_element_type=F32).astype(BF16)


def _a_inproj(x, g, w_main, w_gt):
    t = x.shape[0]
    tm = _tile(t, 1024)
    tn = 1024
    gate_spec = pl.BlockSpec((2 * A_HEADS, D_MODEL), lambda i, j: (0, 0))
    return pl.pallas_call(
        _a_inproj_kernel,
        grid=(t // tm, A_MAIN // tn),
        in_specs=[
            pl.BlockSpec((tm, D_MODEL), lambda i, j: (i, 0)),
            pl.BlockSpec((1, D_MODEL), lambda i, j: (0, 0)),
            pl.BlockSpec((D_MODEL, tn), lambda i, j: (0, j)),
            gate_spec,
            gate_spec,
        ],
        out_specs=[
            pl.BlockSpec((tm, tn), lambda i, j: (i, j)),
            pl.BlockSpec((2 * A_HEADS, tm), lambda i, j: (0, i)),
        ],
        out_shape=[
            jax.ShapeDtypeStruct((t, A_MAIN), BF16),
            jax.ShapeDtypeStruct((2 * A_HEADS, t), F32),
        ],
        scratch_shapes=[pltpu.VMEM((tm, D_MODEL), BF16)],
        compiler_params=_params("parallel", "arbitrary"),
        name="a_inproj",
    )(x, g, w_main, *w_gt)


def _mlstm_kernel(q_ref, k_ref, v_ref, g_ref, bias_ref, c0_ref, n0_ref, m0_ref,
                  h_ref, c_out_ref, n_out_ref, m_ref, c_ref, gate_ref, *, rows, n_seq, seq_len, n_chunks, steps):
    R = rows

    @pl.when(pl.program_id(1) == 0)
    def _():
        c_ref[:, :, :, 0:A_DV] = c0_ref[...]
        c_ref[:, :, :, A_DV:] = jnp.broadcast_to(n0_ref[...], (n_seq, A_HEADS, A_DK, LANES))
        m_ref[...] = m0_ref[...]

    assert R == LANES
    row = lax.broadcasted_iota(jnp.int32, (R, R), 0)
    col = lax.broadcasted_iota(jnp.int32, (R, R), 1)
    valid = (col <= row) & ((row // seq_len) == (col // seq_len))
    lane8 = lax.broadcasted_iota(jnp.int32, (A_HEADS, R), 1)
    bias = bias_ref[...]
    scale = A_DK ** -0.5
    ones = jnp.ones((R, LANES), BF16)
    g = g_ref[...]
    ig_blk = g[0:A_HEADS] + bias[0:A_HEADS]
    b_blk = _seg_cumsum_lanes(_log_sigmoid(g[A_HEADS:] + bias[A_HEADS:]), seq_len)
    gate_ref[0:A_HEADS, :] = ig_blk
    gate_ref[A_HEADS:2 * A_HEADS, :] = b_blk
    gate_ref[2 * A_HEADS:, :] = _seg_cummax_lanes(ig_blk - b_blk, seq_len)

    def per_seq_rows(vals):
        out = vals[0]
        for s in range(1, n_seq):
            out = jnp.where(lane8 >= s * seq_len, vals[s], out)
        return jnp.broadcast_to(out, (A_HEADS, R))

    def chunk(r0):
        rows = pl.ds(r0, R)
        ig = gate_ref[0:A_HEADS, rows]
        b = gate_ref[A_HEADS:2 * A_HEADS, rows]
        cmax = gate_ref[2 * A_HEADS:, rows]
        m_prev = [m_ref[s, :, 0, 0:1] for s in range(n_seq)]
        m_prev_rows = per_seq_rows(m_prev)
        m_t = b + jnp.maximum(m_prev_rows, cmax)
        a = b - m_t
        ends = [(s + 1) * seq_len - 1 for s in range(n_seq)]
        b_last = [b[:, e:e + 1] for e in ends]
        m_new = [m_t[:, e:e + 1] for e in ends]
        w_s = jnp.exp(per_seq_rows(b_last) - b + ig - per_seq_rows(m_new)) * scale
        w_old = [jnp.exp(b_last[s] + m_prev[s] - m_new[s]) for s in range(n_seq)]
        for s in range(n_seq):
            m_ref[s, :, 0, :] = jnp.broadcast_to(m_new[s], (A_HEADS, LANES))
        cols = jnp.concatenate([a, jnp.exp(a + m_prev_rows), jnp.exp(-m_t), w_s,
                                jnp.zeros((R - 4 * A_HEADS, R), F32)], axis=0).T
        c_rows = ig - b
        for h in range(A_HEADS):
            qb = q_ref[rows, h * A_DK:(h + 1) * A_DK]
            kb = k_ref[rows, h * A_DK:(h + 1) * A_DK]
            vb = jnp.concatenate([v_ref[rows, h * A_DV:(h + 1) * A_DV], ones], axis=1)
            a_col = cols[:, h:h + 1]
            w_inter = cols[:, A_HEADS + h:A_HEADS + h + 1]
            e_m = cols[:, 2 * A_HEADS + h:2 * A_HEADS + h + 1]
            w_s_col = cols[:, 3 * A_HEADS + h:3 * A_HEADS + h + 1]
            p = jnp.exp(jnp.where(valid, a_col + c_rows[h:h + 1], NEG))
            qk = lax.dot_general(qb, kb, NT, preferred_element_type=F32) * scale
            s_mat = qk * p
            qc = [jnp.dot(qb[s * seq_len:(s + 1) * seq_len], c_ref[s, h].astype(BF16),
                          preferred_element_type=F32) for s in range(n_seq)]
            qc = qc[0] if n_seq == 1 else jnp.concatenate(qc, axis=0)
            nd = w_inter * qc + jnp.dot(s_mat.astype(BF16), vb, preferred_element_type=F32)
            den = nd[:, A_DV:A_DV + 1]
            h_ref[rows, h * A_DV:(h + 1) * A_DV] = nd[:, 0:A_DV] * (1.0 / jnp.maximum(jnp.abs(den), e_m))
            ksb = (kb.astype(F32) * w_s_col).astype(BF16)
            for s in range(n_seq):
                sl = slice(s * seq_len, (s + 1) * seq_len)
                c_ref[s, h] = w_old[s][h:h + 1] * c_ref[s, h] + lax.dot_general(
                    ksb[sl], vb[sl], TN, preferred_element_type=F32)

    if n_chunks == 1:
        chunk(0)
    else:
        def body(ci, carry):
            chunk(pl.multiple_of(ci * R, R))
            return carry
        lax.fori_loop(0, n_chunks, body, 0)

    @pl.when(pl.program_id(1) == steps - 1)
    def _():
        c_out_ref[...] = c_ref[:, :, :, 0:A_DV]
        n_out_ref[...] = c_ref[:, :, :, A_DV:A_DV + 1]


def _mlstm(u, gt, bias, c0, n0, m0, bsz, seq):
    if seq % LANES == 0:
        rows = LANES
        seq_len, n_seq = rows, 1
        n_chunks = _tile(seq // rows, 1024 // rows)
    else:
        assert LANES % seq == 0 and bsz % (LANES // seq) == 0, (bsz, seq)
        rows, seq_len, n_seq, n_chunks = LANES, seq, LANES // seq, 1
    tok = n_chunks * rows
    steps = seq // (n_chunks * seq_len)
    t = bsz * seq
    tok_map = lambda b, c: (b * steps + c, 0)
    state4 = lambda shape: pl.BlockSpec((n_seq,) + shape, lambda b, c: (b, 0, 0, 0))
    kern = functools.partial(_mlstm_kernel, rows=rows, n_seq=n_seq, seq_len=seq_len, n_chunks=n_chunks, steps=steps)
    return pl.pallas_call(
        kern,
        grid=(bsz // n_seq, steps),
        in_specs=[
            pl.BlockSpec((tok, A_QK), tok_map),
            pl.BlockSpec((tok, A_QK), lambda b, c: (b * steps + c, 1)),
            pl.BlockSpec((tok, A_V), lambda b, c: (b * steps + c, 1)),
            pl.BlockSpec((2 * A_HEADS, tok), lambda b, c: (0, b * steps + c)),
            pl.BlockSpec((2 * A_HEADS, 1), lambda b, c: (0, 0)),
            state4((A_HEADS, A_DK, A_DV)),
            state4((A_HEADS, A_DK, 1)),
            state4((A_HEADS, 1, LANES)),
        ],
        out_specs=[
            pl.BlockSpec((tok, A_V), tok_map),
            state4((A_HEADS, A_DK, A_DV)),
            state4((A_HEADS, A_DK, 1)),
            state4((A_HEADS, 1, LANES)),
        ],
        out_shape=[
            jax.ShapeDtypeStruct((t, A_V), F32),
            jax.ShapeDtypeStruct((bsz, A_HEADS, A_DK, A_DV), F32),
            jax.ShapeDtypeStruct((bsz, A_HEADS, A_DK, 1), F32),
            jax.ShapeDtypeStruct((bsz, A_HEADS, 1, LANES), F32),
        ],
        scratch_shapes=[pltpu.VMEM((n_seq, A_HEADS, A_DK, A_DV + LANES), F32),
                        pltpu.VMEM((3 * A_HEADS, tok), F32)],
        compiler_params=_params("parallel", "arbitrary"),
        name="mlstm",
    )(u, u, u, gt, bias, c0, n0, m0)


def _ple(h, p_ref, pw_ref, pn_ref, pg_ref):
    pp = jnp.dot(p_ref[...].astype(BF16), pw_ref[...], preferred_element_type=F32)
    hn = _rms(h, pn_ref[...])
    gate = _sigmoid(jnp.dot(hn.astype(BF16), pg_ref[...], preferred_element_type=F32))
    return h + pp * gate


def _a_post_kernel(hs_ref, o_ref, z_ref, x_ref, p_ref, hg_ref, wo_ref, pw_ref, pn_ref, pg_ref,
                   out_ref, y_ref):
    for h in range(A_HEADS):
        sl = slice(h * A_DV, (h + 1) * A_DV)
        y = _rms(hs_ref[:, sl], hg_ref[:, sl]) * _sigmoid(o_ref[:, sl].astype(F32))
        z = z_ref[:, sl].astype(F32)
        y_ref[:, sl] = (y * (z * _sigmoid(z))).astype(BF16)
    h1 = x_ref[...] + jnp.dot(y_ref[...], wo_ref[...], preferred_element_type=F32)
    out_ref[...] = _ple(h1, p_ref, pw_ref, pn_ref, pg_ref)


def _const_spec(shape):
    return pl.BlockSpec(shape, lambda i: (0,) * len(shape))


def _a_post(hs, u, x, p, layer, hnorm, w_out, ple_w, ple_norm, ple_wg):
    t = x.shape[0]
    tm = _tile(t, 512)
    return pl.pallas_call(
        _a_post_kernel,
        grid=(t // tm,),
        in_specs=[
            pl.BlockSpec((tm, A_V), lambda i: (i, 0)),
            pl.BlockSpec((tm, A_V), lambda i: (i, 2)),
            pl.BlockSpec((tm, A_V), lambda i: (i, 3)),
            pl.BlockSpec((tm, D_MODEL), lambda i: (i, 0)),
            pl.BlockSpec((None, tm, PLE_DIM), lambda i: (layer, i, 0)),
            _const_spec((1, A_V)),
            _const_spec((A_V, D_MODEL)),
            _const_spec((PLE_DIM, D_MODEL)),
            _const_spec((1, D_MODEL)),
            _const_spec((D_MODEL, D_MODEL)),
        ],
        out_specs=pl.BlockSpec((tm, D_MODEL), lambda i: (i, 0)),
        out_shape=jax.ShapeDtypeStruct((t, D_MODEL), F32),
        scratch_shapes=[pltpu.VMEM((tm, A_V), BF16)],
        compiler_params=_params("parallel"),
        name="a_post",
    )(hs, u, u, x, p, hnorm, w_out, ple_w, ple_norm, ple_wg)


def _head_ms(x, gm_ref):
    return jnp.dot((x * x).astype(BF16), gm_ref[...], preferred_element_type=F32) * (1.0 / B_DH)


def _kv_kernel(h_ref, g_ref, w_ref, wfh_ref, wfl_ref, wfth_ref, wftl_ref, bf_ref, bft_ref, kg_ref, gm_ref,
               k_ref, v_ref, kb_ref, vb_ref, vt_ref, lf_ref, lft_ref):
    xs = _split(_rms(h_ref[...], g_ref[...]))
    kv = jnp.dot(xs[0], w_ref[...], preferred_element_type=F32)
    k = kv[:, :B_W]
    v = kv[:, B_W:]
    kn = k * lax.rsqrt(_head_ms(k, gm_ref) + EPS) * kg_ref[...]
    k_ref[...] = pltpu.einshape("m(hd)->mhd", kn, h=B_HEADS)
    v_ref[...] = pltpu.einshape("m(hd)->mhd", v, h=B_HEADS)
    kb_ref[...] = kn.astype(BF16)
    vb_ref[...] = v.astype(BF16)
    vt_ref[...] = v.T.astype(BF16)
    fg = _dot_split(xs, (wfh_ref[...], wfl_ref[...]), NN)
    lf_ref[...] = _log_sigmoid(fg + bf_ref[...])
    fgt = _dot_split((wfth_ref[...], wftl_ref[...]), xs, NT)
    lft_ref[...] = _log_sigmoid(fgt + bft_ref[...])


def _kv_proj(h, g, w_kv, w_f, w_ft, b_f, b_ft, k_g, gmat):
    t = h.shape[0]
    tm = _tile(t, 512)
    row = lambda w: pl.BlockSpec((tm, w), lambda i: (i, 0))
    return pl.pallas_call(
        _kv_kernel,
        grid=(t // tm,),
        in_specs=[
            row(D_MODEL),
            _const_spec((1, D_MODEL)),
            _const_spec((D_MODEL, 2 * B_W)),
            _const_spec((D_MODEL, B_HEADS)),
            _const_spec((D_MODEL, B_HEADS)),
            _const_spec((B_HEADS, D_MODEL)),
            _const_spec((B_HEADS, D_MODEL)),
            _const_spec((1, B_HEADS)),
            _const_spec((B_HEADS, 1)),
            _const_spec((1, B_W)),
            _const_spec((B_W, B_W)),
        ],
        out_specs=[pl.BlockSpec((tm, B_HEADS, B_DH), lambda i: (i, 0, 0)),
                   pl.BlockSpec((tm, B_HEADS, B_DH), lambda i: (i, 0, 0)), row(B_W), row(B_W),
                   pl.BlockSpec((B_W, tm), lambda i: (0, i)), row(B_HEADS),
                   pl.BlockSpec((B_HEADS, tm), lambda i: (0, i))],
        out_shape=[
            jax.ShapeDtypeStruct((t, B_HEADS, B_DH), F32),
            jax.ShapeDtypeStruct((t, B_HEADS, B_DH), F32),
            jax.ShapeDtypeStruct((t, B_W), BF16),
            jax.ShapeDtypeStruct((t, B_W), BF16),
            jax.ShapeDtypeStruct((B_W, t), BF16),
            jax.ShapeDtypeStruct((t, B_HEADS), F32),
            jax.ShapeDtypeStruct((B_HEADS, t), F32),
        ],
        compiler_params=_params("parallel"),
        name="kv_proj",
    )(h, g, w_kv, *w_f, *w_ft, b_f, b_ft, k_g, gmat)


def _q_kernel(h_ref, g_ref, w_ref, qg_ref, gm_ref, q_ref, z_ref):
    xn = _rms(h_ref[...], g_ref[...])
    qz = jnp.dot(xn.astype(BF16), w_ref[...], preferred_element_type=F32)
    q = qz[:, :B_W]
    z_ref[...] = qz[:, B_W:]
    qn = q * lax.rsqrt(_head_ms(q, gm_ref) + EPS) * qg_ref[...]
    q_ref[...] = (qn * (B_DH ** -0.5 * LOG2E)).astype(BF16)


def _q_proj(h, g, w_in, q_g, gmat):
    t = h.shape[0]
    tm = _tile(t, 512)
    row = lambda w: pl.BlockSpec((tm, w), lambda i: (i, 0))
    return pl.pallas_call(
        _q_kernel,
        grid=(t // tm,),
        in_specs=[row(D_MODEL), _const_spec((1, D_MODEL)), _const_spec((D_MODEL, 2 * B_W)),
                  _const_spec((1, B_W)), _const_spec((B_W, B_W))],
        out_specs=[row(B_W), row(B_W)],
        out_shape=[jax.ShapeDtypeStruct((t, B_W), BF16), jax.ShapeDtypeStruct((t, B_W), F32)],
        compiler_params=_params("parallel"),
        name="q_proj",
    )(h, g, w_in, q_g, gmat)


def _cumsum_kernel(x_ref, f_ref, carry_ref):
    @pl.when(pl.program_id(1) == 0)
    def _():
        carry_ref[...] = jnp.zeros_like(carry_ref)

    w = x_ref.shape[-1]
    f = _seg_cumsum_lanes(x_ref[...], w) + carry_ref[:, 0:1]
    f_ref[...] = f
    carry_ref[...] = jnp.broadcast_to(f[:, w - 1:w], carry_ref.shape)


def _cumsum_rows(x, bsz, width):
    tb = _tile(width, 2048) if width % LANES == 0 and (width // LANES) & (width // LANES - 1) == 0 else width
    nb = width // tb
    if x.ndim == 2:
        in_spec = pl.BlockSpec((B_HEADS, tb), lambda b, j: (0, b * nb + j))
    else:
        in_spec = pl.BlockSpec((None, B_HEADS, tb), lambda b, j: (b, 0, j))
    return pl.pallas_call(
        _cumsum_kernel,
        grid=(bsz, nb),
        in_specs=[in_spec],
        out_specs=pl.BlockSpec((None, B_HEADS, tb), lambda b, j: (b, 0, j)),
        out_shape=jax.ShapeDtypeStruct((bsz, B_HEADS, width), F32),
        scratch_shapes=[pltpu.VMEM((B_HEADS, LANES), F32)],
        compiler_params=_params("parallel", "arbitrary"),
        name="logf_cumsum",
    )(x)


def _cumsum_aug_kernel(x_ref, place_ref, f_ref, ka_ref, carry_ref):
    @pl.when(pl.program_id(1) == 0)
    def _():
        carry_ref[...] = jnp.zeros_like(carry_ref)

    w = x_ref.shape[-1]
    f = _seg_cumsum_lanes(x_ref[...], w) + carry_ref[:, 0:1]
    f_ref[...] = f
    carry_ref[...] = jnp.broadcast_to(f[:, w - 1:w], carry_ref.shape)
    rows = jnp.concatenate(_split3(f * LOG2E) + (jnp.ones((B_HEADS, w), BF16),), axis=0)
    ka_ref[...] = lax.dot_general(rows, place_ref[...], TN, preferred_element_type=F32).astype(BF16)


def _bias_placement():
    r = jnp.arange(4 * B_HEADS)[:, None]
    c = jnp.arange(B_W)[None, :]
    part, head = r // B_HEADS, r % B_HEADS
    lane0 = (head // 2) * LANES + BIAS_LANES * (head % 2)
    is_part = (part < 3) & (c == lane0 + part)
    is_one = (part == 3) & (c >= lane0 + 3) & (c < lane0 + 6)
    return (is_part | is_one).astype(BF16)


def _cumsum_aug(x, bsz, width):
    tb = _tile(width, 2048)
    nb = width // tb
    return pl.pallas_call(
        _cumsum_aug_kernel,
        grid=(bsz, nb),
        in_specs=[pl.BlockSpec((B_HEADS, tb), lambda b, j: (0, b * nb + j)),
                  pl.BlockSpec((4 * B_HEADS, B_W), lambda b, j: (0, 0))],
        out_specs=[pl.BlockSpec((None, B_HEADS, tb), lambda b, j: (b, 0, j)),
                   pl.BlockSpec((tb, B_W), lambda b, j: (b * nb + j, 0))],
        out_shape=[jax.ShapeDtypeStruct((bsz, B_HEADS, width), F32),
                   jax.ShapeDtypeStruct((bsz * width, B_W), BF16)],
        scratch_shapes=[pltpu.VMEM((B_HEADS, LANES), F32)],
        compiler_params=_params("parallel", "arbitrary"),
        name="logf_cumsum_aug",
    )(x, _bias_placement())


def _fox_bounded_kernel(bound_ref, q_ref, k_ref, ka_ref, vt_ref, f_ref, o_ref, qt_ref, acc_ref,
                        p00, p01, p10, p11, *, tq):
    tk = tq // 2
    pbuf = ((p00, p01), (p10, p11))
    qi = pl.program_id(2)
    q0 = pl.multiple_of(qi * tq, tq)
    feat = lax.broadcasted_iota(jnp.int32, (LANES, tq), 0)
    qt = q_ref[...].astype(F32).T
    beta = _split3(f_ref[:, pl.ds(q0, tq)] * LOG2E - bound_ref[...])
    for hh in range(2):
        lo = BIAS_LANES * hh
        qt_ref[hh, 0:LANES, :] = jnp.where((feat >= hh * B_DH) & (feat < (hh + 1) * B_DH), qt, 0.0).astype(BF16)
        aug = jnp.where((feat >= lo) & (feat < lo + 3), -1.0, 0.0)
        for part in range(3):
            aug = jnp.where(feat == lo + 3 + part, beta[part][hh:hh + 1].astype(F32), aug)
        qt_ref[hh, LANES:2 * LANES, :] = aug.astype(BF16)
    acc_ref[...] = jnp.zeros_like(acc_ref)
    ones = jnp.ones((ACC_ROWS - B_DH, tk), BF16)
    kpos = lax.broadcasted_iota(jnp.int32, (tk, tq), 0)
    qpos = lax.broadcasted_iota(jnp.int32, (tk, tq), 1)
    diag_masks = (kpos <= qpos, (kpos + tk) <= qpos)

    def weights(k0, slot, causal=None):
        lhs = jnp.concatenate([k_ref[pl.ds(k0, tk), :], ka_ref[pl.ds(k0, tk), :]], axis=1)
        for hh in range(2):
            e = jnp.dot(lhs, qt_ref[hh], preferred_element_type=F32)
            if causal is not None:
                e = jnp.where(causal, e, NEG)
            pbuf[slot][hh][...] = jnp.exp2(e).astype(BF16)

    def accumulate(k0, slot):
        for hh in range(2):
            vt = jnp.concatenate([vt_ref[hh * B_DH:(hh + 1) * B_DH, pl.ds(k0, tk)], ones], axis=0)
            acc_ref[hh] += jnp.dot(vt, pbuf[slot][hh][...], preferred_element_type=F32)

    def pair(k0, next_causal):
        weights(k0 + tk, 1)
        accumulate(k0, 0)
        weights(k0 + tq, 0, next_causal)
        accumulate(k0 + tk, 1)

    def finish():
        weights(q0 + tk, 1, diag_masks[1])
        accumulate(q0, 0)
        accumulate(q0 + tk, 1)
        o_t = jnp.concatenate(
            [acc_ref[hh, 0:B_DH] * (1.0 / acc_ref[hh, B_DH:B_DH + 1]) for hh in range(2)], axis=0)
        o_ref[...] = o_t.T

    @pl.when(qi == 0)
    def _():
        weights(q0, 0, diag_masks[0])
        finish()

    @pl.when(qi > 0)
    def _():
        weights(0, 0)
        n_pairs = qi - 1

        def unrolled(jj, carry):
            for u in range(PAIR_UNROLL):
                pair(pl.multiple_of((jj * PAIR_UNROLL + u) * tq, tq), None)
            return carry

        def single(jj, carry):
            pair(pl.multiple_of(jj * tq, tq), None)
            return carry

        lax.fori_loop(0, n_pairs // PAIR_UNROLL, unrolled, 0)
        lax.fori_loop((n_pairs // PAIR_UNROLL) * PAIR_UNROLL, n_pairs, single, 0)
        pair(q0 - tq, diag_masks[0])
        finish()


def _fox_prompt_bounded(q, kb, ka, vt, f_row, bound, bsz, seq):
    tq = _tile(seq, 1024)
    nq = seq // tq
    kern = functools.partial(_fox_bounded_kernel, tq=tq)
    return pl.pallas_call(
        kern,
        grid=(bsz, B_PAIRS, nq),
        in_specs=[
            pl.BlockSpec((1, 1), lambda b, p, qi: (0, 0)),
            pl.BlockSpec((tq, LANES), lambda b, p, qi: (b * nq + qi, p)),
            pl.BlockSpec((seq, LANES), lambda b, p, qi: (b, p)),
            pl.BlockSpec((seq, LANES), lambda b, p, qi: (b, p)),
            pl.BlockSpec((LANES, seq), lambda b, p, qi: (p, b)),
            pl.BlockSpec((None, None, 2, seq), lambda b, p, qi: (b, p, 0, 0)),
        ],
        out_specs=pl.BlockSpec((tq, LANES), lambda b, p, qi: (b * nq + qi, p)),
        out_shape=jax.ShapeDtypeStruct((bsz * seq, B_W), F32),
        scratch_shapes=[
            pltpu.VMEM((2, 2 * LANES, tq), BF16),
            pltpu.VMEM((2, ACC_ROWS, tq), F32),
        ] + [pltpu.VMEM((tq // 2, tq), BF16)] * 4,
        compiler_params=_params("parallel", "parallel", "arbitrary"),
        name="fox_prompt_bounded",
    )(bound, q, kb, ka, vt, f_row)


def _fox_kernel(q_ref, k_ref, vt_ref, f_ref, o_ref, qt_ref, m_ref, cm_ref, acc_ref,
                s00, s01, s10, s11, *, tq):
    tk = tq // 2
    sbuf = ((s00, s01), (s10, s11))
    qi = pl.program_id(2)
    q0 = pl.multiple_of(qi * tq, tq)
    feat = lax.broadcasted_iota(jnp.int32, (LANES, tq), 0)
    qt = q_ref[...].astype(F32).T
    qt_ref[0] = jnp.where(feat < B_DH, qt, 0.0).astype(BF16)
    qt_ref[1] = jnp.where(feat >= B_DH, qt, 0.0).astype(BF16)
    m_ref[...] = jnp.full_like(m_ref, NEG)
    acc_ref[...] = jnp.zeros_like(acc_ref)
    f_base = f_ref[:, pl.ds(q0, tq)][:, tq - 1:tq]
    ones = jnp.ones((ACC_ROWS - B_DH, tk), BF16)
    kpos = lax.broadcasted_iota(jnp.int32, (tk, tq), 0)
    qpos = lax.broadcasted_iota(jnp.int32, (tk, tq), 1)
    diag_masks = (kpos <= qpos, (kpos + tk) <= qpos)

    def scores(k0, slot, causal=None):
        kb = k_ref[pl.ds(k0, tk), :]
        fk = (f_ref[:, pl.ds(k0, tk)] - f_base) * LOG2E
        for hh in range(2):
            fk_col = jnp.concatenate(
                [jnp.broadcast_to(fk[hh:hh + 1, c * LANES:(c + 1) * LANES], (LANES, LANES)).T
                 for c in range(tk // LANES)], axis=0)
            t = (jnp.dot(kb, qt_ref[hh], preferred_element_type=F32)
                 - pltpu.repeat(fk_col, tq // LANES, axis=1))
            if causal is not None:
                t = jnp.where(causal, t, NEG)
            sbuf[slot][hh][...] = t
            cm_ref[slot, hh] = jnp.max(t, axis=0, keepdims=True)

    def update(k0, slot):
        for hh in range(2):
            m_prev = m_ref[hh]
            m_new = jnp.maximum(m_prev, cm_ref[slot, hh])
            alpha = jnp.exp2(m_prev - m_new)
            p = jnp.exp2(sbuf[slot][hh][...] - m_new).astype(BF16)
            vt = jnp.concatenate([vt_ref[hh * B_DH:(hh + 1) * B_DH, pl.ds(k0, tk)], ones], axis=0)
            acc_ref[hh] = alpha * acc_ref[hh] + jnp.dot(vt, p, preferred_element_type=F32)
            m_ref[hh] = m_new

    def pair(k0, next_causal):
        scores(k0 + tk, 1)
        update(k0, 0)
        scores(k0 + tq, 0, next_causal)
        update(k0 + tk, 1)

    @pl.when(qi == 0)
    def _():
        scores(q0, 0, diag_masks[0])

    @pl.when(qi > 0)
    def _():
        scores(0, 0)

        def body(jj, carry):
            pair(pl.multiple_of(jj * tq, tq), None)
            return carry

        lax.fori_loop(0, qi - 1, body, 0)
        pair(q0 - tq, diag_masks[0])

    scores(q0 + tk, 1, diag_masks[1])
    update(q0, 0)
    update(q0 + tk, 1)
    o_t = jnp.concatenate(
        [acc_ref[hh, 0:B_DH] * (1.0 / acc_ref[hh, B_DH:B_DH + 1]) for hh in range(2)], axis=0)
    o_ref[...] = o_t.T


def _fox_prompt(q, kb, vt, f_row, bsz, seq):
    tq = _tile(seq, 512)
    tk = tq // 2
    nq = seq // tq
    kern = functools.partial(_fox_kernel, tq=tq)
    return pl.pallas_call(
        kern,
        grid=(bsz, B_PAIRS, nq),
        in_specs=[
            pl.BlockSpec((tq, LANES), lambda b, p, qi: (b * nq + qi, p)),
            pl.BlockSpec((seq, LANES), lambda b, p, qi: (b, p)),
            pl.BlockSpec((LANES, seq), lambda b, p, qi: (p, b)),
            pl.BlockSpec((None, None, 2, seq), lambda b, p, qi: (b, p, 0, 0)),
        ],
        out_specs=pl.BlockSpec((tq, LANES), lambda b, p, qi: (b * nq + qi, p)),
        out_shape=jax.ShapeDtypeStruct((bsz * seq, B_W), F32),
        scratch_shapes=[
            pltpu.VMEM((2, LANES, tq), BF16),
            pltpu.VMEM((2, 1, tq), F32),
            pltpu.VMEM((2, 2, 1, tq), F32),
            pltpu.VMEM((2, ACC_ROWS, tq), F32),
        ] + [pltpu.VMEM((tk, tq), F32)] * 4,
        compiler_params=_params("parallel", "parallel", "arbitrary"),
        name="fox_prompt",
    )(q, kb, vt, f_row)


def _fox_dec_kernel(bound_ref, q_ref, kt_ref, vt_ref, kn_ref, vn_ref, fn_ref, fc_ref, o_ref,
                    fqc_ref, m_ref, l_ref, a_ref, acc_ref, s_ref, p_ref, *, sq, nkb, bounded):
    kj = pl.program_id(1)
    head = lambda h: slice(h * B_DH, (h + 1) * B_DH)

    @pl.when(kj == 0)
    def _():
        r = lax.broadcasted_iota(jnp.int32, (sq, LANES), 0)
        c = lax.broadcasted_iota(jnp.int32, (sq, LANES), 1)
        for h in range(B_HEADS):
            fqc_ref[h] = jnp.sum(jnp.where(r == c, fn_ref[h:h + 1, :], 0.0), axis=1, keepdims=True)
        m_ref[...] = jnp.full_like(m_ref, NEG)
        l_ref[...] = jnp.zeros_like(l_ref)
        acc_ref[...] = jnp.zeros_like(acc_ref)

    def attend(width, score, f_keys, values, valid):
        for h in range(B_HEADS):
            s_ref[h, :, 0:width] = score(h)
        for h in range(B_HEADS):
            t = s_ref[h, :, 0:width] + (fqc_ref[h] - f_keys(h)) * LOG2E
            if valid is not None:
                t = jnp.where(valid, t, NEG)
            if bounded:
                p = jnp.exp2(t - bound_ref[...])
                l_ref[h] += jnp.sum(p, axis=1, keepdims=True)
            else:
                m_prev = m_ref[h]
                m_new = jnp.maximum(m_prev, jnp.max(t, axis=1, keepdims=True))
                alpha = jnp.exp2(m_prev - m_new)
                p = jnp.exp2(t - m_new)
                l_ref[h] = alpha * l_ref[h] + jnp.sum(p, axis=1, keepdims=True)
                m_ref[h] = m_new
                a_ref[h] = alpha
            p_ref[h, :, 0:width] = p.astype(BF16)
        for h in range(B_HEADS):
            pv = values(h, p_ref[h, :, 0:width])
            acc_ref[h] = acc_ref[h] + pv if bounded else a_ref[h] * acc_ref[h] + pv

    @pl.when(kj == 0)
    def _():
        i = lax.broadcasted_iota(jnp.int32, (sq, LANES), 0)
        j = lax.broadcasted_iota(jnp.int32, (sq, LANES), 1)
        attend(
            LANES,
            lambda h: lax.dot_general(q_ref[:, head(h)], kn_ref[:, head(h)], NT, preferred_element_type=F32),
            lambda h: fn_ref[h:h + 1, :],
            lambda h, p: jnp.dot(p, vn_ref[:, head(h)], preferred_element_type=F32),
            j <= i)

    @pl.when(kj > 0)
    def _():
        attend(
            kt_ref.shape[-1],
            lambda h: jnp.dot(q_ref[:, head(h)], kt_ref[h].astype(BF16), preferred_element_type=F32),
            lambda h: fc_ref[h:h + 1, :],
            lambda h, p: lax.dot_general(p, vt_ref[h].astype(BF16), NT, preferred_element_type=F32),
            None)

    @pl.when(kj == nkb)
    def _():
        for h in range(B_HEADS):
            o_ref[:, head(h)] = acc_ref[h] * (1.0 / l_ref[h])


def _fox_decode(q, cache_kt, cache_vt, kn, vn, f_new, f_cache, bound, bounded, bsz, sq):
    past = cache_kt.shape[3]
    tk = _tile(past, 2048)
    nkb = past // tk
    cmap = lambda b, kj: (b, 0, 0, jnp.maximum(kj - 1, 0))
    kern = functools.partial(_fox_dec_kernel, sq=sq, nkb=nkb, bounded=bounded)
    return pl.pallas_call(
        kern,
        grid=(bsz, nkb + 1),
        in_specs=[
            pl.BlockSpec((1, 1), lambda b, kj: (0, 0)),
            pl.BlockSpec((None, sq, B_W), lambda b, kj: (b, 0, 0)),
            pl.BlockSpec((None, B_HEADS, B_DH, tk), cmap),
            pl.BlockSpec((None, B_HEADS, B_DH, tk), cmap),
            pl.BlockSpec((None, LANES, B_W), lambda b, kj: (b, 0, 0)),
            pl.BlockSpec((None, LANES, B_W), lambda b, kj: (b, 0, 0)),
            pl.BlockSpec((None, B_HEADS, LANES), lambda b, kj: (b, 0, 0)),
            pl.BlockSpec((None, B_HEADS, tk), lambda b, kj: (b, 0, jnp.maximum(kj - 1, 0))),
        ],
        out_specs=pl.BlockSpec((None, sq, B_W), lambda b, kj: (b, 0, 0)),
        out_shape=jax.ShapeDtypeStruct((bsz, sq, B_W), F32),
        scratch_shapes=[
            pltpu.VMEM((B_HEADS, sq, 1), F32),
            pltpu.VMEM((B_HEADS, sq, 1), F32),
            pltpu.VMEM((B_HEADS, sq, 1), F32),
            pltpu.VMEM((B_HEADS, sq, 1), F32),
            pltpu.VMEM((B_HEADS, sq, B_DH), F32),
            pltpu.VMEM((B_HEADS, sq, tk), F32),
            pltpu.VMEM((B_HEADS, sq, tk), BF16),
        ],
        compiler_params=_params("parallel", "arbitrary"),
        name="fox_decode",
    )(bound, q, cache_kt, cache_vt, kn, vn, f_new, f_cache)


def _b_post_kernel(o_ref, z_ref, h_ref, p_ref, wo_ref, pw_ref, pn_ref, pg_ref, out_ref):
    z = z_ref[...]
    y = o_ref[...] * (z * _sigmoid(z))
    h2 = h_ref[...] + jnp.dot(y.astype(BF16), wo_ref[...], preferred_element_type=F32)
    out_ref[...] = _ple(h2, p_ref, pw_ref, pn_ref, pg_ref)


def _b_post(o, z, h, p, layer, w_out, ple_w, ple_norm, ple_wg):
    t = h.shape[0]
    tm = _tile(t, 1024)
    row = lambda w: pl.BlockSpec((tm, w), lambda i: (i, 0))
    return pl.pallas_call(
        _b_post_kernel,
        grid=(t // tm,),
        in_specs=[row(B_W), row(B_W), row(D_MODEL), pl.BlockSpec((None, tm, PLE_DIM), lambda i: (layer, i, 0)),
                  _const_spec((B_W, D_MODEL)), _const_spec((PLE_DIM, D_MODEL)),
                  _const_spec((1, D_MODEL)), _const_spec((D_MODEL, D_MODEL))],
        out_specs=row(D_MODEL),
        out_shape=jax.ShapeDtypeStruct((t, D_MODEL), F32),
        compiler_params=_params("parallel"),
        name="b_post",
    )(o, z, h, p, w_out, ple_w, ple_norm, ple_wg)


def _prep_weights(a_norm, a_w_in, a_b_i, a_b_f, a_hnorm, a_w_out, kv_norm, kv_w, kv_b_f, k_norm,
                  b_norm, b_w_in, q_norm, b_w_out, ple_w, ple_norm, ple_w_g):
    head_of_lane = jnp.arange(B_W, dtype=jnp.int32) // B_DH
    return dict(
        a_norm=a_norm[0].reshape(1, D_MODEL),
        a_w_main=a_w_in[0][:, :A_MAIN].astype(BF16),
        a_w_gt=_split(a_w_in[0][:, A_MAIN:].T),
        a_bias=jnp.concatenate([a_b_i[0], a_b_f[0]]).reshape(2 * A_HEADS, 1),
        a_hnorm=a_hnorm[0].reshape(1, A_V),
        a_w_out=a_w_out[0].astype(BF16),
        kv_norm=kv_norm.reshape(1, D_MODEL),
        kv_w=kv_w[:, :2 * B_W].astype(BF16),
        kv_wf=_split(kv_w[:, 2 * B_W:]),
        kv_wft=_split(kv_w[:, 2 * B_W:].T),
        kv_bf=kv_b_f.reshape(1, B_HEADS),
        kv_bft=kv_b_f.reshape(B_HEADS, 1),
        k_norm=jnp.tile(k_norm, B_HEADS).reshape(1, B_W),
        gmat=(head_of_lane[:, None] == head_of_lane[None, :]).astype(BF16),
        b_norm=b_norm[0].reshape(1, D_MODEL),
        b_w_in=b_w_in[0].astype(BF16),
        q_norm=jnp.tile(q_norm[0], B_HEADS).reshape(1, B_W),
        b_w_out=b_w_out[0].astype(BF16),
        ple_w=ple_w.astype(BF16),
        ple_norm=ple_norm.reshape(-1, 1, D_MODEL),
        ple_wg=ple_w_g.astype(BF16),
    )


def _trunk(x, p, c0, n0, m0, past, w):
    bsz, seq, _ = x.shape
    t = bsz * seq
    xf = x.reshape(t, D_MODEL)
    pf = p.reshape(p.shape[0], t, PLE_DIM)

    u, gt = _a_inproj(xf, w["a_norm"], w["a_w_main"], w["a_w_gt"])
    m0r = jnp.broadcast_to(m0.reshape(bsz, A_HEADS, 1, 1), (bsz, A_HEADS, 1, LANES))
    hs, c_new, n_new, m_new = _mlstm(u, gt, w["a_bias"], c0, n0[..., None], m0r, bsz, seq)
    h1 = _a_post(hs, u, xf, pf, 0, w["a_hnorm"], w["a_w_out"], w["ple_w"][0], w["ple_norm"][0], w["ple_wg"][0])

    k, v, kb, vb, vt, lf, lft = _kv_proj(h1, w["kv_norm"], w["kv_w"], w["kv_wf"], w["kv_wft"],
                                     w["kv_bf"], w["kv_bft"], w["k_norm"], w["gmat"])
    q, z = _q_proj(h1, w["b_norm"], w["b_w_in"], w["q_norm"], w["gmat"])
    bound = (8.0 * LOG2E * 1.01) * jnp.max(jnp.abs(w["q_norm"])) * jnp.max(jnp.abs(w["k_norm"])) + 0.05
    bound11 = bound.reshape(1, 1)
    if past is None:
        f_row, ka = _cumsum_aug(lft, bsz, seq)
        f_row = f_row.reshape(bsz, B_PAIRS, 2, seq)
        o = lax.cond(
            bound <= MAX_BOUND,
            lambda: _fox_prompt_bounded(q, kb, ka, vt, f_row, bound11, bsz, seq),
            lambda: _fox_prompt(q, kb, vt, f_row, bsz, seq))
    else:
        cache_k, cache_v, cache_lf = past
        plen = cache_k.shape[1]
        assert seq <= LANES
        width = -(-(plen + seq) // LANES) * LANES
        lf_rows = jnp.concatenate(
            [jnp.swapaxes(cache_lf.astype(F32), 1, 2),
             jnp.swapaxes(lft.reshape(B_HEADS, bsz, seq), 0, 1),
             jnp.zeros((bsz, B_HEADS, width - plen - seq), F32)], axis=2)
        f_all = _cumsum_rows(lf_rows, bsz, width)
        f_new = jnp.pad(f_all[..., plen:plen + seq], ((0, 0), (0, 0), (0, LANES - seq)))
        pad_new = lambda a: jnp.pad(a.reshape(bsz, seq, B_W), ((0, 0), (0, LANES - seq), (0, 0)))
        to_t = lambda a: jnp.transpose(a.astype(F32), (0, 2, 3, 1))
        dec_args = (q.reshape(bsz, seq, B_W), to_t(cache_k), to_t(cache_v), pad_new(kb), pad_new(vb),
                    f_new, f_all[..., :plen], bound11)
        o = lax.cond(
            bound <= MAX_BOUND,
            lambda: _fox_decode(*dec_args, True, bsz, seq),
            lambda: _fox_decode(*dec_args, False, bsz, seq)).reshape(t, B_W)
    y = _b_post(o, z, h1, pf, 1, w["b_w_out"], w["ple_w"][1], w["ple_norm"][1], w["ple_wg"][1])

    return (y.reshape(bsz, seq, D_MODEL),
            c_new[None], n_new.reshape(1, bsz, A_HEADS, A_DK), m_new[None, :, :, 0, 0],
            k.reshape(bsz, seq, B_HEADS, B_DH), v.reshape(bsz, seq, B_HEADS, B_DH),
            lf.reshape(bsz, seq, B_HEADS))


def kernel(x_prompt, x_sample, cache_k, cache_v, cache_logf, state_C, state_n, state_m, p_prompt, p_sample,
           a_norm, a_w_in, a_b_i, a_b_f, a_hnorm, a_w_out, kv_norm, kv_w, kv_b_f, k_norm,
           b_norm, b_w_in, q_norm, b_w_out, ple_w, ple_norm, ple_w_g):
    w = _prep_weights(a_norm, a_w_in, a_b_i, a_b_f, a_hnorm, a_w_out, kv_norm, kv_w, kv_b_f, k_norm,
                      b_norm, b_w_in, q_norm, b_w_out, ple_w, ple_norm, ple_w_g)
    bsz = x_prompt.shape[0]
    c0 = jnp.zeros((bsz, A_HEADS, A_DK, A_DV), F32)
    n0 = jnp.zeros((bsz, A_HEADS, A_DK), F32)
    m0 = jnp.zeros((bsz, A_HEADS), F32)
    prompt = _trunk(x_prompt, p_prompt, c0, n0, m0, None, w)
    sample = _trunk(x_sample, p_sample, state_C[0].astype(F32), state_n[0].astype(F32),
                    state_m[0].astype(F32), (cache_k, cache_v, cache_logf), w)
    return (prompt[0], sample[0]) + prompt[1:] + sample[1:]
```

```python
import functools

import jax
import jax.numpy as jnp
from jax import lax
from jax.experimental import pallas as pl
from jax.experimental.pallas import tpu as pltpu

D_MODEL = 1024
A_HEADS = 8
A_DK = 128
A_DV = 256
A_QK = A_HEADS * A_DK
A_V = A_HEADS * A_DV
A_MAIN = 2 * A_QK + 3 * A_V
B_HEADS = 16
B_DH = 64
B_W = B_HEADS * B_DH
B_PAIRS = B_HEADS // 2
PLE_DIM = 256
EPS = 1e-6
NEG = -1e30
LOG2E = 1.4426950408889634
ACC_ROWS = B_DH + 16
BIAS_LANES = 6
MAX_BOUND = 60.0
PAIR_UNROLL = 4

LANES = 128
VMEM_LIMIT = 56 * 1024 * 1024

F32 = jnp.float32
BF16 = jnp.bfloat16
NN = (((1,), (0,)), ((), ()))
NT = (((1,), (1,)), ((), ()))
TN = (((0,), (0,)), ((), ()))


def _params(*sem):
    return pltpu.CompilerParams(dimension_semantics=sem, vmem_limit_bytes=VMEM_LIMIT)


def _tile(n, pref):
    t = min(n, pref)
    while n % t:
        t //= 2
    return t


def _rms(x, g):
    ms = jnp.mean(x * x, axis=-1, keepdims=True)
    return x * lax.rsqrt(ms + EPS) * g


def _log_sigmoid(x):
    return jnp.minimum(x, 0.0) - jnp.log1p(jnp.exp(-jnp.abs(x)))


def _sigmoid(x):
    return 0.5 * jnp.tanh(0.5 * x) + 0.5


def _seg_cummax_lanes(x, seg):
    w = x.shape[-1]
    pos = lax.broadcasted_iota(jnp.int32, x.shape, x.ndim - 1) % seg
    k = 1
    while k < min(seg, w):
        x = jnp.maximum(x, jnp.where(pos >= k, pltpu.roll(x, k, x.ndim - 1), NEG))
        k *= 2
    return x


def _seg_cumsum_lanes(x, seg):
    w = x.shape[-1]
    pos = lax.broadcasted_iota(jnp.int32, x.shape, x.ndim - 1) % seg
    k = 1
    while k < min(seg, w):
        x = x + jnp.where(pos >= k, pltpu.roll(x, k, x.ndim - 1), 0.0)
        k *= 2
    return x


def _split(x):
    hi = x.astype(BF16)
    return hi, (x - hi.astype(F32)).astype(BF16)


def _dot_split(a, b, dims):
    dot = functools.partial(lax.dot_general, dimension_numbers=dims, preferred_element_type=F32)
    return dot(a[0], b[0]) + (dot(a[0], b[1]) + dot(a[1], b[0]))


def _split3(x):
    hi = x.astype(BF16)
    r = x - hi.astype(F32)
    mid = r.astype(BF16)
    return hi, mid, (r - mid.astype(F32)).astype(BF16)


def _a_inproj_kernel(x_ref, g_ref, w_ref, wgh_ref, wgl_ref, u_ref, gt_ref, xn_ref):
    @pl.when(pl.program_id(1) == 0)
    def _():
        xn = _rms(x_ref[...], g_ref[...])
        xs = _split(xn)
        xn_ref[...] = xs[0]
        gt_ref[...] = _dot_split((wgh_ref[...], wgl_ref[...]), xs, NT)

    u_ref[...] = jnp.dot(xn_ref[...], w_ref[...], preferred_element_type=F32).astype(BF16)


def _a_inproj(x, g, w_main, w_gt):
    t = x.shape[0]
    tm = _tile(t, 1024)
    tn = 2048
    gate_spec = pl.BlockSpec((2 * A_HEADS, D_MODEL), lambda i, j: (0, 0))
    return pl.pallas_call(
        _a_inproj_kernel,
        grid=(t // tm, A_MAIN // tn),
        in_specs=[
            pl.BlockSpec((tm, D_MODEL), lambda i, j: (i, 0)),
            pl.BlockSpec((1, D_MODEL), lambda i, j: (0, 0)),
            pl.BlockSpec((D_MODEL, tn), lambda i, j: (0, j)),
            gate_spec,
            gate_spec,
        ],
        out_specs=[
            pl.BlockSpec((tm, tn), lambda i, j: (i, j)),
            pl.BlockSpec((2 * A_HEADS, tm), lambda i, j: (0, i)),
        ],
        out_shape=[
            jax.ShapeDtypeStruct((t, A_MAIN), BF16),
            jax.ShapeDtypeStruct((2 * A_HEADS, t), F32),
        ],
        scratch_shapes=[pltpu.VMEM((tm, D_MODEL), BF16)],
        compiler_params=_params("parallel", "arbitrary"),
        name="a_inproj",
    )(x, g, w_main, *w_gt)


def _mlstm_kernel(q_ref, k_ref, v_ref, g_ref, bias_ref, c0_ref, n0_ref, m0_ref,
                  h_ref, c_out_ref, n_out_ref, m_ref, c_ref, gate_ref, *, rows, n_seq, seq_len, n_chunks, steps):
    R = rows

    @pl.when(pl.program_id(1) == 0)
    def _():
        c_ref[:, :, :, 0:A_DV] = c0_ref[...]
        c_ref[:, :, :, A_DV:] = jnp.broadcast_to(n0_ref[...], (n_seq, A_HEADS, A_DK, LANES))
        m_ref[...] = m0_ref[...]

    assert R == LANES
    row = lax.broadcasted_iota(jnp.int32, (R, R), 0)
    col = lax.broadcasted_iota(jnp.int32, (R, R), 1)
    valid = (col <= row) & ((row // seq_len) == (col // seq_len))
    lane8 = lax.broadcasted_iota(jnp.int32, (A_HEADS, R), 1)
    bias = bias_ref[...]
    scale = A_DK ** -0.5
    ones = jnp.ones((R, LANES), BF16)
    g = g_ref[...]
    ig_blk = g[0:A_HEADS] + bias[0:A_HEADS]
    b_blk = _seg_cumsum_lanes(_log_sigmoid(g[A_HEADS:] + bias[A_HEADS:]), seq_len)
    gate_ref[0:A_HEADS, :] = ig_blk
    gate_ref[A_HEADS:2 * A_HEADS, :] = b_blk
    gate_ref[2 * A_HEADS:, :] = _seg_cummax_lanes(ig_blk - b_blk, seq_len)

    def per_seq_rows(vals):
        out = vals[0]
        for s in range(1, n_seq):
            out = jnp.where(lane8 >= s * seq_len, vals[s], out)
        return jnp.broadcast_to(out, (A_HEADS, R))

    def chunk(r0):
        rows = pl.ds(r0, R)
        ig = gate_ref[0:A_HEADS, rows]
        b = gate_ref[A_HEADS:2 * A_HEADS, rows]
        cmax = gate_ref[2 * A_HEADS:, rows]
        m_prev = [m_ref[s, :, 0, 0:1] for s in range(n_seq)]
        m_prev_rows = per_seq_rows(m_prev)
        m_t = b + jnp.maximum(m_prev_rows, cmax)
        a = b - m_t
        ends = [(s + 1) * seq_len - 1 for s in range(n_seq)]
        b_last = [b[:, e:e + 1] for e in ends]
        m_new = [m_t[:, e:e + 1] for e in ends]
        w_s = jnp.exp(per_seq_rows(b_last) - b + ig - per_seq_rows(m_new)) * scale
        w_old = [jnp.exp(b_last[s] + m_prev[s] - m_new[s]) for s in range(n_seq)]
        for s in range(n_seq):
            m_ref[s, :, 0, :] = jnp.broadcast_to(m_new[s], (A_HEADS, LANES))
        cols = jnp.concatenate([a, jnp.exp(a + m_prev_rows), jnp.exp(-m_t), w_s,
                                jnp.zeros((R - 4 * A_HEADS, R), F32)], axis=0).T
        c_rows = ig - b
        for h in range(A_HEADS):
            qb = q_ref[rows, h * A_DK:(h + 1) * A_DK]
            kb = k_ref[rows, h * A_DK:(h + 1) * A_DK]
            vb = jnp.concatenate([v_ref[rows, h * A_DV:(h + 1) * A_DV], ones], axis=1)
            a_col = cols[:, h:h + 1]
            w_inter = cols[:, A_HEADS + h:A_HEADS + h + 1]
            e_m = cols[:, 2 * A_HEADS + h:2 * A_HEADS + h + 1]
            w_s_col = cols[:, 3 * A_HEADS + h:3 * A_HEADS + h + 1]
            p = jnp.exp(jnp.where(valid, a_col + c_rows[h:h + 1], NEG))
            qk = lax.dot_general(qb, kb, NT, preferred_element_type=F32) * scale
            s_mat = qk * p
            qc = [jnp.dot(qb[s * seq_len:(s + 1) * seq_len], c_ref[s, h].astype(BF16),
                          preferred_element_type=F32) for s in range(n_seq)]
            qc = qc[0] if n_seq == 1 else jnp.concatenate(qc, axis=0)
            nd = w_inter * qc + jnp.dot(s_mat.astype(BF16), vb, preferred_element_type=F32)
            den = nd[:, A_DV:A_DV + 1]
            h_ref[rows, h * A_DV:(h + 1) * A_DV] = nd[:, 0:A_DV] * (1.0 / jnp.maximum(jnp.abs(den), e_m))
            ksb = (kb.astype(F32) * w_s_col).astype(BF16)
            for s in range(n_seq):
                sl = slice(s * seq_len, (s + 1) * seq_len)
                c_ref[s, h] = w_old[s][h:h + 1] * c_ref[s, h] + lax.dot_general(
                    ksb[sl], vb[sl], TN, preferred_element_type=F32)

    if n_chunks == 1:
        chunk(0)
    else:
        def body(ci, carry):
            chunk(pl.multiple_of(ci * R, R))
            return carry
        lax.fori_loop(0, n_chunks, body, 0)

    @pl.when(pl.program_id(1) == steps - 1)
    def _():
        c_out_ref[...] = c_ref[:, :, :, 0:A_DV]
        n_out_ref[...] = c_ref[:, :, :, A_DV:A_DV + 1]


def _mlstm(u, gt, bias, c0, n0, m0, bsz, seq):
    if seq % LANES == 0:
        rows = LANES
        seq_len, n_seq = rows, 1
        n_chunks = _tile(seq // rows, 1024 // rows)
    else:
        assert LANES % seq == 0 and bsz % (LANES // seq) == 0, (bsz, seq)
        rows, seq_len, n_seq, n_chunks = LANES, seq, LANES // seq, 1
    tok = n_chunks * rows
    steps = seq // (n_chunks * seq_len)
    t = bsz * seq
    tok_map = lambda b, c: (b * steps + c, 0)
    state4 = lambda shape: pl.BlockSpec((n_seq,) + shape, lambda b, c: (b, 0, 0, 0))
    kern = functools.partial(_mlstm_kernel, rows=rows, n_seq=n_seq, seq_len=seq_len, n_chunks=n_chunks, steps=steps)
    return pl.pallas_call(
        kern,
        grid=(bsz // n_seq, steps),
        in_specs=[
            pl.BlockSpec((tok, A_QK), tok_map),
            pl.BlockSpec((tok, A_QK), lambda b, c: (b * steps + c, 1)),
            pl.BlockSpec((tok, A_V), lambda b, c: (b * steps + c, 1)),
            pl.BlockSpec((2 * A_HEADS, tok), lambda b, c: (0, b * steps + c)),
            pl.BlockSpec((2 * A_HEADS, 1), lambda b, c: (0, 0)),
            state4((A_HEADS, A_DK, A_DV)),
            state4((A_HEADS, A_DK, 1)),
            state4((A_HEADS, 1, LANES)),
        ],
        out_specs=[
            pl.BlockSpec((tok, A_V), tok_map),
            state4((A_HEADS, A_DK, A_DV)),
            state4((A_HEADS, A_DK, 1)),
            state4((A_HEADS, 1, LANES)),
        ],
        out_shape=[
            jax.ShapeDtypeStruct((t, A_V), F32),
            jax.ShapeDtypeStruct((bsz, A_HEADS, A_DK, A_DV), F32),
            jax.ShapeDtypeStruct((bsz, A_HEADS, A_DK, 1), F32),
            jax.ShapeDtypeStruct((bsz, A_HEADS, 1, LANES), F32),
        ],
        scratch_shapes=[pltpu.VMEM((n_seq, A_HEADS, A_DK, A_DV + LANES), F32),
                        pltpu.VMEM((3 * A_HEADS, tok), F32)],
        compiler_params=_params("parallel", "arbitrary"),
        name="mlstm",
    )(u, u, u, gt, bias, c0, n0, m0)


def _ple(h, p_ref, pw_ref, pn_ref, pg_ref):
    pp = jnp.dot(p_ref[...].astype(BF16), pw_ref[...], preferred_element_type=F32)
    hn = _rms(h, pn_ref[...])
    gate = _sigmoid(jnp.dot(hn.astype(BF16), pg_ref[...], preferred_element_type=F32))
    return h + pp * gate


def _a_post_kernel(hs_ref, o_ref, z_ref, x_ref, p_ref, hg_ref, wo_ref, pw_ref, pn_ref, pg_ref,
                   out_ref, y_ref):
    for h in range(A_HEADS):
        sl = slice(h * A_DV, (h + 1) * A_DV)
        y = _rms(hs_ref[:, sl], hg_ref[:, sl]) * _sigmoid(o_ref[:, sl].astype(F32))
        z = z_ref[:, sl].astype(F32)
        y_ref[:, sl] = (y * (z * _sigmoid(z))).astype(BF16)
    h1 = x_ref[...] + jnp.dot(y_ref[...], wo_ref[...], preferred_element_type=F32)
    out_ref[...] = _ple(h1, p_ref, pw_ref, pn_ref, pg_ref)


def _const_spec(shape):
    return pl.BlockSpec(shape, lambda i: (0,) * len(shape))


def _a_post(hs, u, x, p, layer, hnorm, w_out, ple_w, ple_norm, ple_wg):
    t = x.shape[0]
    tm = _tile(t, 512)
    return pl.pallas_call(
        _a_post_kernel,
        grid=(t // tm,),
        in_specs=[
            pl.BlockSpec((tm, A_V), lambda i: (i, 0)),
            pl.BlockSpec((tm, A_V), lambda i: (i, 2)),
            pl.BlockSpec((tm, A_V), lambda i: (i, 3)),
            pl.BlockSpec((tm, D_MODEL), lambda i: (i, 0)),
            pl.BlockSpec((None, tm, PLE_DIM), lambda i: (layer, i, 0)),
            _const_spec((1, A_V)),
            _const_spec((A_V, D_MODEL)),
            _const_spec((PLE_DIM, D_MODEL)),
            _const_spec((1, D_MODEL)),
            _const_spec((D_MODEL, D_MODEL)),
        ],
        out_specs=pl.BlockSpec((tm, D_MODEL), lambda i: (i, 0)),
        out_shape=jax.ShapeDtypeStruct((t, D_MODEL), F32),
        scratch_shapes=[pltpu.VMEM((tm, A_V), BF16)],
        compiler_params=_params("parallel"),
        name="a_post",
    )(hs, u, u, x, p, hnorm, w_out, ple_w, ple_norm, ple_wg)


def _head_ms(x, gm_ref):
    return jnp.dot((x * x).astype(BF16), gm_ref[...], preferred_element_type=F32) * (1.0 / B_DH)


def _kv_kernel(h_ref, g_ref, w_ref, wfh_ref, wfl_ref, wfth_ref, wftl_ref, bf_ref, bft_ref, kg_ref, gm_ref,
               k_ref, v_ref, kb_ref, vb_ref, vt_ref, lf_ref, lft_ref):
    xs = _split(_rms(h_ref[...], g_ref[...]))
    kv = jnp.dot(xs[0], w_ref[...], preferred_element_type=F32)
    k = kv[:, :B_W]
    v = kv[:, B_W:]
    kn = k * lax.rsqrt(_head_ms(k, gm_ref) + EPS) * kg_ref[...]
    k_ref[...] = pltpu.einshape("m(hd)->mhd", kn, h=B_HEADS)
    v_ref[...] = pltpu.einshape("m(hd)->mhd", v, h=B_HEADS)
    kb_ref[...] = kn.astype(BF16)
    vb_ref[...] = v.astype(BF16)
    vt_ref[...] = v.T.astype(BF16)
    fg = _dot_split(xs, (wfh_ref[...], wfl_ref[...]), NN)
    lf_ref[...] = _log_sigmoid(fg + bf_ref[...])
    fgt = _dot_split((wfth_ref[...], wftl_ref[...]), xs, NT)
    lft_ref[...] = _log_sigmoid(fgt + bft_ref[...])


def _kv_proj(h, g, w_kv, w_f, w_ft, b_f, b_ft, k_g, gmat):
    t = h.shape[0]
    tm = _tile(t, 512)
    row = lambda w: pl.BlockSpec((tm, w), lambda i: (i, 0))
    return pl.pallas_call(
        _kv_kernel,
        grid=(t // tm,),
        in_specs=[
            row(D_MODEL),
            _const_spec((1, D_MODEL)),
            _const_spec((D_MODEL, 2 * B_W)),
            _const_spec((D_MODEL, B_HEADS)),
            _const_spec((D_MODEL, B_HEADS)),
            _const_spec((B_HEADS, D_MODEL)),
            _const_spec((B_HEADS, D_MODEL)),
            _const_spec((1, B_HEADS)),
            _const_spec((B_HEADS, 1)),
            _const_spec((1, B_W)),
            _const_spec((B_W, B_W)),
        ],
        out_specs=[pl.BlockSpec((tm, B_HEADS, B_DH), lambda i: (i, 0, 0)),
                   pl.BlockSpec((tm, B_HEADS, B_DH), lambda i: (i, 0, 0)), row(B_W), row(B_W),
                   pl.BlockSpec((B_W, tm), lambda i: (0, i)), row(B_HEADS),
                   pl.BlockSpec((B_HEADS, tm), lambda i: (0, i))],
        out_shape=[
            jax.ShapeDtypeStruct((t, B_HEADS, B_DH), F32),
            jax.ShapeDtypeStruct((t, B_HEADS, B_DH), F32),
            jax.ShapeDtypeStruct((t, B_W), BF16),
            jax.ShapeDtypeStruct((t, B_W), BF16),
            jax.ShapeDtypeStruct((B_W, t), BF16),
            jax.ShapeDtypeStruct((t, B_HEADS), F32),
            jax.ShapeDtypeStruct((B_HEADS, t), F32),
        ],
        compiler_params=_params("parallel"),
        name="kv_proj",
    )(h, g, w_kv, *w_f, *w_ft, b_f, b_ft, k_g, gmat)


def _q_kernel(h_ref, g_ref, w_ref, qg_ref, gm_ref, q_ref, z_ref):
    xn = _rms(h_ref[...], g_ref[...])
    qz = jnp.dot(xn.astype(BF16), w_ref[...], preferred_element_type=F32)
    q = qz[:, :B_W]
    z_ref[...] = qz[:, B_W:]
    qn = q * lax.rsqrt(_head_ms(q, gm_ref) + EPS) * qg_ref[...]
    q_ref[...] = (qn * (B_DH ** -0.5 * LOG2E)).astype(BF16)


def _q_proj(h, g, w_in, q_g, gmat):
    t = h.shape[0]
    tm = _tile(t, 512)
    row = lambda w: pl.BlockSpec((tm, w), lambda i: (i, 0))
    return pl.pallas_call(
        _q_kernel,
        grid=(t // tm,),
        in_specs=[row(D_MODEL), _const_spec((1, D_MODEL)), _const_spec((D_MODEL, 2 * B_W)),
                  _const_spec((1, B_W)), _const_spec((B_W, B_W))],
        out_specs=[row(B_W), row(B_W)],
        out_shape=[jax.ShapeDtypeStruct((t, B_W), BF16), jax.ShapeDtypeStruct((t, B_W), F32)],
        compiler_params=_params("parallel"),
        name="q_proj",
    )(h, g, w_in, q_g, gmat)


def _cumsum_kernel(x_ref, f_ref, carry_ref):
    @pl.when(pl.program_id(1) == 0)
    def _():
        carry_ref[...] = jnp.zeros_like(carry_ref)

    w = x_ref.shape[-1]
    f = _seg_cumsum_lanes(x_ref[...], w) + carry_ref[:, 0:1]
    f_ref[...] = f
    carry_ref[...] = jnp.broadcast_to(f[:, w - 1:w], carry_ref.shape)


def _cumsum_rows(x, bsz, width):
    tb = _tile(width, 2048) if width % LANES == 0 and (width // LANES) & (width // LANES - 1) == 0 else width
    nb = width // tb
    if x.ndim == 2:
        in_spec = pl.BlockSpec((B_HEADS, tb), lambda b, j: (0, b * nb + j))
    else:
        in_spec = pl.BlockSpec((None, B_HEADS, tb), lambda b, j: (b, 0, j))
    return pl.pallas_call(
        _cumsum_kernel,
        grid=(bsz, nb),
        in_specs=[in_spec],
        out_specs=pl.BlockSpec((None, B_HEADS, tb), lambda b, j: (b, 0, j)),
        out_shape=jax.ShapeDtypeStruct((bsz, B_HEADS, width), F32),
        scratch_shapes=[pltpu.VMEM((B_HEADS, LANES), F32)],
        compiler_params=_params("parallel", "arbitrary"),
        name="logf_cumsum",
    )(x)


def _cumsum_aug_kernel(x_ref, place_ref, f_ref, ka_ref, carry_ref):
    @pl.when(pl.program_id(1) == 0)
    def _():
        carry_ref[...] = jnp.zeros_like(carry_ref)

    w = x_ref.shape[-1]
    f = _seg_cumsum_lanes(x_ref[...], w) + carry_ref[:, 0:1]
    f_ref[...] = f
    carry_ref[...] = jnp.broadcast_to(f[:, w - 1:w], carry_ref.shape)
    rows = jnp.concatenate(_split3(f * LOG2E) + (jnp.ones((B_HEADS, w), BF16),), axis=0)
    ka_ref[...] = lax.dot_general(rows, place_ref[...], TN, preferred_element_type=F32).astype(BF16)


def _bias_placement():
    r = jnp.arange(4 * B_HEADS)[:, None]
    c = jnp.arange(B_W)[None, :]
    part, head = r // B_HEADS, r % B_HEADS
    lane0 = (head // 2) * LANES + BIAS_LANES * (head % 2)
    is_part = (part < 3) & (c == lane0 + part)
    is_one = (part == 3) & (c >= lane0 + 3) & (c < lane0 + 6)
    return (is_part | is_one).astype(BF16)


def _cumsum_aug(x, bsz, width):
    tb = _tile(width, 2048)
    nb = width // tb
    return pl.pallas_call(
        _cumsum_aug_kernel,
        grid=(bsz, nb),
        in_specs=[pl.BlockSpec((B_HEADS, tb), lambda b, j: (0, b * nb + j)),
                  pl.BlockSpec((4 * B_HEADS, B_W), lambda b, j: (0, 0))],
        out_specs=[pl.BlockSpec((None, B_HEADS, tb), lambda b, j: (b, 0, j)),
                   pl.BlockSpec((tb, B_W), lambda b, j: (b * nb + j, 0))],
        out_shape=[jax.ShapeDtypeStruct((bsz, B_HEADS, width), F32),
                   jax.ShapeDtypeStruct((bsz * width, B_W), BF16)],
        scratch_shapes=[pltpu.VMEM((B_HEADS, LANES), F32)],
        compiler_params=_params("parallel", "arbitrary"),
        name="logf_cumsum_aug",
    )(x, _bias_placement())


def _fox_bounded_kernel(bound_ref, q_ref, k_ref, ka_ref, vt_ref, f_ref, o_ref, qt_ref, acc_ref,
                        p00, p01, p10, p11, *, tq):
    tk = tq // 2
    pbuf = ((p00, p01), (p10, p11))
    qi = pl.program_id(2)
    q0 = pl.multiple_of(qi * tq, tq)
    feat = lax.broadcasted_iota(jnp.int32, (LANES, tq), 0)
    qt = q_ref[...].astype(F32).T
    beta = _split3(f_ref[:, pl.ds(q0, tq)] * LOG2E - bound_ref[...])
    for hh in range(2):
        lo = BIAS_LANES * hh
        qt_ref[hh, 0:LANES, :] = jnp.where((feat >= hh * B_DH) & (feat < (hh + 1) * B_DH), qt, 0.0).astype(BF16)
        aug = jnp.where((feat >= lo) & (feat < lo + 3), -1.0, 0.0)
        for part in range(3):
            aug = jnp.where(feat == lo + 3 + part, beta[part][hh:hh + 1].astype(F32), aug)
        qt_ref[hh, LANES:2 * LANES, :] = aug.astype(BF16)
    acc_ref[...] = jnp.zeros_like(acc_ref)
    ones = jnp.ones((ACC_ROWS - B_DH, tk), BF16)
    kpos = lax.broadcasted_iota(jnp.int32, (tk, tq), 0)
    qpos = lax.broadcasted_iota(jnp.int32, (tk, tq), 1)
    diag_masks = (kpos <= qpos, (kpos + tk) <= qpos)

    def weights(k0, slot, causal=None):
        lhs = jnp.concatenate([k_ref[pl.ds(k0, tk), :], ka_ref[pl.ds(k0, tk), :]], axis=1)
        for hh in range(2):
            e = jnp.dot(lhs, qt_ref[hh], preferred_element_type=F32)
            if causal is not None:
                e = jnp.where(causal, e, NEG)
            pbuf[slot][hh][...] = jnp.exp2(e).astype(BF16)

    def accumulate(k0, slot):
        for hh in range(2):
            vt = jnp.concatenate([vt_ref[hh * B_DH:(hh + 1) * B_DH, pl.ds(k0, tk)], ones], axis=0)
            acc_ref[hh] += jnp.dot(vt, pbuf[slot][hh][...], preferred_element_type=F32)

    def pair(k0, next_causal):
        weights(k0 + tk, 1)
        accumulate(k0, 0)
        weights(k0 + tq, 0, next_causal)
        accumulate(k0 + tk, 1)

    def finish():
        weights(q0 + tk, 1, diag_masks[1])
        accumulate(q0, 0)
        accumulate(q0 + tk, 1)
        o_t = jnp.concatenate(
            [acc_ref[hh, 0:B_DH] * (1.0 / acc_ref[hh, B_DH:B_DH + 1]) for hh in range(2)], axis=0)
        o_ref[...] = o_t.T

    @pl.when(qi == 0)
    def _():
        weights(q0, 0, diag_masks[0])
        finish()

    @pl.when(qi > 0)
    def _():
        weights(0, 0)
        n_pairs = qi - 1

        def unrolled(jj, carry):
            for u in range(PAIR_UNROLL):
                pair(pl.multiple_of((jj * PAIR_UNROLL + u) * tq, tq), None)
            return carry

        def single(jj, carry):
            pair(pl.multiple_of(jj * tq, tq), None)
            return carry

        lax.fori_loop(0, n_pairs // PAIR_UNROLL, unrolled, 0)
        lax.fori_loop((n_pairs // PAIR_UNROLL) * PAIR_UNROLL, n_pairs, single, 0)
        pair(q0 - tq, diag_masks[0])
        finish()


def _fox_prompt_bounded(q, kb, ka, vt, f_row, bound, bsz, seq):
    tq = _tile(seq, 1024)
    nq = seq // tq
    kern = functools.partial(_fox_bounded_kernel, tq=tq)
    return pl.pallas_call(
        kern,
        grid=(bsz, B_PAIRS, nq),
        in_specs=[
            pl.BlockSpec((1, 1), lambda b, p, qi: (0, 0)),
            pl.BlockSpec((tq, LANES), lambda b, p, qi: (b * nq + qi, p)),
            pl.BlockSpec((seq, LANES), lambda b, p, qi: (b, p)),
            pl.BlockSpec((seq, LANES), lambda b, p, qi: (b, p)),
            pl.BlockSpec((LANES, seq), lambda b, p, qi: (p, b)),
            pl.BlockSpec((None, None, 2, seq), lambda b, p, qi: (b, p, 0, 0)),
        ],
        out_specs=pl.BlockSpec((tq, LANES), lambda b, p, qi: (b * nq + qi, p)),
        out_shape=jax.ShapeDtypeStruct((bsz * seq, B_W), F32),
        scratch_shapes=[
            pltpu.VMEM((2, 2 * LANES, tq), BF16),
            pltpu.VMEM((2, ACC_ROWS, tq), F32),
        ] + [pltpu.VMEM((tq // 2, tq), BF16)] * 4,
        compiler_params=_params("parallel", "parallel", "arbitrary"),
        name="fox_prompt_bounded",
    )(bound, q, kb, ka, vt, f_row)


def _fox_kernel(q_ref, k_ref, vt_ref, f_ref, o_ref, qt_ref, m_ref, cm_ref, acc_ref,
                s00, s01, s10, s11, *, tq):
    tk = tq // 2
    sbuf = ((s00, s01), (s10, s11))
    qi = pl.program_id(2)
    q0 = pl.multiple_of(qi * tq, tq)
    feat = lax.broadcasted_iota(jnp.int32, (LANES, tq), 0)
    qt = q_ref[...].astype(F32).T
    qt_ref[0] = jnp.where(feat < B_DH, qt, 0.0).astype(BF16)
    qt_ref[1] = jnp.where(feat >= B_DH, qt, 0.0).astype(BF16)
    m_ref[...] = jnp.full_like(m_ref, NEG)
    acc_ref[...] = jnp.zeros_like(acc_ref)
    f_base = f_ref[:, pl.ds(q0, tq)][:, tq - 1:tq]
    ones = jnp.ones((ACC_ROWS - B_DH, tk), BF16)
    kpos = lax.broadcasted_iota(jnp.int32, (tk, tq), 0)
    qpos = lax.broadcasted_iota(jnp.int32, (tk, tq), 1)
    diag_masks = (kpos <= qpos, (kpos + tk) <= qpos)

    def scores(k0, slot, causal=None):
        kb = k_ref[pl.ds(k0, tk), :]
        fk = (f_ref[:, pl.ds(k0, tk)] - f_base) * LOG2E
        for hh in range(2):
            fk_col = jnp.concatenate(
                [jnp.broadcast_to(fk[hh:hh + 1, c * LANES:(c + 1) * LANES], (LANES, LANES)).T
                 for c in range(tk // LANES)], axis=0)
            t = (jnp.dot(kb, qt_ref[hh], preferred_element_type=F32)
                 - pltpu.repeat(fk_col, tq // LANES, axis=1))
            if causal is not None:
                t = jnp.where(causal, t, NEG)
            sbuf[slot][hh][...] = t
            cm_ref[slot, hh] = jnp.max(t, axis=0, keepdims=True)

    def update(k0, slot):
        for hh in range(2):
            m_prev = m_ref[hh]
            m_new = jnp.maximum(m_prev, cm_ref[slot, hh])
            alpha = jnp.exp2(m_prev - m_new)
            p = jnp.exp2(sbuf[slot][hh][...] - m_new).astype(BF16)
            vt = jnp.concatenate([vt_ref[hh * B_DH:(hh + 1) * B_DH, pl.ds(k0, tk)], ones], axis=0)
            acc_ref[hh] = alpha * acc_ref[hh] + jnp.dot(vt, p, preferred_element_type=F32)
            m_ref[hh] = m_new

    def pair(k0, next_causal):
        scores(k0 + tk, 1)
        update(k0, 0)
        scores(k0 + tq, 0, next_causal)
        update(k0 + tk, 1)

    @pl.when(qi == 0)
    def _():
        scores(q0, 0, diag_masks[0])

    @pl.when(qi > 0)
    def _():
        scores(0, 0)

        def body(jj, carry):
            pair(pl.multiple_of(jj * tq, tq), None)
            return carry

        lax.fori_loop(0, qi - 1, body, 0)
        pair(q0 - tq, diag_masks[0])

    scores(q0 + tk, 1, diag_masks[1])
    update(q0, 0)
    update(q0 + tk, 1)
    o_t = jnp.concatenate(
        [acc_ref[hh, 0:B_DH] * (1.0 / acc_ref[hh, B_DH:B_DH + 1]) for hh in range(2)], axis=0)
    o_ref[...] = o_t.T


def _fox_prompt(q, kb, vt, f_row, bsz, seq):
    tq = _tile(seq, 512)
    tk = tq // 2
    nq = seq // tq
    kern = functools.partial(_fox_kernel, tq=tq)
    return pl.pallas_call(
        kern,
        grid=(bsz, B_PAIRS, nq),
        in_specs=[
            pl.BlockSpec((tq, LANES), lambda b, p, qi: (b * nq + qi, p)),
            pl.BlockSpec((seq, LANES), lambda b, p, qi: (b, p)),
            pl.BlockSpec((LANES, seq), lambda b, p, qi: (p, b)),
            pl.BlockSpec((None, None, 2, seq), lambda b, p, qi: (b, p, 0, 0)),
        ],
        out_specs=pl.BlockSpec((tq, LANES), lambda b, p, qi: (b * nq + qi, p)),
        out_shape=jax.ShapeDtypeStruct((bsz * seq, B_W), F32),
        scratch_shapes=[
            pltpu.VMEM((2, LANES, tq), BF16),
            pltpu.VMEM((2, 1, tq), F32),
            pltpu.VMEM((2, 2, 1, tq), F32),
            pltpu.VMEM((2, ACC_ROWS, tq), F32),
        ] + [pltpu.VMEM((tk, tq), F32)] * 4,
        compiler_params=_params("parallel", "parallel", "arbitrary"),
        name="fox_prompt",
    )(q, kb, vt, f_row)


def _fox_dec_kernel(bound_ref, q_ref, kt_ref, vt_ref, kn_ref, vn_ref, fn_ref, fc_ref, o_ref,
                    fqc_ref, m_ref, l_ref, a_ref, acc_ref, s_ref, p_ref, *, sq, nkb, bounded):
    kj = pl.program_id(1)
    head = lambda h: slice(h * B_DH, (h + 1) * B_DH)

    @pl.when(kj == 0)
    def _():
        r = lax.broadcasted_iota(jnp.int32, (sq, LANES), 0)
        c = lax.broadcasted_iota(jnp.int32, (sq, LANES), 1)
        for h in range(B_HEADS):
            fqc_ref[h] = jnp.sum(jnp.where(r == c, fn_ref[h:h + 1, :], 0.0), axis=1, keepdims=True)
        m_ref[...] = jnp.full_like(m_ref, NEG)
        l_ref[...] = jnp.zeros_like(l_ref)
        acc_ref[...] = jnp.zeros_like(acc_ref)

    def attend(width, score, f_keys, values, valid):
        for h in range(B_HEADS):
            s_ref[h, :, 0:width] = score(h)
        for h in range(B_HEADS):
            t = s_ref[h, :, 0:width] + (fqc_ref[h] - f_keys(h)) * LOG2E
            if valid is not None:
                t = jnp.where(valid, t, NEG)
            if bounded:
                p = jnp.exp2(t - bound_ref[...])
                l_ref[h] += jnp.sum(p, axis=1, keepdims=True)
            else:
                m_prev = m_ref[h]
                m_new = jnp.maximum(m_prev, jnp.max(t, axis=1, keepdims=True))
                alpha = jnp.exp2(m_prev - m_new)
                p = jnp.exp2(t - m_new)
                l_ref[h] = alpha * l_ref[h] + jnp.sum(p, axis=1, keepdims=True)
                m_ref[h] = m_new
                a_ref[h] = alpha
            p_ref[h, :, 0:width] = p.astype(BF16)
        for h in range(B_HEADS):
            pv = values(h, p_ref[h, :, 0:width])
            acc_ref[h] = acc_ref[h] + pv if bounded else a_ref[h] * acc_ref[h] + pv

    @pl.when(kj == 0)
    def _():
        i = lax.broadcasted_iota(jnp.int32, (sq, LANES), 0)
        j = lax.broadcasted_iota(jnp.int32, (sq, LANES), 1)
        attend(
            LANES,
            lambda h: lax.dot_general(q_ref[:, head(h)], kn_ref[:, head(h)], NT, preferred_element_type=F32),
            lambda h: fn_ref[h:h + 1, :],
            lambda h, p: jnp.dot(p, vn_ref[:, head(h)], preferred_element_type=F32),
            j <= i)

    @pl.when(kj > 0)
    def _():
        attend(
            kt_ref.shape[-1],
            lambda h: jnp.dot(q_ref[:, head(h)], kt_ref[h].astype(BF16), preferred_element_type=F32),
            lambda h: fc_ref[h:h + 1, :],
            lambda h, p: lax.dot_general(p, vt_ref[h].astype(BF16), NT, preferred_element_type=F32),
            None)

    @pl.when(kj == nkb)
    def _():
        for h in range(B_HEADS):
            o_ref[:, head(h)] = acc_ref[h] * (1.0 / l_ref[h])


def _fox_decode(q, cache_kt, cache_vt, kn, vn, f_new, f_cache, bound, bounded, bsz, sq):
    past = cache_kt.shape[3]
    tk = _tile(past, 2048)
    nkb = past // tk
    cmap = lambda b, kj: (b, 0, 0, jnp.maximum(kj - 1, 0))
    kern = functools.partial(_fox_dec_kernel, sq=sq, nkb=nkb, bounded=bounded)
    return pl.pallas_call(
        kern,
        grid=(bsz, nkb + 1),
        in_specs=[
            pl.BlockSpec((1, 1), lambda b, kj: (0, 0)),
            pl.BlockSpec((None, sq, B_W), lambda b, kj: (b, 0, 0)),
            pl.BlockSpec((None, B_HEADS, B_DH, tk), cmap),
            pl.BlockSpec((None, B_HEADS, B_DH, tk), cmap),
            pl.BlockSpec((None, LANES, B_W), lambda b, kj: (b, 0, 0)),
            pl.BlockSpec((None, LANES, B_W), lambda b, kj: (b, 0, 0)),
            pl.BlockSpec((None, B_HEADS, LANES), lambda b, kj: (b, 0, 0)),
            pl.BlockSpec((None, B_HEADS, tk), lambda b, kj: (b, 0, jnp.maximum(kj - 1, 0))),
        ],
        out_specs=pl.BlockSpec((None, sq, B_W), lambda b, kj: (b, 0, 0)),
        out_shape=jax.ShapeDtypeStruct((bsz, sq, B_W), F32),
        scratch_shapes=[
            pltpu.VMEM((B_HEADS, sq, 1), F32),
            pltpu.VMEM((B_HEADS, sq, 1), F32),
            pltpu.VMEM((B_HEADS, sq, 1), F32),
            pltpu.VMEM((B_HEADS, sq, 1), F32),
            pltpu.VMEM((B_HEADS, sq, B_DH), F32),
            pltpu.VMEM((B_HEADS, sq, tk), F32),
            pltpu.VMEM((B_HEADS, sq, tk), BF16),
        ],
        compiler_params=_params("parallel", "arbitrary"),
        name="fox_decode",
    )(bound, q, cache_kt, cache_vt, kn, vn, f_new, f_cache)


def _b_post_kernel(o_ref, z_ref, h_ref, p_ref, wo_ref, pw_ref, pn_ref, pg_ref, out_ref):
    z = z_ref[...]
    y = o_ref[...] * (z * _sigmoid(z))
    h2 = h_ref[...] + jnp.dot(y.astype(BF16), wo_ref[...], preferred_element_type=F32)
    out_ref[...] = _ple(h2, p_ref, pw_ref, pn_ref, pg_ref)


def _b_post(o, z, h, p, layer, w_out, ple_w, ple_norm, ple_wg):
    t = h.shape[0]
    tm = _tile(t, 1024)
    row = lambda w: pl.BlockSpec((tm, w), lambda i: (i, 0))
    return pl.pallas_call(
        _b_post_kernel,
        grid=(t // tm,),
        in_specs=[row(B_W), row(B_W), row(D_MODEL), pl.BlockSpec((None, tm, PLE_DIM), lambda i: (layer, i, 0)),
                  _const_spec((B_W, D_MODEL)), _const_spec((PLE_DIM, D_MODEL)),
                  _const_spec((1, D_MODEL)), _const_spec((D_MODEL, D_MODEL))],
        out_specs=row(D_MODEL),
        out_shape=jax.ShapeDtypeStruct((t, D_MODEL), F32),
        compiler_params=_params("parallel"),
        name="b_post",
    )(o, z, h, p, w_out, ple_w, ple_norm, ple_wg)


def _prep_weights(a_norm, a_w_in, a_b_i, a_b_f, a_hnorm, a_w_out, kv_norm, kv_w, kv_b_f, k_norm,
                  b_norm, b_w_in, q_norm, b_w_out, ple_w, ple_norm, ple_w_g):
    head_of_lane = jnp.arange(B_W, dtype=jnp.int32) // B_DH
    return dict(
        a_norm=a_norm[0].reshape(1, D_MODEL),
        a_w_main=a_w_in[0][:, :A_MAIN].astype(BF16),
        a_w_gt=_split(a_w_in[0][:, A_MAIN:].T),
        a_bias=jnp.concatenate([a_b_i[0], a_b_f[0]]).reshape(2 * A_HEADS, 1),
        a_hnorm=a_hnorm[0].reshape(1, A_V),
        a_w_out=a_w_out[0].astype(BF16),
        kv_norm=kv_norm.reshape(1, D_MODEL),
        kv_w=kv_w[:, :2 * B_W].astype(BF16),
        kv_wf=_split(kv_w[:, 2 * B_W:]),
        kv_wft=_split(kv_w[:, 2 * B_W:].T),
        kv_bf=kv_b_f.reshape(1, B_HEADS),
        kv_bft=kv_b_f.reshape(B_HEADS, 1),
        k_norm=jnp.tile(k_norm, B_HEADS).reshape(1, B_W),
        gmat=(head_of_lane[:, None] == head_of_lane[None, :]).astype(BF16),
        b_norm=b_norm[0].reshape(1, D_MODEL),
        b_w_in=b_w_in[0].astype(BF16),
        q_norm=jnp.tile(q_norm[0], B_HEADS).reshape(1, B_W),
        b_w_out=b_w_out[0].astype(BF16),
        ple_w=ple_w.astype(BF16),
        ple_norm=ple_norm.reshape(-1, 1, D_MODEL),
        ple_wg=ple_w_g.astype(BF16),
    )


def _trunk(x, p, c0, n0, m0, past, w):
    bsz, seq, _ = x.shape
    t = bsz * seq
    xf = x.reshape(t, D_MODEL)
    pf = p.reshape(p.shape[0], t, PLE_DIM)

    u, gt = _a_inproj(xf, w["a_norm"], w["a_w_main"], w["a_w_gt"])
    m0r = jnp.broadcast_to(m0.reshape(bsz, A_HEADS, 1, 1), (bsz, A_HEADS, 1, LANES))
    hs, c_new, n_new, m_new = _mlstm(u, gt, w["a_bias"], c0, n0[..., None], m0r, bsz, seq)
    h1 = _a_post(hs, u, xf, pf, 0, w["a_hnorm"], w["a_w_out"], w["ple_w"][0], w["ple_norm"][0], w["ple_wg"][0])

    k, v, kb, vb, vt, lf, lft = _kv_proj(h1, w["kv_norm"], w["kv_w"], w["kv_wf"], w["kv_wft"],
                                     w["kv_bf"], w["kv_bft"], w["k_norm"], w["gmat"])
    q, z = _q_proj(h1, w["b_norm"], w["b_w_in"], w["q_norm"], w["gmat"])
    bound = (8.0 * LOG2E * 1.01) * jnp.max(jnp.abs(w["q_norm"])) * jnp.max(jnp.abs(w["k_norm"])) + 0.05
    bound11 = bound.reshape(1, 1)
    if past is None:
        f_row, ka = _cumsum_aug(lft, bsz, seq)
        f_row = f_row.reshape(bsz, B_PAIRS, 2, seq)
        o = lax.cond(
            bound <= MAX_BOUND,
            lambda: _fox_prompt_bounded(q, kb, ka, vt, f_row, bound11, bsz, seq),
            lambda: _fox_prompt(q, kb, vt, f_row, bsz, seq))
    else:
        cache_k, cache_v, cache_lf = past
        plen = cache_k.shape[1]
        assert seq <= LANES
        width = -(-(plen + seq) // LANES) * LANES
        lf_rows = jnp.concatenate(
            [jnp.swapaxes(cache_lf.astype(F32), 1, 2),
             jnp.swapaxes(lft.reshape(B_HEADS, bsz, seq), 0, 1),
             jnp.zeros((bsz, B_HEADS, width - plen - seq), F32)], axis=2)
        f_all = _cumsum_rows(lf_rows, bsz, width)
        f_new = jnp.pad(f_all[..., plen:plen + seq], ((0, 0), (0, 0), (0, LANES - seq)))
        pad_new = lambda a: jnp.pad(a.reshape(bsz, seq, B_W), ((0, 0), (0, LANES - seq), (0, 0)))
        to_t = lambda a: jnp.transpose(a.astype(F32), (0, 2, 3, 1))
        dec_args = (q.reshape(bsz, seq, B_W), to_t(cache_k), to_t(cache_v), pad_new(kb), pad_new(vb),
                    f_new, f_all[..., :plen], bound11)
        o = lax.cond(
            bound <= MAX_BOUND,
            lambda: _fox_decode(*dec_args, True, bsz, seq),
            lambda: _fox_decode(*dec_args, False, bsz, seq)).reshape(t, B_W)
    y = _b_post(o, z, h1, pf, 1, w["b_w_out"], w["ple_w"][1], w["ple_norm"][1], w["ple_wg"][1])

    return (y.reshape(bsz, seq, D_MODEL),
            c_new[None], n_new.reshape(1, bsz, A_HEADS, A_DK), m_new[None, :, :, 0, 0],
            k.reshape(bsz, seq, B_HEADS, B_DH), v.reshape(bsz, seq, B_HEADS, B_DH),
            lf.reshape(bsz, seq, B_HEADS))


def kernel(x_prompt, x_sample, cache_k, cache_v, cache_logf, state_C, state_n, state_m, p_prompt, p_sample,
           a_norm, a_w_in, a_b_i, a_b_f, a_hnorm, a_w_out, kv_norm, kv_w, kv_b_f, k_norm,
           b_norm, b_w_in, q_norm, b_w_out, ple_w, ple_norm, ple_w_g):
    w = _prep_weights(a_norm, a_w_in, a_b_i, a_b_f, a_hnorm, a_w_out, kv_norm, kv_w, kv_b_f, k_norm,
                      b_norm, b_w_in, q_norm, b_w_out, ple_w, ple_norm, ple_w_g)
    bsz = x_prompt.shape[0]
    c0 = jnp.zeros((bsz, A_HEADS, A_DK, A_DV), F32)
    n0 = jnp.zeros((bsz, A_HEADS, A_DK), F32)
    m0 = jnp.zeros((bsz, A_HEADS), F32)
    prompt = _trunk(x_prompt, p_prompt, c0, n0, m0, None, w)
    sample = _trunk(x_sample, p_sample, state_C[0].astype(F32), state_n[0].astype(F32),
                    state_m[0].astype(F32), (cache_k, cache_v, cache_logf), w)
    return (prompt[0], sample[0]) + prompt[1:] + sample[1:]
```
